```python
import math
import jax, jax.numpy as jnp
from jax import lax
import numpy as np

D_MODEL = 1024
BATCH = 4
SEQ = 8192
DEPTH = 1
DEC_BATCH = 128
DEC_SEQ = 8
PAST_LEN = 16384
PAGE_SIZE = 128

HEAD_DIM = 64
MIX_WIDTH = D_MODEL
ATT_HEADS = MIX_WIDTH // (2 * HEAD_DIM)
ATT_KV_HEADS = ATT_HEADS // 4
ATT_GROUP = ATT_HEADS // ATT_KV_HEADS
WINDOW = 128
ATT_BLOCK = WINDOW
ATT_Q = ATT_HEADS * HEAD_DIM
ATT_KV = ATT_KV_HEADS * HEAD_DIM
DN_HEADS = 8
DN_DK = HEAD_DIM
DN_DV = HEAD_DIM
DN_WIDTH = DN_HEADS * DN_DV
DN_QKV = 2 * DN_HEADS * DN_DK + DN_WIDTH
CONV_W = 4
DN_CHUNK = 64
IN_WIDTH = ATT_Q + 2 * ATT_KV + DN_QKV + DN_WIDTH + 2 * DN_HEADS
N_KEYS = 128
N_EXPERTS = N_KEYS * N_KEYS
PEER_HEADS = 8
PEER_QDIM = 256
PEER_TOPK = 16
PEER_BLOCK = 256
PLE_DIM = 256
EPS = 1e-6

kernel_name = 'hymba_swa_gdn_peer_step'


def rmsnorm(x, g):
    xf = x.astype(jnp.float32)
    y = xf * lax.rsqrt(jnp.mean(xf * xf, axis=-1, keepdims=True) + EPS)
    return (y * g.astype(jnp.float32)).astype(x.dtype)


def l2norm(x):
    return x * lax.rsqrt(jnp.sum(x * x, axis=-1, keepdims=True) + EPS)


def alibi_slopes():
    h = jnp.arange(1, ATT_HEADS + 1, dtype=jnp.float32)
    return jnp.exp2(-8.0 * h / ATT_HEADS)


def _attend(q, k, v, dist, valid, sinks):
    s = jnp.einsum('bnqhgd,bnkhd->bnhgqk', q, k).astype(jnp.float32) * (HEAD_DIM ** -0.5)
    slopes = alibi_slopes().reshape(ATT_KV_HEADS, ATT_GROUP)[None, None, :, :, None, None]
    s = s - slopes * dist.astype(jnp.float32)[None, :, None, None]
    s = jnp.where(valid[None, :, None, None], s, -jnp.inf)
    sink = sinks.astype(jnp.float32).reshape(ATT_KV_HEADS, ATT_GROUP)[None, None, :, :, None, None]
    m = jnp.maximum(jnp.max(s, axis=-1, keepdims=True), sink)
    e = jnp.exp(s - m)
    probs = e / (jnp.sum(e, axis=-1, keepdims=True) + jnp.exp(sink - m))
    return jnp.einsum('bnhgqk,bnkhd->bnqhgd', probs.astype(v.dtype), v)


def swa_prompt(q, k, v, sinks):
    B, L = q.shape[0], q.shape[1]
    N = L // ATT_BLOCK
    qb = q.reshape(B, N, ATT_BLOCK, ATT_KV_HEADS, ATT_GROUP, HEAD_DIM)

    def band(t):
        tp = jnp.pad(t, ((0, 0), (ATT_BLOCK, 0), (0, 0), (0, 0)))
        tp = tp.reshape(B, N + 1, ATT_BLOCK, ATT_KV_HEADS, HEAD_DIM)
        return jnp.concatenate([tp[:, :-1], tp[:, 1:]], axis=2)

    i = jnp.arange(ATT_BLOCK)[:, None]
    j = jnp.arange(2 * ATT_BLOCK)[None, :]
    dist = ATT_BLOCK + i - j
    kpos = jnp.arange(N)[:, None, None] * ATT_BLOCK - ATT_BLOCK + j[None]
    valid = (dist >= 0) & (dist <= WINDOW) & (kpos >= 0)
    out = _attend(qb, band(k), band(v), jnp.broadcast_to(dist, valid.shape), valid, sinks)
    return out.reshape(B, L, ATT_Q), k[:, -WINDOW:], v[:, -WINDOW:]


def swa_sample(q, k, v, k_buf, v_buf, sinks):
    B, L = q.shape[0], q.shape[1]
    kc = jnp.concatenate([k_buf.astype(k.dtype), k], axis=1)
    vc = jnp.concatenate([v_buf.astype(v.dtype), v], axis=1)
    dist = jnp.arange(L)[:, None] - (jnp.arange(WINDOW + L)[None, :] - WINDOW)
    valid = (dist >= 0) & (dist <= WINDOW)
    qb = q.reshape(B, 1, L, ATT_KV_HEADS, ATT_GROUP, HEAD_DIM)
    out = _attend(qb, kc[:, None], vc[:, None], dist[None], valid[None], sinks)
    return out.reshape(B, L, ATT_Q), kc[:, -WINDOW:], vc[:, -WINDOW:]


def causal_conv(x, prefix, w):
    L = x.shape[1]
    xp = jnp.concatenate([prefix.astype(x.dtype), x], axis=1)
    y = xp[:, 0:L] * w[0]
    for j in range(1, CONV_W):
        y = y + xp[:, j:j + L] * w[j]
    return jax.nn.silu(y), xp[:, -(CONV_W - 1):]


def gated_delta_rule(q, k, v, g, beta, s0):
    B, L, H, dk = q.shape
    dv = v.shape[-1]
    C = min(DN_CHUNK, L)
    pad = (-L) % C
    N = (L + pad) // C

    def chunks(t):
        t = jnp.pad(t, [(0, 0), (0, pad)] + [(0, 0)] * (t.ndim - 2))
        t = t.reshape((B, N, C) + t.shape[2:])
        return jnp.moveaxis(t, 3, 1)

    q, k, v, g, beta = chunks(q), chunks(k), chunks(v), chunks(g), chunks(beta)
    G = jnp.cumsum(g, axis=-1)
    idx = jnp.arange(C)
    lower = idx[:, None] >= idx[None, :]
    strict = idx[:, None] > idx[None, :]
    decay = jnp.exp(jnp.where(lower, G[..., :, None] - G[..., None, :], -jnp.inf))
    kb = k * beta[..., None]
    Lm = jnp.where(strict, jnp.einsum('bhnid,bhnjd->bhnij', kb, k) * decay, 0.0)
    M = Lm + jnp.eye(C, dtype=Lm.dtype)
    rhs = jnp.concatenate([v * beta[..., None], kb * jnp.exp(G)[..., None]], axis=-1)
    X = lax.linalg.triangular_solve(M, rhs, left_side=True, lower=True, unit_diagonal=True)
    u, w = X[..., :dv], X[..., dv:]
    qk = jnp.einsum('bhnid,bhnjd->bhnij', q, k) * decay
    g_last = G[..., -1]
    k_tail = k * jnp.exp(g_last[..., None] - G)[..., None]
    q_head = q * jnp.exp(G)[..., None]

    def step(S, xs):
        qh, kt, uc, wc, qkc, gl = xs
        v_new = uc - jnp.einsum('bhcd,bhde->bhce', wc, S)
        o = jnp.einsum('bhcd,bhde->bhce', qh, S) + jnp.einsum('bhij,bhje->bhie', qkc, v_new)
        S = S * jnp.exp(gl)[..., None, None] + jnp.einsum('bhcd,bhce->bhde', kt, v_new)
        return S, o

    xs = tuple(jnp.moveaxis(t, 2, 0) for t in (q_head, k_tail, u, w, qk, g_last))
    S, o = lax.scan(step, s0, xs)
    o = jnp.moveaxis(jnp.moveaxis(o, 0, 2), 1, 3).reshape(B, N * C, H, dv)[:, :L]
    return o, S


def delta_branch(qkv_raw, z, a_raw, b_raw, conv_prefix, s0, conv_w, a_log, dt_bias, dn_norm):
    B, L = qkv_raw.shape[0], qkv_raw.shape[1]
    qkv, conv_new = causal_conv(qkv_raw, conv_prefix, conv_w)
    q, k, v = jnp.split(qkv, [DN_HEADS * DN_DK, 2 * DN_HEADS * DN_DK], axis=-1)
    q = l2norm(q.reshape(B, L, DN_HEADS, DN_DK).astype(jnp.float32)) * (DN_DK ** -0.5)
    k = l2norm(k.reshape(B, L, DN_HEADS, DN_DK).astype(jnp.float32))
    v = v.reshape(B, L, DN_HEADS, DN_DV).astype(jnp.float32)
    beta = jax.nn.sigmoid(b_raw.astype(jnp.float32))
    g = -jnp.exp(a_log.astype(jnp.float32)) * jax.nn.softplus(a_raw.astype(jnp.float32) + dt_bias.astype(jnp.float32))
    o, s_new = gated_delta_rule(q, k, v, g, beta, s0.astype(jnp.float32))
    o = rmsnorm(o, dn_norm) * jax.nn.silu(z.reshape(B, L, DN_HEADS, DN_DV).astype(jnp.float32))
    return o.reshape(B, L, DN_WIDTH).astype(qkv_raw.dtype), conv_new, s_new


def hybrid_mixer(a, kv_buf, conv_prefix, s0, w_in, conv_w, sinks, a_log, dt_bias, dn_norm, w_out):
    B, L = a.shape[0], a.shape[1]
    proj = a @ w_in
    cuts = [ATT_Q, ATT_Q + ATT_KV, ATT_Q + 2 * ATT_KV, ATT_Q + 2 * ATT_KV + DN_QKV,
            ATT_Q + 2 * ATT_KV + DN_QKV + DN_WIDTH, ATT_Q + 2 * ATT_KV + DN_QKV + DN_WIDTH + DN_HEADS]
    qa, ka, va, dqkv, z, ar, br = jnp.split(proj, cuts, axis=-1)
    qa = qa.reshape(B, L, ATT_HEADS, HEAD_DIM)
    ka = ka.reshape(B, L, ATT_KV_HEADS, HEAD_DIM)
    va = va.reshape(B, L, ATT_KV_HEADS, HEAD_DIM)
    if kv_buf is None:
        att, k_new, v_new = swa_prompt(qa, ka, va, sinks)
    else:
        att, k_new, v_new = swa_sample(qa, ka, va, kv_buf[0], kv_buf[1], sinks)
    dn, conv_new, s_new = delta_branch(dqkv, z, ar, br, conv_prefix, s0, conv_w, a_log, dt_bias, dn_norm)
    out = jnp.concatenate([att, dn], axis=-1) @ w_out
    return out, k_new, v_new, conv_new, s_new


def peer(x, wq, keys1, keys2, u_tab, v_tab):
    B, L, D = x.shape
    T = B * L
    blk = min(PEER_BLOCK, T)
    pad = (-T) % blk
    xt = jnp.pad(x.reshape(T, D), ((0, pad), (0, 0))).reshape(-1, blk, D)
    half = PEER_QDIM // 2

    def block(xb):
        q = (xb @ wq).reshape(blk, PEER_HEADS, PEER_QDIM).astype(jnp.float32)
        s1 = jnp.einsum('thd,hnd->thn', q[..., :half], keys1.astype(jnp.float32))
        s2 = jnp.einsum('thd,hnd->thn', q[..., half:], keys2.astype(jnp.float32))
        v1, i1 = lax.top_k(s1, PEER_TOPK)
        v2, i2 = lax.top_k(s2, PEER_TOPK)
        cand = (v1[..., :, None] + v2[..., None, :]).reshape(blk, PEER_HEADS, PEER_TOPK * PEER_TOPK)
        sc, ci = lax.top_k(cand, PEER_TOPK)
        e1 = jnp.take_along_axis(i1, ci // PEER_TOPK, axis=-1)
        e2 = jnp.take_along_axis(i2, ci % PEER_TOPK, axis=-1)
        expert = e1 * N_KEYS + e2
        gate = jax.nn.softmax(sc, axis=-1)
        act = jax.nn.gelu(jnp.einsum('td,thkd->thk', xb, u_tab[expert]).astype(jnp.float32), approximate=False)
        return jnp.einsum('thk,thkd->td', (gate * act).astype(xb.dtype), v_tab[expert])

    y = lax.map(block, xt).reshape(-1, D)[:T]
    return y.reshape(B, L, D)


def decoder_layer(h, p_l, kv_buf, conv_prefix, s0, norm_mix, w_in, conv_w, sinks, a_log, dt_bias, dn_norm,
                  w_out, norm_ffn, peer_wq, keys1, keys2, peer_u, peer_v, norm_ple, ple_in, ple_gate):
    mix, k_new, v_new, conv_new, s_new = hybrid_mixer(rmsnorm(h, norm_mix), kv_buf, conv_prefix, s0, w_in,
                                                      conv_w, sinks, a_log, dt_bias, dn_norm, w_out)
    h = h + mix
    h = h + peer(rmsnorm(h, norm_ffn), peer_wq, keys1, keys2, peer_u, peer_v)
    gate = jax.nn.sigmoid((rmsnorm(h, norm_ple) @ ple_gate).astype(jnp.float32))
    h = h + ((p_l @ ple_in).astype(jnp.float32) * gate).astype(h.dtype)
    return h, k_new, v_new, conv_new, s_new


def setup_inputs(seed: int = 0) -> dict:
    key = jax.random.key(seed)
    ks = jax.random.split(key, 32)
    f32 = jnp.float32
    nrm = lambda k, shape, scale: jax.random.normal(k, shape, f32) * scale
    dt = jnp.exp(jax.random.uniform(ks[10], (DEPTH, DN_HEADS), f32, math.log(1e-3), math.log(1e-1)))
    return {
        'x_prompt': nrm(ks[0], (BATCH, SEQ, D_MODEL), 1.0),
        'x_sample': nrm(ks[1], (DEC_BATCH, DEC_SEQ, D_MODEL), 1.0),
        'p_prompt': nrm(ks[2], (DEPTH, BATCH, SEQ, PLE_DIM), 1.0),
        'p_sample': nrm(ks[3], (DEPTH, DEC_BATCH, DEC_SEQ, PLE_DIM), 1.0),
        'cache_swa_k': nrm(ks[4], (DEPTH, DEC_BATCH, WINDOW, ATT_KV_HEADS, HEAD_DIM), 1.0),
        'cache_swa_v': nrm(ks[5], (DEPTH, DEC_BATCH, WINDOW, ATT_KV_HEADS, HEAD_DIM), 1.0),
        'state_conv': nrm(ks[6], (DEPTH, DEC_BATCH, CONV_W - 1, DN_QKV), 1.0),
        'state_delta': nrm(ks[7], (DEPTH, DEC_BATCH, DN_HEADS, DN_DK, DN_DV), DN_DK ** -0.5),
        'norm_mix': 1.0 + nrm(ks[8], (DEPTH, D_MODEL), 0.02),
        'w_in': nrm(ks[9], (DEPTH, D_MODEL, IN_WIDTH), D_MODEL ** -0.5),
        'conv_w': nrm(ks[11], (DEPTH, CONV_W, DN_QKV), 0.5),
        'attn_sinks': nrm(ks[12], (DEPTH, ATT_HEADS), 0.5),
        'dn_a_log': jnp.log(jax.random.uniform(ks[13], (DEPTH, DN_HEADS), f32, 1.0, 16.0)),
        'dn_dt_bias': dt + jnp.log(-jnp.expm1(-dt)),
        'dn_norm': 1.0 + nrm(ks[14], (DEPTH, DN_DV), 0.02),
        'w_out': nrm(ks[15], (DEPTH, MIX_WIDTH, D_MODEL), MIX_WIDTH ** -0.5),
        'norm_ffn': 1.0 + nrm(ks[16], (DEPTH, D_MODEL), 0.02),
        'peer_wq': nrm(ks[17], (DEPTH, D_MODEL, PEER_HEADS * PEER_QDIM), D_MODEL ** -0.5),
        'peer_keys1': nrm(ks[18], (DEPTH, PEER_HEADS, N_KEYS, PEER_QDIM // 2), (PEER_QDIM // 2) ** -0.5),
        'peer_keys2': nrm(ks[19], (DEPTH, PEER_HEADS, N_KEYS, PEER_QDIM // 2), (PEER_QDIM // 2) ** -0.5),
        'peer_u': nrm(ks[20], (DEPTH, N_EXPERTS, D_MODEL), D_MODEL ** -0.5),
        'peer_v': nrm(ks[21], (DEPTH, N_EXPERTS, D_MODEL), 0.25),
        'norm_ple': 1.0 + nrm(ks[22], (DEPTH, D_MODEL), 0.02),
        'ple_in': nrm(ks[23], (DEPTH, PLE_DIM, D_MODEL), PLE_DIM ** -0.5),
        'ple_gate': nrm(ks[24], (DEPTH, D_MODEL, D_MODEL), D_MODEL ** -0.5),
        'norm_final': 1.0 + nrm(ks[25], (D_MODEL,), 0.02),
    }


def reference(x_prompt, x_sample, p_prompt, p_sample, cache_swa_k, cache_swa_v, state_conv, state_delta,
              norm_mix, w_in, conv_w, attn_sinks, dn_a_log, dn_dt_bias, dn_norm, w_out, norm_ffn,
              peer_wq, peer_keys1, peer_keys2, peer_u, peer_v, norm_ple, ple_in, ple_gate, norm_final):
    hp, hs = x_prompt, x_sample
    Bp = x_prompt.shape[0]
    kp_l, vp_l, cp_l, sp_l, ks_l, vs_l, cs_l, ss_l = [], [], [], [], [], [], [], []
    for l in range(DEPTH):
        w = (norm_mix[l], w_in[l], conv_w[l], attn_sinks[l], dn_a_log[l], dn_dt_bias[l], dn_norm[l],
             w_out[l], norm_ffn[l], peer_wq[l], peer_keys1[l], peer_keys2[l], peer_u[l], peer_v[l],
             norm_ple[l], ple_in[l], ple_gate[l])
        conv0 = jnp.zeros((Bp, CONV_W - 1, DN_QKV), x_prompt.dtype)
        s0 = jnp.zeros((Bp, DN_HEADS, DN_DK, DN_DV), jnp.float32)
        hp, kp, vp, cp, sp = decoder_layer(hp, p_prompt[l], None, conv0, s0, *w)
        hs, kn, vn, cn, sn = decoder_layer(hs, p_sample[l], (cache_swa_k[l], cache_swa_v[l]),
                                           state_conv[l], state_delta[l], *w)
        kp_l.append(kp); vp_l.append(vp); cp_l.append(cp); sp_l.append(sp.astype(state_delta.dtype))
        ks_l.append(kn); vs_l.append(vn); cs_l.append(cn); ss_l.append(sn.astype(state_delta.dtype))
    y_prompt = rmsnorm(hp, norm_final)
    y_sample = rmsnorm(hs, norm_final)
    return (y_prompt, y_sample,
            jnp.stack(kp_l), jnp.stack(vp_l), jnp.stack(cp_l), jnp.stack(sp_l),
            jnp.stack(ks_l), jnp.stack(vs_l), jnp.stack(cs_l), jnp.stack(ss_l))
```

```python
import functools
import math

import jax
import jax.numpy as jnp
from jax import lax
from jax.experimental import pallas as pl
from jax.experimental.pallas import tpu as pltpu

F32 = jnp.float32
BF16 = jnp.bfloat16
I32 = jnp.int32
HIGHEST = lax.Precision.HIGHEST

D_MODEL = 1024
HEAD_DIM = 64
ATT_HEADS = 8
ATT_KV_HEADS = 2
ATT_GROUP = 4
WINDOW = 128
ATT_Q = ATT_HEADS * HEAD_DIM
ATT_KV = ATT_KV_HEADS * HEAD_DIM
DN_HEADS = 8
DN_WIDTH = DN_HEADS * HEAD_DIM
DN_QKV = 3 * DN_WIDTH
CONV_W = 4
DN_CHUNK = 64
IN_WIDTH = ATT_Q + 2 * ATT_KV + DN_QKV + DN_WIDTH + 2 * DN_HEADS
IN_WIDTH_PADDED = 2944
N_KEYS = 128
PEER_HEADS = 8
PEER_QDIM = 256
PEER_TOPK = 16
PEER_SLOTS = PEER_HEADS * PEER_TOPK
PLE_DIM = 256
EPS = 1e-6
NEG_INF = float("-inf")

VMEM_LIMIT = 48 * 1024 * 1024


def _cparams(*sem):
    return pltpu.CompilerParams(dimension_semantics=sem, vmem_limit_bytes=VMEM_LIMIT)


def _rms(x, g):
    return x * lax.rsqrt(jnp.mean(x * x, axis=-1, keepdims=True) + EPS) * g


def _bdot(a, b):
    return jnp.dot(a.astype(BF16), b.astype(BF16), preferred_element_type=F32)


def _fdot(a, b):
    return jnp.dot(a, b, precision=HIGHEST, preferred_element_type=F32)


def _row_tile(t, target):
    tile = min(t, target)
    assert t % tile == 0, (t, tile)
    return tile


def _inproj_kernel(x_ref, g_ref, w_ref, q_ref, k_ref, v_ref, dqkv_ref, z_ref, ab_ref):
    a = _rms(x_ref[...], g_ref[...])
    p = jnp.dot(a.astype(BF16), w_ref[...], preferred_element_type=F32)
    c0, c1, c2, c3, c4 = ATT_Q, ATT_Q + ATT_KV, ATT_Q + 2 * ATT_KV, ATT_Q + 2 * ATT_KV + DN_QKV, \
        ATT_Q + 2 * ATT_KV + DN_QKV + DN_WIDTH
    q_ref[...] = p[:, :c0]
    k_ref[...] = p[:, c0:c1]
    v_ref[...] = p[:, c1:c2]
    dqkv_ref[...] = p[:, c2:c3]
    z_ref[...] = p[:, c3:c4]
    ab_ref[...] = p[:, c4:c4 + 2 * DN_HEADS]


def _inproj(x, g, w_pad):
    t = x.shape[0]
    tm = _row_tile(t, 512)
    widths = (ATT_Q, ATT_KV, ATT_KV, DN_QKV, DN_WIDTH, 2 * DN_HEADS)
    row = lambda i: (i, 0)
    fixed = lambda i: (0, 0)
    return pl.pallas_call(
        _inproj_kernel, name="inproj", grid=(t // tm,),
        out_shape=[jax.ShapeDtypeStruct((t, w), F32) for w in widths],
        in_specs=[pl.BlockSpec((tm, D_MODEL), row), pl.BlockSpec((1, D_MODEL), fixed),
                  pl.BlockSpec((D_MODEL, IN_WIDTH_PADDED), fixed)],
        out_specs=[pl.BlockSpec((tm, w), row) for w in widths],
        compiler_params=_cparams("parallel"))(x, g, w_pad)


def _softmax_sink_pv(s, sink, vband):
    m = jnp.maximum(jnp.max(s, axis=-1, keepdims=True), sink)
    e = jnp.exp(s - m)
    denom = jnp.sum(e, axis=-1, keepdims=True) + jnp.exp(sink - m)
    return _bdot(e / denom, vband)


def _alibi_slope(h):
    return 2.0 ** (-8.0 * (h + 1) / ATT_HEADS)


def _swa_prompt_kernel(sink_ref, q_ref, kc_ref, kp_ref, vc_ref, vp_ref, o_ref):
    n = pl.program_id(1)
    qi = lax.broadcasted_iota(I32, (WINDOW, 2 * WINDOW), 0)
    kj = lax.broadcasted_iota(I32, (WINDOW, 2 * WINDOW), 1)
    dist = WINDOW + qi - kj
    kpos = n * WINDOW - WINDOW + kj
    valid = (dist >= 0) & (dist <= WINDOW) & (kpos >= 0)
    distf = dist.astype(F32)
    kband = jnp.concatenate([kp_ref[...], kc_ref[...]], axis=0)
    vband = jnp.concatenate([vp_ref[...], vc_ref[...]], axis=0)
    outs = []
    for h in range(ATT_HEADS):
        kh = h // ATT_GROUP
        q = q_ref[:, h * HEAD_DIM:(h + 1) * HEAD_DIM]
        k = kband[:, kh * HEAD_DIM:(kh + 1) * HEAD_DIM]
        v = vband[:, kh * HEAD_DIM:(kh + 1) * HEAD_DIM]
        s = lax.dot_general(q.astype(BF16), k.astype(BF16), (((1,), (1,)), ((), ())),
                            preferred_element_type=F32) * (HEAD_DIM ** -0.5)
        s = jnp.where(valid, s - _alibi_slope(h) * distf, NEG_INF)
        outs.append(_softmax_sink_pv(s, sink_ref[h], v))
    o_ref[...] = jnp.concatenate(outs, axis=-1)


def _swa_prompt(q, k, v, sinks, batch, seq):
    nblk = seq // WINDOW
    cur = lambda b, n: (b * nblk + n, 0)
    prev = lambda b, n: (b * nblk + jnp.maximum(n - 1, 0), 0)
    return pl.pallas_call(
        _swa_prompt_kernel, name="swa_prompt", grid=(batch, nblk),
        out_shape=jax.ShapeDtypeStruct((batch * seq, ATT_Q), F32),
        in_specs=[pl.BlockSpec(memory_space=pltpu.SMEM),
                  pl.BlockSpec((WINDOW, ATT_Q), cur),
                  pl.BlockSpec((WINDOW, ATT_KV), cur), pl.BlockSpec((WINDOW, ATT_KV), prev),
                  pl.BlockSpec((WINDOW, ATT_KV), cur), pl.BlockSpec((WINDOW, ATT_KV), prev)],
        out_specs=pl.BlockSpec((WINDOW, ATT_Q), cur),
        compiler_params=_cparams("parallel", "parallel"))(sinks, q, k, k, v, v)


def _swa_sample_kernel(sink_ref, q_ref, k_ref, v_ref, kb_ref, vb_ref, o_ref, ko_ref, vo_ref, *, bt, seq):
    klen = WINDOW + seq
    qi = lax.broadcasted_iota(I32, (seq, klen), 0)
    kj = lax.broadcasted_iota(I32, (seq, klen), 1)
    dist = qi - (kj - WINDOW)
    valid = (dist >= 0) & (dist <= WINDOW)
    distf = dist.astype(F32)
    for b in range(bt):
        kc = jnp.concatenate([kb_ref[b], k_ref[b]], axis=0)
        vc = jnp.concatenate([vb_ref[b], v_ref[b]], axis=0)
        ko_ref[b] = kc[seq:]
        vo_ref[b] = vc[seq:]
        outs = []
        for h in range(ATT_HEADS):
            kh = h // ATT_GROUP
            q = q_ref[b][:, h * HEAD_DIM:(h + 1) * HEAD_DIM]
            k = kc[:, kh * HEAD_DIM:(kh + 1) * HEAD_DIM]
            v = vc[:, kh * HEAD_DIM:(kh + 1) * HEAD_DIM]
            s = lax.dot_general(q.astype(BF16), k.astype(BF16), (((1,), (1,)), ((), ())),
                                preferred_element_type=F32) * (HEAD_DIM ** -0.5)
            s = jnp.where(valid, s - _alibi_slope(h) * distf, NEG_INF)
            outs.append(_softmax_sink_pv(s, sink_ref[h], v))
        o_ref[b] = jnp.concatenate(outs, axis=-1)


def _swa_sample(q, k, v, k_buf, v_buf, sinks):
    batch, seq = q.shape[0], q.shape[1]
    bt = _row_tile(batch, 8)
    blk = lambda w, r: pl.BlockSpec((bt, r, w), lambda i: (i, 0, 0))
    return pl.pallas_call(
        functools.partial(_swa_sample_kernel, bt=bt, seq=seq), name="swa_sample", grid=(batch // bt,),
        out_shape=[jax.ShapeDtypeStruct((batch, seq, ATT_Q), F32),
                   jax.ShapeDtypeStruct((batch, WINDOW, ATT_KV), F32),
                   jax.ShapeDtypeStruct((batch, WINDOW, ATT_KV), F32)],
        in_specs=[pl.BlockSpec(memory_space=pltpu.SMEM), blk(ATT_Q, seq), blk(ATT_KV, seq), blk(ATT_KV, seq),
                  blk(ATT_KV, WINDOW), blk(ATT_KV, WINDOW)],
        out_specs=[blk(ATT_Q, seq), blk(ATT_KV, WINDOW), blk(ATT_KV, WINDOW)],
        compiler_params=_cparams("parallel"))(sinks, q, k, v, k_buf, v_buf)


def _unit_lower_inverse(m_strict, c):
    ri = lax.broadcasted_iota(I32, (c, c), 0)
    ci = lax.broadcasted_iota(I32, (c, c), 1)
    inv = (ri == ci).astype(F32)
    s = 1
    while s < c:
        off = ((ri // (2 * s)) == (ci // (2 * s))) & ((ri // s) % 2 == 1) & ((ci // s) % 2 == 0)
        o = jnp.where(off, m_strict, 0.0)
        inv = inv - _fdot(_fdot(inv, o), inv)
        s *= 2
    return inv


def _gdn_kernel(alog_ref, dtb_ref, dqkv_ref, z_ref, ab_ref, cw_ref, nrm_ref, pre_ref, s0_ref,
                o_ref, sfin_ref, tail_ref, state_ref, *, c):
    n = pl.program_id(1)

    @pl.when(n == 0)
    def _():
        tail_ref[...] = jnp.concatenate([jnp.zeros((8 - (CONV_W - 1), DN_QKV), F32), pre_ref[0]], axis=0)
        state_ref[...] = s0_ref[0]

    x = dqkv_ref[...]
    hist = jnp.concatenate([tail_ref[...], x], axis=0)
    tail_ref[...] = hist[c:]
    cw = cw_ref[...]
    y = x * cw[CONV_W - 1:CONV_W]
    for d in range(1, CONV_W):
        y = y + hist[8 - d:8 - d + c] * cw[CONV_W - 1 - d:CONV_W - d]
    qkv = y * jax.nn.sigmoid(y)

    ab = ab_ref[...]
    ri = lax.broadcasted_iota(I32, (c, c), 0)
    ci = lax.broadcasted_iota(I32, (c, c), 1)
    lower = ri >= ci
    strict = ri > ci
    tri = lower.astype(F32)
    beta_all = jax.nn.sigmoid(ab[:, DN_HEADS:2 * DN_HEADS])
    g_all = -jnp.exp(alog_ref[...]) * jax.nn.softplus(ab[:, 0:DN_HEADS] + dtb_ref[...])
    gcol_all = _fdot(tri, g_all)
    grow_all = lax.dot_general(g_all, tri, (((0,), (1,)), ((), ())), precision=HIGHEST,
                               preferred_element_type=F32)
    z = z_ref[...]
    outs = []
    for h in range(DN_HEADS):
        sl = slice(h * HEAD_DIM, (h + 1) * HEAD_DIM)
        q = qkv[:, sl]
        k = qkv[:, DN_WIDTH + h * HEAD_DIM:DN_WIDTH + (h + 1) * HEAD_DIM]
        v = qkv[:, 2 * DN_WIDTH + h * HEAD_DIM:2 * DN_WIDTH + (h + 1) * HEAD_DIM]
        q = q * lax.rsqrt(jnp.sum(q * q, axis=-1, keepdims=True) + EPS) * (HEAD_DIM ** -0.5)
        k = k * lax.rsqrt(jnp.sum(k * k, axis=-1, keepdims=True) + EPS)
        beta = beta_all[:, h:h + 1]
        gcol = gcol_all[:, h:h + 1]
        grow = grow_all[h:h + 1, :]
        decay = jnp.exp(jnp.where(lower, gcol - grow, NEG_INF))
        kb = k * beta
        kk = lax.dot_general(kb, k, (((1,), (1,)), ((), ())), precision=HIGHEST, preferred_element_type=F32)
        minv = _unit_lower_inverse(jnp.where(strict, kk * decay, 0.0), c)
        eg = jnp.exp(gcol)
        u = _fdot(minv, v * beta)
        w = _fdot(minv, kb * eg)
        qk = lax.dot_general(q, k, (((1,), (1,)), ((), ())), precision=HIGHEST,
                             preferred_element_type=F32) * decay
        glast = gcol[c - 1:c, :]
        ktail = k * jnp.exp(glast - gcol)
        qhead = q * eg
        st = state_ref[h]
        vnew = u - _fdot(w, st)
        o = _fdot(qhead, st) + _fdot(qk, vnew)
        state_ref[h] = st * jnp.exp(glast) + lax.dot_general(
            ktail, vnew, (((0,), (0,)), ((), ())), precision=HIGHEST, preferred_element_type=F32)
        o = _rms(o, nrm_ref[...])
        zh = z[:, sl]
        outs.append(o * (zh * jax.nn.sigmoid(zh)))
    o_ref[...] = jnp.concatenate(outs, axis=-1)

    @pl.when(n == pl.num_programs(1) - 1)
    def _():
        sfin_ref[0] = state_ref[...]


def _gdn(dqkv, z, ab, conv_prefix, s0, conv_w, a_log, dt_bias, dn_norm, batch, seq):
    c = min(DN_CHUNK, seq)
    nchunk = seq // c
    assert seq % c == 0 and c % 8 == 0
    tok = lambda b, n: (b * nchunk + n, 0)
    fixed = lambda b, n: (0, 0)
    return pl.pallas_call(
        functools.partial(_gdn_kernel, c=c), name="gdn", grid=(batch, nchunk),
        out_shape=[jax.ShapeDtypeStruct((batch * seq, DN_WIDTH), F32),
                   jax.ShapeDtypeStruct((batch, DN_HEADS, HEAD_DIM, HEAD_DIM), F32)],
        in_specs=[pl.BlockSpec((1, DN_HEADS), fixed), pl.BlockSpec((1, DN_HEADS), fixed),
                  pl.BlockSpec((c, DN_QKV), tok), pl.BlockSpec((c, DN_WIDTH), tok),
                  pl.BlockSpec((c, 2 * DN_HEADS), tok),
                  pl.BlockSpec((CONV_W, DN_QKV), fixed), pl.BlockSpec((1, HEAD_DIM), fixed),
                  pl.BlockSpec((1, CONV_W - 1, DN_QKV), lambda b, n: (b, 0, 0)),
                  pl.BlockSpec((1, DN_HEADS, HEAD_DIM, HEAD_DIM), lambda b, n: (b, 0, 0, 0))],
        out_specs=[pl.BlockSpec((c, DN_WIDTH), tok),
                   pl.BlockSpec((1, DN_HEADS, HEAD_DIM, HEAD_DIM), lambda b, n: (b, 0, 0, 0))],
        scratch_shapes=[pltpu.VMEM((8, DN_QKV), F32), pltpu.VMEM((DN_HEADS, HEAD_DIM, HEAD_DIM), F32)],
        compiler_params=_cparams("parallel", "arbitrary"))(
            a_log.reshape(1, DN_HEADS), dt_bias.reshape(1, DN_HEADS), dqkv, z, ab, conv_w,
            dn_norm.reshape(1, HEAD_DIM), conv_prefix, s0)


def _outproj_kernel(x_ref, att_ref, dn_ref, wo_ref, g_ref, wq_ref, h_ref, c_ref, q_ref):
    mix = jnp.concatenate([att_ref[...], dn_ref[...]], axis=-1)
    h = x_ref[...] + jnp.dot(mix.astype(BF16), wo_ref[...], preferred_element_type=F32)
    h_ref[...] = h
    cn = _rms(h, g_ref[...])
    c_ref[...] = cn
    q_ref[...] = jnp.dot(cn.astype(BF16), wq_ref[...], preferred_element_type=F32)


def _outproj(x, att, dn, w_out, g_ffn, wq):
    t = x.shape[0]
    tm = _row_tile(t, 512)
    row = lambda i: (i, 0)
    fixed = lambda i: (0, 0)
    qw = PEER_HEADS * PEER_QDIM
    return pl.pallas_call(
        _outproj_kernel, name="outproj", grid=(t // tm,),
        out_shape=[jax.ShapeDtypeStruct((t, D_MODEL), F32), jax.ShapeDtypeStruct((t, D_MODEL), F32),
                   jax.ShapeDtypeStruct((t, qw), F32)],
        in_specs=[pl.BlockSpec((tm, D_MODEL), row), pl.BlockSpec((tm, ATT_Q), row),
                  pl.BlockSpec((tm, DN_WIDTH), row), pl.BlockSpec((D_MODEL, D_MODEL), fixed),
                  pl.BlockSpec((1, D_MODEL), fixed), pl.BlockSpec((D_MODEL, qw), fixed)],
        out_specs=[pl.BlockSpec((tm, D_MODEL), row), pl.BlockSpec((tm, D_MODEL), row),
                   pl.BlockSpec((tm, qw), row)],
        compiler_params=_cparams("parallel"))(x, att, dn, w_out, g_ffn, wq)


def _top16_rows(s, iota):
    big = jnp.int32(1 << 30)
    vals, idxs = [], []
    for _ in range(PEER_TOPK):
        m = jnp.max(s, axis=0, keepdims=True)
        i = jnp.min(jnp.where(s == m, iota, big), axis=0, keepdims=True)
        vals.append(m)
        idxs.append(i)
        s = jnp.where(iota == i, NEG_INF, s)
    return jnp.concatenate(vals, axis=0), jnp.concatenate(idxs, axis=0)


def _candidates(a, b, row8):
    t = a.shape[1]
    lo = row8 < 4
    bc = lambda r, n: jnp.broadcast_to(a[r:r + 1], (n, t))
    b8 = b[0:8]
    b44 = jnp.where(lo, b8, pltpu.roll(b8, 4, 0))
    parts = [bc(0, 8), bc(0, 8), bc(1, 8), bc(2, 8), bc(3, 8),
             jnp.where(lo, bc(4, 8), bc(5, 8)), jnp.where(lo, bc(6, 8), bc(7, 8)), a[8:16]]
    others = [b[0:8], b[8:16], b8, b8, b8, b44, b44, jnp.broadcast_to(b[0:1], (8, t))]
    return parts, others


def _flat_index_column():
    r = lax.broadcasted_iota(I32, (8, 1), 0)
    lo = r < 4
    cols = [r, 8 + r, 16 + r, 32 + r, 48 + r, jnp.where(lo, 64 + r, 80 + r - 4),
            jnp.where(lo, 96 + r, 112 + r - 4), (8 + r) * 16]
    return jnp.concatenate(cols, axis=0)


def _route_kernel(q_ref, k1_ref, k2_ref, idx_ref, gate_ref, *, tb):
    iota = lax.broadcasted_iota(I32, (N_KEYS, tb), 0)
    row8 = lax.broadcasted_iota(I32, (8, tb), 0)
    flat = _flat_index_column()
    big = jnp.int32(1 << 30)
    half = PEER_QDIM // 2
    for h in range(PEER_HEADS):
        q1 = q_ref[:, h * PEER_QDIM:h * PEER_QDIM + half]
        q2 = q_ref[:, h * PEER_QDIM + half:(h + 1) * PEER_QDIM]
        nt = (((1,), (1,)), ((), ()))
        s1 = lax.dot_general(k1_ref[h], q1.astype(BF16), nt, preferred_element_type=F32)
        s2 = lax.dot_general(k2_ref[h], q2.astype(BF16), nt, preferred_element_type=F32)
        v1, i1 = _top16_rows(s1, iota)
        v2, i2 = _top16_rows(s2, iota)
        pa, pb = _candidates(v1, v2, row8)
        cand = jnp.concatenate([x + y for x, y in zip(pa, pb)], axis=0)
        ea, eb = _candidates(i1, i2, row8)
        expert = jnp.concatenate([x * N_KEYS + y for x, y in zip(ea, eb)], axis=0)
        scs, exs = [], []
        for _ in range(PEER_TOPK):
            m = jnp.max(cand, axis=0, keepdims=True)
            c = jnp.min(jnp.where(cand == m, flat, big), axis=0, keepdims=True)
            hit = flat == c
            exs.append(jnp.max(jnp.where(hit, expert, -1), axis=0, keepdims=True))
            scs.append(m)
            cand = jnp.where(hit, NEG_INF, cand)
        sc = jnp.concatenate(scs, axis=0)
        e = jnp.exp(sc - sc[0:1])
        gate_ref[h * PEER_TOPK:(h + 1) * PEER_TOPK, :] = e / jnp.sum(e, axis=0, keepdims=True)
        idx_ref[h * PEER_TOPK:(h + 1) * PEER_TOPK, :] = jnp.concatenate(exs, axis=0)


def _route(q, keys1, keys2):
    t = q.shape[0]
    tb = _row_tile(t, 256)
    kspec = pl.BlockSpec((PEER_HEADS, N_KEYS, PEER_QDIM // 2), lambda i: (0, 0, 0))
    return pl.pallas_call(
        functools.partial(_route_kernel, tb=tb), name="peer_route", grid=(t // tb,),
        out_shape=[jax.ShapeDtypeStruct((PEER_SLOTS, t), I32), jax.ShapeDtypeStruct((PEER_SLOTS, t), F32)],
        in_specs=[pl.BlockSpec((tb, PEER_HEADS * PEER_QDIM), lambda i: (i, 0)), kspec, kspec],
        out_specs=[pl.BlockSpec((PEER_SLOTS, tb), lambda i: (0, i)),
                   pl.BlockSpec((PEER_SLOTS, tb), lambda i: (0, i))],
        compiler_params=_cparams("parallel"))(q, keys1, keys2)


GATHER_SLOTS = 3


def _peer_apply_kernel(idx_ref, c_ref, h_ref, gt_ref, uv_ref, o_ref, buf, sem, *, tb):
    def row_copy(t, k, slot):
        e = idx_ref[t, k]
        return pltpu.make_async_copy(uv_ref.at[pl.ds(e, 1), :], buf.at[slot, pl.ds(k, 1), :], sem.at[slot])

    def issue(t, slot):
        def body(k, carry):
            row_copy(t, k, slot).start()
            return carry
        lax.fori_loop(0, PEER_SLOTS, body, 0, unroll=8)

    def wait(slot):
        pltpu.make_async_copy(uv_ref.at[pl.ds(0, PEER_SLOTS), :], buf.at[slot], sem.at[slot]).wait()

    for t0 in range(GATHER_SLOTS - 1):
        issue(t0, t0)
    lane = lax.broadcasted_iota(I32, (PEER_SLOTS, tb), 1)

    def token(t, carry):
        slot = t % GATHER_SLOTS
        nxt = t + GATHER_SLOTS - 1

        @pl.when(nxt < tb)
        def _():
            issue(nxt, nxt % GATHER_SLOTS)

        wait(slot)
        x = c_ref[pl.ds(t, 1), :]
        u = buf[slot, :, 0:D_MODEL]
        p = u * x
        acc = p[:, 0:128]
        for j in range(1, D_MODEL // 128):
            acc = acc + p[:, j * 128:(j + 1) * 128]
        s = jnp.sum(acc, axis=-1, keepdims=True)
        act = 0.5 * s * (1.0 + lax.erf(s * (1.0 / math.sqrt(2.0))))
        gate = jnp.sum(jnp.where(lane == t, gt_ref[...], 0.0), axis=-1, keepdims=True)
        w = gate * act
        y = jnp.sum(w * buf[slot, :, D_MODEL:2 * D_MODEL], axis=0, keepdims=True)
        o_ref[pl.ds(t, 1), :] = h_ref[pl.ds(t, 1), :] + y
        return carry

    lax.fori_loop(0, tb, token, 0)


def _peer_apply(idx, c, h, gate_t, uv):
    t = c.shape[0]
    tb = _row_tile(t, 128)
    row = lambda i: (i, 0)
    return pl.pallas_call(
        functools.partial(_peer_apply_kernel, tb=tb), name="peer_apply", grid=(t // tb,),
        out_shape=jax.ShapeDtypeStruct((t, D_MODEL), F32),
        in_specs=[pl.BlockSpec((tb, PEER_SLOTS), row, memory_space=pltpu.SMEM),
                  pl.BlockSpec((tb, D_MODEL), row), pl.BlockSpec((tb, D_MODEL), row),
                  pl.BlockSpec((PEER_SLOTS, tb), lambda i: (0, i)),
                  pl.BlockSpec(memory_space=pl.ANY)],
        out_specs=pl.BlockSpec((tb, D_MODEL), row),
        scratch_shapes=[pltpu.VMEM((GATHER_SLOTS, PEER_SLOTS, 2 * D_MODEL), F32),
                        pltpu.SemaphoreType.DMA((GATHER_SLOTS,))],
        compiler_params=_cparams("arbitrary"))(idx, c, h, gate_t, uv)


def _final_kernel(h_ref, p_ref, gple_ref, wg_ref, wp_ref, gfin_ref, y_ref):
    h = h_ref[...]
    e = _rms(h, gple_ref[...])
    gate = jax.nn.sigmoid(jnp.dot(e.astype(BF16), wg_ref[...], preferred_element_type=F32))
    up = jnp.dot(p_ref[...].astype(BF16), wp_ref[...], preferred_element_type=F32)
    y_ref[...] = _rms(h + up * gate, gfin_ref[...])


def _final(h, p, g_ple, w_gate, w_ple, g_final):
    t = h.shape[0]
    tm = _row_tile(t, 512)
    row = lambda i: (i, 0)
    fixed = lambda i: (0, 0)
    return pl.pallas_call(
        _final_kernel, name="ple_final", grid=(t // tm,),
        out_shape=jax.ShapeDtypeStruct((t, D_MODEL), F32),
        in_specs=[pl.BlockSpec((tm, D_MODEL), row), pl.BlockSpec((tm, PLE_DIM), row),
                  pl.BlockSpec((1, D_MODEL), fixed), pl.BlockSpec((D_MODEL, D_MODEL), fixed),
                  pl.BlockSpec((PLE_DIM, D_MODEL), fixed), pl.BlockSpec((1, D_MODEL), fixed)],
        out_specs=pl.BlockSpec((tm, D_MODEL), row),
        compiler_params=_cparams("parallel"))(h, p, g_ple, w_gate, w_ple, g_final)


def _layer(x, p, kv_buf, conv_prefix, s0, wts, batch, seq):
    t = batch * seq
    q, k, v, dqkv, z, ab = _inproj(x, wts["norm_mix"], wts["w_in"])
    if kv_buf is None:
        att = _swa_prompt(q, k, v, wts["sinks"], batch, seq)
        k_new = k.reshape(batch, seq, ATT_KV)[:, -WINDOW:]
        v_new = v.reshape(batch, seq, ATT_KV)[:, -WINDOW:]
    else:
        att, k_new, v_new = _swa_sample(q.reshape(batch, seq, ATT_Q), k.reshape(batch, seq, ATT_KV),
                                        v.reshape(batch, seq, ATT_KV), kv_buf[0], kv_buf[1], wts["sinks"])
        att = att.reshape(t, ATT_Q)
    dn, s_new = _gdn(dqkv, z, ab, conv_prefix, s0, wts["conv_w"], wts["a_log"], wts["dt_bias"],
                     wts["dn_norm"], batch, seq)
    conv_new = dqkv.reshape(batch, seq, DN_QKV)[:, -(CONV_W - 1):]
    h, c, pq = _outproj(x, att, dn, wts["w_out"], wts["norm_ffn"], wts["peer_wq"])
    idx_t, gate_t = _route(pq, wts["keys1"], wts["keys2"])
    h2 = _peer_apply(idx_t.T, c, h, gate_t, wts["uv"])
    y = _final(h2, p, wts["norm_ple"], wts["ple_gate"], wts["ple_in"], wts["norm_final"])
    return y, k_new, v_new, conv_new, s_new


def kernel(x_prompt, x_sample, p_prompt, p_sample, cache_swa_k, cache_swa_v, state_conv, state_delta, norm_mix, w_in, conv_w, attn_sinks, dn_a_log, dn_dt_bias, dn_norm, w_out, norm_ffn, peer_wq, peer_keys1, peer_keys2, peer_u, peer_v, norm_ple, ple_in, ple_gate, norm_final):
    depth = w_in.shape[0]
    assert depth == 1
    bp, lp = x_prompt.shape[0], x_prompt.shape[1]
    bs, ls = x_sample.shape[0], x_sample.shape[1]
    l = 0
    wts = dict(
        norm_mix=norm_mix[l].reshape(1, D_MODEL),
        w_in=jnp.pad(w_in[l], ((0, 0), (0, IN_WIDTH_PADDED - IN_WIDTH))).astype(BF16),
        conv_w=conv_w[l], sinks=attn_sinks[l], a_log=dn_a_log[l], dt_bias=dn_dt_bias[l], dn_norm=dn_norm[l],
        w_out=w_out[l].astype(BF16), norm_ffn=norm_ffn[l].reshape(1, D_MODEL), peer_wq=peer_wq[l].astype(BF16),
        keys1=peer_keys1[l].astype(BF16), keys2=peer_keys2[l].astype(BF16),
        uv=jnp.concatenate([peer_u[l], peer_v[l]], axis=-1),
        norm_ple=norm_ple[l].reshape(1, D_MODEL), ple_in=ple_in[l].astype(BF16),
        ple_gate=ple_gate[l].astype(BF16), norm_final=norm_final.reshape(1, D_MODEL))
    conv0 = jnp.zeros((bp, CONV_W - 1, DN_QKV), F32)
    s0 = jnp.zeros((bp, DN_HEADS, HEAD_DIM, HEAD_DIM), F32)
    yp, kp, vp, cp, sp = _layer(x_prompt.reshape(bp * lp, D_MODEL), p_prompt[l].reshape(bp * lp, PLE_DIM),
                                None, conv0, s0, wts, bp, lp)
    kv_buf = (cache_swa_k[l].reshape(bs, WINDOW, ATT_KV), cache_swa_v[l].reshape(bs, WINDOW, ATT_KV))
    ys, kn, vn, cn, sn = _layer(x_sample.reshape(bs * ls, D_MODEL), p_sample[l].reshape(bs * ls, PLE_DIM),
                                kv_buf, state_conv[l], state_delta[l], wts, bs, ls)
    kvshape = lambda b: (1, b, WINDOW, ATT_KV_HEADS, HEAD_DIM)
    return (yp.reshape(bp, lp, D_MODEL), ys.reshape(bs, ls, D_MODEL),
            kp.reshape(kvshape(bp)), vp.reshape(kvshape(bp)), cp[None], sp[None],
            kn.reshape(kvshape(bs)), vn.reshape(kvshape(bs)), cn[None], sn[None])
```

```python
import functools
import math

import jax
import jax.numpy as jnp
from jax import lax
from jax.experimental import pallas as pl
from jax.experimental.pallas import tpu as pltpu

F32 = jnp.float32
BF16 = jnp.bfloat16
I32 = jnp.int32
HIGHEST = lax.Precision.HIGHEST

D_MODEL = 1024
HEAD_DIM = 64
ATT_HEADS = 8
ATT_KV_HEADS = 2
ATT_GROUP = 4
WINDOW = 128
ATT_Q = ATT_HEADS * HEAD_DIM
ATT_KV = ATT_KV_HEADS * HEAD_DIM
DN_HEADS = 8
DN_WIDTH = DN_HEADS * HEAD_DIM
DN_QKV = 3 * DN_WIDTH
CONV_W = 4
DN_CHUNK = 64
IN_WIDTH = ATT_Q + 2 * ATT_KV + DN_QKV + DN_WIDTH + 2 * DN_HEADS
IN_WIDTH_PADDED = 2944
N_KEYS = 128
PEER_HEADS = 8
PEER_QDIM = 256
PEER_TOPK = 16
PEER_SLOTS = PEER_HEADS * PEER_TOPK
PLE_DIM = 256
EPS = 1e-6
NEG_INF = float("-inf")

VMEM_LIMIT = 48 * 1024 * 1024


def _cparams(*sem):
    return pltpu.CompilerParams(dimension_semantics=sem, vmem_limit_bytes=VMEM_LIMIT)


def _rms(x, g):
    return x * lax.rsqrt(jnp.mean(x * x, axis=-1, keepdims=True) + EPS) * g


def _bdot(a, b):
    return jnp.dot(a.astype(BF16), b.astype(BF16), preferred_element_type=F32)


def _fdot(a, b):
    return jnp.dot(a, b, precision=HIGHEST, preferred_element_type=F32)


def _row_tile(t, target):
    tile = min(t, target)
    assert t % tile == 0, (t, tile)
    return tile


def _inproj_kernel(x_ref, g_ref, w_ref, q_ref, k_ref, v_ref, dqkv_ref, z_ref, ab_ref):
    a = _rms(x_ref[...], g_ref[...])
    p = jnp.dot(a.astype(BF16), w_ref[...], preferred_element_type=F32)
    c0, c1, c2, c3, c4 = ATT_Q, ATT_Q + ATT_KV, ATT_Q + 2 * ATT_KV, ATT_Q + 2 * ATT_KV + DN_QKV, \
        ATT_Q + 2 * ATT_KV + DN_QKV + DN_WIDTH
    q_ref[...] = p[:, :c0]
    k_ref[...] = p[:, c0:c1]
    v_ref[...] = p[:, c1:c2]
    dqkv_ref[...] = p[:, c2:c3]
    z_ref[...] = p[:, c3:c4]
    ab_ref[...] = p[:, c4:c4 + 2 * DN_HEADS]


def _inproj(x, g, w_pad):
    t = x.shape[0]
    tm = _row_tile(t, 512)
    widths = (ATT_Q, ATT_KV, ATT_KV, DN_QKV, DN_WIDTH, 2 * DN_HEADS)
    row = lambda i: (i, 0)
    fixed = lambda i: (0, 0)
    return pl.pallas_call(
        _inproj_kernel, name="inproj", grid=(t // tm,),
        out_shape=[jax.ShapeDtypeStruct((t, w), F32) for w in widths],
        in_specs=[pl.BlockSpec((tm, D_MODEL), row), pl.BlockSpec((1, D_MODEL), fixed),
                  pl.BlockSpec((D_MODEL, IN_WIDTH_PADDED), fixed)],
        out_specs=[pl.BlockSpec((tm, w), row) for w in widths],
        compiler_params=_cparams("parallel"))(x, g, w_pad)


def _softmax_sink_pv(s, sink, vband):
    m = jnp.maximum(jnp.max(s, axis=-1, keepdims=True), sink)
    e = jnp.exp(s - m)
    denom = jnp.sum(e, axis=-1, keepdims=True) + jnp.exp(sink - m)
    return _bdot(e / denom, vband)


def _alibi_slope(h):
    return 2.0 ** (-8.0 * (h + 1) / ATT_HEADS)


def _swa_prompt_kernel(sink_ref, q_ref, kc_ref, kp_ref, vc_ref, vp_ref, o_ref):
    n = pl.program_id(1)
    qi = lax.broadcasted_iota(I32, (WINDOW, 2 * WINDOW), 0)
    kj = lax.broadcasted_iota(I32, (WINDOW, 2 * WINDOW), 1)
    dist = WINDOW + qi - kj
    kpos = n * WINDOW - WINDOW + kj
    valid = (dist >= 0) & (dist <= WINDOW) & (kpos >= 0)
    distf = dist.astype(F32)
    kband = jnp.concatenate([kp_ref[...], kc_ref[...]], axis=0)
    vband = jnp.concatenate([vp_ref[...], vc_ref[...]], axis=0)
    outs = []
    for h in range(ATT_HEADS):
        kh = h // ATT_GROUP
        q = q_ref[:, h * HEAD_DIM:(h + 1) * HEAD_DIM]
        k = kband[:, kh * HEAD_DIM:(kh + 1) * HEAD_DIM]
        v = vband[:, kh * HEAD_DIM:(kh + 1) * HEAD_DIM]
        s = lax.dot_general(q.astype(BF16), k.astype(BF16), (((1,), (1,)), ((), ())),
                            preferred_element_type=F32) * (HEAD_DIM ** -0.5)
        s = jnp.where(valid, s - _alibi_slope(h) * distf, NEG_INF)
        outs.append(_softmax_sink_pv(s, sink_ref[h], v))
    o_ref[...] = jnp.concatenate(outs, axis=-1)


def _swa_prompt(q, k, v, sinks, batch, seq):
    nblk = seq // WINDOW
    cur = lambda b, n: (b * nblk + n, 0)
    prev = lambda b, n: (b * nblk + jnp.maximum(n - 1, 0), 0)
    return pl.pallas_call(
        _swa_prompt_kernel, name="swa_prompt", grid=(batch, nblk),
        out_shape=jax.ShapeDtypeStruct((batch * seq, ATT_Q), F32),
        in_specs=[pl.BlockSpec(memory_space=pltpu.SMEM),
                  pl.BlockSpec((WINDOW, ATT_Q), cur),
                  pl.BlockSpec((WINDOW, ATT_KV), cur), pl.BlockSpec((WINDOW, ATT_KV), prev),
                  pl.BlockSpec((WINDOW, ATT_KV), cur), pl.BlockSpec((WINDOW, ATT_KV), prev)],
        out_specs=pl.BlockSpec((WINDOW, ATT_Q), cur),
        compiler_params=_cparams("parallel", "parallel"))(sinks, q, k, k, v, v)


def _swa_sample_kernel(sink_ref, q_ref, k_ref, v_ref, kb_ref, vb_ref, o_ref, ko_ref, vo_ref, *, bt, seq):
    klen = WINDOW + seq
    qi = lax.broadcasted_iota(I32, (seq, klen), 0)
    kj = lax.broadcasted_iota(I32, (seq, klen), 1)
    dist = qi - (kj - WINDOW)
    valid = (dist >= 0) & (dist <= WINDOW)
    distf = dist.astype(F32)
    for b in range(bt):
        kc = jnp.concatenate([kb_ref[b], k_ref[b]], axis=0)
        vc = jnp.concatenate([vb_ref[b], v_ref[b]], axis=0)
        ko_ref[b] = kc[seq:]
        vo_ref[b] = vc[seq:]
        outs = []
        for h in range(ATT_HEADS):
            kh = h // ATT_GROUP
            q = q_ref[b][:, h * HEAD_DIM:(h + 1) * HEAD_DIM]
            k = kc[:, kh * HEAD_DIM:(kh + 1) * HEAD_DIM]
            v = vc[:, kh * HEAD_DIM:(kh + 1) * HEAD_DIM]
            s = lax.dot_general(q.astype(BF16), k.astype(BF16), (((1,), (1,)), ((), ())),
                                preferred_element_type=F32) * (HEAD_DIM ** -0.5)
            s = jnp.where(valid, s - _alibi_slope(h) * distf, NEG_INF)
            outs.append(_softmax_sink_pv(s, sink_ref[h], v))
        o_ref[b] = jnp.concatenate(outs, axis=-1)


def _swa_sample(q, k, v, k_buf, v_buf, sinks):
    batch, seq = q.shape[0], q.shape[1]
    bt = _row_tile(batch, 8)
    blk = lambda w, r: pl.BlockSpec((bt, r, w), lambda i: (i, 0, 0))
    return pl.pallas_call(
        functools.partial(_swa_sample_kernel, bt=bt, seq=seq), name="swa_sample", grid=(batch // bt,),
        out_shape=[jax.ShapeDtypeStruct((batch, seq, ATT_Q), F32),
                   jax.ShapeDtypeStruct((batch, WINDOW, ATT_KV), F32),
                   jax.ShapeDtypeStruct((batch, WINDOW, ATT_KV), F32)],
        in_specs=[pl.BlockSpec(memory_space=pltpu.SMEM), blk(ATT_Q, seq), blk(ATT_KV, seq), blk(ATT_KV, seq),
                  blk(ATT_KV, WINDOW), blk(ATT_KV, WINDOW)],
        out_specs=[blk(ATT_Q, seq), blk(ATT_KV, WINDOW), blk(ATT_KV, WINDOW)],
        compiler_params=_cparams("parallel"))(sinks, q, k, v, k_buf, v_buf)


_NN = (((1,), (0,)), ((), ()))
_NT = (((1,), (1,)), ((), ()))
_TN = (((0,), (0,)), ((), ()))


def _split_bf16(a):
    hi = a.astype(BF16)
    return hi, (a - hi.astype(F32)).astype(BF16)


def _dot3(a, b, dims=_NN):
    ah, al = _split_bf16(a)
    bh, bl = _split_bf16(b)
    f = lambda x, y: lax.dot_general(x, y, dims, preferred_element_type=F32)
    return f(ah, bh) + (f(ah, bl) + f(al, bh))


def _gdn_kernel(alog_ref, dtb_ref, dqkv_ref, z_ref, ab_ref, cw_ref, nrm_ref, pre_ref, s0_ref,
                o_ref, sfin_ref, tail_ref, state_ref, *, c, g):
    n = pl.program_id(1)
    ngrp = DN_HEADS // g
    r = g * c
    hd = HEAD_DIM

    @pl.when(n == 0)
    def _():
        tail_ref[...] = jnp.concatenate([jnp.zeros((8 - (CONV_W - 1), DN_QKV), F32), pre_ref[0]], axis=0)
        for gi in range(ngrp):
            state_ref[gi] = jnp.concatenate([s0_ref[0, gi * g + j] for j in range(g)], axis=1)

    x = dqkv_ref[...]
    hist = jnp.concatenate([tail_ref[...], x], axis=0)
    tail_ref[...] = hist[c:]
    cw = cw_ref[...]
    y = x * cw[CONV_W - 1:CONV_W]
    for d in range(1, CONV_W):
        y = y + hist[8 - d:8 - d + c] * cw[CONV_W - 1 - d:CONV_W - d]
    qkv = y * jax.nn.sigmoid(y)
    z = z_ref[...]
    ab = ab_ref[...]
    beta_all = jax.nn.sigmoid(ab[:, DN_HEADS:2 * DN_HEADS])
    g_all = -jnp.exp(alog_ref[...]) * jax.nn.softplus(ab[:, 0:DN_HEADS] + dtb_ref[...])

    ri = lax.broadcasted_iota(I32, (r, r), 0)
    ci = lax.broadcasted_iota(I32, (r, r), 1)
    same = (ri // c) == (ci // c)
    lower = same & (ri >= ci)
    strict = same & (ri > ci)
    tri = lower.astype(F32)
    blk = same.astype(F32)
    eye = (ri == ci).astype(F32)
    rowhead = lax.broadcasted_iota(I32, (r, hd), 0) // c

    def stack(a, gi, base):
        return jnp.concatenate([a[:, base + (gi * g + j) * hd:base + (gi * g + j + 1) * hd]
                                for j in range(g)], axis=0)

    def column(a, gi):
        return jnp.concatenate([a[:, gi * g + j:gi * g + j + 1] for j in range(g)], axis=0)

    def own_block(m):
        acc = jnp.where(rowhead == 0, m[:, 0:hd], 0.0)
        for j in range(1, g):
            acc = acc + jnp.where(rowhead == j, m[:, j * hd:(j + 1) * hd], 0.0)
        return acc

    grp = []
    for gi in range(ngrp):
        q = stack(qkv, gi, 0)
        k = stack(qkv, gi, DN_WIDTH)
        v = stack(qkv, gi, 2 * DN_WIDTH)
        q = q * lax.rsqrt(jnp.sum(q * q, axis=-1, keepdims=True) + EPS) * (hd ** -0.5)
        k = k * lax.rsqrt(jnp.sum(k * k, axis=-1, keepdims=True) + EPS)
        beta = column(beta_all, gi)
        gr = column(g_all, gi)
        gcol = _fdot(tri, gr)
        grow = lax.dot_general(gr, tri, (((0,), (1,)), ((), ())), precision=HIGHEST,
                               preferred_element_type=F32)
        glast = _fdot(blk, gr)
        decay = jnp.exp(jnp.where(lower, gcol - grow, NEG_INF))
        kb = k * beta
        lm = jnp.where(strict, _dot3(kb, k, _NT) * decay, 0.0)
        grp.append(dict(q=q, k=k, v=v, beta=beta, gcol=gcol, glast=glast, decay=decay, kb=kb, lm=lm))

    invs = [eye] * ngrp
    s = 1
    while s < c:
        off = ((ri // (2 * s)) == (ci // (2 * s))) & ((ri // s) % 2 == 1) & ((ci // s) % 2 == 0)
        tmp = [_dot3(invs[gi], jnp.where(off, grp[gi]["lm"], 0.0)) for gi in range(ngrp)]
        invs = [invs[gi] - _dot3(tmp[gi], invs[gi]) for gi in range(ngrp)]
        s *= 2

    outs = [None] * DN_HEADS
    for gi in range(ngrp):
        d = grp[gi]
        eg = jnp.exp(d["gcol"])
        sol = _dot3(invs[gi], jnp.concatenate([d["v"] * d["beta"], d["kb"] * eg], axis=1))
        u = sol[:, :hd]
        w = sol[:, hd:]
        qk = _dot3(d["q"], d["k"], _NT) * d["decay"]
        ktail = d["k"] * jnp.exp(d["glast"] - d["gcol"])
        qhead = d["q"] * eg
        st = state_ref[gi]
        vnew = u - own_block(_dot3(w, st))
        o = own_block(_dot3(qhead, st)) + _dot3(qk, vnew)
        vblk = jnp.concatenate([jnp.where(rowhead == j, vnew, 0.0) for j in range(g)], axis=1)
        elast = jnp.exp(d["glast"])
        keep = jnp.concatenate([jnp.broadcast_to(elast[j * c:j * c + 1, :], (1, hd)) for j in range(g)], axis=1)
        state_ref[gi] = st * keep + _dot3(ktail, vblk, _TN)
        zs = stack(z, gi, 0)
        o = _rms(o, nrm_ref[...]) * (zs * jax.nn.sigmoid(zs))
        for j in range(g):
            outs[gi * g + j] = o[j * c:(j + 1) * c]
    o_ref[...] = jnp.concatenate(outs, axis=-1)

    @pl.when(n == pl.num_programs(1) - 1)
    def _():
        for gi in range(ngrp):
            for j in range(g):
                sfin_ref[0, gi * g + j] = state_ref[gi][:, j * hd:(j + 1) * hd]


def _gdn(dqkv, z, ab, conv_prefix, s0, conv_w, a_log, dt_bias, dn_norm, batch, seq):
    c = min(DN_CHUNK, seq)
    nchunk = seq // c
    assert seq % c == 0 and c % 8 == 0
    g = min(DN_HEADS, max(1, 128 // c))
    tok = lambda b, n: (b * nchunk + n, 0)
    fixed = lambda b, n: (0, 0)
    return pl.pallas_call(
        functools.partial(_gdn_kernel, c=c, g=g), name="gdn", grid=(batch, nchunk),
        out_shape=[jax.ShapeDtypeStruct((batch * seq, DN_WIDTH), F32),
                   jax.ShapeDtypeStruct((batch, DN_HEADS, HEAD_DIM, HEAD_DIM), F32)],
        in_specs=[pl.BlockSpec((1, DN_HEADS), fixed), pl.BlockSpec((1, DN_HEADS), fixed),
                  pl.BlockSpec((c, DN_QKV), tok), pl.BlockSpec((c, DN_WIDTH), tok),
                  pl.BlockSpec((c, 2 * DN_HEADS), tok),
                  pl.BlockSpec((CONV_W, DN_QKV), fixed), pl.BlockSpec((1, HEAD_DIM), fixed),
                  pl.BlockSpec((1, CONV_W - 1, DN_QKV), lambda b, n: (b, 0, 0)),
                  pl.BlockSpec((1, DN_HEADS, HEAD_DIM, HEAD_DIM), lambda b, n: (b, 0, 0, 0))],
        out_specs=[pl.BlockSpec((c, DN_WIDTH), tok),
                   pl.BlockSpec((1, DN_HEADS, HEAD_DIM, HEAD_DIM), lambda b, n: (b, 0, 0, 0))],
        scratch_shapes=[pltpu.VMEM((8, DN_QKV), F32),
                        pltpu.VMEM((DN_HEADS // g, HEAD_DIM, g * HEAD_DIM), F32)],
        compiler_params=_cparams("parallel", "arbitrary"))(
            a_log.reshape(1, DN_HEADS), dt_bias.reshape(1, DN_HEADS), dqkv, z, ab, conv_w,
            dn_norm.reshape(1, HEAD_DIM), conv_prefix, s0)


def _outproj_kernel(x_ref, att_ref, dn_ref, wo_ref, g_ref, wq_ref, h_ref, c_ref, q_ref):
    mix = jnp.concatenate([att_ref[...], dn_ref[...]], axis=-1)
    h = x_ref[...] + jnp.dot(mix.astype(BF16), wo_ref[...], preferred_element_type=F32)
    h_ref[...] = h
    cn = _rms(h, g_ref[...])
    c_ref[...] = cn
    q_ref[...] = jnp.dot(cn.astype(BF16), wq_ref[...], preferred_element_type=F32)


def _outproj(x, att, dn, w_out, g_ffn, wq):
    t = x.shape[0]
    tm = _row_tile(t, 512)
    row = lambda i: (i, 0)
    fixed = lambda i: (0, 0)
    qw = PEER_HEADS * PEER_QDIM
    return pl.pallas_call(
        _outproj_kernel, name="outproj", grid=(t // tm,),
        out_shape=[jax.ShapeDtypeStruct((t, D_MODEL), F32), jax.ShapeDtypeStruct((t, D_MODEL), F32),
                   jax.ShapeDtypeStruct((t, qw), F32)],
        in_specs=[pl.BlockSpec((tm, D_MODEL), row), pl.BlockSpec((tm, ATT_Q), row),
                  pl.BlockSpec((tm, DN_WIDTH), row), pl.BlockSpec((D_MODEL, D_MODEL), fixed),
                  pl.BlockSpec((1, D_MODEL), fixed), pl.BlockSpec((D_MODEL, qw), fixed)],
        out_specs=[pl.BlockSpec((tm, D_MODEL), row), pl.BlockSpec((tm, D_MODEL), row),
                   pl.BlockSpec((tm, qw), row)],
        compiler_params=_cparams("parallel"))(x, att, dn, w_out, g_ffn, wq)


def _top16_rows(s, iota):
    big = jnp.int32(1 << 30)
    vals, idxs = [], []
    for _ in range(PEER_TOPK):
        m = jnp.max(s, axis=0, keepdims=True)
        i = jnp.min(jnp.where(s == m, iota, big), axis=0, keepdims=True)
        vals.append(m)
        idxs.append(i)
        s = jnp.where(iota == i, NEG_INF, s)
    return jnp.concatenate(vals, axis=0), jnp.concatenate(idxs, axis=0)


def _candidates(a, b, row8):
    t = a.shape[1]
    lo = row8 < 4
    bc = lambda r, n: jnp.broadcast_to(a[r:r + 1], (n, t))
    b8 = b[0:8]
    b44 = jnp.where(lo, b8, pltpu.roll(b8, 4, 0))
    parts = [bc(0, 8), bc(0, 8), bc(1, 8), bc(2, 8), bc(3, 8),
             jnp.where(lo, bc(4, 8), bc(5, 8)), jnp.where(lo, bc(6, 8), bc(7, 8)), a[8:16]]
    others = [b[0:8], b[8:16], b8, b8, b8, b44, b44, jnp.broadcast_to(b[0:1], (8, t))]
    return parts, others


def _flat_index_column():
    r = lax.broadcasted_iota(I32, (8, 1), 0)
    lo = r < 4
    cols = [r, 8 + r, 16 + r, 32 + r, 48 + r, jnp.where(lo, 64 + r, 80 + r - 4),
            jnp.where(lo, 96 + r, 112 + r - 4), (8 + r) * 16]
    return jnp.concatenate(cols, axis=0)


def _route_kernel(q_ref, k1_ref, k2_ref, idx_ref, gate_ref, *, tb):
    iota = lax.broadcasted_iota(I32, (N_KEYS, tb), 0)
    row8 = lax.broadcasted_iota(I32, (8, tb), 0)
    flat = _flat_index_column()
    big = jnp.int32(1 << 30)
    half = PEER_QDIM // 2
    for h in range(PEER_HEADS):
        q1 = q_ref[:, h * PEER_QDIM:h * PEER_QDIM + half]
        q2 = q_ref[:, h * PEER_QDIM + half:(h + 1) * PEER_QDIM]
        nt = (((1,), (1,)), ((), ()))
        s1 = lax.dot_general(k1_ref[h], q1.astype(BF16), nt, preferred_element_type=F32)
        s2 = lax.dot_general(k2_ref[h], q2.astype(BF16), nt, preferred_element_type=F32)
        v1, i1 = _top16_rows(s1, iota)
        v2, i2 = _top16_rows(s2, iota)
        pa, pb = _candidates(v1, v2, row8)
        cand = jnp.concatenate([x + y for x, y in zip(pa, pb)], axis=0)
        ea, eb = _candidates(i1, i2, row8)
        expert = jnp.concatenate([x * N_KEYS + y for x, y in zip(ea, eb)], axis=0)
        scs, exs = [], []
        for _ in range(PEER_TOPK):
            m = jnp.max(cand, axis=0, keepdims=True)
            c = jnp.min(jnp.where(cand == m, flat, big), axis=0, keepdims=True)
            hit = flat == c
            exs.append(jnp.max(jnp.where(hit, expert, -1), axis=0, keepdims=True))
            scs.append(m)
            cand = jnp.where(hit, NEG_INF, cand)
        sc = jnp.concatenate(scs, axis=0)
        e = jnp.exp(sc - sc[0:1])
        gate_ref[h * PEER_TOPK:(h + 1) * PEER_TOPK, :] = e / jnp.sum(e, axis=0, keepdims=True)
        idx_ref[h * PEER_TOPK:(h + 1) * PEER_TOPK, :] = jnp.concatenate(exs, axis=0)


def _route(q, keys1, keys2):
    t = q.shape[0]
    tb = _row_tile(t, 256)
    kspec = pl.BlockSpec((PEER_HEADS, N_KEYS, PEER_QDIM // 2), lambda i: (0, 0, 0))
    return pl.pallas_call(
        functools.partial(_route_kernel, tb=tb), name="peer_route", grid=(t // tb,),
        out_shape=[jax.ShapeDtypeStruct((PEER_SLOTS, t), I32), jax.ShapeDtypeStruct((PEER_SLOTS, t), F32)],
        in_specs=[pl.BlockSpec((tb, PEER_HEADS * PEER_QDIM), lambda i: (i, 0)), kspec, kspec],
        out_specs=[pl.BlockSpec((PEER_SLOTS, tb), lambda i: (0, i)),
                   pl.BlockSpec((PEER_SLOTS, tb), lambda i: (0, i))],
        compiler_params=_cparams("parallel"))(q, keys1, keys2)


GATHER_SLOTS = 3


def _peer_apply_kernel(idx_ref, c_ref, h_ref, gt_ref, uv_ref, o_ref, buf, sem, *, tb):
    ahead = GATHER_SLOTS - 1

    def issue(t, slot):
        for k in range(PEER_SLOTS):
            e = idx_ref[t, k]
            pltpu.make_async_copy(uv_ref.at[pl.ds(e, 1), :], buf.at[slot, pl.ds(k, 1), :],
                                  sem.at[slot]).start(priority=k % 2)

    def wait(slot):
        pltpu.make_async_copy(uv_ref.at[pl.ds(0, PEER_SLOTS), :], buf.at[slot], sem.at[slot]).wait()

    lane = lax.broadcasted_iota(I32, (PEER_SLOTS, tb), 1)

    def apply(t, slot):
        x = c_ref[pl.ds(t, 1), :]
        p = buf[slot, :, 0:D_MODEL] * x
        acc = p[:, 0:128]
        for j in range(1, D_MODEL // 128):
            acc = acc + p[:, j * 128:(j + 1) * 128]
        s = jnp.sum(acc, axis=-1, keepdims=True)
        act = 0.5 * s * (1.0 + lax.erf(s * (1.0 / math.sqrt(2.0))))
        gate = jnp.sum(jnp.where(lane == t, gt_ref[...], 0.0), axis=-1, keepdims=True)
        w = gate * act
        y = jnp.sum(w * buf[slot, :, D_MODEL:2 * D_MODEL], axis=0, keepdims=True)
        o_ref[pl.ds(t, 1), :] = h_ref[pl.ds(t, 1), :] + y

    for t0 in range(ahead):
        issue(t0, t0)

    def body(i, carry):
        for j in range(GATHER_SLOTS):
            t = i * GATHER_SLOTS + j
            issue(t + ahead, (j + ahead) % GATHER_SLOTS)
            wait(j)
            apply(t, j)
        return carry

    assert (tb - ahead) % GATHER_SLOTS == 0
    lax.fori_loop(0, (tb - ahead) // GATHER_SLOTS, body, 0)
    for t in range(tb - ahead, tb):
        wait(t % GATHER_SLOTS)
        apply(t, t % GATHER_SLOTS)


def _peer_apply(idx, c, h, gate_t, uv):
    t = c.shape[0]
    tb = _row_tile(t, 128)
    row = lambda i: (i, 0)
    return pl.pallas_call(
        functools.partial(_peer_apply_kernel, tb=tb), name="peer_apply", grid=(t // tb,),
        out_shape=jax.ShapeDtypeStruct((t, D_MODEL), F32),
        in_specs=[pl.BlockSpec((tb, PEER_SLOTS), row, memory_space=pltpu.SMEM),
                  pl.BlockSpec((tb, D_MODEL), row), pl.BlockSpec((tb, D_MODEL), row),
                  pl.BlockSpec((PEER_SLOTS, tb), lambda i: (0, i)),
                  pl.BlockSpec(memory_space=pl.ANY)],
        out_specs=pl.BlockSpec((tb, D_MODEL), row),
        scratch_shapes=[pltpu.VMEM((GATHER_SLOTS, PEER_SLOTS, 2 * D_MODEL), F32),
                        pltpu.SemaphoreType.DMA((GATHER_SLOTS,))],
        compiler_params=_cparams("arbitrary"))(idx, c, h, gate_t, uv)


def _final_kernel(h_ref, p_ref, gple_ref, wg_ref, wp_ref, gfin_ref, y_ref):
    h = h_ref[...]
    e = _rms(h, gple_ref[...])
    gate = jax.nn.sigmoid(jnp.dot(e.astype(BF16), wg_ref[...], preferred_element_type=F32))
    up = jnp.dot(p_ref[...].astype(BF16), wp_ref[...], preferred_element_type=F32)
    y_ref[...] = _rms(h + up * gate, gfin_ref[...])


def _final(h, p, g_ple, w_gate, w_ple, g_final):
    t = h.shape[0]
    tm = _row_tile(t, 512)
    row = lambda i: (i, 0)
    fixed = lambda i: (0, 0)
    return pl.pallas_call(
        _final_kernel, name="ple_final", grid=(t // tm,),
        out_shape=jax.ShapeDtypeStruct((t, D_MODEL), F32),
        in_specs=[pl.BlockSpec((tm, D_MODEL), row), pl.BlockSpec((tm, PLE_DIM), row),
                  pl.BlockSpec((1, D_MODEL), fixed), pl.BlockSpec((D_MODEL, D_MODEL), fixed),
                  pl.BlockSpec((PLE_DIM, D_MODEL), fixed), pl.BlockSpec((1, D_MODEL), fixed)],
        out_specs=pl.BlockSpec((tm, D_MODEL), row),
        compiler_params=_cparams("parallel"))(h, p, g_ple, w_gate, w_ple, g_final)


def _layer(x, p, kv_buf, conv_prefix, s0, wts, batch, seq):
    t = batch * seq
    q, k, v, dqkv, z, ab = _inproj(x, wts["norm_mix"], wts["w_in"])
    if kv_buf is None:
        att = _swa_prompt(q, k, v, wts["sinks"], batch, seq)
        k_new = k.reshape(batch, seq, ATT_KV)[:, -WINDOW:]
        v_new = v.reshape(batch, seq, ATT_KV)[:, -WINDOW:]
    else:
        att, k_new, v_new = _swa_sample(q.reshape(batch, seq, ATT_Q), k.reshape(batch, seq, ATT_KV),
                                        v.reshape(batch, seq, ATT_KV), kv_buf[0], kv_buf[1], wts["sinks"])
        att = att.reshape(t, ATT_Q)
    dn, s_new = _gdn(dqkv, z, ab, conv_prefix, s0, wts["conv_w"], wts["a_log"], wts["dt_bias"],
                     wts["dn_norm"], batch, seq)
    conv_new = dqkv.reshape(batch, seq, DN_QKV)[:, -(CONV_W - 1):]
    h, c, pq = _outproj(x, att, dn, wts["w_out"], wts["norm_ffn"], wts["peer_wq"])
    idx_t, gate_t = _route(pq, wts["keys1"], wts["keys2"])
    h2 = _peer_apply(idx_t.T, c, h, gate_t, wts["uv"])
    y = _final(h2, p, wts["norm_ple"], wts["ple_gate"], wts["ple_in"], wts["norm_final"])
    return y, k_new, v_new, conv_new, s_new


def kernel(x_prompt, x_sample, p_prompt, p_sample, cache_swa_k, cache_swa_v, state_conv, state_delta, norm_mix, w_in, conv_w, attn_sinks, dn_a_log, dn_dt_bias, dn_norm, w_out, norm_ffn, peer_wq, peer_keys1, peer_keys2, peer_u, peer_v, norm_ple, ple_in, ple_gate, norm_final):
    depth = w_in.shape[0]
    assert depth == 1
    bp, lp = x_prompt.shape[0], x_prompt.shape[1]
    bs, ls = x_sample.shape[0], x_sample.shape[1]
    l = 0
    wts = dict(
        norm_mix=norm_mix[l].reshape(1, D_MODEL),
        w_in=jnp.pad(w_in[l], ((0, 0), (0, IN_WIDTH_PADDED - IN_WIDTH))).astype(BF16),
        conv_w=conv_w[l], sinks=attn_sinks[l], a_log=dn_a_log[l], dt_bias=dn_dt_bias[l], dn_norm=dn_norm[l],
        w_out=w_out[l].astype(BF16), norm_ffn=norm_ffn[l].reshape(1, D_MODEL), peer_wq=peer_wq[l].astype(BF16),
        keys1=peer_keys1[l].astype(BF16), keys2=peer_keys2[l].astype(BF16),
        uv=jnp.concatenate([peer_u[l], peer_v[l]], axis=-1),
        norm_ple=norm_ple[l].reshape(1, D_MODEL), ple_in=ple_in[l].astype(BF16),
        ple_gate=ple_gate[l].astype(BF16), norm_final=norm_final.reshape(1, D_MODEL))
    conv0 = jnp.zeros((bp, CONV_W - 1, DN_QKV), F32)
    s0 = jnp.zeros((bp, DN_HEADS, HEAD_DIM, HEAD_DIM), F32)
    yp, kp, vp, cp, sp = _layer(x_prompt.reshape(bp * lp, D_MODEL), p_prompt[l].reshape(bp * lp, PLE_DIM),
                                None, conv0, s0, wts, bp, lp)
    kv_buf = (cache_swa_k[l].reshape(bs, WINDOW, ATT_KV), cache_swa_v[l].reshape(bs, WINDOW, ATT_KV))
    ys, kn, vn, cn, sn = _layer(x_sample.reshape(bs * ls, D_MODEL), p_sample[l].reshape(bs * ls, PLE_DIM),
                                kv_buf, state_conv[l], state_delta[l], wts, bs, ls)
    kvshape = lambda b: (1, b, WINDOW, ATT_KV_HEADS, HEAD_DIM)
    return (yp.reshape(bp, lp, D_MODEL), ys.reshape(bs, ls, D_MODEL),
            kp.reshape(kvshape(bp)), vp.reshape(kvshape(bp)), cp[None], sp[None],
            kn.reshape(kvshape(bs)), vn.reshape(kvshape(bs)), cn[None], sn[None])
```

```python
import functools
import math

import jax
import jax.numpy as jnp
from jax import lax
from jax.experimental import pallas as pl
from jax.experimental.pallas import tpu as pltpu

F32 = jnp.float32
BF16 = jnp.bfloat16
I32 = jnp.int32
HIGHEST = lax.Precision.HIGHEST

D_MODEL = 1024
HEAD_DIM = 64
ATT_HEADS = 8
ATT_KV_HEADS = 2
ATT_GROUP = 4
WINDOW = 128
ATT_Q = ATT_HEADS * HEAD_DIM
ATT_KV = ATT_KV_HEADS * HEAD_DIM
DN_HEADS = 8
DN_WIDTH = DN_HEADS * HEAD_DIM
DN_QKV = 3 * DN_WIDTH
CONV_W = 4
DN_CHUNK = 64
IN_WIDTH = ATT_Q + 2 * ATT_KV + DN_QKV + DN_WIDTH + 2 * DN_HEADS
IN_WIDTH_PADDED = 2944
N_KEYS = 128
PEER_HEADS = 8
PEER_QDIM = 256
PEER_TOPK = 16
PEER_SLOTS = PEER_HEADS * PEER_TOPK
PLE_DIM = 256
EPS = 1e-6
NEG_INF = float("-inf")

VMEM_LIMIT = 48 * 1024 * 1024


def _cparams(*sem):
    return pltpu.CompilerParams(dimension_semantics=sem, vmem_limit_bytes=VMEM_LIMIT)


def _rms(x, g):
    return x * lax.rsqrt(jnp.mean(x * x, axis=-1, keepdims=True) + EPS) * g


def _bdot(a, b):
    return jnp.dot(a.astype(BF16), b.astype(BF16), preferred_element_type=F32)


def _fdot(a, b):
    return jnp.dot(a, b, precision=HIGHEST, preferred_element_type=F32)


def _row_tile(t, target):
    tile = min(t, target)
    assert t % tile == 0, (t, tile)
    return tile


def _inproj_kernel(x_ref, g_ref, w_ref, q_ref, k_ref, v_ref, dqkv_ref, z_ref, ab_ref):
    a = _rms(x_ref[...], g_ref[...])
    p = jnp.dot(a.astype(BF16), w_ref[...], preferred_element_type=F32)
    c0, c1, c2, c3, c4 = ATT_Q, ATT_Q + ATT_KV, ATT_Q + 2 * ATT_KV, ATT_Q + 2 * ATT_KV + DN_QKV, \
        ATT_Q + 2 * ATT_KV + DN_QKV + DN_WIDTH
    q_ref[...] = p[:, :c0]
    k_ref[...] = p[:, c0:c1]
    v_ref[...] = p[:, c1:c2]
    dqkv_ref[...] = p[:, c2:c3]
    z_ref[...] = p[:, c3:c4]
    ab_ref[...] = p[:, c4:c4 + 2 * DN_HEADS]


def _inproj(x, g, w_pad):
    t = x.shape[0]
    tm = _row_tile(t, 512)
    widths = (ATT_Q, ATT_KV, ATT_KV, DN_QKV, DN_WIDTH, 2 * DN_HEADS)
    row = lambda i: (i, 0)
    fixed = lambda i: (0, 0)
    return pl.pallas_call(
        _inproj_kernel, name="inproj", grid=(t // tm,),
        out_shape=[jax.ShapeDtypeStruct((t, w), F32) for w in widths],
        in_specs=[pl.BlockSpec((tm, D_MODEL), row), pl.BlockSpec((1, D_MODEL), fixed),
                  pl.BlockSpec((D_MODEL, IN_WIDTH_PADDED), fixed)],
        out_specs=[pl.BlockSpec((tm, w), row) for w in widths],
        compiler_params=_cparams("parallel"))(x, g, w_pad)


def _softmax_sink_pv(s, sink, vband):
    m = jnp.maximum(jnp.max(s, axis=-1, keepdims=True), sink)
    e = jnp.exp(s - m)
    denom = jnp.sum(e, axis=-1, keepdims=True) + jnp.exp(sink - m)
    return _bdot(e / denom, vband)


def _alibi_slope(h):
    return 2.0 ** (-8.0 * (h + 1) / ATT_HEADS)


def _swa_prompt_kernel(sink_ref, q_ref, kc_ref, kp_ref, vc_ref, vp_ref, o_ref):
    n = pl.program_id(1)
    qi = lax.broadcasted_iota(I32, (WINDOW, 2 * WINDOW), 0)
    kj = lax.broadcasted_iota(I32, (WINDOW, 2 * WINDOW), 1)
    dist = WINDOW + qi - kj
    kpos = n * WINDOW - WINDOW + kj
    valid = (dist >= 0) & (dist <= WINDOW) & (kpos >= 0)
    distf = dist.astype(F32)
    kband = jnp.concatenate([kp_ref[...], kc_ref[...]], axis=0)
    vband = jnp.concatenate([vp_ref[...], vc_ref[...]], axis=0)
    outs = []
    for h in range(ATT_HEADS):
        kh = h // ATT_GROUP
        q = q_ref[:, h * HEAD_DIM:(h + 1) * HEAD_DIM]
        k = kband[:, kh * HEAD_DIM:(kh + 1) * HEAD_DIM]
        v = vband[:, kh * HEAD_DIM:(kh + 1) * HEAD_DIM]
        s = lax.dot_general(q.astype(BF16), k.astype(BF16), (((1,), (1,)), ((), ())),
                            preferred_element_type=F32) * (HEAD_DIM ** -0.5)
        s = jnp.where(valid, s - _alibi_slope(h) * distf, NEG_INF)
        outs.append(_softmax_sink_pv(s, sink_ref[h], v))
    o_ref[...] = jnp.concatenate(outs, axis=-1)


def _swa_prompt(q, k, v, sinks, batch, seq):
    nblk = seq // WINDOW
    cur = lambda b, n: (b * nblk + n, 0)
    prev = lambda b, n: (b * nblk + jnp.maximum(n - 1, 0), 0)
    return pl.pallas_call(
        _swa_prompt_kernel, name="swa_prompt", grid=(batch, nblk),
        out_shape=jax.ShapeDtypeStruct((batch * seq, ATT_Q), F32),
        in_specs=[pl.BlockSpec(memory_space=pltpu.SMEM),
                  pl.BlockSpec((WINDOW, ATT_Q), cur),
                  pl.BlockSpec((WINDOW, ATT_KV), cur), pl.BlockSpec((WINDOW, ATT_KV), prev),
                  pl.BlockSpec((WINDOW, ATT_KV), cur), pl.BlockSpec((WINDOW, ATT_KV), prev)],
        out_specs=pl.BlockSpec((WINDOW, ATT_Q), cur),
        compiler_params=_cparams("parallel", "parallel"))(sinks, q, k, k, v, v)


def _swa_sample_kernel(sink_ref, q_ref, k_ref, v_ref, kb_ref, vb_ref, o_ref, ko_ref, vo_ref, *, bt, seq):
    klen = WINDOW + seq
    qi = lax.broadcasted_iota(I32, (seq, klen), 0)
    kj = lax.broadcasted_iota(I32, (seq, klen), 1)
    dist = qi - (kj - WINDOW)
    valid = (dist >= 0) & (dist <= WINDOW)
    distf = dist.astype(F32)
    for b in range(bt):
        kc = jnp.concatenate([kb_ref[b], k_ref[b]], axis=0)
        vc = jnp.concatenate([vb_ref[b], v_ref[b]], axis=0)
        ko_ref[b] = kc[seq:]
        vo_ref[b] = vc[seq:]
        outs = []
        for h in range(ATT_HEADS):
            kh = h // ATT_GROUP
            q = q_ref[b][:, h * HEAD_DIM:(h + 1) * HEAD_DIM]
            k = kc[:, kh * HEAD_DIM:(kh + 1) * HEAD_DIM]
            v = vc[:, kh * HEAD_DIM:(kh + 1) * HEAD_DIM]
            s = lax.dot_general(q.astype(BF16), k.astype(BF16), (((1,), (1,)), ((), ())),
                                preferred_element_type=F32) * (HEAD_DIM ** -0.5)
            s = jnp.where(valid, s - _alibi_slope(h) * distf, NEG_INF)
            outs.append(_softmax_sink_pv(s, sink_ref[h], v))
        o_ref[b] = jnp.concatenate(outs, axis=-1)


def _swa_sample(q, k, v, k_buf, v_buf, sinks):
    batch, seq = q.shape[0], q.shape[1]
    bt = _row_tile(batch, 8)
    blk = lambda w, r: pl.BlockSpec((bt, r, w), lambda i: (i, 0, 0))
    return pl.pallas_call(
        functools.partial(_swa_sample_kernel, bt=bt, seq=seq), name="swa_sample", grid=(batch // bt,),
        out_shape=[jax.ShapeDtypeStruct((batch, seq, ATT_Q), F32),
                   jax.ShapeDtypeStruct((batch, WINDOW, ATT_KV), F32),
                   jax.ShapeDtypeStruct((batch, WINDOW, ATT_KV), F32)],
        in_specs=[pl.BlockSpec(memory_space=pltpu.SMEM), blk(ATT_Q, seq), blk(ATT_KV, seq), blk(ATT_KV, seq),
                  blk(ATT_KV, WINDOW), blk(ATT_KV, WINDOW)],
        out_specs=[blk(ATT_Q, seq), blk(ATT_KV, WINDOW), blk(ATT_KV, WINDOW)],
        compiler_params=_cparams("parallel"))(sinks, q, k, v, k_buf, v_buf)


_NN = (((1,), (0,)), ((), ()))
_NT = (((1,), (1,)), ((), ()))
_TN = (((0,), (0,)), ((), ()))


def _split_bf16(a):
    hi = a.astype(BF16)
    return hi, (a - hi.astype(F32)).astype(BF16)


def _dot3(a, b, dims=_NN):
    ah, al = _split_bf16(a)
    bh, bl = _split_bf16(b)
    f = lambda x, y: lax.dot_general(x, y, dims, preferred_element_type=F32)
    return f(ah, bh) + (f(ah, bl) + f(al, bh))


def _gdn_kernel(alog_ref, dtb_ref, dqkv_ref, z_ref, ab_ref, cw_ref, nrm_ref, pre_ref, s0_ref,
                o_ref, sfin_ref, tail_ref, state_ref, *, c, g):
    n = pl.program_id(1)
    ngrp = DN_HEADS // g
    r = g * c
    hd = HEAD_DIM

    @pl.when(n == 0)
    def _():
        tail_ref[...] = jnp.concatenate([jnp.zeros((8 - (CONV_W - 1), DN_QKV), F32), pre_ref[0]], axis=0)
        for gi in range(ngrp):
            state_ref[gi] = jnp.concatenate([s0_ref[0, gi * g + j] for j in range(g)], axis=1)

    x = dqkv_ref[...]
    hist = jnp.concatenate([tail_ref[...], x], axis=0)
    tail_ref[...] = hist[c:]
    cw = cw_ref[...]
    y = x * cw[CONV_W - 1:CONV_W]
    for d in range(1, CONV_W):
        y = y + hist[8 - d:8 - d + c] * cw[CONV_W - 1 - d:CONV_W - d]
    qkv = y * jax.nn.sigmoid(y)
    z = z_ref[...]
    ab = ab_ref[...]
    beta_all = jax.nn.sigmoid(ab[:, DN_HEADS:2 * DN_HEADS])
    g_all = -jnp.exp(alog_ref[...]) * jax.nn.softplus(ab[:, 0:DN_HEADS] + dtb_ref[...])

    ri = lax.broadcasted_iota(I32, (r, r), 0)
    ci = lax.broadcasted_iota(I32, (r, r), 1)
    same = (ri // c) == (ci // c)
    lower = same & (ri >= ci)
    strict = same & (ri > ci)
    tri = lower.astype(F32)
    blk = same.astype(F32)
    eye = (ri == ci).astype(F32)
    rowhead = lax.broadcasted_iota(I32, (r, hd), 0) // c

    def stack(a, gi, base):
        return jnp.concatenate([a[:, base + (gi * g + j) * hd:base + (gi * g + j + 1) * hd]
                                for j in range(g)], axis=0)

    def column(a, gi):
        return jnp.concatenate([a[:, gi * g + j:gi * g + j + 1] for j in range(g)], axis=0)

    def own_block(m):
        acc = jnp.where(rowhead == 0, m[:, 0:hd], 0.0)
        for j in range(1, g):
            acc = acc + jnp.where(rowhead == j, m[:, j * hd:(j + 1) * hd], 0.0)
        return acc

    grp = []
    for gi in range(ngrp):
        q = stack(qkv, gi, 0)
        k = stack(qkv, gi, DN_WIDTH)
        v = stack(qkv, gi, 2 * DN_WIDTH)
        q = q * lax.rsqrt(jnp.sum(q * q, axis=-1, keepdims=True) + EPS) * (hd ** -0.5)
        k = k * lax.rsqrt(jnp.sum(k * k, axis=-1, keepdims=True) + EPS)
        beta = column(beta_all, gi)
        gr = column(g_all, gi)
        gcol = _fdot(tri, gr)
        grow = lax.dot_general(gr, tri, (((0,), (1,)), ((), ())), precision=HIGHEST,
                               preferred_element_type=F32)
        glast = _fdot(blk, gr)
        decay = jnp.exp(jnp.where(lower, gcol - grow, NEG_INF))
        kb = k * beta
        lm = jnp.where(strict, _dot3(kb, k, _NT) * decay, 0.0)
        grp.append(dict(q=q, k=k, v=v, beta=beta, gcol=gcol, glast=glast, decay=decay, kb=kb, lm=lm))

    invs = [eye] * ngrp
    s = 1
    while s < c:
        off = ((ri // (2 * s)) == (ci // (2 * s))) & ((ri // s) % 2 == 1) & ((ci // s) % 2 == 0)
        tmp = [_dot3(invs[gi], jnp.where(off, grp[gi]["lm"], 0.0)) for gi in range(ngrp)]
        invs = [invs[gi] - _dot3(tmp[gi], invs[gi]) for gi in range(ngrp)]
        s *= 2

    outs = [None] * DN_HEADS
    for gi in range(ngrp):
        d = grp[gi]
        eg = jnp.exp(d["gcol"])
        sol = _dot3(invs[gi], jnp.concatenate([d["v"] * d["beta"], d["kb"] * eg], axis=1))
        u = sol[:, :hd]
        w = sol[:, hd:]
        qk = _dot3(d["q"], d["k"], _NT) * d["decay"]
        ktail = d["k"] * jnp.exp(d["glast"] - d["gcol"])
        qhead = d["q"] * eg
        st = state_ref[gi]
        vnew = u - own_block(_dot3(w, st))
        o = own_block(_dot3(qhead, st)) + _dot3(qk, vnew)
        vblk = jnp.concatenate([jnp.where(rowhead == j, vnew, 0.0) for j in range(g)], axis=1)
        elast = jnp.exp(d["glast"])
        keep = jnp.concatenate([jnp.broadcast_to(elast[j * c:j * c + 1, :], (1, hd)) for j in range(g)], axis=1)
        state_ref[gi] = st * keep + _dot3(ktail, vblk, _TN)
        zs = stack(z, gi, 0)
        o = _rms(o, nrm_ref[...]) * (zs * jax.nn.sigmoid(zs))
        for j in range(g):
            outs[gi * g + j] = o[j * c:(j + 1) * c]
    o_ref[...] = jnp.concatenate(outs, axis=-1)

    @pl.when(n == pl.num_programs(1) - 1)
    def _():
        for gi in range(ngrp):
            for j in range(g):
                sfin_ref[0, gi * g + j] = state_ref[gi][:, j * hd:(j + 1) * hd]


def _gdn(dqkv, z, ab, conv_prefix, s0, conv_w, a_log, dt_bias, dn_norm, batch, seq):
    c = min(DN_CHUNK, seq)
    nchunk = seq // c
    assert seq % c == 0 and c % 8 == 0
    g = min(DN_HEADS, max(1, 128 // c))
    tok = lambda b, n: (b * nchunk + n, 0)
    fixed = lambda b, n: (0, 0)
    return pl.pallas_call(
        functools.partial(_gdn_kernel, c=c, g=g), name="gdn", grid=(batch, nchunk),
        out_shape=[jax.ShapeDtypeStruct((batch * seq, DN_WIDTH), F32),
                   jax.ShapeDtypeStruct((batch, DN_HEADS, HEAD_DIM, HEAD_DIM), F32)],
        in_specs=[pl.BlockSpec((1, DN_HEADS), fixed), pl.BlockSpec((1, DN_HEADS), fixed),
                  pl.BlockSpec((c, DN_QKV), tok), pl.BlockSpec((c, DN_WIDTH), tok),
                  pl.BlockSpec((c, 2 * DN_HEADS), tok),
                  pl.BlockSpec((CONV_W, DN_QKV), fixed), pl.BlockSpec((1, HEAD_DIM), fixed),
                  pl.BlockSpec((1, CONV_W - 1, DN_QKV), lambda b, n: (b, 0, 0)),
                  pl.BlockSpec((1, DN_HEADS, HEAD_DIM, HEAD_DIM), lambda b, n: (b, 0, 0, 0))],
        out_specs=[pl.BlockSpec((c, DN_WIDTH), tok),
                   pl.BlockSpec((1, DN_HEADS, HEAD_DIM, HEAD_DIM), lambda b, n: (b, 0, 0, 0))],
        scratch_shapes=[pltpu.VMEM((8, DN_QKV), F32),
                        pltpu.VMEM((DN_HEADS // g, HEAD_DIM, g * HEAD_DIM), F32)],
        compiler_params=_cparams("parallel", "arbitrary"))(
            a_log.reshape(1, DN_HEADS), dt_bias.reshape(1, DN_HEADS), dqkv, z, ab, conv_w,
            dn_norm.reshape(1, HEAD_DIM), conv_prefix, s0)


def _outproj_kernel(x_ref, att_ref, dn_ref, wo_ref, g_ref, wq_ref, h_ref, c_ref, q_ref):
    mix = jnp.concatenate([att_ref[...], dn_ref[...]], axis=-1)
    h = x_ref[...] + jnp.dot(mix.astype(BF16), wo_ref[...], preferred_element_type=F32)
    h_ref[...] = h
    cn = _rms(h, g_ref[...])
    c_ref[...] = cn
    q_ref[...] = jnp.dot(cn.astype(BF16), wq_ref[...], preferred_element_type=F32)


def _outproj(x, att, dn, w_out, g_ffn, wq):
    t = x.shape[0]
    tm = _row_tile(t, 512)
    row = lambda i: (i, 0)
    fixed = lambda i: (0, 0)
    qw = PEER_HEADS * PEER_QDIM
    return pl.pallas_call(
        _outproj_kernel, name="outproj", grid=(t // tm,),
        out_shape=[jax.ShapeDtypeStruct((t, D_MODEL), F32), jax.ShapeDtypeStruct((t, D_MODEL), F32),
                   jax.ShapeDtypeStruct((t, qw), F32)],
        in_specs=[pl.BlockSpec((tm, D_MODEL), row), pl.BlockSpec((tm, ATT_Q), row),
                  pl.BlockSpec((tm, DN_WIDTH), row), pl.BlockSpec((D_MODEL, D_MODEL), fixed),
                  pl.BlockSpec((1, D_MODEL), fixed), pl.BlockSpec((D_MODEL, qw), fixed)],
        out_specs=[pl.BlockSpec((tm, D_MODEL), row), pl.BlockSpec((tm, D_MODEL), row),
                   pl.BlockSpec((tm, qw), row)],
        compiler_params=_cparams("parallel"))(x, att, dn, w_out, g_ffn, wq)


def _top16_rows(s, iota):
    big = jnp.int32(1 << 30)
    vals, idxs = [], []
    for _ in range(PEER_TOPK):
        m = jnp.max(s, axis=0, keepdims=True)
        i = jnp.min(jnp.where(s == m, iota, big), axis=0, keepdims=True)
        vals.append(m)
        idxs.append(i)
        s = jnp.where(iota == i, NEG_INF, s)
    return jnp.concatenate(vals, axis=0), jnp.concatenate(idxs, axis=0)


def _candidates(a, b, row8):
    t = a.shape[1]
    lo = row8 < 4
    bc = lambda r, n: jnp.broadcast_to(a[r:r + 1], (n, t))
    b8 = b[0:8]
    b44 = jnp.where(lo, b8, pltpu.roll(b8, 4, 0))
    parts = [bc(0, 8), bc(0, 8), bc(1, 8), bc(2, 8), bc(3, 8),
             jnp.where(lo, bc(4, 8), bc(5, 8)), jnp.where(lo, bc(6, 8), bc(7, 8)), a[8:16]]
    others = [b[0:8], b[8:16], b8, b8, b8, b44, b44, jnp.broadcast_to(b[0:1], (8, t))]
    return parts, others


def _flat_index_column():
    r = lax.broadcasted_iota(I32, (8, 1), 0)
    lo = r < 4
    cols = [r, 8 + r, 16 + r, 32 + r, 48 + r, jnp.where(lo, 64 + r, 80 + r - 4),
            jnp.where(lo, 96 + r, 112 + r - 4), (8 + r) * 16]
    return jnp.concatenate(cols, axis=0)


def _route_kernel(q_ref, k1_ref, k2_ref, idx_ref, gate_ref, *, tb):
    iota = lax.broadcasted_iota(I32, (N_KEYS, tb), 0)
    row8 = lax.broadcasted_iota(I32, (8, tb), 0)
    flat = _flat_index_column()
    big = jnp.int32(1 << 30)
    half = PEER_QDIM // 2
    for h in range(PEER_HEADS):
        q1 = q_ref[:, h * PEER_QDIM:h * PEER_QDIM + half]
        q2 = q_ref[:, h * PEER_QDIM + half:(h + 1) * PEER_QDIM]
        nt = (((1,), (1,)), ((), ()))
        s1 = lax.dot_general(k1_ref[h], q1.astype(BF16), nt, preferred_element_type=F32)
        s2 = lax.dot_general(k2_ref[h], q2.astype(BF16), nt, preferred_element_type=F32)
        v1, i1 = _top16_rows(s1, iota)
        v2, i2 = _top16_rows(s2, iota)
        pa, pb = _candidates(v1, v2, row8)
        cand = jnp.concatenate([x + y for x, y in zip(pa, pb)], axis=0)
        ea, eb = _candidates(i1, i2, row8)
        expert = jnp.concatenate([x * N_KEYS + y for x, y in zip(ea, eb)], axis=0)
        scs, exs = [], []
        for _ in range(PEER_TOPK):
            m = jnp.max(cand, axis=0, keepdims=True)
            c = jnp.min(jnp.where(cand == m, flat, big), axis=0, keepdims=True)
            hit = flat == c
            exs.append(jnp.max(jnp.where(hit, expert, -1), axis=0, keepdims=True))
            scs.append(m)
            cand = jnp.where(hit, NEG_INF, cand)
        sc = jnp.concatenate(scs, axis=0)
        e = jnp.exp(sc - sc[0:1])
        gate_ref[h * PEER_TOPK:(h + 1) * PEER_TOPK, :] = e / jnp.sum(e, axis=0, keepdims=True)
        idx_ref[h * PEER_TOPK:(h + 1) * PEER_TOPK, :] = jnp.concatenate(exs, axis=0)


def _route(q, keys1, keys2):
    t = q.shape[0]
    tb = _row_tile(t, 256)
    kspec = pl.BlockSpec((PEER_HEADS, N_KEYS, PEER_QDIM // 2), lambda i: (0, 0, 0))
    return pl.pallas_call(
        functools.partial(_route_kernel, tb=tb), name="peer_route", grid=(t // tb,),
        out_shape=[jax.ShapeDtypeStruct((PEER_SLOTS, t), I32), jax.ShapeDtypeStruct((PEER_SLOTS, t), F32)],
        in_specs=[pl.BlockSpec((tb, PEER_HEADS * PEER_QDIM), lambda i: (i, 0)), kspec, kspec],
        out_specs=[pl.BlockSpec((PEER_SLOTS, tb), lambda i: (0, i)),
                   pl.BlockSpec((PEER_SLOTS, tb), lambda i: (0, i))],
        compiler_params=_cparams("parallel"))(q, keys1, keys2)


GATHER_SLOTS = 3


def _peer_apply_kernel(idx_ref, c_ref, h_ref, gt_ref, uv_ref, o_ref, buf, sem, *, tb):
    ahead = GATHER_SLOTS - 1

    def issue(t, slot):
        for k in range(PEER_SLOTS):
            e = idx_ref[t, k]
            pltpu.make_async_copy(uv_ref.at[e], buf.at[slot, pl.ds(k, 1), :],
                                  sem.at[slot]).start(priority=k % 2)

    def wait(slot):
        pltpu.make_async_copy(buf.at[slot], buf.at[slot], sem.at[slot]).wait()

    lane = lax.broadcasted_iota(I32, (PEER_SLOTS, tb), 1)

    def apply(t, slot):
        x = c_ref[pl.ds(t, 1), :]
        p = buf[slot, :, 0:D_MODEL] * x
        acc = p[:, 0:128]
        for j in range(1, D_MODEL // 128):
            acc = acc + p[:, j * 128:(j + 1) * 128]
        s = jnp.sum(acc, axis=-1, keepdims=True)
        act = 0.5 * s * (1.0 + lax.erf(s * (1.0 / math.sqrt(2.0))))
        gate = jnp.sum(jnp.where(lane == t, gt_ref[...], 0.0), axis=-1, keepdims=True)
        w = gate * act
        y = jnp.sum(w * buf[slot, :, D_MODEL:2 * D_MODEL], axis=0, keepdims=True)
        o_ref[pl.ds(t, 1), :] = h_ref[pl.ds(t, 1), :] + y

    for t0 in range(ahead):
        issue(t0, t0)

    def body(i, carry):
        for j in range(GATHER_SLOTS):
            t = i * GATHER_SLOTS + j
            issue(t + ahead, (j + ahead) % GATHER_SLOTS)
            wait(j)
            apply(t, j)
        return carry

    assert (tb - ahead) % GATHER_SLOTS == 0
    lax.fori_loop(0, (tb - ahead) // GATHER_SLOTS, body, 0)
    for t in range(tb - ahead, tb):
        wait(t % GATHER_SLOTS)
        apply(t, t % GATHER_SLOTS)


def _peer_apply(idx, c, h, gate_t, uv):
    t = c.shape[0]
    tb = _row_tile(t, 128)
    row = lambda i: (i, 0)
    return pl.pallas_call(
        functools.partial(_peer_apply_kernel, tb=tb), name="peer_apply", grid=(t // tb,),
        out_shape=jax.ShapeDtypeStruct((t, D_MODEL), F32),
        in_specs=[pl.BlockSpec((tb, PEER_SLOTS), row, memory_space=pltpu.SMEM),
                  pl.BlockSpec((tb, D_MODEL), row), pl.BlockSpec((tb, D_MODEL), row),
                  pl.BlockSpec((PEER_SLOTS, tb), lambda i: (0, i)),
                  pl.BlockSpec(memory_space=pl.ANY)],
        out_specs=pl.BlockSpec((tb, D_MODEL), row),
        scratch_shapes=[pltpu.VMEM((GATHER_SLOTS, PEER_SLOTS, 2 * D_MODEL), F32),
                        pltpu.SemaphoreType.DMA((GATHER_SLOTS,))],
        compiler_params=_cparams("arbitrary"))(idx, c, h, gate_t, uv.reshape(uv.shape[0], 1, uv.shape[1]))


def _final_kernel(h_ref, p_ref, gple_ref, wg_ref, wp_ref, gfin_ref, y_ref):
    h = h_ref[...]
    e = _rms(h, gple_ref[...])
    gate = jax.nn.sigmoid(jnp.dot(e.astype(BF16), wg_ref[...], preferred_element_type=F32))
    up = jnp.dot(p_ref[...].astype(BF16), wp_ref[...], preferred_element_type=F32)
    y_ref[...] = _rms(h + up * gate, gfin_ref[...])


def _final(h, p, g_ple, w_gate, w_ple, g_final):
    t = h.shape[0]
    tm = _row_tile(t, 512)
    row = lambda i: (i, 0)
    fixed = lambda i: (0, 0)
    return pl.pallas_call(
        _final_kernel, name="ple_final", grid=(t // tm,),
        out_shape=jax.ShapeDtypeStruct((t, D_MODEL), F32),
        in_specs=[pl.BlockSpec((tm, D_MODEL), row), pl.BlockSpec((tm, PLE_DIM), row),
                  pl.BlockSpec((1, D_MODEL), fixed), pl.BlockSpec((D_MODEL, D_MODEL), fixed),
                  pl.BlockSpec((PLE_DIM, D_MODEL), fixed), pl.BlockSpec((1, D_MODEL), fixed)],
        out_specs=pl.BlockSpec((tm, D_MODEL), row),
        compiler_params=_cparams("parallel"))(h, p, g_ple, w_gate, w_ple, g_final)


def _layer(x, p, kv_buf, conv_prefix, s0, wts, batch, seq):
    t = batch * seq
    q, k, v, dqkv, z, ab = _inproj(x, wts["norm_mix"], wts["w_in"])
    if kv_buf is None:
        att = _swa_prompt(q, k, v, wts["sinks"], batch, seq)
        k_new = k.reshape(batch, seq, ATT_KV)[:, -WINDOW:]
        v_new = v.reshape(batch, seq, ATT_KV)[:, -WINDOW:]
    else:
        att, k_new, v_new = _swa_sample(q.reshape(batch, seq, ATT_Q), k.reshape(batch, seq, ATT_KV),
                                        v.reshape(batch, seq, ATT_KV), kv_buf[0], kv_buf[1], wts["sinks"])
        att = att.reshape(t, ATT_Q)
    dn, s_new = _gdn(dqkv, z, ab, conv_prefix, s0, wts["conv_w"], wts["a_log"], wts["dt_bias"],
                     wts["dn_norm"], batch, seq)
    conv_new = dqkv.reshape(batch, seq, DN_QKV)[:, -(CONV_W - 1):]
    h, c, pq = _outproj(x, att, dn, wts["w_out"], wts["norm_ffn"], wts["peer_wq"])
    idx_t, gate_t = _route(pq, wts["keys1"], wts["keys2"])
    h2 = _peer_apply(idx_t.T, c, h, gate_t, wts["uv"])
    y = _final(h2, p, wts["norm_ple"], wts["ple_gate"], wts["ple_in"], wts["norm_final"])
    return y, k_new, v_new, conv_new, s_new


def kernel(x_prompt, x_sample, p_prompt, p_sample, cache_swa_k, cache_swa_v, state_conv, state_delta, norm_mix, w_in, conv_w, attn_sinks, dn_a_log, dn_dt_bias, dn_norm, w_out, norm_ffn, peer_wq, peer_keys1, peer_keys2, peer_u, peer_v, norm_ple, ple_in, ple_gate, norm_final):
    depth = w_in.shape[0]
    assert depth == 1
    bp, lp = x_prompt.shape[0], x_prompt.shape[1]
    bs, ls = x_sample.shape[0], x_sample.shape[1]
    l = 0
    wts = dict(
        norm_mix=norm_mix[l].reshape(1, D_MODEL),
        w_in=jnp.pad(w_in[l], ((0, 0), (0, IN_WIDTH_PADDED - IN_WIDTH))).astype(BF16),
        conv_w=conv_w[l], sinks=attn_sinks[l], a_log=dn_a_log[l], dt_bias=dn_dt_bias[l], dn_norm=dn_norm[l],
        w_out=w_out[l].astype(BF16), norm_ffn=norm_ffn[l].reshape(1, D_MODEL), peer_wq=peer_wq[l].astype(BF16),
        keys1=peer_keys1[l].astype(BF16), keys2=peer_keys2[l].astype(BF16),
        uv=jnp.concatenate([peer_u[l], peer_v[l]], axis=-1),
        norm_ple=norm_ple[l].reshape(1, D_MODEL), ple_in=ple_in[l].astype(BF16),
        ple_gate=ple_gate[l].astype(BF16), norm_final=norm_final.reshape(1, D_MODEL))
    conv0 = jnp.zeros((bp, CONV_W - 1, DN_QKV), F32)
    s0 = jnp.zeros((bp, DN_HEADS, HEAD_DIM, HEAD_DIM), F32)
    yp, kp, vp, cp, sp = _layer(x_prompt.reshape(bp * lp, D_MODEL), p_prompt[l].reshape(bp * lp, PLE_DIM),
                                None, conv0, s0, wts, bp, lp)
    kv_buf = (cache_swa_k[l].reshape(bs, WINDOW, ATT_KV), cache_swa_v[l].reshape(bs, WINDOW, ATT_KV))
    ys, kn, vn, cn, sn = _layer(x_sample.reshape(bs * ls, D_MODEL), p_sample[l].reshape(bs * ls, PLE_DIM),
                                kv_buf, state_conv[l], state_delta[l], wts, bs, ls)
    kvshape = lambda b: (1, b, WINDOW, ATT_KV_HEADS, HEAD_DIM)
    return (yp.reshape(bp, lp, D_MODEL), ys.reshape(bs, ls, D_MODEL),
            kp.reshape(kvshape(bp)), vp.reshape(kvshape(bp)), cp[None], sp[None],
            kn.reshape(kvshape(bs)), vn.reshape(kvshape(bs)), cn[None], sn[None])
```

```python
import functools
import math

import jax
import jax.numpy as jnp
from jax import lax
from jax.experimental import pallas as pl
from jax.experimental.pallas import tpu as pltpu

F32 = jnp.float32
BF16 = jnp.bfloat16
I32 = jnp.int32
HIGHEST = lax.Precision.HIGHEST

D_MODEL = 1024
HEAD_DIM = 64
ATT_HEADS = 8
ATT_KV_HEADS = 2
ATT_GROUP = 4
WINDOW = 128
ATT_Q = ATT_HEADS * HEAD_DIM
ATT_KV = ATT_KV_HEADS * HEAD_DIM
DN_HEADS = 8
DN_WIDTH = DN_HEADS * HEAD_DIM
DN_QKV = 3 * DN_WIDTH
CONV_W = 4
DN_CHUNK = 64
IN_WIDTH = ATT_Q + 2 * ATT_KV + DN_QKV + DN_WIDTH + 2 * DN_HEADS
IN_WIDTH_PADDED = 2944
N_KEYS = 128
PEER_HEADS = 8
PEER_QDIM = 256
PEER_TOPK = 16
PEER_SLOTS = PEER_HEADS * PEER_TOPK
PLE_DIM = 256
EPS = 1e-6
NEG_INF = float("-inf")

VMEM_LIMIT = 48 * 1024 * 1024


def _cparams(*sem):
    return pltpu.CompilerParams(dimension_semantics=sem, vmem_limit_bytes=VMEM_LIMIT)


def _rms(x, g):
    return x * lax.rsqrt(jnp.mean(x * x, axis=-1, keepdims=True) + EPS) * g


def _bdot(a, b):
    return jnp.dot(a.astype(BF16), b.astype(BF16), preferred_element_type=F32)


def _fdot(a, b):
    return jnp.dot(a, b, precision=HIGHEST, preferred_element_type=F32)


_NN = (((1,), (0,)), ((), ()))
_NT = (((1,), (1,)), ((), ()))
_TN = (((0,), (0,)), ((), ()))


def _row_tile(t, target):
    tile = min(t, target)
    assert t % tile == 0, (t, tile)
    return tile


def _inproj_kernel(x_ref, g_ref, w_ref, q_ref, k_ref, v_ref, dqkv_ref, z_ref, ab_ref):
    a = _rms(x_ref[...], g_ref[...])
    p = jnp.dot(a.astype(BF16), w_ref[...], preferred_element_type=F32)
    c0, c1, c2, c3, c4 = ATT_Q, ATT_Q + ATT_KV, ATT_Q + 2 * ATT_KV, ATT_Q + 2 * ATT_KV + DN_QKV, \
        ATT_Q + 2 * ATT_KV + DN_QKV + DN_WIDTH
    q_ref[...] = p[:, :c0]
    k_ref[...] = p[:, c0:c1]
    v_ref[...] = p[:, c1:c2]
    dqkv_ref[...] = p[:, c2:c3]
    z_ref[...] = p[:, c3:c4]
    ab_ref[...] = p[:, c4:c4 + 2 * DN_HEADS]


def _inproj(x, g, w_pad):
    t = x.shape[0]
    tm = _row_tile(t, 512)
    widths = (ATT_Q, ATT_KV, ATT_KV, DN_QKV, DN_WIDTH, 2 * DN_HEADS)
    row = lambda i: (i, 0)
    fixed = lambda i: (0, 0)
    return pl.pallas_call(
        _inproj_kernel, name="inproj", grid=(t // tm,),
        out_shape=[jax.ShapeDtypeStruct((t, w), F32) for w in widths],
        in_specs=[pl.BlockSpec((tm, D_MODEL), row), pl.BlockSpec((1, D_MODEL), fixed),
                  pl.BlockSpec((D_MODEL, IN_WIDTH_PADDED), fixed)],
        out_specs=[pl.BlockSpec((tm, w), row) for w in widths],
        compiler_params=_cparams("parallel"))(x, g, w_pad)


def _softmax_sink_pv(s, sink, vband):
    m = jnp.maximum(jnp.max(s, axis=-1, keepdims=True), sink)
    e = jnp.exp(s - m)
    denom = jnp.sum(e, axis=-1, keepdims=True) + jnp.exp(sink - m)
    return _bdot(e / denom, vband)


def _alibi_slope(h):
    return 2.0 ** (-8.0 * (h + 1) / ATT_HEADS)


def _group_columns(kh, rows, sink_ref):
    rg = lax.broadcasted_iota(I32, (ATT_GROUP * rows, 1), 0) // rows
    slope = jnp.full((ATT_GROUP * rows, 1), _alibi_slope(kh * ATT_GROUP), F32)
    sink = jnp.full((ATT_GROUP * rows, 1), sink_ref[kh * ATT_GROUP], F32)
    for j in range(1, ATT_GROUP):
        slope = jnp.where(rg == j, _alibi_slope(kh * ATT_GROUP + j), slope)
        sink = jnp.where(rg == j, sink_ref[kh * ATT_GROUP + j], sink)
    return slope, sink


def _attend_kv_head(q_all, kband, vband, kh, dist, valid, sink_ref):
    rows = q_all.shape[0]
    heads = range(kh * ATT_GROUP, (kh + 1) * ATT_GROUP)
    q4 = jnp.concatenate([q_all[:, h * HEAD_DIM:(h + 1) * HEAD_DIM] for h in heads], axis=0)
    k = kband[:, kh * HEAD_DIM:(kh + 1) * HEAD_DIM]
    v = vband[:, kh * HEAD_DIM:(kh + 1) * HEAD_DIM]
    s = lax.dot_general(q4.astype(BF16), k.astype(BF16), _NT, preferred_element_type=F32) * (HEAD_DIM ** -0.5)
    slope, sink = _group_columns(kh, rows, sink_ref)
    s = jnp.where(valid, s - slope * dist, NEG_INF)
    o4 = _softmax_sink_pv(s, sink, v)
    return [o4[j * rows:(j + 1) * rows] for j in range(ATT_GROUP)]


def _swa_prompt_kernel(sink_ref, q_ref, kc_ref, kp_ref, vc_ref, vp_ref, o_ref):
    n = pl.program_id(1)
    rows = ATT_GROUP * WINDOW
    qi = lax.broadcasted_iota(I32, (rows, 2 * WINDOW), 0) % WINDOW
    kj = lax.broadcasted_iota(I32, (rows, 2 * WINDOW), 1)
    dist = WINDOW + qi - kj
    kpos = n * WINDOW - WINDOW + kj
    valid = (dist >= 0) & (dist <= WINDOW) & (kpos >= 0)
    distf = dist.astype(F32)
    kband = jnp.concatenate([kp_ref[...], kc_ref[...]], axis=0)
    vband = jnp.concatenate([vp_ref[...], vc_ref[...]], axis=0)
    q_all = q_ref[...]
    outs = []
    for kh in range(ATT_KV_HEADS):
        outs += _attend_kv_head(q_all, kband, vband, kh, distf, valid, sink_ref)
    o_ref[...] = jnp.concatenate(outs, axis=-1)


def _swa_prompt(q, k, v, sinks, batch, seq):
    nblk = seq // WINDOW
    cur = lambda b, n: (b * nblk + n, 0)
    prev = lambda b, n: (b * nblk + jnp.maximum(n - 1, 0), 0)
    return pl.pallas_call(
        _swa_prompt_kernel, name="swa_prompt", grid=(batch, nblk),
        out_shape=jax.ShapeDtypeStruct((batch * seq, ATT_Q), F32),
        in_specs=[pl.BlockSpec(memory_space=pltpu.SMEM),
                  pl.BlockSpec((WINDOW, ATT_Q), cur),
                  pl.BlockSpec((WINDOW, ATT_KV), cur), pl.BlockSpec((WINDOW, ATT_KV), prev),
                  pl.BlockSpec((WINDOW, ATT_KV), cur), pl.BlockSpec((WINDOW, ATT_KV), prev)],
        out_specs=pl.BlockSpec((WINDOW, ATT_Q), cur),
        compiler_params=_cparams("parallel", "parallel"))(sinks, q, k, k, v, v)


def _swa_sample_kernel(sink_ref, q_ref, k_ref, v_ref, kb_ref, vb_ref, o_ref, ko_ref, vo_ref, *, bt, seq):
    klen = WINDOW + seq
    rows = ATT_GROUP * seq
    qi = lax.broadcasted_iota(I32, (rows, klen), 0) % seq
    kj = lax.broadcasted_iota(I32, (rows, klen), 1)
    dist = qi - (kj - WINDOW)
    valid = (dist >= 0) & (dist <= WINDOW)
    distf = dist.astype(F32)
    for b in range(bt):
        kc = jnp.concatenate([kb_ref[b], k_ref[b]], axis=0)
        vc = jnp.concatenate([vb_ref[b], v_ref[b]], axis=0)
        ko_ref[b] = kc[seq:]
        vo_ref[b] = vc[seq:]
        q_all = q_ref[b]
        outs = []
        for kh in range(ATT_KV_HEADS):
            outs += _attend_kv_head(q_all, kc, vc, kh, distf, valid, sink_ref)
        o_ref[b] = jnp.concatenate(outs, axis=-1)


def _swa_sample(q, k, v, k_buf, v_buf, sinks):
    batch, seq = q.shape[0], q.shape[1]
    bt = _row_tile(batch, 8)
    blk = lambda w, r: pl.BlockSpec((bt, r, w), lambda i: (i, 0, 0))
    return pl.pallas_call(
        functools.partial(_swa_sample_kernel, bt=bt, seq=seq), name="swa_sample", grid=(batch // bt,),
        out_shape=[jax.ShapeDtypeStruct((batch, seq, ATT_Q), F32),
                   jax.ShapeDtypeStruct((batch, WINDOW, ATT_KV), F32),
                   jax.ShapeDtypeStruct((batch, WINDOW, ATT_KV), F32)],
        in_specs=[pl.BlockSpec(memory_space=pltpu.SMEM), blk(ATT_Q, seq), blk(ATT_KV, seq), blk(ATT_KV, seq),
                  blk(ATT_KV, WINDOW), blk(ATT_KV, WINDOW)],
        out_specs=[blk(ATT_Q, seq), blk(ATT_KV, WINDOW), blk(ATT_KV, WINDOW)],
        compiler_params=_cparams("parallel"))(sinks, q, k, v, k_buf, v_buf)


def _split_bf16(a):
    hi = a.astype(BF16)
    return hi, (a - hi.astype(F32)).astype(BF16)


MXU_DEPTH = 256


def _dot3(a, b, dims=_NN):
    (ca,), (cb,) = dims[0]
    depth = a.shape[ca]
    ah, al = _split_bf16(a)
    bh, bl = _split_bf16(b)
    f = lambda x, y: lax.dot_general(x, y, dims, preferred_element_type=F32)
    if 3 * depth <= MXU_DEPTH:
        return f(jnp.concatenate([ah, al, ah], axis=ca), jnp.concatenate([bh, bh, bl], axis=cb))
    return f(jnp.concatenate([ah, al], axis=ca), jnp.concatenate([bh, bh], axis=cb)) + f(ah, bl)


def _gdn_kernel(alog_ref, dtb_ref, dqkv_ref, z_ref, ab_ref, cw_ref, nrm_ref, pre_ref, s0_ref,
                o_ref, sfin_ref, tail_ref, state_ref, *, c, g):
    n = pl.program_id(1)
    ngrp = DN_HEADS // g
    r = g * c
    hd = HEAD_DIM

    @pl.when(n == 0)
    def _():
        tail_ref[...] = jnp.concatenate([jnp.zeros((8 - (CONV_W - 1), DN_QKV), F32), pre_ref[0]], axis=0)
        for gi in range(ngrp):
            state_ref[gi] = jnp.concatenate([s0_ref[0, gi * g + j] for j in range(g)], axis=1)

    x = dqkv_ref[...]
    hist = jnp.concatenate([tail_ref[...], x], axis=0)
    tail_ref[...] = hist[c:]
    cw = cw_ref[...]
    y = x * cw[CONV_W - 1:CONV_W]
    for d in range(1, CONV_W):
        y = y + hist[8 - d:8 - d + c] * cw[CONV_W - 1 - d:CONV_W - d]
    qkv = y * jax.nn.sigmoid(y)
    z = z_ref[...]
    ab = ab_ref[...]
    beta_all = jax.nn.sigmoid(ab[:, DN_HEADS:2 * DN_HEADS])
    g_all = -jnp.exp(alog_ref[...]) * jax.nn.softplus(ab[:, 0:DN_HEADS] + dtb_ref[...])

    ri = lax.broadcasted_iota(I32, (r, r), 0)
    ci = lax.broadcasted_iota(I32, (r, r), 1)
    same = (ri // c) == (ci // c)
    lower = same & (ri >= ci)
    strict = same & (ri > ci)
    tri = lower.astype(F32)
    blk = same.astype(F32)
    eye = (ri == ci).astype(F32)
    rowhead = lax.broadcasted_iota(I32, (r, hd), 0) // c

    def stack(a, gi, base):
        return jnp.concatenate([a[:, base + (gi * g + j) * hd:base + (gi * g + j + 1) * hd]
                                for j in range(g)], axis=0)

    def column(a, gi):
        return jnp.concatenate([a[:, gi * g + j:gi * g + j + 1] for j in range(g)], axis=0)

    def own_block(m):
        acc = jnp.where(rowhead == 0, m[:, 0:hd], 0.0)
        for j in range(1, g):
            acc = acc + jnp.where(rowhead == j, m[:, j * hd:(j + 1) * hd], 0.0)
        return acc

    grp = []
    for gi in range(ngrp):
        q = stack(qkv, gi, 0)
        k = stack(qkv, gi, DN_WIDTH)
        v = stack(qkv, gi, 2 * DN_WIDTH)
        q = q * lax.rsqrt(jnp.sum(q * q, axis=-1, keepdims=True) + EPS) * (hd ** -0.5)
        k = k * lax.rsqrt(jnp.sum(k * k, axis=-1, keepdims=True) + EPS)
        beta = column(beta_all, gi)
        gr = column(g_all, gi)
        gcol = _fdot(tri, gr)
        grow = lax.dot_general(gr, tri, (((0,), (1,)), ((), ())), precision=HIGHEST,
                               preferred_element_type=F32)
        glast = _fdot(blk, gr)
        decay = jnp.exp(jnp.where(lower, gcol - grow, NEG_INF))
        kb = k * beta
        lm = jnp.where(strict, _dot3(kb, k, _NT) * decay, 0.0)
        grp.append(dict(q=q, k=k, v=v, beta=beta, gcol=gcol, glast=glast, decay=decay, kb=kb, lm=lm))

    pair = ((ri // 2) == (ci // 2)) & (ri % 2 == 1) & (ci % 2 == 0)
    invs = [eye - jnp.where(pair, grp[gi]["lm"], 0.0) for gi in range(ngrp)]
    s = 2
    while s < c:
        off = ((ri // (2 * s)) == (ci // (2 * s))) & ((ri // s) % 2 == 1) & ((ci // s) % 2 == 0)
        tmp = [_dot3(invs[gi], jnp.where(off, grp[gi]["lm"], 0.0)) for gi in range(ngrp)]
        invs = [invs[gi] - _dot3(tmp[gi], invs[gi]) for gi in range(ngrp)]
        s *= 2

    outs = [None] * DN_HEADS
    for gi in range(ngrp):
        d = grp[gi]
        eg = jnp.exp(d["gcol"])
        sol = _dot3(invs[gi], jnp.concatenate([d["v"] * d["beta"], d["kb"] * eg], axis=1))
        u = sol[:, :hd]
        w = sol[:, hd:]
        qk = _dot3(d["q"], d["k"], _NT) * d["decay"]
        ktail = d["k"] * jnp.exp(d["glast"] - d["gcol"])
        qhead = d["q"] * eg
        st = state_ref[gi]
        vnew = u - own_block(_dot3(w, st))
        o = own_block(_dot3(qhead, st)) + _dot3(qk, vnew)
        vblk = jnp.concatenate([jnp.where(rowhead == j, vnew, 0.0) for j in range(g)], axis=1)
        elast = jnp.exp(d["glast"])
        keep = jnp.concatenate([jnp.broadcast_to(elast[j * c:j * c + 1, :], (1, hd)) for j in range(g)], axis=1)
        state_ref[gi] = st * keep + _dot3(ktail, vblk, _TN)
        zs = stack(z, gi, 0)
        o = _rms(o, nrm_ref[...]) * (zs * jax.nn.sigmoid(zs))
        for j in range(g):
            outs[gi * g + j] = o[j * c:(j + 1) * c]
    o_ref[...] = jnp.concatenate(outs, axis=-1)

    @pl.when(n == pl.num_programs(1) - 1)
    def _():
        for gi in range(ngrp):
            for j in range(g):
                sfin_ref[0, gi * g + j] = state_ref[gi][:, j * hd:(j + 1) * hd]


def _gdn(dqkv, z, ab, conv_prefix, s0, conv_w, a_log, dt_bias, dn_norm, batch, seq):
    c = min(DN_CHUNK, seq)
    nchunk = seq // c
    assert seq % c == 0 and c % 8 == 0
    g = min(DN_HEADS, max(1, 128 // c))
    tok = lambda b, n: (b * nchunk + n, 0)
    fixed = lambda b, n: (0, 0)
    return pl.pallas_call(
        functools.partial(_gdn_kernel, c=c, g=g), name="gdn", grid=(batch, nchunk),
        out_shape=[jax.ShapeDtypeStruct((batch * seq, DN_WIDTH), F32),
                   jax.ShapeDtypeStruct((batch, DN_HEADS, HEAD_DIM, HEAD_DIM), F32)],
        in_specs=[pl.BlockSpec((1, DN_HEADS), fixed), pl.BlockSpec((1, DN_HEADS), fixed),
                  pl.BlockSpec((c, DN_QKV), tok), pl.BlockSpec((c, DN_WIDTH), tok),
                  pl.BlockSpec((c, 2 * DN_HEADS), tok),
                  pl.BlockSpec((CONV_W, DN_QKV), fixed), pl.BlockSpec((1, HEAD_DIM), fixed),
                  pl.BlockSpec((1, CONV_W - 1, DN_QKV), lambda b, n: (b, 0, 0)),
                  pl.BlockSpec((1, DN_HEADS, HEAD_DIM, HEAD_DIM), lambda b, n: (b, 0, 0, 0))],
        out_specs=[pl.BlockSpec((c, DN_WIDTH), tok),
                   pl.BlockSpec((1, DN_HEADS, HEAD_DIM, HEAD_DIM), lambda b, n: (b, 0, 0, 0))],
        scratch_shapes=[pltpu.VMEM((8, DN_QKV), F32),
                        pltpu.VMEM((DN_HEADS // g, HEAD_DIM, g * HEAD_DIM), F32)],
        compiler_params=_cparams("parallel", "arbitrary"))(
            a_log.reshape(1, DN_HEADS), dt_bias.reshape(1, DN_HEADS), dqkv, z, ab, conv_w,
            dn_norm.reshape(1, HEAD_DIM), conv_prefix, s0)


def _outproj_kernel(x_ref, att_ref, dn_ref, wo_ref, g_ref, wq_ref, h_ref, c_ref, q_ref):
    mix = jnp.concatenate([att_ref[...], dn_ref[...]], axis=-1)
    h = x_ref[...] + jnp.dot(mix.astype(BF16), wo_ref[...], preferred_element_type=F32)
    h_ref[...] = h
    cn = _rms(h, g_ref[...])
    c_ref[...] = cn
    q_ref[...] = jnp.dot(cn.astype(BF16), wq_ref[...], preferred_element_type=F32)


def _outproj(x, att, dn, w_out, g_ffn, wq):
    t = x.shape[0]
    tm = _row_tile(t, 512)
    row = lambda i: (i, 0)
    fixed = lambda i: (0, 0)
    qw = PEER_HEADS * PEER_QDIM
    return pl.pallas_call(
        _outproj_kernel, name="outproj", grid=(t // tm,),
        out_shape=[jax.ShapeDtypeStruct((t, D_MODEL), F32), jax.ShapeDtypeStruct((t, D_MODEL), F32),
                   jax.ShapeDtypeStruct((t, qw), F32)],
        in_specs=[pl.BlockSpec((tm, D_MODEL), row), pl.BlockSpec((tm, ATT_Q), row),
                  pl.BlockSpec((tm, DN_WIDTH), row), pl.BlockSpec((D_MODEL, D_MODEL), fixed),
                  pl.BlockSpec((1, D_MODEL), fixed), pl.BlockSpec((D_MODEL, qw), fixed)],
        out_specs=[pl.BlockSpec((tm, D_MODEL), row), pl.BlockSpec((tm, D_MODEL), row),
                   pl.BlockSpec((tm, qw), row)],
        compiler_params=_cparams("parallel"))(x, att, dn, w_out, g_ffn, wq)


def _top16_rows(s, iota):
    big = jnp.int32(1 << 30)
    vals, idxs = [], []
    for _ in range(PEER_TOPK):
        m = jnp.max(s, axis=0, keepdims=True)
        i = jnp.min(jnp.where(s == m, iota, big), axis=0, keepdims=True)
        vals.append(m)
        idxs.append(i)
        s = jnp.where(iota == i, NEG_INF, s)
    return jnp.concatenate(vals, axis=0), jnp.concatenate(idxs, axis=0)


def _candidates(a, b, row8):
    t = a.shape[1]
    lo = row8 < 4
    bc = lambda r, n: jnp.broadcast_to(a[r:r + 1], (n, t))
    b8 = b[0:8]
    b44 = jnp.where(lo, b8, pltpu.roll(b8, 4, 0))
    parts = [bc(0, 8), bc(0, 8), bc(1, 8), bc(2, 8), bc(3, 8),
             jnp.where(lo, bc(4, 8), bc(5, 8)), jnp.where(lo, bc(6, 8), bc(7, 8)), a[8:16]]
    others = [b[0:8], b[8:16], b8, b8, b8, b44, b44, jnp.broadcast_to(b[0:1], (8, t))]
    return parts, others


def _flat_index_column():
    r = lax.broadcasted_iota(I32, (8, 1), 0)
    lo = r < 4
    cols = [r, 8 + r, 16 + r, 32 + r, 48 + r, jnp.where(lo, 64 + r, 80 + r - 4),
            jnp.where(lo, 96 + r, 112 + r - 4), (8 + r) * 16]
    return jnp.concatenate(cols, axis=0)


def _route_kernel(q_ref, k1_ref, k2_ref, idx_ref, gate_ref, *, tb):
    iota = lax.broadcasted_iota(I32, (N_KEYS, tb), 0)
    row8 = lax.broadcasted_iota(I32, (8, tb), 0)
    flat = _flat_index_column()
    big = jnp.int32(1 << 30)
    half = PEER_QDIM // 2
    for h in range(PEER_HEADS):
        q1 = q_ref[:, h * PEER_QDIM:h * PEER_QDIM + half]
        q2 = q_ref[:, h * PEER_QDIM + half:(h + 1) * PEER_QDIM]
        nt = (((1,), (1,)), ((), ()))
        s1 = lax.dot_general(k1_ref[h], q1.astype(BF16), nt, preferred_element_type=F32)
        s2 = lax.dot_general(k2_ref[h], q2.astype(BF16), nt, preferred_element_type=F32)
        v1, i1 = _top16_rows(s1, iota)
        v2, i2 = _top16_rows(s2, iota)
        pa, pb = _candidates(v1, v2, row8)
        cand = jnp.concatenate([x + y for x, y in zip(pa, pb)], axis=0)
        ea, eb = _candidates(i1, i2, row8)
        expert = jnp.concatenate([x * N_KEYS + y for x, y in zip(ea, eb)], axis=0)
        scs, exs = [], []
        for _ in range(PEER_TOPK):
            m = jnp.max(cand, axis=0, keepdims=True)
            c = jnp.min(jnp.where(cand == m, flat, big), axis=0, keepdims=True)
            hit = flat == c
            exs.append(jnp.max(jnp.where(hit, expert, -1), axis=0, keepdims=True))
            scs.append(m)
            cand = jnp.where(hit, NEG_INF, cand)
        sc = jnp.concatenate(scs, axis=0)
        e = jnp.exp(sc - sc[0:1])
        gate_ref[h * PEER_TOPK:(h + 1) * PEER_TOPK, :] = e / jnp.sum(e, axis=0, keepdims=True)
        idx_ref[h * PEER_TOPK:(h + 1) * PEER_TOPK, :] = jnp.concatenate(exs, axis=0)


def _route(q, keys1, keys2):
    t = q.shape[0]
    tb = _row_tile(t, 256)
    kspec = pl.BlockSpec((PEER_HEADS, N_KEYS, PEER_QDIM // 2), lambda i: (0, 0, 0))
    return pl.pallas_call(
        functools.partial(_route_kernel, tb=tb), name="peer_route", grid=(t // tb,),
        out_shape=[jax.ShapeDtypeStruct((PEER_SLOTS, t), I32), jax.ShapeDtypeStruct((PEER_SLOTS, t), F32)],
        in_specs=[pl.BlockSpec((tb, PEER_HEADS * PEER_QDIM), lambda i: (i, 0)), kspec, kspec],
        out_specs=[pl.BlockSpec((PEER_SLOTS, tb), lambda i: (0, i)),
                   pl.BlockSpec((PEER_SLOTS, tb), lambda i: (0, i))],
        compiler_params=_cparams("parallel"))(q, keys1, keys2)


GATHER_SLOTS = 3


def _peer_apply_kernel(idx_ref, c_ref, h_ref, gt_ref, uv_ref, o_ref, buf, sem, *, tb):
    ahead = GATHER_SLOTS - 1

    def issue(t, slot):
        for k in range(PEER_SLOTS):
            e = idx_ref[t, k]
            pltpu.make_async_copy(uv_ref.at[e], buf.at[slot, pl.ds(k, 1), :],
                                  sem.at[slot]).start(priority=k % 2)

    def wait(slot):
        pltpu.make_async_copy(buf.at[slot], buf.at[slot], sem.at[slot]).wait()

    lane = lax.broadcasted_iota(I32, (PEER_SLOTS, tb), 1)

    def apply(t, slot):
        x = c_ref[pl.ds(t, 1), :]
        p = buf[slot, :, 0:D_MODEL] * x
        acc = p[:, 0:128]
        for j in range(1, D_MODEL // 128):
            acc = acc + p[:, j * 128:(j + 1) * 128]
        s = jnp.sum(acc, axis=-1, keepdims=True)
        act = 0.5 * s * (1.0 + lax.erf(s * (1.0 / math.sqrt(2.0))))
        gate = jnp.sum(jnp.where(lane == t, gt_ref[...], 0.0), axis=-1, keepdims=True)
        w = gate * act
        y = jnp.sum(w * buf[slot, :, D_MODEL:2 * D_MODEL], axis=0, keepdims=True)
        o_ref[pl.ds(t, 1), :] = h_ref[pl.ds(t, 1), :] + y

    for t0 in range(ahead):
        issue(t0, t0)

    def body(i, carry):
        for j in range(GATHER_SLOTS):
            t = i * GATHER_SLOTS + j
            issue(t + ahead, (j + ahead) % GATHER_SLOTS)
            wait(j)
            apply(t, j)
        return carry

    nmain = (tb - ahead) // GATHER_SLOTS
    lax.fori_loop(0, nmain, body, 0)
    for t in range(nmain * GATHER_SLOTS, tb):
        if t + ahead < tb:
            issue(t + ahead, (t + ahead) % GATHER_SLOTS)
        wait(t % GATHER_SLOTS)
        apply(t, t % GATHER_SLOTS)


def _peer_apply(idx, c, h, gate_t, uv):
    t = c.shape[0]
    tb = _row_tile(t, 256)
    row = lambda i: (i, 0)
    return pl.pallas_call(
        functools.partial(_peer_apply_kernel, tb=tb), name="peer_apply", grid=(t // tb,),
        out_shape=jax.ShapeDtypeStruct((t, D_MODEL), F32),
        in_specs=[pl.BlockSpec((tb, PEER_SLOTS), row, memory_space=pltpu.SMEM),
                  pl.BlockSpec((tb, D_MODEL), row), pl.BlockSpec((tb, D_MODEL), row),
                  pl.BlockSpec((PEER_SLOTS, tb), lambda i: (0, i)),
                  pl.BlockSpec(memory_space=pl.ANY)],
        out_specs=pl.BlockSpec((tb, D_MODEL), row),
        scratch_shapes=[pltpu.VMEM((GATHER_SLOTS, PEER_SLOTS, 2 * D_MODEL), F32),
                        pltpu.SemaphoreType.DMA((GATHER_SLOTS,))],
        compiler_params=_cparams("arbitrary"))(idx, c, h, gate_t, uv)


def _final_kernel(h_ref, p_ref, gple_ref, wg_ref, wp_ref, gfin_ref, y_ref):
    h = h_ref[...]
    e = _rms(h, gple_ref[...])
    gate = jax.nn.sigmoid(jnp.dot(e.astype(BF16), wg_ref[...], preferred_element_type=F32))
    up = jnp.dot(p_ref[...].astype(BF16), wp_ref[...], preferred_element_type=F32)
    y_ref[...] = _rms(h + up * gate, gfin_ref[...])


def _final(h, p, g_ple, w_gate, w_ple, g_final):
    t = h.shape[0]
    tm = _row_tile(t, 512)
    row = lambda i: (i, 0)
    fixed = lambda i: (0, 0)
    return pl.pallas_call(
        _final_kernel, name="ple_final", grid=(t // tm,),
        out_shape=jax.ShapeDtypeStruct((t, D_MODEL), F32),
        in_specs=[pl.BlockSpec((tm, D_MODEL), row), pl.BlockSpec((tm, PLE_DIM), row),
                  pl.BlockSpec((1, D_MODEL), fixed), pl.BlockSpec((D_MODEL, D_MODEL), fixed),
                  pl.BlockSpec((PLE_DIM, D_MODEL), fixed), pl.BlockSpec((1, D_MODEL), fixed)],
        out_specs=pl.BlockSpec((tm, D_MODEL), row),
        compiler_params=_cparams("parallel"))(h, p, g_ple, w_gate, w_ple, g_final)


def _layer(x, p, kv_buf, conv_prefix, s0, wts, batch, seq):
    t = batch * seq
    q, k, v, dqkv, z, ab = _inproj(x, wts["norm_mix"], wts["w_in"])
    if kv_buf is None:
        att = _swa_prompt(q, k, v, wts["sinks"], batch, seq)
        k_new = k.reshape(batch, seq, ATT_KV)[:, -WINDOW:]
        v_new = v.reshape(batch, seq, ATT_KV)[:, -WINDOW:]
    else:
        att, k_new, v_new = _swa_sample(q.reshape(batch, seq, ATT_Q), k.reshape(batch, seq, ATT_KV),
                                        v.reshape(batch, seq, ATT_KV), kv_buf[0], kv_buf[1], wts["sinks"])
        att = att.reshape(t, ATT_Q)
    dn, s_new = _gdn(dqkv, z, ab, conv_prefix, s0, wts["conv_w"], wts["a_log"], wts["dt_bias"],
                     wts["dn_norm"], batch, seq)
    conv_new = dqkv.reshape(batch, seq, DN_QKV)[:, -(CONV_W - 1):]
    h, c, pq = _outproj(x, att, dn, wts["w_out"], wts["norm_ffn"], wts["peer_wq"])
    idx_t, gate_t = _route(pq, wts["keys1"], wts["keys2"])
    h2 = _peer_apply(idx_t.T, c, h, gate_t, wts["uv"])
    y = _final(h2, p, wts["norm_ple"], wts["ple_gate"], wts["ple_in"], wts["norm_final"])
    return y, k_new, v_new, conv_new, s_new


def kernel(x_prompt, x_sample, p_prompt, p_sample, cache_swa_k, cache_swa_v, state_conv, state_delta, norm_mix, w_in, conv_w, attn_sinks, dn_a_log, dn_dt_bias, dn_norm, w_out, norm_ffn, peer_wq, peer_keys1, peer_keys2, peer_u, peer_v, norm_ple, ple_in, ple_gate, norm_final):
    depth = w_in.shape[0]
    assert depth == 1
    bp, lp = x_prompt.shape[0], x_prompt.shape[1]
    bs, ls = x_sample.shape[0], x_sample.shape[1]
    l = 0
    wts = dict(
        norm_mix=norm_mix[l].reshape(1, D_MODEL),
        w_in=jnp.pad(w_in[l], ((0, 0), (0, IN_WIDTH_PADDED - IN_WIDTH))).astype(BF16),
        conv_w=conv_w[l], sinks=attn_sinks[l], a_log=dn_a_log[l], dt_bias=dn_dt_bias[l], dn_norm=dn_norm[l],
        w_out=w_out[l].astype(BF16), norm_ffn=norm_ffn[l].reshape(1, D_MODEL), peer_wq=peer_wq[l].astype(BF16),
        keys1=peer_keys1[l].astype(BF16), keys2=peer_keys2[l].astype(BF16),
        uv=jnp.concatenate([peer_u[l][:, None, :], peer_v[l][:, None, :]], axis=-1),
        norm_ple=norm_ple[l].reshape(1, D_MODEL), ple_in=ple_in[l].astype(BF16),
        ple_gate=ple_gate[l].astype(BF16), norm_final=norm_final.reshape(1, D_MODEL))
    conv0 = jnp.zeros((bp, CONV_W - 1, DN_QKV), F32)
    s0 = jnp.zeros((bp, DN_HEADS, HEAD_DIM, HEAD_DIM), F32)
    yp, kp, vp, cp, sp = _layer(x_prompt.reshape(bp * lp, D_MODEL), p_prompt[l].reshape(bp * lp, PLE_DIM),
                                None, conv0, s0, wts, bp, lp)
    kv_buf = (cache_swa_k[l].reshape(bs, WINDOW, ATT_KV), cache_swa_v[l].reshape(bs, WINDOW, ATT_KV))
    ys, kn, vn, cn, sn = _layer(x_sample.reshape(bs * ls, D_MODEL), p_sample[l].reshape(bs * ls, PLE_DIM),
                                kv_buf, state_conv[l], state_delta[l], wts, bs, ls)
    kvshape = lambda b: (1, b, WINDOW, ATT_KV_HEADS, HEAD_DIM)
    return (yp.reshape(bp, lp, D_MODEL), ys.reshape(bs, ls, D_MODEL),
            kp.reshape(kvshape(bp)), vp.reshape(kvshape(bp)), cp[None], sp[None],
            kn.reshape(kvshape(bs)), vn.reshape(kvshape(bs)), cn[None], sn[None])
```

```python
import functools
import math

import jax
import jax.numpy as jnp
from jax import lax
from jax.experimental import pallas as pl
from jax.experimental.pallas import tpu as pltpu

F32 = jnp.float32
BF16 = jnp.bfloat16
I32 = jnp.int32
HIGHEST = lax.Precision.HIGHEST

D_MODEL = 1024
HEAD_DIM = 64
ATT_HEADS = 8
ATT_KV_HEADS = 2
ATT_GROUP = 4
WINDOW = 128
ATT_Q = ATT_HEADS * HEAD_DIM
ATT_KV = ATT_KV_HEADS * HEAD_DIM
DN_HEADS = 8
DN_WIDTH = DN_HEADS * HEAD_DIM
DN_QKV = 3 * DN_WIDTH
CONV_W = 4
DN_CHUNK = 64
IN_WIDTH = ATT_Q + 2 * ATT_KV + DN_QKV + DN_WIDTH + 2 * DN_HEADS
IN_WIDTH_PADDED = 2944
N_KEYS = 128
PEER_HEADS = 8
PEER_QDIM = 256
PEER_TOPK = 16
PEER_SLOTS = PEER_HEADS * PEER_TOPK
PLE_DIM = 256
EPS = 1e-6
NEG_INF = float("-inf")

VMEM_LIMIT = 48 * 1024 * 1024


def _cparams(*sem):
    return pltpu.CompilerParams(dimension_semantics=sem, vmem_limit_bytes=VMEM_LIMIT)


def _rms(x, g):
    return x * lax.rsqrt(jnp.mean(x * x, axis=-1, keepdims=True) + EPS) * g


def _bdot(a, b):
    return jnp.dot(a.astype(BF16), b.astype(BF16), preferred_element_type=F32)


def _fdot(a, b):
    return jnp.dot(a, b, precision=HIGHEST, preferred_element_type=F32)


_NN = (((1,), (0,)), ((), ()))
_NT = (((1,), (1,)), ((), ()))
_TN = (((0,), (0,)), ((), ()))


def _row_tile(t, target):
    tile = min(t, target)
    assert t % tile == 0, (t, tile)
    return tile


def _inproj_kernel(x_ref, g_ref, w_ref, q_ref, k_ref, v_ref, dqkv_ref, z_ref, ab_ref):
    a = _rms(x_ref[...], g_ref[...])
    p = jnp.dot(a.astype(BF16), w_ref[...], preferred_element_type=F32)
    c0, c1, c2, c3, c4 = ATT_Q, ATT_Q + ATT_KV, ATT_Q + 2 * ATT_KV, ATT_Q + 2 * ATT_KV + DN_QKV, \
        ATT_Q + 2 * ATT_KV + DN_QKV + DN_WIDTH
    q_ref[...] = p[:, :c0]
    k_ref[...] = p[:, c0:c1]
    v_ref[...] = p[:, c1:c2]
    dqkv_ref[...] = p[:, c2:c3]
    z_ref[...] = p[:, c3:c4]
    ab_ref[...] = p[:, c4:c4 + 2 * DN_HEADS]


def _inproj(x, g, w_pad):
    t = x.shape[0]
    tm = _row_tile(t, 512)
    widths = (ATT_Q, ATT_KV, ATT_KV, DN_QKV, DN_WIDTH, 2 * DN_HEADS)
    row = lambda i: (i, 0)
    fixed = lambda i: (0, 0)
    return pl.pallas_call(
        _inproj_kernel, name="inproj", grid=(t // tm,),
        out_shape=[jax.ShapeDtypeStruct((t, w), F32) for w in widths],
        in_specs=[pl.BlockSpec((tm, D_MODEL), row), pl.BlockSpec((1, D_MODEL), fixed),
                  pl.BlockSpec((D_MODEL, IN_WIDTH_PADDED), fixed)],
        out_specs=[pl.BlockSpec((tm, w), row) for w in widths],
        compiler_params=_cparams("parallel"))(x, g, w_pad)


def _softmax_sink_pv(s, sink, vband):
    m = jnp.maximum(jnp.max(s, axis=-1, keepdims=True), sink)
    e = jnp.exp(s - m)
    denom = jnp.sum(e, axis=-1, keepdims=True) + jnp.exp(sink - m)
    return _bdot(e / denom, vband)


def _alibi_slope(h):
    return 2.0 ** (-8.0 * (h + 1) / ATT_HEADS)


def _group_columns(kh, rows, sink_ref):
    rg = lax.broadcasted_iota(I32, (ATT_GROUP * rows, 1), 0) // rows
    slope = jnp.full((ATT_GROUP * rows, 1), _alibi_slope(kh * ATT_GROUP), F32)
    sink = jnp.full((ATT_GROUP * rows, 1), sink_ref[kh * ATT_GROUP], F32)
    for j in range(1, ATT_GROUP):
        slope = jnp.where(rg == j, _alibi_slope(kh * ATT_GROUP + j), slope)
        sink = jnp.where(rg == j, sink_ref[kh * ATT_GROUP + j], sink)
    return slope, sink


def _attend_kv_head(q_all, kband, vband, kh, dist, valid, sink_ref):
    rows = q_all.shape[0]
    heads = range(kh * ATT_GROUP, (kh + 1) * ATT_GROUP)
    q4 = jnp.concatenate([q_all[:, h * HEAD_DIM:(h + 1) * HEAD_DIM] for h in heads], axis=0)
    k = kband[:, kh * HEAD_DIM:(kh + 1) * HEAD_DIM]
    v = vband[:, kh * HEAD_DIM:(kh + 1) * HEAD_DIM]
    s = lax.dot_general(q4.astype(BF16), k.astype(BF16), _NT, preferred_element_type=F32) * (HEAD_DIM ** -0.5)
    slope, sink = _group_columns(kh, rows, sink_ref)
    s = jnp.where(valid, s - slope * dist, NEG_INF)
    o4 = _softmax_sink_pv(s, sink, v)
    return [o4[j * rows:(j + 1) * rows] for j in range(ATT_GROUP)]


def _swa_prompt_kernel(sink_ref, q_ref, kc_ref, kp_ref, vc_ref, vp_ref, o_ref):
    n = pl.program_id(1)
    rows = ATT_GROUP * WINDOW
    qi = lax.broadcasted_iota(I32, (rows, 2 * WINDOW), 0) % WINDOW
    kj = lax.broadcasted_iota(I32, (rows, 2 * WINDOW), 1)
    dist = WINDOW + qi - kj
    kpos = n * WINDOW - WINDOW + kj
    valid = (dist >= 0) & (dist <= WINDOW) & (kpos >= 0)
    distf = dist.astype(F32)
    kband = jnp.concatenate([kp_ref[...], kc_ref[...]], axis=0)
    vband = jnp.concatenate([vp_ref[...], vc_ref[...]], axis=0)
    q_all = q_ref[...]
    outs = []
    for kh in range(ATT_KV_HEADS):
        outs += _attend_kv_head(q_all, kband, vband, kh, distf, valid, sink_ref)
    o_ref[...] = jnp.concatenate(outs, axis=-1)


def _swa_prompt(q, k, v, sinks, batch, seq):
    nblk = seq // WINDOW
    cur = lambda b, n: (b * nblk + n, 0)
    prev = lambda b, n: (b * nblk + jnp.maximum(n - 1, 0), 0)
    return pl.pallas_call(
        _swa_prompt_kernel, name="swa_prompt", grid=(batch, nblk),
        out_shape=jax.ShapeDtypeStruct((batch * seq, ATT_Q), F32),
        in_specs=[pl.BlockSpec(memory_space=pltpu.SMEM),
                  pl.BlockSpec((WINDOW, ATT_Q), cur),
                  pl.BlockSpec((WINDOW, ATT_KV), cur), pl.BlockSpec((WINDOW, ATT_KV), prev),
                  pl.BlockSpec((WINDOW, ATT_KV), cur), pl.BlockSpec((WINDOW, ATT_KV), prev)],
        out_specs=pl.BlockSpec((WINDOW, ATT_Q), cur),
        compiler_params=_cparams("parallel", "parallel"))(sinks, q, k, k, v, v)


def _swa_sample_kernel(sink_ref, q_ref, k_ref, v_ref, kb_ref, vb_ref, o_ref, ko_ref, vo_ref, *, bt, seq):
    klen = WINDOW + seq
    rows = ATT_GROUP * seq
    qi = lax.broadcasted_iota(I32, (rows, klen), 0) % seq
    kj = lax.broadcasted_iota(I32, (rows, klen), 1)
    dist = qi - (kj - WINDOW)
    valid = (dist >= 0) & (dist <= WINDOW)
    distf = dist.astype(F32)
    for b in range(bt):
        kc = jnp.concatenate([kb_ref[b], k_ref[b]], axis=0)
        vc = jnp.concatenate([vb_ref[b], v_ref[b]], axis=0)
        ko_ref[b] = kc[seq:]
        vo_ref[b] = vc[seq:]
        q_all = q_ref[b]
        outs = []
        for kh in range(ATT_KV_HEADS):
            outs += _attend_kv_head(q_all, kc, vc, kh, distf, valid, sink_ref)
        o_ref[b] = jnp.concatenate(outs, axis=-1)


def _swa_sample(q, k, v, k_buf, v_buf, sinks):
    batch, seq = q.shape[0], q.shape[1]
    bt = _row_tile(batch, 8)
    blk = lambda w, r: pl.BlockSpec((bt, r, w), lambda i: (i, 0, 0))
    return pl.pallas_call(
        functools.partial(_swa_sample_kernel, bt=bt, seq=seq), name="swa_sample", grid=(batch // bt,),
        out_shape=[jax.ShapeDtypeStruct((batch, seq, ATT_Q), F32),
                   jax.ShapeDtypeStruct((batch, WINDOW, ATT_KV), F32),
                   jax.ShapeDtypeStruct((batch, WINDOW, ATT_KV), F32)],
        in_specs=[pl.BlockSpec(memory_space=pltpu.SMEM), blk(ATT_Q, seq), blk(ATT_KV, seq), blk(ATT_KV, seq),
                  blk(ATT_KV, WINDOW), blk(ATT_KV, WINDOW)],
        out_specs=[blk(ATT_Q, seq), blk(ATT_KV, WINDOW), blk(ATT_KV, WINDOW)],
        compiler_params=_cparams("parallel"))(sinks, q, k, v, k_buf, v_buf)


def _split_bf16(a):
    hi = a.astype(BF16)
    return hi, (a - hi.astype(F32)).astype(BF16)


MXU_DEPTH = 256


def _dot3(a, b, dims=_NN):
    (ca,), (cb,) = dims[0]
    depth = a.shape[ca]
    ah, al = _split_bf16(a)
    bh, bl = _split_bf16(b)
    f = lambda x, y: lax.dot_general(x, y, dims, preferred_element_type=F32)
    if 3 * depth <= MXU_DEPTH:
        return f(jnp.concatenate([ah, al, ah], axis=ca), jnp.concatenate([bh, bh, bl], axis=cb))
    return f(jnp.concatenate([ah, al], axis=ca), jnp.concatenate([bh, bh], axis=cb)) + f(ah, bl)


def _gdn_kernel(alog_ref, dtb_ref, dqkv_ref, z_ref, ab_ref, cw_ref, nrm_ref, pre_ref, s0_ref,
                o_ref, sfin_ref, tail_ref, state_ref, *, c, g, nsub):
    n = pl.program_id(1)
    ngrp = DN_HEADS // g
    r = g * c
    rows = nsub * c
    hd = HEAD_DIM

    @pl.when(n == 0)
    def _():
        tail_ref[...] = jnp.concatenate([jnp.zeros((8 - (CONV_W - 1), DN_QKV), F32), pre_ref[0]], axis=0)
        for gi in range(ngrp):
            state_ref[gi] = jnp.concatenate([s0_ref[0, gi * g + j] for j in range(g)], axis=1)

    x = dqkv_ref[...]
    hist = jnp.concatenate([tail_ref[...], x], axis=0)
    tail_ref[...] = hist[rows:]
    cw = cw_ref[...]
    y = x * cw[CONV_W - 1:CONV_W]
    for d in range(1, CONV_W):
        y = y + hist[8 - d:8 - d + rows] * cw[CONV_W - 1 - d:CONV_W - d]
    qkv = y * jax.nn.sigmoid(y)
    z = z_ref[...]
    ab = ab_ref[...]
    beta_all = jax.nn.sigmoid(ab[:, DN_HEADS:2 * DN_HEADS])
    g_all = -jnp.exp(alog_ref[...]) * jax.nn.softplus(ab[:, 0:DN_HEADS] + dtb_ref[...])

    ri = lax.broadcasted_iota(I32, (r, r), 0)
    ci = lax.broadcasted_iota(I32, (r, r), 1)
    same = (ri // c) == (ci // c)
    lower = same & (ri >= ci)
    strict = same & (ri > ci)
    tri = lower.astype(F32)
    blk = same.astype(F32)
    eye = (ri == ci).astype(F32)
    rowhead = lax.broadcasted_iota(I32, (r, hd), 0) // c

    def stack(a, sub, gi, base):
        return jnp.concatenate([a[sub * c:(sub + 1) * c, base + (gi * g + j) * hd:base + (gi * g + j + 1) * hd]
                                for j in range(g)], axis=0)

    def column(a, sub, gi):
        return jnp.concatenate([a[sub * c:(sub + 1) * c, gi * g + j:gi * g + j + 1] for j in range(g)], axis=0)

    def own_block(m):
        acc = jnp.where(rowhead == 0, m[:, 0:hd], 0.0)
        for j in range(1, g):
            acc = acc + jnp.where(rowhead == j, m[:, j * hd:(j + 1) * hd], 0.0)
        return acc

    keys = [(sub, gi) for sub in range(nsub) for gi in range(ngrp)]
    grp = {}
    for sub, gi in keys:
        q = stack(qkv, sub, gi, 0)
        k = stack(qkv, sub, gi, DN_WIDTH)
        v = stack(qkv, sub, gi, 2 * DN_WIDTH)
        q = q * lax.rsqrt(jnp.sum(q * q, axis=-1, keepdims=True) + EPS) * (hd ** -0.5)
        k = k * lax.rsqrt(jnp.sum(k * k, axis=-1, keepdims=True) + EPS)
        beta = column(beta_all, sub, gi)
        gr = column(g_all, sub, gi)
        gcol = _fdot(tri, gr)
        grow = lax.dot_general(gr, tri, (((0,), (1,)), ((), ())), precision=HIGHEST,
                               preferred_element_type=F32)
        glast = _fdot(blk, gr)
        decay = jnp.exp(jnp.where(lower, gcol - grow, NEG_INF))
        kb = k * beta
        lm = jnp.where(strict, _dot3(kb, k, _NT) * decay, 0.0)
        grp[sub, gi] = dict(q=q, k=k, v=v, beta=beta, gcol=gcol, glast=glast, decay=decay, kb=kb, lm=lm)

    pair = ((ri // 2) == (ci // 2)) & (ri % 2 == 1) & (ci % 2 == 0)
    invs = {key: eye - jnp.where(pair, grp[key]["lm"], 0.0) for key in keys}
    s = 2
    while s < c:
        off = ((ri // (2 * s)) == (ci // (2 * s))) & ((ri // s) % 2 == 1) & ((ci // s) % 2 == 0)
        tmp = {key: _dot3(invs[key], jnp.where(off, grp[key]["lm"], 0.0)) for key in keys}
        invs = {key: invs[key] - _dot3(tmp[key], invs[key]) for key in keys}
        s *= 2

    for sub in range(nsub):
        outs = [None] * DN_HEADS
        for gi in range(ngrp):
            d = grp[sub, gi]
            eg = jnp.exp(d["gcol"])
            sol = _dot3(invs[sub, gi], jnp.concatenate([d["v"] * d["beta"], d["kb"] * eg], axis=1))
            u = sol[:, :hd]
            w = sol[:, hd:]
            qk = _dot3(d["q"], d["k"], _NT) * d["decay"]
            ktail = d["k"] * jnp.exp(d["glast"] - d["gcol"])
            qhead = d["q"] * eg
            st = state_ref[gi]
            vnew = u - own_block(_dot3(w, st))
            o = own_block(_dot3(qhead, st)) + _dot3(qk, vnew)
            vblk = jnp.concatenate([jnp.where(rowhead == j, vnew, 0.0) for j in range(g)], axis=1)
            elast = jnp.exp(d["glast"])
            keep = jnp.concatenate([jnp.broadcast_to(elast[j * c:j * c + 1, :], (1, hd)) for j in range(g)],
                                   axis=1)
            state_ref[gi] = st * keep + _dot3(ktail, vblk, _TN)
            zs = stack(z, sub, gi, 0)
            o = _rms(o, nrm_ref[...]) * (zs * jax.nn.sigmoid(zs))
            for j in range(g):
                outs[gi * g + j] = o[j * c:(j + 1) * c]
        o_ref[sub * c:(sub + 1) * c, :] = jnp.concatenate(outs, axis=-1)

    @pl.when(n == pl.num_programs(1) - 1)
    def _():
        for gi in range(ngrp):
            for j in range(g):
                sfin_ref[0, gi * g + j] = state_ref[gi][:, j * hd:(j + 1) * hd]


GDN_CHUNKS_PER_STEP = 2


def _gdn(dqkv, z, ab, conv_prefix, s0, conv_w, a_log, dt_bias, dn_norm, batch, seq):
    c = min(DN_CHUNK, seq)
    nchunk = seq // c
    assert seq % c == 0 and c % 8 == 0
    g = min(DN_HEADS, max(1, 128 // c))
    nsub = GDN_CHUNKS_PER_STEP if nchunk % GDN_CHUNKS_PER_STEP == 0 else 1
    nstep = nchunk // nsub
    rows = nsub * c
    tok = lambda b, n: (b * nstep + n, 0)
    fixed = lambda b, n: (0, 0)
    return pl.pallas_call(
        functools.partial(_gdn_kernel, c=c, g=g, nsub=nsub), name="gdn", grid=(batch, nstep),
        out_shape=[jax.ShapeDtypeStruct((batch * seq, DN_WIDTH), F32),
                   jax.ShapeDtypeStruct((batch, DN_HEADS, HEAD_DIM, HEAD_DIM), F32)],
        in_specs=[pl.BlockSpec((1, DN_HEADS), fixed), pl.BlockSpec((1, DN_HEADS), fixed),
                  pl.BlockSpec((rows, DN_QKV), tok), pl.BlockSpec((rows, DN_WIDTH), tok),
                  pl.BlockSpec((rows, 2 * DN_HEADS), tok),
                  pl.BlockSpec((CONV_W, DN_QKV), fixed), pl.BlockSpec((1, HEAD_DIM), fixed),
                  pl.BlockSpec((1, CONV_W - 1, DN_QKV), lambda b, n: (b, 0, 0)),
                  pl.BlockSpec((1, DN_HEADS, HEAD_DIM, HEAD_DIM), lambda b, n: (b, 0, 0, 0))],
        out_specs=[pl.BlockSpec((rows, DN_WIDTH), tok),
                   pl.BlockSpec((1, DN_HEADS, HEAD_DIM, HEAD_DIM), lambda b, n: (b, 0, 0, 0))],
        scratch_shapes=[pltpu.VMEM((8, DN_QKV), F32),
                        pltpu.VMEM((DN_HEADS // g, HEAD_DIM, g * HEAD_DIM), F32)],
        compiler_params=_cparams("parallel", "arbitrary"))(
            a_log.reshape(1, DN_HEADS), dt_bias.reshape(1, DN_HEADS), dqkv, z, ab, conv_w,
            dn_norm.reshape(1, HEAD_DIM), conv_prefix, s0)


def _outproj_kernel(x_ref, att_ref, dn_ref, wo_ref, g_ref, wq_ref, h_ref, c_ref, q_ref):
    mix = jnp.concatenate([att_ref[...], dn_ref[...]], axis=-1)
    h = x_ref[...] + jnp.dot(mix.astype(BF16), wo_ref[...], preferred_element_type=F32)
    h_ref[...] = h
    cn = _rms(h, g_ref[...])
    c_ref[...] = cn
    q_ref[...] = jnp.dot(cn.astype(BF16), wq_ref[...], preferred_element_type=F32)


def _outproj(x, att, dn, w_out, g_ffn, wq):
    t = x.shape[0]
    tm = _row_tile(t, 512)
    row = lambda i: (i, 0)
    fixed = lambda i: (0, 0)
    qw = PEER_HEADS * PEER_QDIM
    return pl.pallas_call(
        _outproj_kernel, name="outproj", grid=(t // tm,),
        out_shape=[jax.ShapeDtypeStruct((t, D_MODEL), F32), jax.ShapeDtypeStruct((t, D_MODEL), F32),
                   jax.ShapeDtypeStruct((t, qw), F32)],
        in_specs=[pl.BlockSpec((tm, D_MODEL), row), pl.BlockSpec((tm, ATT_Q), row),
                  pl.BlockSpec((tm, DN_WIDTH), row), pl.BlockSpec((D_MODEL, D_MODEL), fixed),
                  pl.BlockSpec((1, D_MODEL), fixed), pl.BlockSpec((D_MODEL, qw), fixed)],
        out_specs=[pl.BlockSpec((tm, D_MODEL), row), pl.BlockSpec((tm, D_MODEL), row),
                   pl.BlockSpec((tm, qw), row)],
        compiler_params=_cparams("parallel"))(x, att, dn, w_out, g_ffn, wq)


def _top16_rows(s, iota):
    big = jnp.int32(1 << 30)
    vals, idxs = [], []
    for _ in range(PEER_TOPK):
        m = jnp.max(s, axis=0, keepdims=True)
        i = jnp.min(jnp.where(s == m, iota, big), axis=0, keepdims=True)
        vals.append(m)
        idxs.append(i)
        s = jnp.where(iota == i, NEG_INF, s)
    return jnp.concatenate(vals, axis=0), jnp.concatenate(idxs, axis=0)


def _candidates(a, b, row8):
    t = a.shape[1]
    lo = row8 < 4
    bc = lambda r, n: jnp.broadcast_to(a[r:r + 1], (n, t))
    b8 = b[0:8]
    b44 = jnp.where(lo, b8, pltpu.roll(b8, 4, 0))
    parts = [bc(0, 8), bc(0, 8), bc(1, 8), bc(2, 8), bc(3, 8),
             jnp.where(lo, bc(4, 8), bc(5, 8)), jnp.where(lo, bc(6, 8), bc(7, 8)), a[8:16]]
    others = [b[0:8], b[8:16], b8, b8, b8, b44, b44, jnp.broadcast_to(b[0:1], (8, t))]
    return parts, others


def _flat_index_column():
    r = lax.broadcasted_iota(I32, (8, 1), 0)
    lo = r < 4
    cols = [r, 8 + r, 16 + r, 32 + r, 48 + r, jnp.where(lo, 64 + r, 80 + r - 4),
            jnp.where(lo, 96 + r, 112 + r - 4), (8 + r) * 16]
    return jnp.concatenate(cols, axis=0)


def _route_kernel(q_ref, k1_ref, k2_ref, idx_ref, gate_ref, *, tb):
    iota = lax.broadcasted_iota(I32, (N_KEYS, tb), 0)
    row8 = lax.broadcasted_iota(I32, (8, tb), 0)
    flat = _flat_index_column()
    big = jnp.int32(1 << 30)
    half = PEER_QDIM // 2
    for h in range(PEER_HEADS):
        q1 = q_ref[:, h * PEER_QDIM:h * PEER_QDIM + half]
        q2 = q_ref[:, h * PEER_QDIM + half:(h + 1) * PEER_QDIM]
        nt = (((1,), (1,)), ((), ()))
        s1 = lax.dot_general(k1_ref[h], q1.astype(BF16), nt, preferred_element_type=F32)
        s2 = lax.dot_general(k2_ref[h], q2.astype(BF16), nt, preferred_element_type=F32)
        v1, i1 = _top16_rows(s1, iota)
        v2, i2 = _top16_rows(s2, iota)
        pa, pb = _candidates(v1, v2, row8)
        cand = jnp.concatenate([x + y for x, y in zip(pa, pb)], axis=0)
        ea, eb = _candidates(i1, i2, row8)
        expert = jnp.concatenate([x * N_KEYS + y for x, y in zip(ea, eb)], axis=0)
        scs, exs = [], []
        for _ in range(PEER_TOPK):
            m = jnp.max(cand, axis=0, keepdims=True)
            c = jnp.min(jnp.where(cand == m, flat, big), axis=0, keepdims=True)
            hit = flat == c
            exs.append(jnp.max(jnp.where(hit, expert, -1), axis=0, keepdims=True))
            scs.append(m)
            cand = jnp.where(hit, NEG_INF, cand)
        sc = jnp.concatenate(scs, axis=0)
        e = jnp.exp(sc - sc[0:1])
        gate_ref[h * PEER_TOPK:(h + 1) * PEER_TOPK, :] = e / jnp.sum(e, axis=0, keepdims=True)
        idx_ref[h * PEER_TOPK:(h + 1) * PEER_TOPK, :] = jnp.concatenate(exs, axis=0)


def _route(q, keys1, keys2):
    t = q.shape[0]
    tb = _row_tile(t, 256)
    kspec = pl.BlockSpec((PEER_HEADS, N_KEYS, PEER_QDIM // 2), lambda i: (0, 0, 0))
    return pl.pallas_call(
        functools.partial(_route_kernel, tb=tb), name="peer_route", grid=(t // tb,),
        out_shape=[jax.ShapeDtypeStruct((PEER_SLOTS, t), I32), jax.ShapeDtypeStruct((PEER_SLOTS, t), F32)],
        in_specs=[pl.BlockSpec((tb, PEER_HEADS * PEER_QDIM), lambda i: (i, 0)), kspec, kspec],
        out_specs=[pl.BlockSpec((PEER_SLOTS, tb), lambda i: (0, i)),
                   pl.BlockSpec((PEER_SLOTS, tb), lambda i: (0, i))],
        compiler_params=_cparams("parallel"))(q, keys1, keys2)


GATHER_SLOTS = 3


def _peer_apply_kernel(idx_ref, c_ref, h_ref, gt_ref, uv_ref, o_ref, buf, sem, *, tb):
    ahead = GATHER_SLOTS - 1

    def issue(t, slot):
        for k in range(PEER_SLOTS):
            e = idx_ref[t, k]
            pltpu.make_async_copy(uv_ref.at[e], buf.at[slot, pl.ds(k, 1), :],
                                  sem.at[slot]).start(priority=k % 2)

    def wait(slot):
        pltpu.make_async_copy(buf.at[slot], buf.at[slot], sem.at[slot]).wait()

    lane = lax.broadcasted_iota(I32, (PEER_SLOTS, tb), 1)

    def apply(t, slot):
        x = c_ref[pl.ds(t, 1), :]
        p = buf[slot, :, 0:D_MODEL] * x
        acc = p[:, 0:128]
        for j in range(1, D_MODEL // 128):
            acc = acc + p[:, j * 128:(j + 1) * 128]
        s = jnp.sum(acc, axis=-1, keepdims=True)
        act = 0.5 * s * (1.0 + lax.erf(s * (1.0 / math.sqrt(2.0))))
        gate = jnp.sum(jnp.where(lane == t, gt_ref[...], 0.0), axis=-1, keepdims=True)
        w = gate * act
        y = jnp.sum(w * buf[slot, :, D_MODEL:2 * D_MODEL], axis=0, keepdims=True)
        o_ref[pl.ds(t, 1), :] = h_ref[pl.ds(t, 1), :] + y

    for t0 in range(ahead):
        issue(t0, t0)

    def body(i, carry):
        for j in range(GATHER_SLOTS):
            t = i * GATHER_SLOTS + j
            issue(t + ahead, (j + ahead) % GATHER_SLOTS)
            wait(j)
            apply(t, j)
        return carry

    nmain = (tb - ahead) // GATHER_SLOTS
    lax.fori_loop(0, nmain, body, 0)
    for t in range(nmain * GATHER_SLOTS, tb):
        if t + ahead < tb:
            issue(t + ahead, (t + ahead) % GATHER_SLOTS)
        wait(t % GATHER_SLOTS)
        apply(t, t % GATHER_SLOTS)


def _peer_apply(idx, c, h, gate_t, uv):
    t = c.shape[0]
    tb = _row_tile(t, 256)
    row = lambda i: (i, 0)
    return pl.pallas_call(
        functools.partial(_peer_apply_kernel, tb=tb), name="peer_apply", grid=(t // tb,),
        out_shape=jax.ShapeDtypeStruct((t, D_MODEL), F32),
        in_specs=[pl.BlockSpec((tb, PEER_SLOTS), row, memory_space=pltpu.SMEM),
                  pl.BlockSpec((tb, D_MODEL), row), pl.BlockSpec((tb, D_MODEL), row),
                  pl.BlockSpec((PEER_SLOTS, tb), lambda i: (0, i)),
                  pl.BlockSpec(memory_space=pl.ANY)],
        out_specs=pl.BlockSpec((tb, D_MODEL), row),
        scratch_shapes=[pltpu.VMEM((GATHER_SLOTS, PEER_SLOTS, 2 * D_MODEL), F32),
                        pltpu.SemaphoreType.DMA((GATHER_SLOTS,))],
        compiler_params=_cparams("arbitrary"))(idx, c, h, gate_t, uv)


def _final_kernel(h_ref, p_ref, gple_ref, wg_ref, wp_ref, gfin_ref, y_ref):
    h = h_ref[...]
    e = _rms(h, gple_ref[...])
    gate = jax.nn.sigmoid(jnp.dot(e.astype(BF16), wg_ref[...], preferred_element_type=F32))
    up = jnp.dot(p_ref[...].astype(BF16), wp_ref[...], preferred_element_type=F32)
    y_ref[...] = _rms(h + up * gate, gfin_ref[...])


def _final(h, p, g_ple, w_gate, w_ple, g_final):
    t = h.shape[0]
    tm = _row_tile(t, 512)
    row = lambda i: (i, 0)
    fixed = lambda i: (0, 0)
    return pl.pallas_call(
        _final_kernel, name="ple_final", grid=(t // tm,),
        out_shape=jax.ShapeDtypeStruct((t, D_MODEL), F32),
        in_specs=[pl.BlockSpec((tm, D_MODEL), row), pl.BlockSpec((tm, PLE_DIM), row),
                  pl.BlockSpec((1, D_MODEL), fixed), pl.BlockSpec((D_MODEL, D_MODEL), fixed),
                  pl.BlockSpec((PLE_DIM, D_MODEL), fixed), pl.BlockSpec((1, D_MODEL), fixed)],
        out_specs=pl.BlockSpec((tm, D_MODEL), row),
        compiler_params=_cparams("parallel"))(h, p, g_ple, w_gate, w_ple, g_final)


def _layer(x, p, kv_buf, conv_prefix, s0, wts, batch, seq):
    t = batch * seq
    q, k, v, dqkv, z, ab = _inproj(x, wts["norm_mix"], wts["w_in"])
    if kv_buf is None:
        att = _swa_prompt(q, k, v, wts["sinks"], batch, seq)
        k_new = k.reshape(batch, seq, ATT_KV)[:, -WINDOW:]
        v_new = v.reshape(batch, seq, ATT_KV)[:, -WINDOW:]
    else:
        att, k_new, v_new = _swa_sample(q.reshape(batch, seq, ATT_Q), k.reshape(batch, seq, ATT_KV),
                                        v.reshape(batch, seq, ATT_KV), kv_buf[0], kv_buf[1], wts["sinks"])
        att = att.reshape(t, ATT_Q)
    dn, s_new = _gdn(dqkv, z, ab, conv_prefix, s0, wts["conv_w"], wts["a_log"], wts["dt_bias"],
                     wts["dn_norm"], batch, seq)
    conv_new = dqkv.reshape(batch, seq, DN_QKV)[:, -(CONV_W - 1):]
    h, c, pq = _outproj(x, att, dn, wts["w_out"], wts["norm_ffn"], wts["peer_wq"])
    idx_t, gate_t = _route(pq, wts["keys1"], wts["keys2"])
    h2 = _peer_apply(idx_t.T, c, h, gate_t, wts["uv"])
    y = _final(h2, p, wts["norm_ple"], wts["ple_gate"], wts["ple_in"], wts["norm_final"])
    return y, k_new, v_new, conv_new, s_new


def kernel(x_prompt, x_sample, p_prompt, p_sample, cache_swa_k, cache_swa_v, state_conv, state_delta, norm_mix, w_in, conv_w, attn_sinks, dn_a_log, dn_dt_bias, dn_norm, w_out, norm_ffn, peer_wq, peer_keys1, peer_keys2, peer_u, peer_v, norm_ple, ple_in, ple_gate, norm_final):
    depth = w_in.shape[0]
    assert depth == 1
    bp, lp = x_prompt.shape[0], x_prompt.shape[1]
    bs, ls = x_sample.shape[0], x_sample.shape[1]
    l = 0
    wts = dict(
        norm_mix=norm_mix[l].reshape(1, D_MODEL),
        w_in=jnp.pad(w_in[l], ((0, 0), (0, IN_WIDTH_PADDED - IN_WIDTH))).astype(BF16),
        conv_w=conv_w[l], sinks=attn_sinks[l], a_log=dn_a_log[l], dt_bias=dn_dt_bias[l], dn_norm=dn_norm[l],
        w_out=w_out[l].astype(BF16), norm_ffn=norm_ffn[l].reshape(1, D_MODEL), peer_wq=peer_wq[l].astype(BF16),
        keys1=peer_keys1[l].astype(BF16), keys2=peer_keys2[l].astype(BF16),
        uv=jnp.concatenate([peer_u[l][:, None, :], peer_v[l][:, None, :]], axis=-1),
        norm_ple=norm_ple[l].reshape(1, D_MODEL), ple_in=ple_in[l].astype(BF16),
        ple_gate=ple_gate[l].astype(BF16), norm_final=norm_final.reshape(1, D_MODEL))
    conv0 = jnp.zeros((bp, CONV_W - 1, DN_QKV), F32)
    s0 = jnp.zeros((bp, DN_HEADS, HEAD_DIM, HEAD_DIM), F32)
    yp, kp, vp, cp, sp = _layer(x_prompt.reshape(bp * lp, D_MODEL), p_prompt[l].reshape(bp * lp, PLE_DIM),
                                None, conv0, s0, wts, bp, lp)
    kv_buf = (cache_swa_k[l].reshape(bs, WINDOW, ATT_KV), cache_swa_v[l].reshape(bs, WINDOW, ATT_KV))
    ys, kn, vn, cn, sn = _layer(x_sample.reshape(bs * ls, D_MODEL), p_sample[l].reshape(bs * ls, PLE_DIM),
                                kv_buf, state_conv[l], state_delta[l], wts, bs, ls)
    kvshape = lambda b: (1, b, WINDOW, ATT_KV_HEADS, HEAD_DIM)
    return (yp.reshape(bp, lp, D_MODEL), ys.reshape(bs, ls, D_MODEL),
            kp.reshape(kvshape(bp)), vp.reshape(kvshape(bp)), cp[None], sp[None],
            kn.reshape(kvshape(bs)), vn.reshape(kvshape(bs)), cn[None], sn[None])
```

```python
import functools
import math

import jax
import jax.numpy as jnp
from jax import lax
from jax.experimental import pallas as pl
from jax.experimental.pallas import tpu as pltpu

F32 = jnp.float32
BF16 = jnp.bfloat16
I32 = jnp.int32
HIGHEST = lax.Precision.HIGHEST

D_MODEL = 1024
HEAD_DIM = 64
ATT_HEADS = 8
ATT_KV_HEADS = 2
ATT_GROUP = 4
WINDOW = 128
ATT_Q = ATT_HEADS * HEAD_DIM
ATT_KV = ATT_KV_HEADS * HEAD_DIM
DN_HEADS = 8
DN_WIDTH = DN_HEADS * HEAD_DIM
DN_QKV = 3 * DN_WIDTH
CONV_W = 4
DN_CHUNK = 64
IN_WIDTH = ATT_Q + 2 * ATT_KV + DN_QKV + DN_WIDTH + 2 * DN_HEADS
IN_WIDTH_PADDED = 2944
N_KEYS = 128
PEER_HEADS = 8
PEER_QDIM = 256
PEER_TOPK = 16
PEER_SLOTS = PEER_HEADS * PEER_TOPK
PLE_DIM = 256
EPS = 1e-6
NEG_INF = float("-inf")
INDEX_SENTINEL = 1.0e9

VMEM_LIMIT = 48 * 1024 * 1024


def _cparams(*sem):
    return pltpu.CompilerParams(dimension_semantics=sem, vmem_limit_bytes=VMEM_LIMIT)


def _rms(x, g):
    return x * lax.rsqrt(jnp.mean(x * x, axis=-1, keepdims=True) + EPS) * g


def _bdot(a, b):
    return jnp.dot(a.astype(BF16), b.astype(BF16), preferred_element_type=F32)


def _fdot(a, b):
    return jnp.dot(a, b, precision=HIGHEST, preferred_element_type=F32)


_NN = (((1,), (0,)), ((), ()))
_NT = (((1,), (1,)), ((), ()))
_TN = (((0,), (0,)), ((), ()))


def _row_tile(t, target):
    tile = min(t, target)
    assert t % tile == 0, (t, tile)
    return tile


def _inproj_kernel(x_ref, g_ref, w_ref, q_ref, k_ref, v_ref, dqkv_ref, z_ref, ab_ref):
    a = _rms(x_ref[...], g_ref[...])
    p = jnp.dot(a.astype(BF16), w_ref[...], preferred_element_type=F32)
    c0, c1, c2, c3, c4 = ATT_Q, ATT_Q + ATT_KV, ATT_Q + 2 * ATT_KV, ATT_Q + 2 * ATT_KV + DN_QKV, \
        ATT_Q + 2 * ATT_KV + DN_QKV + DN_WIDTH
    q_ref[...] = p[:, :c0]
    k_ref[...] = p[:, c0:c1]
    v_ref[...] = p[:, c1:c2]
    dqkv_ref[...] = p[:, c2:c3]
    z_ref[...] = p[:, c3:c4]
    ab_ref[...] = p[:, c4:c4 + 2 * DN_HEADS]


def _inproj(x, g, w_pad):
    t = x.shape[0]
    tm = _row_tile(t, 512)
    widths = (ATT_Q, ATT_KV, ATT_KV, DN_QKV, DN_WIDTH, 2 * DN_HEADS)
    row = lambda i: (i, 0)
    fixed = lambda i: (0, 0)
    return pl.pallas_call(
        _inproj_kernel, name="inproj", grid=(t // tm,),
        out_shape=[jax.ShapeDtypeStruct((t, w), F32) for w in widths],
        in_specs=[pl.BlockSpec((tm, D_MODEL), row), pl.BlockSpec((1, D_MODEL), fixed),
                  pl.BlockSpec((D_MODEL, IN_WIDTH_PADDED), fixed)],
        out_specs=[pl.BlockSpec((tm, w), row) for w in widths],
        compiler_params=_cparams("parallel"))(x, g, w_pad)


def _softmax_sink_pv(s, sink, vband):
    m = jnp.maximum(jnp.max(s, axis=-1, keepdims=True), sink)
    e = jnp.exp(s - m)
    denom = jnp.sum(e, axis=-1, keepdims=True) + jnp.exp(sink - m)
    return _bdot(e / denom, vband)


def _alibi_slope(h):
    return 2.0 ** (-8.0 * (h + 1) / ATT_HEADS)


def _group_columns(kh, rows, sink_ref):
    rg = lax.broadcasted_iota(I32, (ATT_GROUP * rows, 1), 0) // rows
    slope = jnp.full((ATT_GROUP * rows, 1), _alibi_slope(kh * ATT_GROUP), F32)
    sink = jnp.full((ATT_GROUP * rows, 1), sink_ref[kh * ATT_GROUP], F32)
    for j in range(1, ATT_GROUP):
        slope = jnp.where(rg == j, _alibi_slope(kh * ATT_GROUP + j), slope)
        sink = jnp.where(rg == j, sink_ref[kh * ATT_GROUP + j], sink)
    return slope, sink


def _attend_kv_head(q_all, kband, vband, kh, dist, valid, sink_ref):
    rows = q_all.shape[0]
    heads = range(kh * ATT_GROUP, (kh + 1) * ATT_GROUP)
    q4 = jnp.concatenate([q_all[:, h * HEAD_DIM:(h + 1) * HEAD_DIM] for h in heads], axis=0)
    k = kband[:, kh * HEAD_DIM:(kh + 1) * HEAD_DIM]
    v = vband[:, kh * HEAD_DIM:(kh + 1) * HEAD_DIM]
    s = lax.dot_general(q4.astype(BF16), k.astype(BF16), _NT, preferred_element_type=F32) * (HEAD_DIM ** -0.5)
    slope, sink = _group_columns(kh, rows, sink_ref)
    s = jnp.where(valid, s - slope * dist, NEG_INF)
    o4 = _softmax_sink_pv(s, sink, v)
    return [o4[j * rows:(j + 1) * rows] for j in range(ATT_GROUP)]


def _swa_prompt_kernel(sink_ref, q_ref, kc_ref, kp_ref, vc_ref, vp_ref, o_ref):
    n = pl.program_id(1)
    rows = ATT_GROUP * WINDOW
    qi = lax.broadcasted_iota(I32, (rows, 2 * WINDOW), 0) % WINDOW
    kj = lax.broadcasted_iota(I32, (rows, 2 * WINDOW), 1)
    dist = WINDOW + qi - kj
    kpos = n * WINDOW - WINDOW + kj
    valid = (dist >= 0) & (dist <= WINDOW) & (kpos >= 0)
    distf = dist.astype(F32)
    kband = jnp.concatenate([kp_ref[...], kc_ref[...]], axis=0)
    vband = jnp.concatenate([vp_ref[...], vc_ref[...]], axis=0)
    q_all = q_ref[...]
    outs = []
    for kh in range(ATT_KV_HEADS):
        outs += _attend_kv_head(q_all, kband, vband, kh, distf, valid, sink_ref)
    o_ref[...] = jnp.concatenate(outs, axis=-1)


def _swa_prompt(q, k, v, sinks, batch, seq):
    nblk = seq // WINDOW
    cur = lambda b, n: (b * nblk + n, 0)
    prev = lambda b, n: (b * nblk + jnp.maximum(n - 1, 0), 0)
    return pl.pallas_call(
        _swa_prompt_kernel, name="swa_prompt", grid=(batch, nblk),
        out_shape=jax.ShapeDtypeStruct((batch * seq, ATT_Q), F32),
        in_specs=[pl.BlockSpec(memory_space=pltpu.SMEM),
                  pl.BlockSpec((WINDOW, ATT_Q), cur),
                  pl.BlockSpec((WINDOW, ATT_KV), cur), pl.BlockSpec((WINDOW, ATT_KV), prev),
                  pl.BlockSpec((WINDOW, ATT_KV), cur), pl.BlockSpec((WINDOW, ATT_KV), prev)],
        out_specs=pl.BlockSpec((WINDOW, ATT_Q), cur),
        compiler_params=_cparams("parallel", "parallel"))(sinks, q, k, k, v, v)


def _swa_sample_kernel(sink_ref, q_ref, k_ref, v_ref, kb_ref, vb_ref, o_ref, ko_ref, vo_ref, *, bt, seq):
    klen = WINDOW + seq
    rows = ATT_GROUP * seq
    qi = lax.broadcasted_iota(I32, (rows, klen), 0) % seq
    kj = lax.broadcasted_iota(I32, (rows, klen), 1)
    dist = qi - (kj - WINDOW)
    valid = (dist >= 0) & (dist <= WINDOW)
    distf = dist.astype(F32)
    for b in range(bt):
        kc = jnp.concatenate([kb_ref[b], k_ref[b]], axis=0)
        vc = jnp.concatenate([vb_ref[b], v_ref[b]], axis=0)
        ko_ref[b] = kc[seq:]
        vo_ref[b] = vc[seq:]
        q_all = q_ref[b]
        outs = []
        for kh in range(ATT_KV_HEADS):
            outs += _attend_kv_head(q_all, kc, vc, kh, distf, valid, sink_ref)
        o_ref[b] = jnp.concatenate(outs, axis=-1)


def _swa_sample(q, k, v, k_buf, v_buf, sinks):
    batch, seq = q.shape[0], q.shape[1]
    bt = _row_tile(batch, 8)
    blk = lambda w, r: pl.BlockSpec((bt, r, w), lambda i: (i, 0, 0))
    return pl.pallas_call(
        functools.partial(_swa_sample_kernel, bt=bt, seq=seq), name="swa_sample", grid=(batch // bt,),
        out_shape=[jax.ShapeDtypeStruct((batch, seq, ATT_Q), F32),
                   jax.ShapeDtypeStruct((batch, WINDOW, ATT_KV), F32),
                   jax.ShapeDtypeStruct((batch, WINDOW, ATT_KV), F32)],
        in_specs=[pl.BlockSpec(memory_space=pltpu.SMEM), blk(ATT_Q, seq), blk(ATT_KV, seq), blk(ATT_KV, seq),
                  blk(ATT_KV, WINDOW), blk(ATT_KV, WINDOW)],
        out_specs=[blk(ATT_Q, seq), blk(ATT_KV, WINDOW), blk(ATT_KV, WINDOW)],
        compiler_params=_cparams("parallel"))(sinks, q, k, v, k_buf, v_buf)


def _split_bf16(a, cache):
    if id(a) not in cache:
        hi = a.astype(BF16)
        cache[id(a)] = (a, (hi, (a - hi.astype(F32)).astype(BF16)))
    return cache[id(a)][1]


MXU_DEPTH = 256


def _dot3_impl(a, b, dims=_NN, *, cache):
    (ca,), (cb,) = dims[0]
    depth = a.shape[ca]
    ah, al = _split_bf16(a, cache)
    bh, bl = _split_bf16(b, cache)
    f = lambda x, y: lax.dot_general(x, y, dims, preferred_element_type=F32)
    if 3 * depth <= MXU_DEPTH:
        return f(jnp.concatenate([ah, al, ah], axis=ca), jnp.concatenate([bh, bh, bl], axis=cb))
    return f(jnp.concatenate([ah, al], axis=ca), jnp.concatenate([bh, bh], axis=cb)) + f(ah, bl)


def _gdn_kernel(alog_ref, dtb_ref, dqkv_ref, z_ref, ab_ref, cw_ref, nrm_ref, pre_ref, s0_ref,
                o_ref, sfin_ref, tail_ref, state_ref, *, c, g, nsub):
    n = pl.program_id(1)
    ngrp = DN_HEADS // g
    r = g * c
    rows = nsub * c
    hd = HEAD_DIM
    _dot3 = functools.partial(_dot3_impl, cache={})

    @pl.when(n == 0)
    def _():
        tail_ref[...] = jnp.concatenate([jnp.zeros((8 - (CONV_W - 1), DN_QKV), F32), pre_ref[0]], axis=0)
        for gi in range(ngrp):
            state_ref[gi] = jnp.concatenate([s0_ref[0, gi * g + j] for j in range(g)], axis=1)

    x = dqkv_ref[...]
    hist = jnp.concatenate([tail_ref[...], x], axis=0)
    tail_ref[...] = hist[rows:]
    cw = cw_ref[...]
    y = x * cw[CONV_W - 1:CONV_W]
    for d in range(1, CONV_W):
        y = y + hist[8 - d:8 - d + rows] * cw[CONV_W - 1 - d:CONV_W - d]
    qkv = y * jax.nn.sigmoid(y)
    z = z_ref[...]
    ab = ab_ref[...]
    beta_all = jax.nn.sigmoid(ab[:, DN_HEADS:2 * DN_HEADS])
    g_all = -jnp.exp(alog_ref[...]) * jax.nn.softplus(ab[:, 0:DN_HEADS] + dtb_ref[...])

    ri = lax.broadcasted_iota(I32, (r, r), 0)
    ci = lax.broadcasted_iota(I32, (r, r), 1)
    same = (ri // c) == (ci // c)
    lower = same & (ri >= ci)
    strict = same & (ri > ci)
    tri = lower.astype(F32)
    blk = same.astype(F32)
    eye = (ri == ci).astype(F32)
    rowhead = lax.broadcasted_iota(I32, (r, hd), 0) // c

    def stack(a, sub, gi, base):
        return jnp.concatenate([a[sub * c:(sub + 1) * c, base + (gi * g + j) * hd:base + (gi * g + j + 1) * hd]
                                for j in range(g)], axis=0)

    def column(a, sub, gi):
        return jnp.concatenate([a[sub * c:(sub + 1) * c, gi * g + j:gi * g + j + 1] for j in range(g)], axis=0)

    def own_block(m):
        acc = jnp.where(rowhead == 0, m[:, 0:hd], 0.0)
        for j in range(1, g):
            acc = acc + jnp.where(rowhead == j, m[:, j * hd:(j + 1) * hd], 0.0)
        return acc

    keys = [(sub, gi) for sub in range(nsub) for gi in range(ngrp)]
    grp = {}
    for sub, gi in keys:
        q = stack(qkv, sub, gi, 0)
        k = stack(qkv, sub, gi, DN_WIDTH)
        v = stack(qkv, sub, gi, 2 * DN_WIDTH)
        q = q * lax.rsqrt(jnp.sum(q * q, axis=-1, keepdims=True) + EPS) * (hd ** -0.5)
        k = k * lax.rsqrt(jnp.sum(k * k, axis=-1, keepdims=True) + EPS)
        beta = column(beta_all, sub, gi)
        gr = column(g_all, sub, gi)
        gcol = _fdot(tri, gr)
        grow = lax.dot_general(gr, tri, (((0,), (1,)), ((), ())), precision=HIGHEST,
                               preferred_element_type=F32)
        glast = _fdot(blk, gr)
        decay = jnp.exp(jnp.where(lower, gcol - grow, NEG_INF))
        kb = k * beta
        lm = jnp.where(strict, _dot3(kb, k, _NT) * decay, 0.0)
        grp[sub, gi] = dict(q=q, k=k, v=v, beta=beta, gcol=gcol, glast=glast, decay=decay, kb=kb, lm=lm)

    pair = ((ri // 2) == (ci // 2)) & (ri % 2 == 1) & (ci % 2 == 0)
    invs = {key: eye - jnp.where(pair, grp[key]["lm"], 0.0) for key in keys}
    s = 2
    while s < c:
        off = ((ri // (2 * s)) == (ci // (2 * s))) & ((ri // s) % 2 == 1) & ((ci // s) % 2 == 0)
        tmp = {key: _dot3(invs[key], jnp.where(off, grp[key]["lm"], 0.0)) for key in keys}
        invs = {key: invs[key] - _dot3(tmp[key], invs[key]) for key in keys}
        s *= 2

    for sub in range(nsub):
        outs = [None] * DN_HEADS
        for gi in range(ngrp):
            d = grp[sub, gi]
            eg = jnp.exp(d["gcol"])
            sol = _dot3(invs[sub, gi], jnp.concatenate([d["v"] * d["beta"], d["kb"] * eg], axis=1))
            u = sol[:, :hd]
            w = sol[:, hd:]
            qk = _dot3(d["q"], d["k"], _NT) * d["decay"]
            ktail = d["k"] * jnp.exp(d["glast"] - d["gcol"])
            qhead = d["q"] * eg
            st = state_ref[gi]
            vnew = u - own_block(_dot3(w, st))
            o = own_block(_dot3(qhead, st)) + _dot3(qk, vnew)
            vblk = jnp.concatenate([jnp.where(rowhead == j, vnew, 0.0) for j in range(g)], axis=1)
            elast = jnp.exp(d["glast"])
            keep = jnp.concatenate([jnp.broadcast_to(elast[j * c:j * c + 1, :], (1, hd)) for j in range(g)],
                                   axis=1)
            state_ref[gi] = st * keep + _dot3(ktail, vblk, _TN)
            zs = stack(z, sub, gi, 0)
            o = _rms(o, nrm_ref[...]) * (zs * jax.nn.sigmoid(zs))
            for j in range(g):
                outs[gi * g + j] = o[j * c:(j + 1) * c]
        o_ref[sub * c:(sub + 1) * c, :] = jnp.concatenate(outs, axis=-1)

    @pl.when(n == pl.num_programs(1) - 1)
    def _():
        for gi in range(ngrp):
            for j in range(g):
                sfin_ref[0, gi * g + j] = state_ref[gi][:, j * hd:(j + 1) * hd]


GDN_CHUNKS_PER_STEP = 2


def _gdn(dqkv, z, ab, conv_prefix, s0, conv_w, a_log, dt_bias, dn_norm, batch, seq):
    c = min(DN_CHUNK, seq)
    nchunk = seq // c
    assert seq % c == 0 and c % 8 == 0
    g = min(DN_HEADS, max(1, 128 // c))
    nsub = GDN_CHUNKS_PER_STEP if nchunk % GDN_CHUNKS_PER_STEP == 0 else 1
    nstep = nchunk // nsub
    rows = nsub * c
    tok = lambda b, n: (b * nstep + n, 0)
    fixed = lambda b, n: (0, 0)
    return pl.pallas_call(
        functools.partial(_gdn_kernel, c=c, g=g, nsub=nsub), name="gdn", grid=(batch, nstep),
        out_shape=[jax.ShapeDtypeStruct((batch * seq, DN_WIDTH), F32),
                   jax.ShapeDtypeStruct((batch, DN_HEADS, HEAD_DIM, HEAD_DIM), F32)],
        in_specs=[pl.BlockSpec((1, DN_HEADS), fixed), pl.BlockSpec((1, DN_HEADS), fixed),
                  pl.BlockSpec((rows, DN_QKV), tok), pl.BlockSpec((rows, DN_WIDTH), tok),
                  pl.BlockSpec((rows, 2 * DN_HEADS), tok),
                  pl.BlockSpec((CONV_W, DN_QKV), fixed), pl.BlockSpec((1, HEAD_DIM), fixed),
                  pl.BlockSpec((1, CONV_W - 1, DN_QKV), lambda b, n: (b, 0, 0)),
                  pl.BlockSpec((1, DN_HEADS, HEAD_DIM, HEAD_DIM), lambda b, n: (b, 0, 0, 0))],
        out_specs=[pl.BlockSpec((rows, DN_WIDTH), tok),
                   pl.BlockSpec((1, DN_HEADS, HEAD_DIM, HEAD_DIM), lambda b, n: (b, 0, 0, 0))],
        scratch_shapes=[pltpu.VMEM((8, DN_QKV), F32),
                        pltpu.VMEM((DN_HEADS // g, HEAD_DIM, g * HEAD_DIM), F32)],
        compiler_params=_cparams("parallel", "arbitrary"))(
            a_log.reshape(1, DN_HEADS), dt_bias.reshape(1, DN_HEADS), dqkv, z, ab, conv_w,
            dn_norm.reshape(1, HEAD_DIM), conv_prefix, s0)


def _outproj_kernel(x_ref, att_ref, dn_ref, wo_ref, g_ref, wq_ref, h_ref, c_ref, q_ref):
    mix = jnp.concatenate([att_ref[...], dn_ref[...]], axis=-1)
    h = x_ref[...] + jnp.dot(mix.astype(BF16), wo_ref[...], preferred_element_type=F32)
    h_ref[...] = h
    cn = _rms(h, g_ref[...])
    c_ref[...] = cn
    q_ref[...] = jnp.dot(cn.astype(BF16), wq_ref[...], preferred_element_type=F32)


def _outproj(x, att, dn, w_out, g_ffn, wq):
    t = x.shape[0]
    tm = _row_tile(t, 512)
    row = lambda i: (i, 0)
    fixed = lambda i: (0, 0)
    qw = PEER_HEADS * PEER_QDIM
    return pl.pallas_call(
        _outproj_kernel, name="outproj", grid=(t // tm,),
        out_shape=[jax.ShapeDtypeStruct((t, D_MODEL), F32), jax.ShapeDtypeStruct((t, D_MODEL), F32),
                   jax.ShapeDtypeStruct((t, qw), F32)],
        in_specs=[pl.BlockSpec((tm, D_MODEL), row), pl.BlockSpec((tm, ATT_Q), row),
                  pl.BlockSpec((tm, DN_WIDTH), row), pl.BlockSpec((D_MODEL, D_MODEL), fixed),
                  pl.BlockSpec((1, D_MODEL), fixed), pl.BlockSpec((D_MODEL, qw), fixed)],
        out_specs=[pl.BlockSpec((tm, D_MODEL), row), pl.BlockSpec((tm, D_MODEL), row),
                   pl.BlockSpec((tm, qw), row)],
        compiler_params=_cparams("parallel"))(x, att, dn, w_out, g_ffn, wq)


def _top16_rows(s, iota):
    big = jnp.float32(INDEX_SENTINEL)
    vals, idxs = [], []
    for _ in range(PEER_TOPK):
        m = jnp.max(s, axis=0, keepdims=True)
        i = jnp.min(jnp.where(s == m, iota, big), axis=0, keepdims=True)
        vals.append(m)
        idxs.append(i)
        s = jnp.where(iota == i, NEG_INF, s)
    return jnp.concatenate(vals, axis=0), jnp.concatenate(idxs, axis=0)


def _candidates(a, b, row8):
    t = a.shape[1]
    lo = row8 < 4
    bc = lambda r, n: jnp.broadcast_to(a[r:r + 1], (n, t))
    b8 = b[0:8]
    b44 = jnp.where(lo, b8, pltpu.roll(b8, 4, 0))
    parts = [bc(0, 8), bc(0, 8), bc(1, 8), bc(2, 8), bc(3, 8),
             jnp.where(lo, bc(4, 8), bc(5, 8)), jnp.where(lo, bc(6, 8), bc(7, 8)), a[8:16]]
    others = [b[0:8], b[8:16], b8, b8, b8, b44, b44, jnp.broadcast_to(b[0:1], (8, t))]
    return parts, others


def _flat_index_column():
    r = lax.broadcasted_iota(I32, (8, 1), 0)
    lo = r < 4
    cols = [r, 8 + r, 16 + r, 32 + r, 48 + r, jnp.where(lo, 64 + r, 80 + r - 4),
            jnp.where(lo, 96 + r, 112 + r - 4), (8 + r) * 16]
    return jnp.concatenate(cols, axis=0).astype(F32)


def _route_kernel(q_ref, k1_ref, k2_ref, idx_ref, gate_ref, *, tb):
    iota = lax.broadcasted_iota(I32, (N_KEYS, tb), 0).astype(F32)
    row8 = lax.broadcasted_iota(I32, (8, tb), 0)
    flat = jnp.broadcast_to(_flat_index_column(), (4 * PEER_TOPK, tb))
    big = jnp.float32(INDEX_SENTINEL)
    half = PEER_QDIM // 2
    for h in range(PEER_HEADS):
        q1 = q_ref[:, h * PEER_QDIM:h * PEER_QDIM + half]
        q2 = q_ref[:, h * PEER_QDIM + half:(h + 1) * PEER_QDIM]
        nt = (((1,), (1,)), ((), ()))
        s1 = lax.dot_general(k1_ref[h], q1.astype(BF16), nt, preferred_element_type=F32)
        s2 = lax.dot_general(k2_ref[h], q2.astype(BF16), nt, preferred_element_type=F32)
        v1, i1 = _top16_rows(s1, iota)
        v2, i2 = _top16_rows(s2, iota)
        pa, pb = _candidates(v1, v2, row8)
        cand = jnp.concatenate([x + y for x, y in zip(pa, pb)], axis=0)
        ea, eb = _candidates(i1, i2, row8)
        expert = jnp.concatenate([x * float(N_KEYS) + y for x, y in zip(ea, eb)], axis=0)
        scs, exs = [], []
        for _ in range(PEER_TOPK):
            m = jnp.max(cand, axis=0, keepdims=True)
            c = jnp.min(jnp.where(cand == m, flat, big), axis=0, keepdims=True)
            hit = flat == c
            exs.append(jnp.max(jnp.where(hit, expert, -1.0), axis=0, keepdims=True))
            scs.append(m)
            cand = jnp.where(hit, NEG_INF, cand)
        sc = jnp.concatenate(scs, axis=0)
        e = jnp.exp(sc - sc[0:1])
        gate_ref[h * PEER_TOPK:(h + 1) * PEER_TOPK, :] = e / jnp.sum(e, axis=0, keepdims=True)
        idx_ref[h * PEER_TOPK:(h + 1) * PEER_TOPK, :] = jnp.concatenate(exs, axis=0).astype(I32)


def _route(q, keys1, keys2):
    t = q.shape[0]
    tb = _row_tile(t, 256)
    kspec = pl.BlockSpec((PEER_HEADS, N_KEYS, PEER_QDIM // 2), lambda i: (0, 0, 0))
    return pl.pallas_call(
        functools.partial(_route_kernel, tb=tb), name="peer_route", grid=(t // tb,),
        out_shape=[jax.ShapeDtypeStruct((PEER_SLOTS, t), I32), jax.ShapeDtypeStruct((PEER_SLOTS, t), F32)],
        in_specs=[pl.BlockSpec((tb, PEER_HEADS * PEER_QDIM), lambda i: (i, 0)), kspec, kspec],
        out_specs=[pl.BlockSpec((PEER_SLOTS, tb), lambda i: (0, i)),
                   pl.BlockSpec((PEER_SLOTS, tb), lambda i: (0, i))],
        compiler_params=_cparams("parallel"))(q, keys1, keys2)


GATHER_SLOTS = 3


def _peer_apply_kernel(idx_ref, c_ref, h_ref, gt_ref, uv_ref, o_ref, buf, sem, *, tb):
    ahead = GATHER_SLOTS - 1

    def issue(t, slot):
        for k in range(PEER_SLOTS):
            e = idx_ref[k, t]
            pltpu.make_async_copy(uv_ref.at[e], buf.at[slot, pl.ds(k, 1), :],
                                  sem.at[slot]).start(priority=k % 2)

    def wait(slot):
        pltpu.make_async_copy(buf.at[slot], buf.at[slot], sem.at[slot]).wait()

    lane = lax.broadcasted_iota(I32, (PEER_SLOTS, tb), 1)

    def apply(t, slot):
        x = c_ref[pl.ds(t, 1), :]
        p = buf[slot, :, 0:D_MODEL] * x
        acc = p[:, 0:128]
        for j in range(1, D_MODEL // 128):
            acc = acc + p[:, j * 128:(j + 1) * 128]
        s = jnp.sum(acc, axis=-1, keepdims=True)
        act = 0.5 * s * (1.0 + lax.erf(s * (1.0 / math.sqrt(2.0))))
        gate = jnp.sum(jnp.where(lane == t, gt_ref[...], 0.0), axis=-1, keepdims=True)
        w = gate * act
        y = jnp.sum(w * buf[slot, :, D_MODEL:2 * D_MODEL], axis=0, keepdims=True)
        o_ref[pl.ds(t, 1), :] = h_ref[pl.ds(t, 1), :] + y

    for t0 in range(ahead):
        issue(t0, t0)

    def body(i, carry):
        for j in range(GATHER_SLOTS):
            t = i * GATHER_SLOTS + j
            issue(t + ahead, (j + ahead) % GATHER_SLOTS)
            wait(j)
            apply(t, j)
        return carry

    nmain = (tb - ahead) // GATHER_SLOTS
    lax.fori_loop(0, nmain, body, 0)
    for t in range(nmain * GATHER_SLOTS, tb):
        if t + ahead < tb:
            issue(t + ahead, (t + ahead) % GATHER_SLOTS)
        wait(t % GATHER_SLOTS)
        apply(t, t % GATHER_SLOTS)


def _peer_apply(idx_t, c, h, gate_t, uv):
    t = c.shape[0]
    tb = _row_tile(t, 256)
    row = lambda i: (i, 0)
    return pl.pallas_call(
        functools.partial(_peer_apply_kernel, tb=tb), name="peer_apply", grid=(t // tb,),
        out_shape=jax.ShapeDtypeStruct((t, D_MODEL), F32),
        in_specs=[pl.BlockSpec((PEER_SLOTS, tb), lambda i: (0, i), memory_space=pltpu.SMEM),
                  pl.BlockSpec((tb, D_MODEL), row), pl.BlockSpec((tb, D_MODEL), row),
                  pl.BlockSpec((PEER_SLOTS, tb), lambda i: (0, i)),
                  pl.BlockSpec(memory_space=pl.ANY)],
        out_specs=pl.BlockSpec((tb, D_MODEL), row),
        scratch_shapes=[pltpu.VMEM((GATHER_SLOTS, PEER_SLOTS, 2 * D_MODEL), F32),
                        pltpu.SemaphoreType.DMA((GATHER_SLOTS,))],
        compiler_params=_cparams("arbitrary"))(idx_t, c, h, gate_t, uv)


def _final_kernel(h_ref, p_ref, gple_ref, wg_ref, wp_ref, gfin_ref, y_ref):
    h = h_ref[...]
    e = _rms(h, gple_ref[...])
    gate = jax.nn.sigmoid(jnp.dot(e.astype(BF16), wg_ref[...], preferred_element_type=F32))
    up = jnp.dot(p_ref[...].astype(BF16), wp_ref[...], preferred_element_type=F32)
    y_ref[...] = _rms(h + up * gate, gfin_ref[...])


def _final(h, p, g_ple, w_gate, w_ple, g_final):
    t = h.shape[0]
    tm = _row_tile(t, 512)
    row = lambda i: (i, 0)
    fixed = lambda i: (0, 0)
    return pl.pallas_call(
        _final_kernel, name="ple_final", grid=(t // tm,),
        out_shape=jax.ShapeDtypeStruct((t, D_MODEL), F32),
        in_specs=[pl.BlockSpec((tm, D_MODEL), row), pl.BlockSpec((tm, PLE_DIM), row),
                  pl.BlockSpec((1, D_MODEL), fixed), pl.BlockSpec((D_MODEL, D_MODEL), fixed),
                  pl.BlockSpec((PLE_DIM, D_MODEL), fixed), pl.BlockSpec((1, D_MODEL), fixed)],
        out_specs=pl.BlockSpec((tm, D_MODEL), row),
        compiler_params=_cparams("parallel"))(h, p, g_ple, w_gate, w_ple, g_final)


def _layer(x, p, kv_buf, conv_prefix, s0, wts, batch, seq):
    t = batch * seq
    q, k, v, dqkv, z, ab = _inproj(x, wts["norm_mix"], wts["w_in"])
    if kv_buf is None:
        att = _swa_prompt(q, k, v, wts["sinks"], batch, seq)
        k_new = k.reshape(batch, seq, ATT_KV)[:, -WINDOW:]
        v_new = v.reshape(batch, seq, ATT_KV)[:, -WINDOW:]
    else:
        att, k_new, v_new = _swa_sample(q.reshape(batch, seq, ATT_Q), k.reshape(batch, seq, ATT_KV),
                                        v.reshape(batch, seq, ATT_KV), kv_buf[0], kv_buf[1], wts["sinks"])
        att = att.reshape(t, ATT_Q)
    dn, s_new = _gdn(dqkv, z, ab, conv_prefix, s0, wts["conv_w"], wts["a_log"], wts["dt_bias"],
                     wts["dn_norm"], batch, seq)
    conv_new = dqkv.reshape(batch, seq, DN_QKV)[:, -(CONV_W - 1):]
    h, c, pq = _outproj(x, att, dn, wts["w_out"], wts["norm_ffn"], wts["peer_wq"])
    idx_t, gate_t = _route(pq, wts["keys1"], wts["keys2"])
    h2 = _peer_apply(idx_t, c, h, gate_t, wts["uv"])
    y = _final(h2, p, wts["norm_ple"], wts["ple_gate"], wts["ple_in"], wts["norm_final"])
    return y, k_new, v_new, conv_new, s_new


def kernel(x_prompt, x_sample, p_prompt, p_sample, cache_swa_k, cache_swa_v, state_conv, state_delta, norm_mix, w_in, conv_w, attn_sinks, dn_a_log, dn_dt_bias, dn_norm, w_out, norm_ffn, peer_wq, peer_keys1, peer_keys2, peer_u, peer_v, norm_ple, ple_in, ple_gate, norm_final):
    depth = w_in.shape[0]
    assert depth == 1
    bp, lp = x_prompt.shape[0], x_prompt.shape[1]
    bs, ls = x_sample.shape[0], x_sample.shape[1]
    l = 0
    wts = dict(
        norm_mix=norm_mix[l].reshape(1, D_MODEL),
        w_in=jnp.pad(w_in[l], ((0, 0), (0, IN_WIDTH_PADDED - IN_WIDTH))).astype(BF16),
        conv_w=conv_w[l], sinks=attn_sinks[l], a_log=dn_a_log[l], dt_bias=dn_dt_bias[l], dn_norm=dn_norm[l],
        w_out=w_out[l].astype(BF16), norm_ffn=norm_ffn[l].reshape(1, D_MODEL), peer_wq=peer_wq[l].astype(BF16),
        keys1=peer_keys1[l].astype(BF16), keys2=peer_keys2[l].astype(BF16),
        uv=jnp.concatenate([peer_u[l][:, None, :], peer_v[l][:, None, :]], axis=-1),
        norm_ple=norm_ple[l].reshape(1, D_MODEL), ple_in=ple_in[l].astype(BF16),
        ple_gate=ple_gate[l].astype(BF16), norm_final=norm_final.reshape(1, D_MODEL))
    conv0 = jnp.zeros((bp, CONV_W - 1, DN_QKV), F32)
    s0 = jnp.zeros((bp, DN_HEADS, HEAD_DIM, HEAD_DIM), F32)
    yp, kp, vp, cp, sp = _layer(x_prompt.reshape(bp * lp, D_MODEL), p_prompt[l].reshape(bp * lp, PLE_DIM),
                                None, conv0, s0, wts, bp, lp)
    kv_buf = (cache_swa_k[l].reshape(bs, WINDOW, ATT_KV), cache_swa_v[l].reshape(bs, WINDOW, ATT_KV))
    ys, kn, vn, cn, sn = _layer(x_sample.reshape(bs * ls, D_MODEL), p_sample[l].reshape(bs * ls, PLE_DIM),
                                kv_buf, state_conv[l], state_delta[l], wts, bs, ls)
    kvshape = lambda b: (1, b, WINDOW, ATT_KV_HEADS, HEAD_DIM)
    return (yp.reshape(bp, lp, D_MODEL), ys.reshape(bs, ls, D_MODEL),
            kp.reshape(kvshape(bp)), vp.reshape(kvshape(bp)), cp[None], sp[None],
            kn.reshape(kvshape(bs)), vn.reshape(kvshape(bs)), cn[None], sn[None])
```

```python
import functools
import math

import jax
import jax.numpy as jnp
from jax import lax
from jax.experimental import pallas as pl
from jax.experimental.pallas import tpu as pltpu

F32 = jnp.float32
BF16 = jnp.bfloat16
I32 = jnp.int32
HIGHEST = lax.Precision.HIGHEST

D_MODEL = 1024
HEAD_DIM = 64
ATT_HEADS = 8
ATT_KV_HEADS = 2
ATT_GROUP = 4
WINDOW = 128
ATT_Q = ATT_HEADS * HEAD_DIM
ATT_KV = ATT_KV_HEADS * HEAD_DIM
DN_HEADS = 8
DN_WIDTH = DN_HEADS * HEAD_DIM
DN_QKV = 3 * DN_WIDTH
CONV_W = 4
DN_CHUNK = 64
IN_WIDTH = ATT_Q + 2 * ATT_KV + DN_QKV + DN_WIDTH + 2 * DN_HEADS
IN_WIDTH_PADDED = 2944
N_KEYS = 128
PEER_HEADS = 8
PEER_QDIM = 256
PEER_TOPK = 16
PEER_SLOTS = PEER_HEADS * PEER_TOPK
PLE_DIM = 256
EPS = 1e-6
NEG_INF = float("-inf")
INDEX_SENTINEL = 1.0e9

VMEM_LIMIT = 48 * 1024 * 1024


def _cparams(*sem):
    return pltpu.CompilerParams(dimension_semantics=sem, vmem_limit_bytes=VMEM_LIMIT)


def _rms(x, g):
    return x * lax.rsqrt(jnp.mean(x * x, axis=-1, keepdims=True) + EPS) * g


def _bdot(a, b):
    return jnp.dot(a.astype(BF16), b.astype(BF16), preferred_element_type=F32)


def _fdot(a, b):
    return jnp.dot(a, b, precision=HIGHEST, preferred_element_type=F32)


_NN = (((1,), (0,)), ((), ()))
_NT = (((1,), (1,)), ((), ()))
_TN = (((0,), (0,)), ((), ()))


def _row_tile(t, target):
    tile = min(t, target)
    assert t % tile == 0, (t, tile)
    return tile


def _inproj_kernel(x_ref, g_ref, w_ref, q_ref, k_ref, v_ref, dqkv_ref, z_ref, ab_ref):
    a = _rms(x_ref[...], g_ref[...])
    p = jnp.dot(a.astype(BF16), w_ref[...], preferred_element_type=F32)
    c0, c1, c2, c3, c4 = ATT_Q, ATT_Q + ATT_KV, ATT_Q + 2 * ATT_KV, ATT_Q + 2 * ATT_KV + DN_QKV, \
        ATT_Q + 2 * ATT_KV + DN_QKV + DN_WIDTH
    q_ref[...] = p[:, :c0]
    k_ref[...] = p[:, c0:c1]
    v_ref[...] = p[:, c1:c2]
    dqkv_ref[...] = p[:, c2:c3]
    z_ref[...] = p[:, c3:c4]
    ab_ref[...] = p[:, c4:c4 + 2 * DN_HEADS]


def _inproj(x, g, w_pad):
    t = x.shape[0]
    tm = _row_tile(t, 512)
    widths = (ATT_Q, ATT_KV, ATT_KV, DN_QKV, DN_WIDTH, 2 * DN_HEADS)
    row = lambda i: (i, 0)
    fixed = lambda i: (0, 0)
    return pl.pallas_call(
        _inproj_kernel, name="inproj", grid=(t // tm,),
        out_shape=[jax.ShapeDtypeStruct((t, w), F32) for w in widths],
        in_specs=[pl.BlockSpec((tm, D_MODEL), row), pl.BlockSpec((1, D_MODEL), fixed),
                  pl.BlockSpec((D_MODEL, IN_WIDTH_PADDED), fixed)],
        out_specs=[pl.BlockSpec((tm, w), row) for w in widths],
        compiler_params=_cparams("parallel"))(x, g, w_pad)


def _softmax_sink_pv(s, sink, vband):
    m = jnp.maximum(jnp.max(s, axis=-1, keepdims=True), sink)
    e = jnp.exp(s - m)
    denom = jnp.sum(e, axis=-1, keepdims=True) + jnp.exp(sink - m)
    return _bdot(e / denom, vband)


def _alibi_slope(h):
    return 2.0 ** (-8.0 * (h + 1) / ATT_HEADS)


def _group_columns(kh, rows, sink_ref):
    rg = lax.broadcasted_iota(I32, (ATT_GROUP * rows, 1), 0) // rows
    slope = jnp.full((ATT_GROUP * rows, 1), _alibi_slope(kh * ATT_GROUP), F32)
    sink = jnp.full((ATT_GROUP * rows, 1), sink_ref[kh * ATT_GROUP], F32)
    for j in range(1, ATT_GROUP):
        slope = jnp.where(rg == j, _alibi_slope(kh * ATT_GROUP + j), slope)
        sink = jnp.where(rg == j, sink_ref[kh * ATT_GROUP + j], sink)
    return slope, sink


def _attend_kv_head(q_all, kband, vband, kh, dist, valid, sink_ref):
    rows = q_all.shape[0]
    heads = range(kh * ATT_GROUP, (kh + 1) * ATT_GROUP)
    q4 = jnp.concatenate([q_all[:, h * HEAD_DIM:(h + 1) * HEAD_DIM] for h in heads], axis=0)
    k = kband[:, kh * HEAD_DIM:(kh + 1) * HEAD_DIM]
    v = vband[:, kh * HEAD_DIM:(kh + 1) * HEAD_DIM]
    s = lax.dot_general(q4.astype(BF16), k.astype(BF16), _NT, preferred_element_type=F32) * (HEAD_DIM ** -0.5)
    slope, sink = _group_columns(kh, rows, sink_ref)
    s = jnp.where(valid, s - slope * dist, NEG_INF)
    o4 = _softmax_sink_pv(s, sink, v)
    return [o4[j * rows:(j + 1) * rows] for j in range(ATT_GROUP)]


def _swa_prompt_kernel(sink_ref, q_ref, kc_ref, kp_ref, vc_ref, vp_ref, o_ref):
    n = pl.program_id(1)
    rows = ATT_GROUP * WINDOW
    qi = lax.broadcasted_iota(I32, (rows, 2 * WINDOW), 0) % WINDOW
    kj = lax.broadcasted_iota(I32, (rows, 2 * WINDOW), 1)
    dist = WINDOW + qi - kj
    kpos = n * WINDOW - WINDOW + kj
    valid = (dist >= 0) & (dist <= WINDOW) & (kpos >= 0)
    distf = dist.astype(F32)
    kband = jnp.concatenate([kp_ref[...], kc_ref[...]], axis=0)
    vband = jnp.concatenate([vp_ref[...], vc_ref[...]], axis=0)
    q_all = q_ref[...]
    outs = []
    for kh in range(ATT_KV_HEADS):
        outs += _attend_kv_head(q_all, kband, vband, kh, distf, valid, sink_ref)
    o_ref[...] = jnp.concatenate(outs, axis=-1)


def _swa_prompt(q, k, v, sinks, batch, seq):
    nblk = seq // WINDOW
    cur = lambda b, n: (b * nblk + n, 0)
    prev = lambda b, n: (b * nblk + jnp.maximum(n - 1, 0), 0)
    return pl.pallas_call(
        _swa_prompt_kernel, name="swa_prompt", grid=(batch, nblk),
        out_shape=jax.ShapeDtypeStruct((batch * seq, ATT_Q), F32),
        in_specs=[pl.BlockSpec(memory_space=pltpu.SMEM),
                  pl.BlockSpec((WINDOW, ATT_Q), cur),
                  pl.BlockSpec((WINDOW, ATT_KV), cur), pl.BlockSpec((WINDOW, ATT_KV), prev),
                  pl.BlockSpec((WINDOW, ATT_KV), cur), pl.BlockSpec((WINDOW, ATT_KV), prev)],
        out_specs=pl.BlockSpec((WINDOW, ATT_Q), cur),
        compiler_params=_cparams("parallel", "parallel"))(sinks, q, k, k, v, v)


def _swa_sample_kernel(sink_ref, q_ref, k_ref, v_ref, kb_ref, vb_ref, o_ref, ko_ref, vo_ref, *, bt, seq):
    klen = WINDOW + seq
    rows = ATT_GROUP * seq
    qi = lax.broadcasted_iota(I32, (rows, klen), 0) % seq
    kj = lax.broadcasted_iota(I32, (rows, klen), 1)
    dist = qi - (kj - WINDOW)
    valid = (dist >= 0) & (dist <= WINDOW)
    distf = dist.astype(F32)
    for b in range(bt):
        kc = jnp.concatenate([kb_ref[b], k_ref[b]], axis=0)
        vc = jnp.concatenate([vb_ref[b], v_ref[b]], axis=0)
        ko_ref[b] = kc[seq:]
        vo_ref[b] = vc[seq:]
        q_all = q_ref[b]
        outs = []
        for kh in range(ATT_KV_HEADS):
            outs += _attend_kv_head(q_all, kc, vc, kh, distf, valid, sink_ref)
        o_ref[b] = jnp.concatenate(outs, axis=-1)


def _swa_sample(q, k, v, k_buf, v_buf, sinks):
    batch, seq = q.shape[0], q.shape[1]
    bt = _row_tile(batch, 8)
    blk = lambda w, r: pl.BlockSpec((bt, r, w), lambda i: (i, 0, 0))
    return pl.pallas_call(
        functools.partial(_swa_sample_kernel, bt=bt, seq=seq), name="swa_sample", grid=(batch // bt,),
        out_shape=[jax.ShapeDtypeStruct((batch, seq, ATT_Q), F32),
                   jax.ShapeDtypeStruct((batch, WINDOW, ATT_KV), F32),
                   jax.ShapeDtypeStruct((batch, WINDOW, ATT_KV), F32)],
        in_specs=[pl.BlockSpec(memory_space=pltpu.SMEM), blk(ATT_Q, seq), blk(ATT_KV, seq), blk(ATT_KV, seq),
                  blk(ATT_KV, WINDOW), blk(ATT_KV, WINDOW)],
        out_specs=[blk(ATT_Q, seq), blk(ATT_KV, WINDOW), blk(ATT_KV, WINDOW)],
        compiler_params=_cparams("parallel"))(sinks, q, k, v, k_buf, v_buf)


def _split_bf16(a, cache):
    if id(a) not in cache:
        hi = a.astype(BF16)
        cache[id(a)] = (a, (hi, (a - hi.astype(F32)).astype(BF16)))
    return cache[id(a)][1]


MXU_DEPTH = 256


def _dot3_impl(a, b, dims=_NN, *, cache):
    (ca,), (cb,) = dims[0]
    depth = a.shape[ca]
    ah, al = _split_bf16(a, cache)
    bh, bl = _split_bf16(b, cache)
    f = lambda x, y: lax.dot_general(x, y, dims, preferred_element_type=F32)
    if 3 * depth <= MXU_DEPTH:
        return f(jnp.concatenate([ah, al, ah], axis=ca), jnp.concatenate([bh, bh, bl], axis=cb))
    return f(jnp.concatenate([ah, al], axis=ca), jnp.concatenate([bh, bh], axis=cb)) + f(ah, bl)


def _gdn_kernel(alog_ref, dtb_ref, dqkv_ref, z_ref, ab_ref, cw_ref, nrm_ref, pre_ref, s0_ref,
                o_ref, sfin_ref, tail_ref, state_ref, *, c, g, nsub):
    n = pl.program_id(1)
    ngrp = DN_HEADS // g
    r = g * c
    rows = nsub * c
    hd = HEAD_DIM
    _dot3 = functools.partial(_dot3_impl, cache={})

    @pl.when(n == 0)
    def _():
        tail_ref[...] = jnp.concatenate([jnp.zeros((8 - (CONV_W - 1), DN_QKV), F32), pre_ref[0]], axis=0)
        for gi in range(ngrp):
            state_ref[gi] = jnp.concatenate([s0_ref[0, gi * g + j] for j in range(g)], axis=1)

    x = dqkv_ref[...]
    hist = jnp.concatenate([tail_ref[...], x], axis=0)
    tail_ref[...] = hist[rows:]
    cw = cw_ref[...]
    y = x * cw[CONV_W - 1:CONV_W]
    for d in range(1, CONV_W):
        y = y + hist[8 - d:8 - d + rows] * cw[CONV_W - 1 - d:CONV_W - d]
    qkv = y * jax.nn.sigmoid(y)
    z = z_ref[...]
    ab = ab_ref[...]
    beta_all = jax.nn.sigmoid(ab[:, DN_HEADS:2 * DN_HEADS])
    g_all = -jnp.exp(alog_ref[...]) * jax.nn.softplus(ab[:, 0:DN_HEADS] + dtb_ref[...])

    ri = lax.broadcasted_iota(I32, (r, r), 0)
    ci = lax.broadcasted_iota(I32, (r, r), 1)
    same = (ri // c) == (ci // c)
    lower = same & (ri >= ci)
    strict = same & (ri > ci)
    tri = lower.astype(F32)
    blk = same.astype(F32)
    eye = (ri == ci).astype(F32)
    rowhead = lax.broadcasted_iota(I32, (r, hd), 0) // c

    def stack(a, sub, gi, base):
        return jnp.concatenate([a[sub * c:(sub + 1) * c, base + (gi * g + j) * hd:base + (gi * g + j + 1) * hd]
                                for j in range(g)], axis=0)

    def column(a, sub, gi):
        return jnp.concatenate([a[sub * c:(sub + 1) * c, gi * g + j:gi * g + j + 1] for j in range(g)], axis=0)

    def own_block(m):
        acc = jnp.where(rowhead == 0, m[:, 0:hd], 0.0)
        for j in range(1, g):
            acc = acc + jnp.where(rowhead == j, m[:, j * hd:(j + 1) * hd], 0.0)
        return acc

    keys = [(sub, gi) for sub in range(nsub) for gi in range(ngrp)]
    grp = {}
    for sub, gi in keys:
        q = stack(qkv, sub, gi, 0)
        k = stack(qkv, sub, gi, DN_WIDTH)
        v = stack(qkv, sub, gi, 2 * DN_WIDTH)
        q = q * lax.rsqrt(jnp.sum(q * q, axis=-1, keepdims=True) + EPS) * (hd ** -0.5)
        k = k * lax.rsqrt(jnp.sum(k * k, axis=-1, keepdims=True) + EPS)
        beta = column(beta_all, sub, gi)
        gr = column(g_all, sub, gi)
        gcol = _fdot(tri, gr)
        grow = lax.dot_general(gr, tri, (((0,), (1,)), ((), ())), precision=HIGHEST,
                               preferred_element_type=F32)
        glast = _fdot(blk, gr)
        decay = jnp.exp(jnp.where(lower, gcol - grow, NEG_INF))
        kb = k * beta
        lm = jnp.where(strict, _dot3(kb, k, _NT) * decay, 0.0)
        grp[sub, gi] = dict(q=q, k=k, v=v, beta=beta, gcol=gcol, glast=glast, decay=decay, kb=kb, lm=lm)

    pair = ((ri // 2) == (ci // 2)) & (ri % 2 == 1) & (ci % 2 == 0)
    invs = {key: eye - jnp.where(pair, grp[key]["lm"], 0.0) for key in keys}
    s = 2
    while s < c:
        off = ((ri // (2 * s)) == (ci // (2 * s))) & ((ri // s) % 2 == 1) & ((ci // s) % 2 == 0)
        tmp = {key: _dot3(invs[key], jnp.where(off, grp[key]["lm"], 0.0)) for key in keys}
        invs = {key: invs[key] - _dot3(tmp[key], invs[key]) for key in keys}
        s *= 2

    for sub in range(nsub):
        outs = [None] * DN_HEADS
        for gi in range(ngrp):
            d = grp[sub, gi]
            eg = jnp.exp(d["gcol"])
            sol = _dot3(invs[sub, gi], jnp.concatenate([d["v"] * d["beta"], d["kb"] * eg], axis=1))
            u = sol[:, :hd]
            w = sol[:, hd:]
            qk = _dot3(d["q"], d["k"], _NT) * d["decay"]
            ktail = d["k"] * jnp.exp(d["glast"] - d["gcol"])
            qhead = d["q"] * eg
            st = state_ref[gi]
            vnew = u - own_block(_dot3(w, st))
            o = own_block(_dot3(qhead, st)) + _dot3(qk, vnew)
            vblk = jnp.concatenate([jnp.where(rowhead == j, vnew, 0.0) for j in range(g)], axis=1)
            elast = jnp.exp(d["glast"])
            keep = jnp.concatenate([jnp.broadcast_to(elast[j * c:j * c + 1, :], (1, hd)) for j in range(g)],
                                   axis=1)
            state_ref[gi] = st * keep + _dot3(ktail, vblk, _TN)
            zs = stack(z, sub, gi, 0)
            o = _rms(o, nrm_ref[...]) * (zs * jax.nn.sigmoid(zs))
            for j in range(g):
                outs[gi * g + j] = o[j * c:(j + 1) * c]
        o_ref[sub * c:(sub + 1) * c, :] = jnp.concatenate(outs, axis=-1)

    @pl.when(n == pl.num_programs(1) - 1)
    def _():
        for gi in range(ngrp):
            for j in range(g):
                sfin_ref[0, gi * g + j] = state_ref[gi][:, j * hd:(j + 1) * hd]


GDN_CHUNKS_PER_STEP = 2


def _gdn(dqkv, z, ab, conv_prefix, s0, conv_w, a_log, dt_bias, dn_norm, batch, seq):
    c = min(DN_CHUNK, seq)
    nchunk = seq // c
    assert seq % c == 0 and c % 8 == 0
    g = min(DN_HEADS, max(1, 128 // c))
    nsub = GDN_CHUNKS_PER_STEP if nchunk % GDN_CHUNKS_PER_STEP == 0 else 1
    nstep = nchunk // nsub
    rows = nsub * c
    tok = lambda b, n: (b * nstep + n, 0)
    fixed = lambda b, n: (0, 0)
    return pl.pallas_call(
        functools.partial(_gdn_kernel, c=c, g=g, nsub=nsub), name="gdn", grid=(batch, nstep),
        out_shape=[jax.ShapeDtypeStruct((batch * seq, DN_WIDTH), F32),
                   jax.ShapeDtypeStruct((batch, DN_HEADS, HEAD_DIM, HEAD_DIM), F32)],
        in_specs=[pl.BlockSpec((1, DN_HEADS), fixed), pl.BlockSpec((1, DN_HEADS), fixed),
                  pl.BlockSpec((rows, DN_QKV), tok), pl.BlockSpec((rows, DN_WIDTH), tok),
                  pl.BlockSpec((rows, 2 * DN_HEADS), tok),
                  pl.BlockSpec((CONV_W, DN_QKV), fixed), pl.BlockSpec((1, HEAD_DIM), fixed),
                  pl.BlockSpec((1, CONV_W - 1, DN_QKV), lambda b, n: (b, 0, 0)),
                  pl.BlockSpec((1, DN_HEADS, HEAD_DIM, HEAD_DIM), lambda b, n: (b, 0, 0, 0))],
        out_specs=[pl.BlockSpec((rows, DN_WIDTH), tok),
                   pl.BlockSpec((1, DN_HEADS, HEAD_DIM, HEAD_DIM), lambda b, n: (b, 0, 0, 0))],
        scratch_shapes=[pltpu.VMEM((8, DN_QKV), F32),
                        pltpu.VMEM((DN_HEADS // g, HEAD_DIM, g * HEAD_DIM), F32)],
        compiler_params=_cparams("parallel", "arbitrary"))(
            a_log.reshape(1, DN_HEADS), dt_bias.reshape(1, DN_HEADS), dqkv, z, ab, conv_w,
            dn_norm.reshape(1, HEAD_DIM), conv_prefix, s0)


def _outproj_kernel(x_ref, att_ref, dn_ref, wo_ref, g_ref, wq_ref, h_ref, c_ref, q_ref):
    mix = jnp.concatenate([att_ref[...], dn_ref[...]], axis=-1)
    h = x_ref[...] + jnp.dot(mix.astype(BF16), wo_ref[...], preferred_element_type=F32)
    h_ref[...] = h
    cn = _rms(h, g_ref[...])
    c_ref[...] = cn
    q_ref[...] = jnp.dot(cn.astype(BF16), wq_ref[...], preferred_element_type=F32)


def _outproj(x, att, dn, w_out, g_ffn, wq):
    t = x.shape[0]
    tm = _row_tile(t, 512)
    row = lambda i: (i, 0)
    fixed = lambda i: (0, 0)
    qw = PEER_HEADS * PEER_QDIM
    return pl.pallas_call(
        _outproj_kernel, name="outproj", grid=(t // tm,),
        out_shape=[jax.ShapeDtypeStruct((t, D_MODEL), F32), jax.ShapeDtypeStruct((t, D_MODEL), F32),
                   jax.ShapeDtypeStruct((t, qw), F32)],
        in_specs=[pl.BlockSpec((tm, D_MODEL), row), pl.BlockSpec((tm, ATT_Q), row),
                  pl.BlockSpec((tm, DN_WIDTH), row), pl.BlockSpec((D_MODEL, D_MODEL), fixed),
                  pl.BlockSpec((1, D_MODEL), fixed), pl.BlockSpec((D_MODEL, qw), fixed)],
        out_specs=[pl.BlockSpec((tm, D_MODEL), row), pl.BlockSpec((tm, D_MODEL), row),
                   pl.BlockSpec((tm, qw), row)],
        compiler_params=_cparams("parallel"))(x, att, dn, w_out, g_ffn, wq)


def _top16_rows(s, iota):
    big = jnp.float32(INDEX_SENTINEL)
    vals, idxs = [], []
    for _ in range(PEER_TOPK):
        m = jnp.max(s, axis=0, keepdims=True)
        i = jnp.min(jnp.where(s == m, iota, big), axis=0, keepdims=True)
        vals.append(m)
        idxs.append(i)
        s = jnp.where(iota == i, NEG_INF, s)
    return jnp.concatenate(vals, axis=0), jnp.concatenate(idxs, axis=0)


def _candidates(a, b, row8):
    t = a.shape[1]
    lo = row8 < 4
    bc = lambda r, n: jnp.broadcast_to(a[r:r + 1], (n, t))
    b8 = b[0:8]
    b44 = jnp.where(lo, b8, pltpu.roll(b8, 4, 0))
    parts = [bc(0, 8), bc(0, 8), bc(1, 8), bc(2, 8), bc(3, 8),
             jnp.where(lo, bc(4, 8), bc(5, 8)), jnp.where(lo, bc(6, 8), bc(7, 8)), a[8:16]]
    others = [b[0:8], b[8:16], b8, b8, b8, b44, b44, jnp.broadcast_to(b[0:1], (8, t))]
    return parts, others


def _flat_index_column():
    r = lax.broadcasted_iota(I32, (8, 1), 0)
    lo = r < 4
    cols = [r, 8 + r, 16 + r, 32 + r, 48 + r, jnp.where(lo, 64 + r, 80 + r - 4),
            jnp.where(lo, 96 + r, 112 + r - 4), (8 + r) * 16]
    return jnp.concatenate(cols, axis=0).astype(F32)


def _route_kernel(q_ref, k1_ref, k2_ref, idx_ref, gate_ref, *, tb):
    iota = lax.broadcasted_iota(I32, (N_KEYS, tb), 0).astype(F32)
    row8 = lax.broadcasted_iota(I32, (8, tb), 0)
    flat = jnp.broadcast_to(_flat_index_column(), (4 * PEER_TOPK, tb))
    big = jnp.float32(INDEX_SENTINEL)
    half = PEER_QDIM // 2
    for h in range(PEER_HEADS):
        q1 = q_ref[:, h * PEER_QDIM:h * PEER_QDIM + half]
        q2 = q_ref[:, h * PEER_QDIM + half:(h + 1) * PEER_QDIM]
        nt = (((1,), (1,)), ((), ()))
        s1 = lax.dot_general(k1_ref[h], q1.astype(BF16), nt, preferred_element_type=F32)
        s2 = lax.dot_general(k2_ref[h], q2.astype(BF16), nt, preferred_element_type=F32)
        v1, i1 = _top16_rows(s1, iota)
        v2, i2 = _top16_rows(s2, iota)
        pa, pb = _candidates(v1, v2, row8)
        cand = jnp.concatenate([x + y for x, y in zip(pa, pb)], axis=0)
        ea, eb = _candidates(i1, i2, row8)
        expert = jnp.concatenate([x * float(N_KEYS) + y for x, y in zip(ea, eb)], axis=0)
        scs, exs = [], []
        for _ in range(PEER_TOPK):
            m = jnp.max(cand, axis=0, keepdims=True)
            c = jnp.min(jnp.where(cand == m, flat, big), axis=0, keepdims=True)
            hit = flat == c
            exs.append(jnp.max(jnp.where(hit, expert, -1.0), axis=0, keepdims=True))
            scs.append(m)
            cand = jnp.where(hit, NEG_INF, cand)
        sc = jnp.concatenate(scs, axis=0)
        e = jnp.exp(sc - sc[0:1])
        gate_ref[h * PEER_TOPK:(h + 1) * PEER_TOPK, :] = e / jnp.sum(e, axis=0, keepdims=True)
        idx_ref[h * PEER_TOPK:(h + 1) * PEER_TOPK, :] = jnp.concatenate(exs, axis=0).astype(I32)


def _route(q, keys1, keys2):
    t = q.shape[0]
    tb = _row_tile(t, 256)
    kspec = pl.BlockSpec((PEER_HEADS, N_KEYS, PEER_QDIM // 2), lambda i: (0, 0, 0))
    return pl.pallas_call(
        functools.partial(_route_kernel, tb=tb), name="peer_route", grid=(t // tb,),
        out_shape=[jax.ShapeDtypeStruct((PEER_SLOTS, t), I32), jax.ShapeDtypeStruct((PEER_SLOTS, t), F32)],
        in_specs=[pl.BlockSpec((tb, PEER_HEADS * PEER_QDIM), lambda i: (i, 0)), kspec, kspec],
        out_specs=[pl.BlockSpec((PEER_SLOTS, tb), lambda i: (0, i)),
                   pl.BlockSpec((PEER_SLOTS, tb), lambda i: (0, i))],
        compiler_params=_cparams("parallel"))(q, keys1, keys2)


GATHER_SLOTS = 3


def _peer_apply_kernel(idx_ref, c_ref, h_ref, gt_ref, uv_ref, o_ref, buf, sem, *, tb):
    ahead = GATHER_SLOTS - 1

    def issue(t, slot):
        for k in range(PEER_SLOTS):
            e = idx_ref[k, t]
            pltpu.make_async_copy(uv_ref.at[e], buf.at[slot, pl.ds(k, 1), :],
                                  sem.at[slot]).start(priority=k % 2)

    def wait(slot):
        pltpu.make_async_copy(buf.at[slot], buf.at[slot], sem.at[slot]).wait()

    lane = lax.broadcasted_iota(I32, (PEER_SLOTS, tb), 1)

    def apply(t, slot):
        x = c_ref[pl.ds(t, 1), :]
        packed = buf[slot]
        u = lax.bitcast_convert_type(packed & jnp.uint32(0xFFFF0000), F32)
        v = lax.bitcast_convert_type(packed << 16, F32)
        p = u * x
        acc = p[:, 0:128]
        for j in range(1, D_MODEL // 128):
            acc = acc + p[:, j * 128:(j + 1) * 128]
        s = jnp.sum(acc, axis=-1, keepdims=True)
        act = 0.5 * s * (1.0 + lax.erf(s * (1.0 / math.sqrt(2.0))))
        gate = jnp.sum(jnp.where(lane == t, gt_ref[...], 0.0), axis=-1, keepdims=True)
        w = gate * act
        y = jnp.sum(w * v, axis=0, keepdims=True)
        o_ref[pl.ds(t, 1), :] = h_ref[pl.ds(t, 1), :] + y

    for t0 in range(ahead):
        issue(t0, t0)

    def body(i, carry):
        for j in range(GATHER_SLOTS):
            t = i * GATHER_SLOTS + j
            issue(t + ahead, (j + ahead) % GATHER_SLOTS)
            wait(j)
            apply(t, j)
        return carry

    nmain = (tb - ahead) // GATHER_SLOTS
    lax.fori_loop(0, nmain, body, 0)
    for t in range(nmain * GATHER_SLOTS, tb):
        if t + ahead < tb:
            issue(t + ahead, (t + ahead) % GATHER_SLOTS)
        wait(t % GATHER_SLOTS)
        apply(t, t % GATHER_SLOTS)


def _pack_expert_rows(u, v):
    hi = lax.bitcast_convert_type(u.astype(BF16), jnp.uint16).astype(jnp.uint32) << 16
    lo = lax.bitcast_convert_type(v.astype(BF16), jnp.uint16).astype(jnp.uint32)
    return (hi | lo)[:, None, :]


def _peer_apply(idx_t, c, h, gate_t, uv):
    t = c.shape[0]
    tb = _row_tile(t, 256)
    row = lambda i: (i, 0)
    return pl.pallas_call(
        functools.partial(_peer_apply_kernel, tb=tb), name="peer_apply", grid=(t // tb,),
        out_shape=jax.ShapeDtypeStruct((t, D_MODEL), F32),
        in_specs=[pl.BlockSpec((PEER_SLOTS, tb), lambda i: (0, i), memory_space=pltpu.SMEM),
                  pl.BlockSpec((tb, D_MODEL), row), pl.BlockSpec((tb, D_MODEL), row),
                  pl.BlockSpec((PEER_SLOTS, tb), lambda i: (0, i)),
                  pl.BlockSpec(memory_space=pl.ANY)],
        out_specs=pl.BlockSpec((tb, D_MODEL), row),
        scratch_shapes=[pltpu.VMEM((GATHER_SLOTS, PEER_SLOTS, D_MODEL), jnp.uint32),
                        pltpu.SemaphoreType.DMA((GATHER_SLOTS,))],
        compiler_params=_cparams("arbitrary"))(idx_t, c, h, gate_t, uv)


def _final_kernel(h_ref, p_ref, gple_ref, wg_ref, wp_ref, gfin_ref, y_ref):
    h = h_ref[...]
    e = _rms(h, gple_ref[...])
    gate = jax.nn.sigmoid(jnp.dot(e.astype(BF16), wg_ref[...], preferred_element_type=F32))
    up = jnp.dot(p_ref[...].astype(BF16), wp_ref[...], preferred_element_type=F32)
    y_ref[...] = _rms(h + up * gate, gfin_ref[...])


def _final(h, p, g_ple, w_gate, w_ple, g_final):
    t = h.shape[0]
    tm = _row_tile(t, 512)
    row = lambda i: (i, 0)
    fixed = lambda i: (0, 0)
    return pl.pallas_call(
        _final_kernel, name="ple_final", grid=(t // tm,),
        out_shape=jax.ShapeDtypeStruct((t, D_MODEL), F32),
        in_specs=[pl.BlockSpec((tm, D_MODEL), row), pl.BlockSpec((tm, PLE_DIM), row),
                  pl.BlockSpec((1, D_MODEL), fixed), pl.BlockSpec((D_MODEL, D_MODEL), fixed),
                  pl.BlockSpec((PLE_DIM, D_MODEL), fixed), pl.BlockSpec((1, D_MODEL), fixed)],
        out_specs=pl.BlockSpec((tm, D_MODEL), row),
        compiler_params=_cparams("parallel"))(h, p, g_ple, w_gate, w_ple, g_final)


def _layer(x, p, kv_buf, conv_prefix, s0, wts, batch, seq):
    t = batch * seq
    q, k, v, dqkv, z, ab = _inproj(x, wts["norm_mix"], wts["w_in"])
    if kv_buf is None:
        att = _swa_prompt(q, k, v, wts["sinks"], batch, seq)
        k_new = k.reshape(batch, seq, ATT_KV)[:, -WINDOW:]
        v_new = v.reshape(batch, seq, ATT_KV)[:, -WINDOW:]
    else:
        att, k_new, v_new = _swa_sample(q.reshape(batch, seq, ATT_Q), k.reshape(batch, seq, ATT_KV),
                                        v.reshape(batch, seq, ATT_KV), kv_buf[0], kv_buf[1], wts["sinks"])
        att = att.reshape(t, ATT_Q)
    dn, s_new = _gdn(dqkv, z, ab, conv_prefix, s0, wts["conv_w"], wts["a_log"], wts["dt_bias"],
                     wts["dn_norm"], batch, seq)
    conv_new = dqkv.reshape(batch, seq, DN_QKV)[:, -(CONV_W - 1):]
    h, c, pq = _outproj(x, att, dn, wts["w_out"], wts["norm_ffn"], wts["peer_wq"])
    idx_t, gate_t = _route(pq, wts["keys1"], wts["keys2"])
    h2 = _peer_apply(idx_t, c, h, gate_t, wts["uv"])
    y = _final(h2, p, wts["norm_ple"], wts["ple_gate"], wts["ple_in"], wts["norm_final"])
    return y, k_new, v_new, conv_new, s_new


def kernel(x_prompt, x_sample, p_prompt, p_sample, cache_swa_k, cache_swa_v, state_conv, state_delta, norm_mix, w_in, conv_w, attn_sinks, dn_a_log, dn_dt_bias, dn_norm, w_out, norm_ffn, peer_wq, peer_keys1, peer_keys2, peer_u, peer_v, norm_ple, ple_in, ple_gate, norm_final):
    depth = w_in.shape[0]
    assert depth == 1
    bp, lp = x_prompt.shape[0], x_prompt.shape[1]
    bs, ls = x_sample.shape[0], x_sample.shape[1]
    l = 0
    wts = dict(
        norm_mix=norm_mix[l].reshape(1, D_MODEL),
        w_in=jnp.pad(w_in[l], ((0, 0), (0, IN_WIDTH_PADDED - IN_WIDTH))).astype(BF16),
        conv_w=conv_w[l], sinks=attn_sinks[l], a_log=dn_a_log[l], dt_bias=dn_dt_bias[l], dn_norm=dn_norm[l],
        w_out=w_out[l].astype(BF16), norm_ffn=norm_ffn[l].reshape(1, D_MODEL), peer_wq=peer_wq[l].astype(BF16),
        keys1=peer_keys1[l].astype(BF16), keys2=peer_keys2[l].astype(BF16),
        uv=_pack_expert_rows(peer_u[l], peer_v[l]),
        norm_ple=norm_ple[l].reshape(1, D_MODEL), ple_in=ple_in[l].astype(BF16),
        ple_gate=ple_gate[l].astype(BF16), norm_final=norm_final.reshape(1, D_MODEL))
    conv0 = jnp.zeros((bp, CONV_W - 1, DN_QKV), F32)
    s0 = jnp.zeros((bp, DN_HEADS, HEAD_DIM, HEAD_DIM), F32)
    yp, kp, vp, cp, sp = _layer(x_prompt.reshape(bp * lp, D_MODEL), p_prompt[l].reshape(bp * lp, PLE_DIM),
                                None, conv0, s0, wts, bp, lp)
    kv_buf = (cache_swa_k[l].reshape(bs, WINDOW, ATT_KV), cache_swa_v[l].reshape(bs, WINDOW, ATT_KV))
    ys, kn, vn, cn, sn = _layer(x_sample.reshape(bs * ls, D_MODEL), p_sample[l].reshape(bs * ls, PLE_DIM),
                                kv_buf, state_conv[l], state_delta[l], wts, bs, ls)
    kvshape = lambda b: (1, b, WINDOW, ATT_KV_HEADS, HEAD_DIM)
    return (yp.reshape(bp, lp, D_MODEL), ys.reshape(bs, ls, D_MODEL),
            kp.reshape(kvshape(bp)), vp.reshape(kvshape(bp)), cp[None], sp[None],
            kn.reshape(kvshape(bs)), vn.reshape(kvshape(bs)), cn[None], sn[None])
```

```python
import dataclasses
import functools
import math

import jax
import jax.numpy as jnp
from jax import lax
from jax.experimental import pallas as pl
from jax.experimental.pallas import tpu as pltpu
from jax.experimental.pallas import tpu_sc as plsc

F32 = jnp.float32
BF16 = jnp.bfloat16
I32 = jnp.int32
HIGHEST = lax.Precision.HIGHEST

D_MODEL = 1024
HEAD_DIM = 64
ATT_HEADS = 8
ATT_KV_HEADS = 2
ATT_GROUP = 4
WINDOW = 128
ATT_Q = ATT_HEADS * HEAD_DIM
ATT_KV = ATT_KV_HEADS * HEAD_DIM
DN_HEADS = 8
DN_WIDTH = DN_HEADS * HEAD_DIM
DN_QKV = 3 * DN_WIDTH
CONV_W = 4
DN_CHUNK = 64
IN_WIDTH = ATT_Q + 2 * ATT_KV + DN_QKV + DN_WIDTH + 2 * DN_HEADS
IN_WIDTH_PADDED = 2944
N_KEYS = 128
PEER_HEADS = 8
PEER_QDIM = 256
PEER_TOPK = 16
PEER_SLOTS = PEER_HEADS * PEER_TOPK
PLE_DIM = 256
EPS = 1e-6
NEG_INF = float("-inf")
INDEX_SENTINEL = 1.0e9

VMEM_LIMIT = 48 * 1024 * 1024


def _cparams(*sem):
    return pltpu.CompilerParams(dimension_semantics=sem, vmem_limit_bytes=VMEM_LIMIT)


def _rms(x, g):
    return x * lax.rsqrt(jnp.mean(x * x, axis=-1, keepdims=True) + EPS) * g


def _bdot(a, b):
    return jnp.dot(a.astype(BF16), b.astype(BF16), preferred_element_type=F32)


def _fdot(a, b):
    return jnp.dot(a, b, precision=HIGHEST, preferred_element_type=F32)


_NN = (((1,), (0,)), ((), ()))
_NT = (((1,), (1,)), ((), ()))
_TN = (((0,), (0,)), ((), ()))


def _row_tile(t, target):
    tile = min(t, target)
    assert t % tile == 0, (t, tile)
    return tile


def _inproj_kernel(x_ref, g_ref, w_ref, q_ref, k_ref, v_ref, dqkv_ref, z_ref, ab_ref):
    a = _rms(x_ref[...], g_ref[...])
    p = jnp.dot(a.astype(BF16), w_ref[...], preferred_element_type=F32)
    c0, c1, c2, c3, c4 = ATT_Q, ATT_Q + ATT_KV, ATT_Q + 2 * ATT_KV, ATT_Q + 2 * ATT_KV + DN_QKV, \
        ATT_Q + 2 * ATT_KV + DN_QKV + DN_WIDTH
    q_ref[...] = p[:, :c0]
    k_ref[...] = p[:, c0:c1]
    v_ref[...] = p[:, c1:c2]
    dqkv_ref[...] = p[:, c2:c3]
    z_ref[...] = p[:, c3:c4]
    ab_ref[...] = p[:, c4:c4 + 2 * DN_HEADS]


def _inproj(x, g, w_pad):
    t = x.shape[0]
    tm = _row_tile(t, 512)
    widths = (ATT_Q, ATT_KV, ATT_KV, DN_QKV, DN_WIDTH, 2 * DN_HEADS)
    row = lambda i: (i, 0)
    fixed = lambda i: (0, 0)
    return pl.pallas_call(
        _inproj_kernel, name="inproj", grid=(t // tm,),
        out_shape=[jax.ShapeDtypeStruct((t, w), F32) for w in widths],
        in_specs=[pl.BlockSpec((tm, D_MODEL), row), pl.BlockSpec((1, D_MODEL), fixed),
                  pl.BlockSpec((D_MODEL, IN_WIDTH_PADDED), fixed)],
        out_specs=[pl.BlockSpec((tm, w), row) for w in widths],
        compiler_params=_cparams("parallel"))(x, g, w_pad)


def _softmax_sink_pv(s, sink, vband):
    m = jnp.maximum(jnp.max(s, axis=-1, keepdims=True), sink)
    e = jnp.exp(s - m)
    denom = jnp.sum(e, axis=-1, keepdims=True) + jnp.exp(sink - m)
    return _bdot(e / denom, vband)


def _alibi_slope(h):
    return 2.0 ** (-8.0 * (h + 1) / ATT_HEADS)


def _group_columns(kh, rows, sink_ref):
    rg = lax.broadcasted_iota(I32, (ATT_GROUP * rows, 1), 0) // rows
    slope = jnp.full((ATT_GROUP * rows, 1), _alibi_slope(kh * ATT_GROUP), F32)
    sink = jnp.full((ATT_GROUP * rows, 1), sink_ref[kh * ATT_GROUP], F32)
    for j in range(1, ATT_GROUP):
        slope = jnp.where(rg == j, _alibi_slope(kh * ATT_GROUP + j), slope)
        sink = jnp.where(rg == j, sink_ref[kh * ATT_GROUP + j], sink)
    return slope, sink


def _attend_kv_head(q_all, kband, vband, kh, dist, valid, sink_ref):
    rows = q_all.shape[0]
    heads = range(kh * ATT_GROUP, (kh + 1) * ATT_GROUP)
    q4 = jnp.concatenate([q_all[:, h * HEAD_DIM:(h + 1) * HEAD_DIM] for h in heads], axis=0)
    k = kband[:, kh * HEAD_DIM:(kh + 1) * HEAD_DIM]
    v = vband[:, kh * HEAD_DIM:(kh + 1) * HEAD_DIM]
    s = lax.dot_general(q4.astype(BF16), k.astype(BF16), _NT, preferred_element_type=F32) * (HEAD_DIM ** -0.5)
    slope, sink = _group_columns(kh, rows, sink_ref)
    s = jnp.where(valid, s - slope * dist, NEG_INF)
    o4 = _softmax_sink_pv(s, sink, v)
    return [o4[j * rows:(j + 1) * rows] for j in range(ATT_GROUP)]


def _swa_prompt_kernel(sink_ref, q_ref, kc_ref, kp_ref, vc_ref, vp_ref, o_ref):
    n = pl.program_id(1)
    rows = ATT_GROUP * WINDOW
    qi = lax.broadcasted_iota(I32, (rows, 2 * WINDOW), 0) % WINDOW
    kj = lax.broadcasted_iota(I32, (rows, 2 * WINDOW), 1)
    dist = WINDOW + qi - kj
    kpos = n * WINDOW - WINDOW + kj
    valid = (dist >= 0) & (dist <= WINDOW) & (kpos >= 0)
    distf = dist.astype(F32)
    kband = jnp.concatenate([kp_ref[...], kc_ref[...]], axis=0)
    vband = jnp.concatenate([vp_ref[...], vc_ref[...]], axis=0)
    q_all = q_ref[...]
    outs = []
    for kh in range(ATT_KV_HEADS):
        outs += _attend_kv_head(q_all, kband, vband, kh, distf, valid, sink_ref)
    o_ref[...] = jnp.concatenate(outs, axis=-1)


def _swa_prompt(q, k, v, sinks, batch, seq):
    nblk = seq // WINDOW
    cur = lambda b, n: (b * nblk + n, 0)
    prev = lambda b, n: (b * nblk + jnp.maximum(n - 1, 0), 0)
    return pl.pallas_call(
        _swa_prompt_kernel, name="swa_prompt", grid=(batch, nblk),
        out_shape=jax.ShapeDtypeStruct((batch * seq, ATT_Q), F32),
        in_specs=[pl.BlockSpec(memory_space=pltpu.SMEM),
                  pl.BlockSpec((WINDOW, ATT_Q), cur),
                  pl.BlockSpec((WINDOW, ATT_KV), cur), pl.BlockSpec((WINDOW, ATT_KV), prev),
                  pl.BlockSpec((WINDOW, ATT_KV), cur), pl.BlockSpec((WINDOW, ATT_KV), prev)],
        out_specs=pl.BlockSpec((WINDOW, ATT_Q), cur),
        compiler_params=_cparams("parallel", "parallel"))(sinks, q, k, k, v, v)


def _swa_sample_kernel(sink_ref, q_ref, k_ref, v_ref, kb_ref, vb_ref, o_ref, ko_ref, vo_ref, *, bt, seq):
    klen = WINDOW + seq
    rows = ATT_GROUP * seq
    qi = lax.broadcasted_iota(I32, (rows, klen), 0) % seq
    kj = lax.broadcasted_iota(I32, (rows, klen), 1)
    dist = qi - (kj - WINDOW)
    valid = (dist >= 0) & (dist <= WINDOW)
    distf = dist.astype(F32)
    for b in range(bt):
        kc = jnp.concatenate([kb_ref[b], k_ref[b]], axis=0)
        vc = jnp.concatenate([vb_ref[b], v_ref[b]], axis=0)
        ko_ref[b] = kc[seq:]
        vo_ref[b] = vc[seq:]
        q_all = q_ref[b]
        outs = []
        for kh in range(ATT_KV_HEADS):
            outs += _attend_kv_head(q_all, kc, vc, kh, distf, valid, sink_ref)
        o_ref[b] = jnp.concatenate(outs, axis=-1)


def _swa_sample(q, k, v, k_buf, v_buf, sinks):
    batch, seq = q.shape[0], q.shape[1]
    bt = _row_tile(batch, 8)
    blk = lambda w, r: pl.BlockSpec((bt, r, w), lambda i: (i, 0, 0))
    return pl.pallas_call(
        functools.partial(_swa_sample_kernel, bt=bt, seq=seq), name="swa_sample", grid=(batch // bt,),
        out_shape=[jax.ShapeDtypeStruct((batch, seq, ATT_Q), F32),
                   jax.ShapeDtypeStruct((batch, WINDOW, ATT_KV), F32),
                   jax.ShapeDtypeStruct((batch, WINDOW, ATT_KV), F32)],
        in_specs=[pl.BlockSpec(memory_space=pltpu.SMEM), blk(ATT_Q, seq), blk(ATT_KV, seq), blk(ATT_KV, seq),
                  blk(ATT_KV, WINDOW), blk(ATT_KV, WINDOW)],
        out_specs=[blk(ATT_Q, seq), blk(ATT_KV, WINDOW), blk(ATT_KV, WINDOW)],
        compiler_params=_cparams("parallel"))(sinks, q, k, v, k_buf, v_buf)


def _split_bf16(a, cache):
    if id(a) not in cache:
        hi = a.astype(BF16)
        cache[id(a)] = (a, (hi, (a - hi.astype(F32)).astype(BF16)))
    return cache[id(a)][1]


MXU_DEPTH = 256


def _dot3_impl(a, b, dims=_NN, *, cache):
    (ca,), (cb,) = dims[0]
    depth = a.shape[ca]
    ah, al = _split_bf16(a, cache)
    bh, bl = _split_bf16(b, cache)
    f = lambda x, y: lax.dot_general(x, y, dims, preferred_element_type=F32)
    if 3 * depth <= MXU_DEPTH:
        return f(jnp.concatenate([ah, al, ah], axis=ca), jnp.concatenate([bh, bh, bl], axis=cb))
    return f(jnp.concatenate([ah, al], axis=ca), jnp.concatenate([bh, bh], axis=cb)) + f(ah, bl)


def _gdn_kernel(alog_ref, dtb_ref, dqkv_ref, z_ref, ab_ref, cw_ref, nrm_ref, pre_ref, s0_ref,
                o_ref, sfin_ref, tail_ref, state_ref, *, c, g, nsub):
    n = pl.program_id(1)
    ngrp = DN_HEADS // g
    r = g * c
    rows = nsub * c
    hd = HEAD_DIM
    _dot3 = functools.partial(_dot3_impl, cache={})

    @pl.when(n == 0)
    def _():
        tail_ref[...] = jnp.concatenate([jnp.zeros((8 - (CONV_W - 1), DN_QKV), F32), pre_ref[0]], axis=0)
        for gi in range(ngrp):
            state_ref[gi] = jnp.concatenate([s0_ref[0, gi * g + j] for j in range(g)], axis=1)

    x = dqkv_ref[...]
    hist = jnp.concatenate([tail_ref[...], x], axis=0)
    tail_ref[...] = hist[rows:]
    cw = cw_ref[...]
    y = x * cw[CONV_W - 1:CONV_W]
    for d in range(1, CONV_W):
        y = y + hist[8 - d:8 - d + rows] * cw[CONV_W - 1 - d:CONV_W - d]
    qkv = y * jax.nn.sigmoid(y)
    z = z_ref[...]
    ab = ab_ref[...]
    beta_all = jax.nn.sigmoid(ab[:, DN_HEADS:2 * DN_HEADS])
    g_all = -jnp.exp(alog_ref[...]) * jax.nn.softplus(ab[:, 0:DN_HEADS] + dtb_ref[...])

    ri = lax.broadcasted_iota(I32, (r, r), 0)
    ci = lax.broadcasted_iota(I32, (r, r), 1)
    same = (ri // c) == (ci // c)
    lower = same & (ri >= ci)
    strict = same & (ri > ci)
    tri = lower.astype(F32)
    blk = same.astype(F32)
    eye = (ri == ci).astype(F32)
    rowhead = lax.broadcasted_iota(I32, (r, hd), 0) // c

    def stack(a, sub, gi, base):
        return jnp.concatenate([a[sub * c:(sub + 1) * c, base + (gi * g + j) * hd:base + (gi * g + j + 1) * hd]
                                for j in range(g)], axis=0)

    def column(a, sub, gi):
        return jnp.concatenate([a[sub * c:(sub + 1) * c, gi * g + j:gi * g + j + 1] for j in range(g)], axis=0)

    def own_block(m):
        acc = jnp.where(rowhead == 0, m[:, 0:hd], 0.0)
        for j in range(1, g):
            acc = acc + jnp.where(rowhead == j, m[:, j * hd:(j + 1) * hd], 0.0)
        return acc

    keys = [(sub, gi) for sub in range(nsub) for gi in range(ngrp)]
    grp = {}
    for sub, gi in keys:
        q = stack(qkv, sub, gi, 0)
        k = stack(qkv, sub, gi, DN_WIDTH)
        v = stack(qkv, sub, gi, 2 * DN_WIDTH)
        q = q * lax.rsqrt(jnp.sum(q * q, axis=-1, keepdims=True) + EPS) * (hd ** -0.5)
        k = k * lax.rsqrt(jnp.sum(k * k, axis=-1, keepdims=True) + EPS)
        beta = column(beta_all, sub, gi)
        gr = column(g_all, sub, gi)
        gcol = _fdot(tri, gr)
        grow = lax.dot_general(gr, tri, (((0,), (1,)), ((), ())), precision=HIGHEST,
                               preferred_element_type=F32)
        glast = _fdot(blk, gr)
        decay = jnp.exp(jnp.where(lower, gcol - grow, NEG_INF))
        kb = k * beta
        lm = jnp.where(strict, _dot3(kb, k, _NT) * decay, 0.0)
        grp[sub, gi] = dict(q=q, k=k, v=v, beta=beta, gcol=gcol, glast=glast, decay=decay, kb=kb, lm=lm)

    pair = ((ri // 2) == (ci // 2)) & (ri % 2 == 1) & (ci % 2 == 0)
    invs = {key: eye - jnp.where(pair, grp[key]["lm"], 0.0) for key in keys}
    s = 2
    while s < c:
        off = ((ri // (2 * s)) == (ci // (2 * s))) & ((ri // s) % 2 == 1) & ((ci // s) % 2 == 0)
        tmp = {key: _dot3(invs[key], jnp.where(off, grp[key]["lm"], 0.0)) for key in keys}
        invs = {key: invs[key] - _dot3(tmp[key], invs[key]) for key in keys}
        s *= 2

    for sub in range(nsub):
        outs = [None] * DN_HEADS
        for gi in range(ngrp):
            d = grp[sub, gi]
            eg = jnp.exp(d["gcol"])
            sol = _dot3(invs[sub, gi], jnp.concatenate([d["v"] * d["beta"], d["kb"] * eg], axis=1))
            u = sol[:, :hd]
            w = sol[:, hd:]
            qk = _dot3(d["q"], d["k"], _NT) * d["decay"]
            ktail = d["k"] * jnp.exp(d["glast"] - d["gcol"])
            qhead = d["q"] * eg
            st = state_ref[gi]
            vnew = u - own_block(_dot3(w, st))
            o = own_block(_dot3(qhead, st)) + _dot3(qk, vnew)
            vblk = jnp.concatenate([jnp.where(rowhead == j, vnew, 0.0) for j in range(g)], axis=1)
            elast = jnp.exp(d["glast"])
            keep = jnp.concatenate([jnp.broadcast_to(elast[j * c:j * c + 1, :], (1, hd)) for j in range(g)],
                                   axis=1)
            state_ref[gi] = st * keep + _dot3(ktail, vblk, _TN)
            zs = stack(z, sub, gi, 0)
            o = _rms(o, nrm_ref[...]) * (zs * jax.nn.sigmoid(zs))
            for j in range(g):
                outs[gi * g + j] = o[j * c:(j + 1) * c]
        o_ref[sub * c:(sub + 1) * c, :] = jnp.concatenate(outs, axis=-1)

    @pl.when(n == pl.num_programs(1) - 1)
    def _():
        for gi in range(ngrp):
            for j in range(g):
                sfin_ref[0, gi * g + j] = state_ref[gi][:, j * hd:(j + 1) * hd]


GDN_CHUNKS_PER_STEP = 2


def _gdn(dqkv, z, ab, conv_prefix, s0, conv_w, a_log, dt_bias, dn_norm, batch, seq):
    c = min(DN_CHUNK, seq)
    nchunk = seq // c
    assert seq % c == 0 and c % 8 == 0
    g = min(DN_HEADS, max(1, 128 // c))
    nsub = GDN_CHUNKS_PER_STEP if nchunk % GDN_CHUNKS_PER_STEP == 0 else 1
    nstep = nchunk // nsub
    rows = nsub * c
    tok = lambda b, n: (b * nstep + n, 0)
    fixed = lambda b, n: (0, 0)
    return pl.pallas_call(
        functools.partial(_gdn_kernel, c=c, g=g, nsub=nsub), name="gdn", grid=(batch, nstep),
        out_shape=[jax.ShapeDtypeStruct((batch * seq, DN_WIDTH), F32),
                   jax.ShapeDtypeStruct((batch, DN_HEADS, HEAD_DIM, HEAD_DIM), F32)],
        in_specs=[pl.BlockSpec((1, DN_HEADS), fixed), pl.BlockSpec((1, DN_HEADS), fixed),
                  pl.BlockSpec((rows, DN_QKV), tok), pl.BlockSpec((rows, DN_WIDTH), tok),
                  pl.BlockSpec((rows, 2 * DN_HEADS), tok),
                  pl.BlockSpec((CONV_W, DN_QKV), fixed), pl.BlockSpec((1, HEAD_DIM), fixed),
                  pl.BlockSpec((1, CONV_W - 1, DN_QKV), lambda b, n: (b, 0, 0)),
                  pl.BlockSpec((1, DN_HEADS, HEAD_DIM, HEAD_DIM), lambda b, n: (b, 0, 0, 0))],
        out_specs=[pl.BlockSpec((rows, DN_WIDTH), tok),
                   pl.BlockSpec((1, DN_HEADS, HEAD_DIM, HEAD_DIM), lambda b, n: (b, 0, 0, 0))],
        scratch_shapes=[pltpu.VMEM((8, DN_QKV), F32),
                        pltpu.VMEM((DN_HEADS // g, HEAD_DIM, g * HEAD_DIM), F32)],
        compiler_params=_cparams("parallel", "arbitrary"))(
            a_log.reshape(1, DN_HEADS), dt_bias.reshape(1, DN_HEADS), dqkv, z, ab, conv_w,
            dn_norm.reshape(1, HEAD_DIM), conv_prefix, s0)


def _outproj_kernel(x_ref, att_ref, dn_ref, wo_ref, g_ref, wq_ref, h_ref, c_ref, q_ref):
    mix = jnp.concatenate([att_ref[...], dn_ref[...]], axis=-1)
    h = x_ref[...] + jnp.dot(mix.astype(BF16), wo_ref[...], preferred_element_type=F32)
    h_ref[...] = h
    cn = _rms(h, g_ref[...])
    c_ref[...] = cn
    q_ref[...] = jnp.dot(cn.astype(BF16), wq_ref[...], preferred_element_type=F32)


def _outproj(x, att, dn, w_out, g_ffn, wq):
    t = x.shape[0]
    tm = _row_tile(t, 512)
    row = lambda i: (i, 0)
    fixed = lambda i: (0, 0)
    qw = PEER_HEADS * PEER_QDIM
    return pl.pallas_call(
        _outproj_kernel, name="outproj", grid=(t // tm,),
        out_shape=[jax.ShapeDtypeStruct((t, D_MODEL), F32), jax.ShapeDtypeStruct((t, D_MODEL), F32),
                   jax.ShapeDtypeStruct((t, qw), F32)],
        in_specs=[pl.BlockSpec((tm, D_MODEL), row), pl.BlockSpec((tm, ATT_Q), row),
                  pl.BlockSpec((tm, DN_WIDTH), row), pl.BlockSpec((D_MODEL, D_MODEL), fixed),
                  pl.BlockSpec((1, D_MODEL), fixed), pl.BlockSpec((D_MODEL, qw), fixed)],
        out_specs=[pl.BlockSpec((tm, D_MODEL), row), pl.BlockSpec((tm, D_MODEL), row),
                   pl.BlockSpec((tm, qw), row)],
        compiler_params=_cparams("parallel"))(x, att, dn, w_out, g_ffn, wq)


def _top16_rows(s, iota):
    big = jnp.float32(INDEX_SENTINEL)
    vals, idxs = [], []
    for _ in range(PEER_TOPK):
        m = jnp.max(s, axis=0, keepdims=True)
        i = jnp.min(jnp.where(s == m, iota, big), axis=0, keepdims=True)
        vals.append(m)
        idxs.append(i)
        s = jnp.where(iota == i, NEG_INF, s)
    return jnp.concatenate(vals, axis=0), jnp.concatenate(idxs, axis=0)


def _candidates(a, b, row8):
    t = a.shape[1]
    lo = row8 < 4
    bc = lambda r, n: jnp.broadcast_to(a[r:r + 1], (n, t))
    b8 = b[0:8]
    b44 = jnp.where(lo, b8, pltpu.roll(b8, 4, 0))
    parts = [bc(0, 8), bc(0, 8), bc(1, 8), bc(2, 8), bc(3, 8),
             jnp.where(lo, bc(4, 8), bc(5, 8)), jnp.where(lo, bc(6, 8), bc(7, 8)), a[8:16]]
    others = [b[0:8], b[8:16], b8, b8, b8, b44, b44, jnp.broadcast_to(b[0:1], (8, t))]
    return parts, others


def _flat_index_column():
    r = lax.broadcasted_iota(I32, (8, 1), 0)
    lo = r < 4
    cols = [r, 8 + r, 16 + r, 32 + r, 48 + r, jnp.where(lo, 64 + r, 80 + r - 4),
            jnp.where(lo, 96 + r, 112 + r - 4), (8 + r) * 16]
    return jnp.concatenate(cols, axis=0).astype(F32)


def _route_kernel(q_ref, k1_ref, k2_ref, idx_ref, gate_ref, *, tb):
    iota = lax.broadcasted_iota(I32, (N_KEYS, tb), 0).astype(F32)
    row8 = lax.broadcasted_iota(I32, (8, tb), 0)
    flat = jnp.broadcast_to(_flat_index_column(), (4 * PEER_TOPK, tb))
    big = jnp.float32(INDEX_SENTINEL)
    half = PEER_QDIM // 2
    for h in range(PEER_HEADS):
        q1 = q_ref[:, h * PEER_QDIM:h * PEER_QDIM + half]
        q2 = q_ref[:, h * PEER_QDIM + half:(h + 1) * PEER_QDIM]
        nt = (((1,), (1,)), ((), ()))
        s1 = lax.dot_general(k1_ref[h], q1.astype(BF16), nt, preferred_element_type=F32)
        s2 = lax.dot_general(k2_ref[h], q2.astype(BF16), nt, preferred_element_type=F32)
        v1, i1 = _top16_rows(s1, iota)
        v2, i2 = _top16_rows(s2, iota)
        pa, pb = _candidates(v1, v2, row8)
        cand = jnp.concatenate([x + y for x, y in zip(pa, pb)], axis=0)
        ea, eb = _candidates(i1, i2, row8)
        expert = jnp.concatenate([x * float(N_KEYS) + y for x, y in zip(ea, eb)], axis=0)
        scs, exs = [], []
        for _ in range(PEER_TOPK):
            m = jnp.max(cand, axis=0, keepdims=True)
            c = jnp.min(jnp.where(cand == m, flat, big), axis=0, keepdims=True)
            hit = flat == c
            exs.append(jnp.max(jnp.where(hit, expert, -1.0), axis=0, keepdims=True))
            scs.append(m)
            cand = jnp.where(hit, NEG_INF, cand)
        sc = jnp.concatenate(scs, axis=0)
        e = jnp.exp(sc - sc[0:1])
        gate_ref[h * PEER_TOPK:(h + 1) * PEER_TOPK, :] = e / jnp.sum(e, axis=0, keepdims=True)
        idx_ref[h * PEER_TOPK:(h + 1) * PEER_TOPK, :] = jnp.concatenate(exs, axis=0).astype(I32)


def _route(q, keys1, keys2):
    t = q.shape[0]
    tb = _row_tile(t, 256)
    kspec = pl.BlockSpec((PEER_HEADS, N_KEYS, PEER_QDIM // 2), lambda i: (0, 0, 0))
    return pl.pallas_call(
        functools.partial(_route_kernel, tb=tb), name="peer_route", grid=(t // tb,),
        out_shape=[jax.ShapeDtypeStruct((PEER_SLOTS, t), I32), jax.ShapeDtypeStruct((PEER_SLOTS, t), F32)],
        in_specs=[pl.BlockSpec((tb, PEER_HEADS * PEER_QDIM), lambda i: (i, 0)), kspec, kspec],
        out_specs=[pl.BlockSpec((PEER_SLOTS, tb), lambda i: (0, i)),
                   pl.BlockSpec((PEER_SLOTS, tb), lambda i: (0, i))],
        compiler_params=_cparams("parallel"))(q, keys1, keys2)


GATHER_SLOTS = 3
SC_WORKERS = 32
SC_GATHER_ROWS = 16
SC_BUFFERS = 6
STAGED_TOKENS_PER_STEP = 16


def _peer_token(packed, x, gate):
    u = lax.bitcast_convert_type(packed & jnp.uint32(0xFFFF0000), F32)
    v = lax.bitcast_convert_type(packed << 16, F32)
    p = u * x
    acc = p[:, 0:128]
    for j in range(1, D_MODEL // 128):
        acc = acc + p[:, j * 128:(j + 1) * 128]
    s = jnp.sum(acc, axis=-1, keepdims=True)
    act = 0.5 * s * (1.0 + lax.erf(s * (1.0 / math.sqrt(2.0))))
    return jnp.sum((gate * act) * v, axis=0, keepdims=True)


def _peer_apply_kernel(idx_ref, c_ref, h_ref, gt_ref, uv_ref, o_ref, buf, sem, *, tb):
    ahead = GATHER_SLOTS - 1

    def issue(t, slot):
        for k in range(PEER_SLOTS):
            e = idx_ref[k, t]
            pltpu.make_async_copy(uv_ref.at[e], buf.at[slot, pl.ds(k, 1), :],
                                  sem.at[slot]).start(priority=k % 2)

    def wait(slot):
        pltpu.make_async_copy(buf.at[slot], buf.at[slot], sem.at[slot]).wait()

    lane = lax.broadcasted_iota(I32, (PEER_SLOTS, tb), 1)

    def apply(t, slot):
        gate = jnp.sum(jnp.where(lane == t, gt_ref[...], 0.0), axis=-1, keepdims=True)
        y = _peer_token(buf[slot], c_ref[pl.ds(t, 1), :], gate)
        o_ref[pl.ds(t, 1), :] = h_ref[pl.ds(t, 1), :] + y

    for t0 in range(ahead):
        issue(t0, t0)

    def body(i, carry):
        for j in range(GATHER_SLOTS):
            t = i * GATHER_SLOTS + j
            issue(t + ahead, (j + ahead) % GATHER_SLOTS)
            wait(j)
            apply(t, j)
        return carry

    nmain = (tb - ahead) // GATHER_SLOTS
    lax.fori_loop(0, nmain, body, 0)
    for t in range(nmain * GATHER_SLOTS, tb):
        if t + ahead < tb:
            issue(t + ahead, (t + ahead) % GATHER_SLOTS)
        wait(t % GATHER_SLOTS)
        apply(t, t % GATHER_SLOTS)


def _pack_expert_rows(u, v):
    hi = lax.bitcast_convert_type(u.astype(BF16), jnp.uint16).astype(jnp.uint32) << 16
    lo = lax.bitcast_convert_type(v.astype(BF16), jnp.uint16).astype(jnp.uint32)
    return hi | lo


def _peer_apply_direct(idx_t, c, h, gate_t, uv_rows, t_direct):
    t = c.shape[0]
    tb = _row_tile(t_direct, 256)
    row = lambda i: (i, 0)
    return pl.pallas_call(
        functools.partial(_peer_apply_kernel, tb=tb), name="peer_apply", grid=(t_direct // tb,),
        out_shape=jax.ShapeDtypeStruct((t, D_MODEL), F32),
        in_specs=[pl.BlockSpec((PEER_SLOTS, tb), lambda i: (0, i), memory_space=pltpu.SMEM),
                  pl.BlockSpec((tb, D_MODEL), row), pl.BlockSpec((tb, D_MODEL), row),
                  pl.BlockSpec((PEER_SLOTS, tb), lambda i: (0, i)),
                  pl.BlockSpec(memory_space=pl.ANY)],
        out_specs=pl.BlockSpec((tb, D_MODEL), row),
        scratch_shapes=[pltpu.VMEM((GATHER_SLOTS, PEER_SLOTS, D_MODEL), jnp.uint32),
                        pltpu.SemaphoreType.DMA((GATHER_SLOTS,))],
        compiler_params=_cparams("arbitrary"))(idx_t, c, h, gate_t, uv_rows)


def _sc_stage_rows(table, idx):
    t_total, nchunk_tok, _ = idx.shape
    d = table.shape[1]
    tpw = t_total // SC_WORKERS
    assert t_total % SC_WORKERS == 0 and tpw >= 2
    nb = SC_BUFFERS
    ahead = nb // 2
    params = pltpu.CompilerParams()
    if "needs_layout_passes" in pltpu.CompilerParams.__dataclass_fields__:
        params = dataclasses.replace(params, needs_layout_passes=False)
    mesh = plsc.VectorSubcoreMesh(core_axis_name="c", subcore_axis_name="s")

    @functools.partial(
        pl.kernel, mesh=mesh, compiler_params=params,
        out_type=jax.ShapeDtypeStruct((t_total * nchunk_tok * SC_GATHER_ROWS, d), table.dtype),
        scratch_types=[pltpu.VMEM((2, nchunk_tok, SC_GATHER_ROWS), I32),
                       pltpu.VMEM((nb, SC_GATHER_ROWS, d), table.dtype),
                       pltpu.SemaphoreType.DMA((nb,)), pltpu.SemaphoreType.DMA((nb,)),
                       pltpu.SemaphoreType.DMA((2,))])
    def stage(table_hbm, idx_hbm, out_hbm, idx_v, rows_v, sem_g, sem_w, sem_t):
        wid = lax.axis_index("s") * 2 + lax.axis_index("c")
        base = wid * tpw
        nchunks = tpw * nchunk_tok

        def gather(g, slot):
            ts = (g // nchunk_tok) % 2
            return pltpu.make_async_copy(table_hbm.at[idx_v.at[ts, g % nchunk_tok]], rows_v.at[slot],
                                         sem_g.at[slot])

        def write(g, slot):
            dst = out_hbm.at[pl.ds((base * nchunk_tok + g) * SC_GATHER_ROWS, SC_GATHER_ROWS)]
            return pltpu.make_async_copy(rows_v.at[slot], dst, sem_w.at[slot])

        def idx_copy(i, ts):
            return pltpu.make_async_copy(idx_hbm.at[base + i], idx_v.at[ts], sem_t.at[ts])

        idx_copy(0, 0).start()
        idx_copy(0, 0).wait()
        idx_copy(1, 1).start()
        for g0 in range(ahead):
            gather(g0, g0).start()

        @pl.loop(0, nchunks)
        def _(g):
            slot = g % nb
            nxt = g + ahead
            nslot = nxt % nb

            @pl.when(nxt < nchunks)
            def _():
                @pl.when(nxt >= nb)
                def _():
                    write(nxt - nb, nslot).wait()

                @pl.when(nxt % nchunk_tok == 0)
                def _():
                    idx_copy(nxt // nchunk_tok, (nxt // nchunk_tok) % 2).wait()

                gather(nxt, nslot).start()

            gather(g, slot).wait()
            write(g, slot).start()

            @pl.when((g % nchunk_tok == nchunk_tok - 1) & (g // nchunk_tok + 2 < tpw))
            def _():
                idx_copy(g // nchunk_tok + 2, (g // nchunk_tok) % 2).start()

        for j in range(nb):
            gl = nchunks - nb + j
            write(gl, gl % nb).wait()

    return stage(table, idx)


def _peer_staged_kernel(st_ref, c_ref, h_ref, gt_ref, prev_ref, o_ref, *, tbs):
    del prev_ref
    i = pl.program_id(0)
    lane = lax.broadcasted_iota(I32, (PEER_SLOTS, 128), 1)
    for t in range(tbs):
        tg = i * tbs + t
        col0 = pl.multiple_of((tg // 128) * 128, 128)
        gate = jnp.sum(jnp.where(lane == tg % 128, gt_ref[:, pl.ds(col0, 128)], 0.0), axis=-1, keepdims=True)
        y = _peer_token(st_ref[t], c_ref[pl.ds(t, 1), :], gate)
        o_ref[pl.ds(t, 1), :] = h_ref[pl.ds(t, 1), :] + y


def _peer_apply_staged(staged, c, h, gate_t_staged, partial, t_direct):
    t_staged = staged.shape[0]
    tbs = STAGED_TOKENS_PER_STEP
    assert t_staged % tbs == 0 and t_direct % tbs == 0 and t_staged % 128 == 0
    off = t_direct // tbs
    row = lambda i: (i + off, 0)
    return pl.pallas_call(
        functools.partial(_peer_staged_kernel, tbs=tbs), name="peer_apply_staged", grid=(t_staged // tbs,),
        out_shape=jax.ShapeDtypeStruct(partial.shape, F32),
        in_specs=[pl.BlockSpec((tbs, PEER_SLOTS, D_MODEL), lambda i: (i, 0, 0)),
                  pl.BlockSpec((tbs, D_MODEL), row), pl.BlockSpec((tbs, D_MODEL), row),
                  pl.BlockSpec((PEER_SLOTS, t_staged), lambda i: (0, 0)),
                  pl.BlockSpec(memory_space=pl.ANY)],
        out_specs=pl.BlockSpec((tbs, D_MODEL), row),
        input_output_aliases={4: 0},
        compiler_params=_cparams("arbitrary"))(staged, c, h, gate_t_staged, partial)


def _peer_apply(idx_t, c, h, gate_t, uv, t_staged):
    t = c.shape[0]
    t_direct = t - t_staged
    out = _peer_apply_direct(idx_t, c, h, gate_t, uv[:, None, :], t_direct)
    if t_staged:
        idx_sc = idx_t[:, t_direct:].T.reshape(t_staged, PEER_SLOTS // SC_GATHER_ROWS, SC_GATHER_ROWS)
        staged = _sc_stage_rows(uv, idx_sc).reshape(t_staged, PEER_SLOTS, D_MODEL)
        out = _peer_apply_staged(staged, c, h, gate_t[:, t_direct:], out, t_direct)
    return out


def _final_kernel(h_ref, p_ref, gple_ref, wg_ref, wp_ref, gfin_ref, y_ref):
    h = h_ref[...]
    e = _rms(h, gple_ref[...])
    gate = jax.nn.sigmoid(jnp.dot(e.astype(BF16), wg_ref[...], preferred_element_type=F32))
    up = jnp.dot(p_ref[...].astype(BF16), wp_ref[...], preferred_element_type=F32)
    y_ref[...] = _rms(h + up * gate, gfin_ref[...])


def _final(h, p, g_ple, w_gate, w_ple, g_final):
    t = h.shape[0]
    tm = _row_tile(t, 512)
    row = lambda i: (i, 0)
    fixed = lambda i: (0, 0)
    return pl.pallas_call(
        _final_kernel, name="ple_final", grid=(t // tm,),
        out_shape=jax.ShapeDtypeStruct((t, D_MODEL), F32),
        in_specs=[pl.BlockSpec((tm, D_MODEL), row), pl.BlockSpec((tm, PLE_DIM), row),
                  pl.BlockSpec((1, D_MODEL), fixed), pl.BlockSpec((D_MODEL, D_MODEL), fixed),
                  pl.BlockSpec((PLE_DIM, D_MODEL), fixed), pl.BlockSpec((1, D_MODEL), fixed)],
        out_specs=pl.BlockSpec((tm, D_MODEL), row),
        compiler_params=_cparams("parallel"))(h, p, g_ple, w_gate, w_ple, g_final)


STAGED_FRACTION = 0.5


def _staged_tokens(t):
    unit = SC_WORKERS * 128
    n = int(t * STAGED_FRACTION) // unit * unit
    return n if n >= 2 * unit else 0


def _layer(x, p, kv_buf, conv_prefix, s0, wts, batch, seq):
    t = batch * seq
    q, k, v, dqkv, z, ab = _inproj(x, wts["norm_mix"], wts["w_in"])
    if kv_buf is None:
        att = _swa_prompt(q, k, v, wts["sinks"], batch, seq)
        k_new = k.reshape(batch, seq, ATT_KV)[:, -WINDOW:]
        v_new = v.reshape(batch, seq, ATT_KV)[:, -WINDOW:]
    else:
        att, k_new, v_new = _swa_sample(q.reshape(batch, seq, ATT_Q), k.reshape(batch, seq, ATT_KV),
                                        v.reshape(batch, seq, ATT_KV), kv_buf[0], kv_buf[1], wts["sinks"])
        att = att.reshape(t, ATT_Q)
    dn, s_new = _gdn(dqkv, z, ab, conv_prefix, s0, wts["conv_w"], wts["a_log"], wts["dt_bias"],
                     wts["dn_norm"], batch, seq)
    conv_new = dqkv.reshape(batch, seq, DN_QKV)[:, -(CONV_W - 1):]
    h, c, pq = _outproj(x, att, dn, wts["w_out"], wts["norm_ffn"], wts["peer_wq"])
    idx_t, gate_t = _route(pq, wts["keys1"], wts["keys2"])
    h2 = _peer_apply(idx_t, c, h, gate_t, wts["uv"], _staged_tokens(t))
    y = _final(h2, p, wts["norm_ple"], wts["ple_gate"], wts["ple_in"], wts["norm_final"])
    return y, k_new, v_new, conv_new, s_new


def kernel(x_prompt, x_sample, p_prompt, p_sample, cache_swa_k, cache_swa_v, state_conv, state_delta, norm_mix, w_in, conv_w, attn_sinks, dn_a_log, dn_dt_bias, dn_norm, w_out, norm_ffn, peer_wq, peer_keys1, peer_keys2, peer_u, peer_v, norm_ple, ple_in, ple_gate, norm_final):
    depth = w_in.shape[0]
    assert depth == 1
    bp, lp = x_prompt.shape[0], x_prompt.shape[1]
    bs, ls = x_sample.shape[0], x_sample.shape[1]
    l = 0
    wts = dict(
        norm_mix=norm_mix[l].reshape(1, D_MODEL),
        w_in=jnp.pad(w_in[l], ((0, 0), (0, IN_WIDTH_PADDED - IN_WIDTH))).astype(BF16),
        conv_w=conv_w[l], sinks=attn_sinks[l], a_log=dn_a_log[l], dt_bias=dn_dt_bias[l], dn_norm=dn_norm[l],
        w_out=w_out[l].astype(BF16), norm_ffn=norm_ffn[l].reshape(1, D_MODEL), peer_wq=peer_wq[l].astype(BF16),
        keys1=peer_keys1[l].astype(BF16), keys2=peer_keys2[l].astype(BF16),
        uv=_pack_expert_rows(peer_u[l], peer_v[l]),
        norm_ple=norm_ple[l].reshape(1, D_MODEL), ple_in=ple_in[l].astype(BF16),
        ple_gate=ple_gate[l].astype(BF16), norm_final=norm_final.reshape(1, D_MODEL))
    conv0 = jnp.zeros((bp, CONV_W - 1, DN_QKV), F32)
    s0 = jnp.zeros((bp, DN_HEADS, HEAD_DIM, HEAD_DIM), F32)
    yp, kp, vp, cp, sp = _layer(x_prompt.reshape(bp * lp, D_MODEL), p_prompt[l].reshape(bp * lp, PLE_DIM),
                                None, conv0, s0, wts, bp, lp)
    kv_buf = (cache_swa_k[l].reshape(bs, WINDOW, ATT_KV), cache_swa_v[l].reshape(bs, WINDOW, ATT_KV))
    ys, kn, vn, cn, sn = _layer(x_sample.reshape(bs * ls, D_MODEL), p_sample[l].reshape(bs * ls, PLE_DIM),
                                kv_buf, state_conv[l], state_delta[l], wts, bs, ls)
    kvshape = lambda b: (1, b, WINDOW, ATT_KV_HEADS, HEAD_DIM)
    return (yp.reshape(bp, lp, D_MODEL), ys.reshape(bs, ls, D_MODEL),
            kp.reshape(kvshape(bp)), vp.reshape(kvshape(bp)), cp[None], sp[None],
            kn.reshape(kvshape(bs)), vn.reshape(kvshape(bs)), cn[None], sn[None])
```

```python
import dataclasses
import functools
import math

import jax
import jax.numpy as jnp
from jax import lax
from jax.experimental import pallas as pl
from jax.experimental.pallas import tpu as pltpu
from jax.experimental.pallas import tpu_sc as plsc

F32 = jnp.float32
BF16 = jnp.bfloat16
I32 = jnp.int32
HIGHEST = lax.Precision.HIGHEST

D_MODEL = 1024
HEAD_DIM = 64
ATT_HEADS = 8
ATT_KV_HEADS = 2
ATT_GROUP = 4
WINDOW = 128
ATT_Q = ATT_HEADS * HEAD_DIM
ATT_KV = ATT_KV_HEADS * HEAD_DIM
DN_HEADS = 8
DN_WIDTH = DN_HEADS * HEAD_DIM
DN_QKV = 3 * DN_WIDTH
CONV_W = 4
DN_CHUNK = 64
IN_WIDTH = ATT_Q + 2 * ATT_KV + DN_QKV + DN_WIDTH + 2 * DN_HEADS
IN_WIDTH_PADDED = 2944
N_KEYS = 128
PEER_HEADS = 8
PEER_QDIM = 256
PEER_TOPK = 16
PEER_SLOTS = PEER_HEADS * PEER_TOPK
PLE_DIM = 256
EPS = 1e-6
NEG_INF = float("-inf")
INDEX_SENTINEL = 1.0e9

VMEM_LIMIT = 48 * 1024 * 1024


def _cparams(*sem):
    return pltpu.CompilerParams(dimension_semantics=sem, vmem_limit_bytes=VMEM_LIMIT)


def _rms(x, g):
    return x * lax.rsqrt(jnp.mean(x * x, axis=-1, keepdims=True) + EPS) * g


def _bdot(a, b):
    return jnp.dot(a.astype(BF16), b.astype(BF16), preferred_element_type=F32)


def _fdot(a, b):
    return jnp.dot(a, b, precision=HIGHEST, preferred_element_type=F32)


_NN = (((1,), (0,)), ((), ()))
_NT = (((1,), (1,)), ((), ()))
_TN = (((0,), (0,)), ((), ()))


def _row_tile(t, target):
    tile = min(t, target)
    assert t % tile == 0, (t, tile)
    return tile


def _inproj_kernel(x_ref, g_ref, w_ref, q_ref, k_ref, v_ref, dqkv_ref, z_ref, ab_ref):
    a = _rms(x_ref[...], g_ref[...])
    p = jnp.dot(a.astype(BF16), w_ref[...], preferred_element_type=F32)
    c0, c1, c2, c3, c4 = ATT_Q, ATT_Q + ATT_KV, ATT_Q + 2 * ATT_KV, ATT_Q + 2 * ATT_KV + DN_QKV, \
        ATT_Q + 2 * ATT_KV + DN_QKV + DN_WIDTH
    q_ref[...] = p[:, :c0]
    k_ref[...] = p[:, c0:c1]
    v_ref[...] = p[:, c1:c2]
    dqkv_ref[...] = p[:, c2:c3]
    z_ref[...] = p[:, c3:c4]
    ab_ref[...] = p[:, c4:c4 + 2 * DN_HEADS]


def _inproj(x, g, w_pad):
    t = x.shape[0]
    tm = _row_tile(t, 512)
    widths = (ATT_Q, ATT_KV, ATT_KV, DN_QKV, DN_WIDTH, 2 * DN_HEADS)
    row = lambda i: (i, 0)
    fixed = lambda i: (0, 0)
    return pl.pallas_call(
        _inproj_kernel, name="inproj", grid=(t // tm,),
        out_shape=[jax.ShapeDtypeStruct((t, w), F32) for w in widths],
        in_specs=[pl.BlockSpec((tm, D_MODEL), row), pl.BlockSpec((1, D_MODEL), fixed),
                  pl.BlockSpec((D_MODEL, IN_WIDTH_PADDED), fixed)],
        out_specs=[pl.BlockSpec((tm, w), row) for w in widths],
        compiler_params=_cparams("parallel"))(x, g, w_pad)


def _softmax_sink_pv(s, sink, vband):
    m = jnp.maximum(jnp.max(s, axis=-1, keepdims=True), sink)
    e = jnp.exp(s - m)
    denom = jnp.sum(e, axis=-1, keepdims=True) + jnp.exp(sink - m)
    return _bdot(e / denom, vband)


def _alibi_slope(h):
    return 2.0 ** (-8.0 * (h + 1) / ATT_HEADS)


def _group_columns(kh, rows, sink_ref):
    rg = lax.broadcasted_iota(I32, (ATT_GROUP * rows, 1), 0) // rows
    slope = jnp.full((ATT_GROUP * rows, 1), _alibi_slope(kh * ATT_GROUP), F32)
    sink = jnp.full((ATT_GROUP * rows, 1), sink_ref[kh * ATT_GROUP], F32)
    for j in range(1, ATT_GROUP):
        slope = jnp.where(rg == j, _alibi_slope(kh * ATT_GROUP + j), slope)
        sink = jnp.where(rg == j, sink_ref[kh * ATT_GROUP + j], sink)
    return slope, sink


def _attend_kv_head(q_all, kband, vband, kh, dist, valid, sink_ref):
    rows = q_all.shape[0]
    heads = range(kh * ATT_GROUP, (kh + 1) * ATT_GROUP)
    q4 = jnp.concatenate([q_all[:, h * HEAD_DIM:(h + 1) * HEAD_DIM] for h in heads], axis=0)
    k = kband[:, kh * HEAD_DIM:(kh + 1) * HEAD_DIM]
    v = vband[:, kh * HEAD_DIM:(kh + 1) * HEAD_DIM]
    s = lax.dot_general(q4.astype(BF16), k.astype(BF16), _NT, preferred_element_type=F32) * (HEAD_DIM ** -0.5)
    slope, sink = _group_columns(kh, rows, sink_ref)
    s = jnp.where(valid, s - slope * dist, NEG_INF)
    o4 = _softmax_sink_pv(s, sink, v)
    return [o4[j * rows:(j + 1) * rows] for j in range(ATT_GROUP)]


def _swa_prompt_kernel(sink_ref, q_ref, kc_ref, kp_ref, vc_ref, vp_ref, o_ref):
    n = pl.program_id(1)
    rows = ATT_GROUP * WINDOW
    qi = lax.broadcasted_iota(I32, (rows, 2 * WINDOW), 0) % WINDOW
    kj = lax.broadcasted_iota(I32, (rows, 2 * WINDOW), 1)
    dist = WINDOW + qi - kj
    kpos = n * WINDOW - WINDOW + kj
    valid = (dist >= 0) & (dist <= WINDOW) & (kpos >= 0)
    distf = dist.astype(F32)
    kband = jnp.concatenate([kp_ref[...], kc_ref[...]], axis=0)
    vband = jnp.concatenate([vp_ref[...], vc_ref[...]], axis=0)
    q_all = q_ref[...]
    outs = []
    for kh in range(ATT_KV_HEADS):
        outs += _attend_kv_head(q_all, kband, vband, kh, distf, valid, sink_ref)
    o_ref[...] = jnp.concatenate(outs, axis=-1)


def _swa_prompt(q, k, v, sinks, batch, seq):
    nblk = seq // WINDOW
    cur = lambda b, n: (b * nblk + n, 0)
    prev = lambda b, n: (b * nblk + jnp.maximum(n - 1, 0), 0)
    return pl.pallas_call(
        _swa_prompt_kernel, name="swa_prompt", grid=(batch, nblk),
        out_shape=jax.ShapeDtypeStruct((batch * seq, ATT_Q), F32),
        in_specs=[pl.BlockSpec(memory_space=pltpu.SMEM),
                  pl.BlockSpec((WINDOW, ATT_Q), cur),
                  pl.BlockSpec((WINDOW, ATT_KV), cur), pl.BlockSpec((WINDOW, ATT_KV), prev),
                  pl.BlockSpec((WINDOW, ATT_KV), cur), pl.BlockSpec((WINDOW, ATT_KV), prev)],
        out_specs=pl.BlockSpec((WINDOW, ATT_Q), cur),
        compiler_params=_cparams("parallel", "parallel"))(sinks, q, k, k, v, v)


def _swa_sample_kernel(sink_ref, q_ref, k_ref, v_ref, kb_ref, vb_ref, o_ref, ko_ref, vo_ref, *, bt, seq):
    klen = WINDOW + seq
    rows = ATT_GROUP * seq
    qi = lax.broadcasted_iota(I32, (rows, klen), 0) % seq
    kj = lax.broadcasted_iota(I32, (rows, klen), 1)
    dist = qi - (kj - WINDOW)
    valid = (dist >= 0) & (dist <= WINDOW)
    distf = dist.astype(F32)
    for b in range(bt):
        kc = jnp.concatenate([kb_ref[b], k_ref[b]], axis=0)
        vc = jnp.concatenate([vb_ref[b], v_ref[b]], axis=0)
        ko_ref[b] = kc[seq:]
        vo_ref[b] = vc[seq:]
        q_all = q_ref[b]
        outs = []
        for kh in range(ATT_KV_HEADS):
            outs += _attend_kv_head(q_all, kc, vc, kh, distf, valid, sink_ref)
        o_ref[b] = jnp.concatenate(outs, axis=-1)


def _swa_sample(q, k, v, k_buf, v_buf, sinks):
    batch, seq = q.shape[0], q.shape[1]
    bt = _row_tile(batch, 8)
    blk = lambda w, r: pl.BlockSpec((bt, r, w), lambda i: (i, 0, 0))
    return pl.pallas_call(
        functools.partial(_swa_sample_kernel, bt=bt, seq=seq), name="swa_sample", grid=(batch // bt,),
        out_shape=[jax.ShapeDtypeStruct((batch, seq, ATT_Q), F32),
                   jax.ShapeDtypeStruct((batch, WINDOW, ATT_KV), F32),
                   jax.ShapeDtypeStruct((batch, WINDOW, ATT_KV), F32)],
        in_specs=[pl.BlockSpec(memory_space=pltpu.SMEM), blk(ATT_Q, seq), blk(ATT_KV, seq), blk(ATT_KV, seq),
                  blk(ATT_KV, WINDOW), blk(ATT_KV, WINDOW)],
        out_specs=[blk(ATT_Q, seq), blk(ATT_KV, WINDOW), blk(ATT_KV, WINDOW)],
        compiler_params=_cparams("parallel"))(sinks, q, k, v, k_buf, v_buf)


def _split_bf16(a, cache):
    if id(a) not in cache:
        hi = a.astype(BF16)
        cache[id(a)] = (a, (hi, (a - hi.astype(F32)).astype(BF16)))
    return cache[id(a)][1]


MXU_DEPTH = 256


def _dot3_impl(a, b, dims=_NN, *, cache):
    (ca,), (cb,) = dims[0]
    depth = a.shape[ca]
    ah, al = _split_bf16(a, cache)
    bh, bl = _split_bf16(b, cache)
    f = lambda x, y: lax.dot_general(x, y, dims, preferred_element_type=F32)
    if 3 * depth <= MXU_DEPTH:
        return f(jnp.concatenate([ah, al, ah], axis=ca), jnp.concatenate([bh, bh, bl], axis=cb))
    return f(jnp.concatenate([ah, al], axis=ca), jnp.concatenate([bh, bh], axis=cb)) + f(ah, bl)


def _gdn_kernel(alog_ref, dtb_ref, dqkv_ref, z_ref, ab_ref, cw_ref, nrm_ref, pre_ref, s0_ref,
                o_ref, sfin_ref, tail_ref, state_ref, *, c, g, nsub):
    n = pl.program_id(1)
    ngrp = DN_HEADS // g
    r = g * c
    rows = nsub * c
    hd = HEAD_DIM
    _dot3 = functools.partial(_dot3_impl, cache={})

    @pl.when(n == 0)
    def _():
        tail_ref[...] = jnp.concatenate([jnp.zeros((8 - (CONV_W - 1), DN_QKV), F32), pre_ref[0]], axis=0)
        for gi in range(ngrp):
            state_ref[gi] = jnp.concatenate([s0_ref[0, gi * g + j] for j in range(g)], axis=1)

    x = dqkv_ref[...]
    hist = jnp.concatenate([tail_ref[...], x], axis=0)
    tail_ref[...] = hist[rows:]
    cw = cw_ref[...]
    y = x * cw[CONV_W - 1:CONV_W]
    for d in range(1, CONV_W):
        y = y + hist[8 - d:8 - d + rows] * cw[CONV_W - 1 - d:CONV_W - d]
    qkv = y * jax.nn.sigmoid(y)
    z = z_ref[...]
    ab = ab_ref[...]
    beta_all = jax.nn.sigmoid(ab[:, DN_HEADS:2 * DN_HEADS])
    g_all = -jnp.exp(alog_ref[...]) * jax.nn.softplus(ab[:, 0:DN_HEADS] + dtb_ref[...])

    ri = lax.broadcasted_iota(I32, (r, r), 0)
    ci = lax.broadcasted_iota(I32, (r, r), 1)
    same = (ri // c) == (ci // c)
    lower = same & (ri >= ci)
    strict = same & (ri > ci)
    tri = lower.astype(F32)
    blk = same.astype(F32)
    eye = (ri == ci).astype(F32)
    rowhead = lax.broadcasted_iota(I32, (r, hd), 0) // c

    def stack(a, sub, gi, base):
        return jnp.concatenate([a[sub * c:(sub + 1) * c, base + (gi * g + j) * hd:base + (gi * g + j + 1) * hd]
                                for j in range(g)], axis=0)

    def column(a, sub, gi):
        return jnp.concatenate([a[sub * c:(sub + 1) * c, gi * g + j:gi * g + j + 1] for j in range(g)], axis=0)

    def own_block(m):
        acc = jnp.where(rowhead == 0, m[:, 0:hd], 0.0)
        for j in range(1, g):
            acc = acc + jnp.where(rowhead == j, m[:, j * hd:(j + 1) * hd], 0.0)
        return acc

    keys = [(sub, gi) for sub in range(nsub) for gi in range(ngrp)]
    grp = {}
    for sub, gi in keys:
        q = stack(qkv, sub, gi, 0)
        k = stack(qkv, sub, gi, DN_WIDTH)
        v = stack(qkv, sub, gi, 2 * DN_WIDTH)
        q = q * lax.rsqrt(jnp.sum(q * q, axis=-1, keepdims=True) + EPS) * (hd ** -0.5)
        k = k * lax.rsqrt(jnp.sum(k * k, axis=-1, keepdims=True) + EPS)
        beta = column(beta_all, sub, gi)
        gr = column(g_all, sub, gi)
        gcol = _fdot(tri, gr)
        grow = lax.dot_general(gr, tri, (((0,), (1,)), ((), ())), precision=HIGHEST,
                               preferred_element_type=F32)
        glast = _fdot(blk, gr)
        decay = jnp.exp(jnp.where(lower, gcol - grow, NEG_INF))
        kb = k * beta
        lm = jnp.where(strict, _dot3(kb, k, _NT) * decay, 0.0)
        grp[sub, gi] = dict(q=q, k=k, v=v, beta=beta, gcol=gcol, glast=glast, decay=decay, kb=kb, lm=lm)

    pair = ((ri // 2) == (ci // 2)) & (ri % 2 == 1) & (ci % 2 == 0)
    invs = {key: eye - jnp.where(pair, grp[key]["lm"], 0.0) for key in keys}
    s = 2
    while s < c:
        off = ((ri // (2 * s)) == (ci // (2 * s))) & ((ri // s) % 2 == 1) & ((ci // s) % 2 == 0)
        tmp = {key: _dot3(invs[key], jnp.where(off, grp[key]["lm"], 0.0)) for key in keys}
        invs = {key: invs[key] - _dot3(tmp[key], invs[key]) for key in keys}
        s *= 2

    for sub in range(nsub):
        outs = [None] * DN_HEADS
        for gi in range(ngrp):
            d = grp[sub, gi]
            eg = jnp.exp(d["gcol"])
            sol = _dot3(invs[sub, gi], jnp.concatenate([d["v"] * d["beta"], d["kb"] * eg], axis=1))
            u = sol[:, :hd]
            w = sol[:, hd:]
            qk = _dot3(d["q"], d["k"], _NT) * d["decay"]
            ktail = d["k"] * jnp.exp(d["glast"] - d["gcol"])
            qhead = d["q"] * eg
            st = state_ref[gi]
            vnew = u - own_block(_dot3(w, st))
            o = own_block(_dot3(qhead, st)) + _dot3(qk, vnew)
            vblk = jnp.concatenate([jnp.where(rowhead == j, vnew, 0.0) for j in range(g)], axis=1)
            elast = jnp.exp(d["glast"])
            keep = jnp.concatenate([jnp.broadcast_to(elast[j * c:j * c + 1, :], (1, hd)) for j in range(g)],
                                   axis=1)
            state_ref[gi] = st * keep + _dot3(ktail, vblk, _TN)
            zs = stack(z, sub, gi, 0)
            o = _rms(o, nrm_ref[...]) * (zs * jax.nn.sigmoid(zs))
            for j in range(g):
                outs[gi * g + j] = o[j * c:(j + 1) * c]
        o_ref[sub * c:(sub + 1) * c, :] = jnp.concatenate(outs, axis=-1)

    @pl.when(n == pl.num_programs(1) - 1)
    def _():
        for gi in range(ngrp):
            for j in range(g):
                sfin_ref[0, gi * g + j] = state_ref[gi][:, j * hd:(j + 1) * hd]


GDN_CHUNKS_PER_STEP = 2


def _gdn(dqkv, z, ab, conv_prefix, s0, conv_w, a_log, dt_bias, dn_norm, batch, seq):
    c = min(DN_CHUNK, seq)
    nchunk = seq // c
    assert seq % c == 0 and c % 8 == 0
    g = min(DN_HEADS, max(1, 128 // c))
    nsub = GDN_CHUNKS_PER_STEP if nchunk % GDN_CHUNKS_PER_STEP == 0 else 1
    nstep = nchunk // nsub
    rows = nsub * c
    tok = lambda b, n: (b * nstep + n, 0)
    fixed = lambda b, n: (0, 0)
    return pl.pallas_call(
        functools.partial(_gdn_kernel, c=c, g=g, nsub=nsub), name="gdn", grid=(batch, nstep),
        out_shape=[jax.ShapeDtypeStruct((batch * seq, DN_WIDTH), F32),
                   jax.ShapeDtypeStruct((batch, DN_HEADS, HEAD_DIM, HEAD_DIM), F32)],
        in_specs=[pl.BlockSpec((1, DN_HEADS), fixed), pl.BlockSpec((1, DN_HEADS), fixed),
                  pl.BlockSpec((rows, DN_QKV), tok), pl.BlockSpec((rows, DN_WIDTH), tok),
                  pl.BlockSpec((rows, 2 * DN_HEADS), tok),
                  pl.BlockSpec((CONV_W, DN_QKV), fixed), pl.BlockSpec((1, HEAD_DIM), fixed),
                  pl.BlockSpec((1, CONV_W - 1, DN_QKV), lambda b, n: (b, 0, 0)),
                  pl.BlockSpec((1, DN_HEADS, HEAD_DIM, HEAD_DIM), lambda b, n: (b, 0, 0, 0))],
        out_specs=[pl.BlockSpec((rows, DN_WIDTH), tok),
                   pl.BlockSpec((1, DN_HEADS, HEAD_DIM, HEAD_DIM), lambda b, n: (b, 0, 0, 0))],
        scratch_shapes=[pltpu.VMEM((8, DN_QKV), F32),
                        pltpu.VMEM((DN_HEADS // g, HEAD_DIM, g * HEAD_DIM), F32)],
        compiler_params=_cparams("parallel", "arbitrary"))(
            a_log.reshape(1, DN_HEADS), dt_bias.reshape(1, DN_HEADS), dqkv, z, ab, conv_w,
            dn_norm.reshape(1, HEAD_DIM), conv_prefix, s0)


def _outproj_kernel(x_ref, att_ref, dn_ref, wo_ref, g_ref, wq_ref, h_ref, c_ref, q_ref):
    mix = jnp.concatenate([att_ref[...], dn_ref[...]], axis=-1)
    h = x_ref[...] + jnp.dot(mix.astype(BF16), wo_ref[...], preferred_element_type=F32)
    h_ref[...] = h
    cn = _rms(h, g_ref[...])
    c_ref[...] = cn
    q_ref[...] = jnp.dot(cn.astype(BF16), wq_ref[...], preferred_element_type=F32)


def _outproj(x, att, dn, w_out, g_ffn, wq):
    t = x.shape[0]
    tm = _row_tile(t, 512)
    row = lambda i: (i, 0)
    fixed = lambda i: (0, 0)
    qw = PEER_HEADS * PEER_QDIM
    return pl.pallas_call(
        _outproj_kernel, name="outproj", grid=(t // tm,),
        out_shape=[jax.ShapeDtypeStruct((t, D_MODEL), F32), jax.ShapeDtypeStruct((t, D_MODEL), F32),
                   jax.ShapeDtypeStruct((t, qw), F32)],
        in_specs=[pl.BlockSpec((tm, D_MODEL), row), pl.BlockSpec((tm, ATT_Q), row),
                  pl.BlockSpec((tm, DN_WIDTH), row), pl.BlockSpec((D_MODEL, D_MODEL), fixed),
                  pl.BlockSpec((1, D_MODEL), fixed), pl.BlockSpec((D_MODEL, qw), fixed)],
        out_specs=[pl.BlockSpec((tm, D_MODEL), row), pl.BlockSpec((tm, D_MODEL), row),
                   pl.BlockSpec((tm, qw), row)],
        compiler_params=_cparams("parallel"))(x, att, dn, w_out, g_ffn, wq)


def _top16_rows(s, iota):
    big = jnp.float32(INDEX_SENTINEL)
    vals, idxs = [], []
    for _ in range(PEER_TOPK):
        m = jnp.max(s, axis=0, keepdims=True)
        i = jnp.min(jnp.where(s == m, iota, big), axis=0, keepdims=True)
        vals.append(m)
        idxs.append(i)
        s = jnp.where(iota == i, NEG_INF, s)
    return jnp.concatenate(vals, axis=0), jnp.concatenate(idxs, axis=0)


def _candidates(a, b, row8):
    t = a.shape[1]
    lo = row8 < 4
    bc = lambda r, n: jnp.broadcast_to(a[r:r + 1], (n, t))
    b8 = b[0:8]
    b44 = jnp.where(lo, b8, pltpu.roll(b8, 4, 0))
    parts = [bc(0, 8), bc(0, 8), bc(1, 8), bc(2, 8), bc(3, 8),
             jnp.where(lo, bc(4, 8), bc(5, 8)), jnp.where(lo, bc(6, 8), bc(7, 8)), a[8:16]]
    others = [b[0:8], b[8:16], b8, b8, b8, b44, b44, jnp.broadcast_to(b[0:1], (8, t))]
    return parts, others


def _flat_index_column():
    r = lax.broadcasted_iota(I32, (8, 1), 0)
    lo = r < 4
    cols = [r, 8 + r, 16 + r, 32 + r, 48 + r, jnp.where(lo, 64 + r, 80 + r - 4),
            jnp.where(lo, 96 + r, 112 + r - 4), (8 + r) * 16]
    return jnp.concatenate(cols, axis=0).astype(F32)


def _route_kernel(q_ref, k1_ref, k2_ref, idx_ref, gate_ref, *, tb):
    iota = lax.broadcasted_iota(I32, (N_KEYS, tb), 0).astype(F32)
    row8 = lax.broadcasted_iota(I32, (8, tb), 0)
    flat = jnp.broadcast_to(_flat_index_column(), (4 * PEER_TOPK, tb))
    big = jnp.float32(INDEX_SENTINEL)
    half = PEER_QDIM // 2
    for h in range(PEER_HEADS):
        q1 = q_ref[:, h * PEER_QDIM:h * PEER_QDIM + half]
        q2 = q_ref[:, h * PEER_QDIM + half:(h + 1) * PEER_QDIM]
        nt = (((1,), (1,)), ((), ()))
        s1 = lax.dot_general(k1_ref[h], q1.astype(BF16), nt, preferred_element_type=F32)
        s2 = lax.dot_general(k2_ref[h], q2.astype(BF16), nt, preferred_element_type=F32)
        v1, i1 = _top16_rows(s1, iota)
        v2, i2 = _top16_rows(s2, iota)
        pa, pb = _candidates(v1, v2, row8)
        cand = jnp.concatenate([x + y for x, y in zip(pa, pb)], axis=0)
        ea, eb = _candidates(i1, i2, row8)
        expert = jnp.concatenate([x * float(N_KEYS) + y for x, y in zip(ea, eb)], axis=0)
        scs, exs = [], []
        for _ in range(PEER_TOPK):
            m = jnp.max(cand, axis=0, keepdims=True)
            c = jnp.min(jnp.where(cand == m, flat, big), axis=0, keepdims=True)
            hit = flat == c
            exs.append(jnp.max(jnp.where(hit, expert, -1.0), axis=0, keepdims=True))
            scs.append(m)
            cand = jnp.where(hit, NEG_INF, cand)
        sc = jnp.concatenate(scs, axis=0)
        e = jnp.exp(sc - sc[0:1])
        gate_ref[h * PEER_TOPK:(h + 1) * PEER_TOPK, :] = e / jnp.sum(e, axis=0, keepdims=True)
        idx_ref[h * PEER_TOPK:(h + 1) * PEER_TOPK, :] = jnp.concatenate(exs, axis=0).astype(I32)


def _route(q, keys1, keys2):
    t = q.shape[0]
    tb = _row_tile(t, 256)
    kspec = pl.BlockSpec((PEER_HEADS, N_KEYS, PEER_QDIM // 2), lambda i: (0, 0, 0))
    return pl.pallas_call(
        functools.partial(_route_kernel, tb=tb), name="peer_route", grid=(t // tb,),
        out_shape=[jax.ShapeDtypeStruct((PEER_SLOTS, t), I32), jax.ShapeDtypeStruct((PEER_SLOTS, t), F32)],
        in_specs=[pl.BlockSpec((tb, PEER_HEADS * PEER_QDIM), lambda i: (i, 0)), kspec, kspec],
        out_specs=[pl.BlockSpec((PEER_SLOTS, tb), lambda i: (0, i)),
                   pl.BlockSpec((PEER_SLOTS, tb), lambda i: (0, i))],
        compiler_params=_cparams("parallel"))(q, keys1, keys2)


GATHER_SLOTS = 3
SC_WORKERS = 32
SC_GATHER_ROWS = 16
SC_BUFFERS = 6
STAGED_TOKENS_PER_STEP = 16


def _peer_token(packed, x, gate):
    u = lax.bitcast_convert_type(packed & jnp.uint32(0xFFFF0000), F32)
    v = lax.bitcast_convert_type(packed << 16, F32)
    p = u * x
    acc = p[:, 0:128]
    for j in range(1, D_MODEL // 128):
        acc = acc + p[:, j * 128:(j + 1) * 128]
    s = jnp.sum(acc, axis=-1, keepdims=True)
    act = 0.5 * s * (1.0 + lax.erf(s * (1.0 / math.sqrt(2.0))))
    return jnp.sum((gate * act) * v, axis=0, keepdims=True)


def _peer_apply_kernel(idx_ref, c_ref, h_ref, gt_ref, uv_ref, o_ref, buf, sem, *, tb):
    ahead = GATHER_SLOTS - 1

    def issue(t, slot):
        for k in range(PEER_SLOTS):
            e = idx_ref[k, t]
            pltpu.make_async_copy(uv_ref.at[e], buf.at[slot, pl.ds(k, 1), :],
                                  sem.at[slot]).start(priority=k % 2)

    def wait(slot):
        pltpu.make_async_copy(buf.at[slot], buf.at[slot], sem.at[slot]).wait()

    lane = lax.broadcasted_iota(I32, (PEER_SLOTS, tb), 1)

    def apply(t, slot):
        gate = jnp.sum(jnp.where(lane == t, gt_ref[...], 0.0), axis=-1, keepdims=True)
        y = _peer_token(buf[slot], c_ref[pl.ds(t, 1), :], gate)
        o_ref[pl.ds(t, 1), :] = h_ref[pl.ds(t, 1), :] + y

    for t0 in range(ahead):
        issue(t0, t0)

    def body(i, carry):
        for j in range(GATHER_SLOTS):
            t = i * GATHER_SLOTS + j
            issue(t + ahead, (j + ahead) % GATHER_SLOTS)
            wait(j)
            apply(t, j)
        return carry

    nmain = (tb - ahead) // GATHER_SLOTS
    lax.fori_loop(0, nmain, body, 0)
    for t in range(nmain * GATHER_SLOTS, tb):
        if t + ahead < tb:
            issue(t + ahead, (t + ahead) % GATHER_SLOTS)
        wait(t % GATHER_SLOTS)
        apply(t, t % GATHER_SLOTS)


def _pack_expert_rows(u, v):
    hi = lax.bitcast_convert_type(u.astype(BF16), jnp.uint16).astype(jnp.uint32) << 16
    lo = lax.bitcast_convert_type(v.astype(BF16), jnp.uint16).astype(jnp.uint32)
    return hi | lo


def _peer_apply_direct(idx_t, c, h, gate_t, uv_rows, t_direct):
    t = c.shape[0]
    tb = _row_tile(t_direct, 256)
    row = lambda i: (i, 0)
    return pl.pallas_call(
        functools.partial(_peer_apply_kernel, tb=tb), name="peer_apply", grid=(t_direct // tb,),
        out_shape=jax.ShapeDtypeStruct((t, D_MODEL), F32),
        in_specs=[pl.BlockSpec((PEER_SLOTS, tb), lambda i: (0, i), memory_space=pltpu.SMEM),
                  pl.BlockSpec((tb, D_MODEL), row), pl.BlockSpec((tb, D_MODEL), row),
                  pl.BlockSpec((PEER_SLOTS, tb), lambda i: (0, i)),
                  pl.BlockSpec(memory_space=pl.ANY)],
        out_specs=pl.BlockSpec((tb, D_MODEL), row),
        scratch_shapes=[pltpu.VMEM((GATHER_SLOTS, PEER_SLOTS, D_MODEL), jnp.uint32),
                        pltpu.SemaphoreType.DMA((GATHER_SLOTS,))],
        compiler_params=_cparams("arbitrary"))(idx_t, c, h, gate_t, uv_rows)


def _sc_stage_rows(table, idx):
    t_total, nchunk_tok, _ = idx.shape
    d = table.shape[1]
    tpw = t_total // SC_WORKERS
    assert t_total % SC_WORKERS == 0 and tpw >= 2
    nb = SC_BUFFERS
    ahead = nb // 2
    params = pltpu.CompilerParams()
    if "needs_layout_passes" in pltpu.CompilerParams.__dataclass_fields__:
        params = dataclasses.replace(params, needs_layout_passes=False)
    mesh = plsc.VectorSubcoreMesh(core_axis_name="c", subcore_axis_name="s")

    @functools.partial(
        pl.kernel, mesh=mesh, compiler_params=params,
        out_type=jax.ShapeDtypeStruct((t_total * nchunk_tok * SC_GATHER_ROWS, d), table.dtype),
        scratch_types=[pltpu.VMEM((2, nchunk_tok, SC_GATHER_ROWS), I32),
                       pltpu.VMEM((nb, SC_GATHER_ROWS, d), table.dtype),
                       pltpu.SemaphoreType.DMA((nb,)), pltpu.SemaphoreType.DMA((nb,)),
                       pltpu.SemaphoreType.DMA((2,))])
    def stage(table_hbm, idx_hbm, out_hbm, idx_v, rows_v, sem_g, sem_w, sem_t):
        wid = lax.axis_index("s") * 2 + lax.axis_index("c")
        base = wid * tpw
        nchunks = tpw * nchunk_tok

        def gather(g, slot):
            ts = (g // nchunk_tok) % 2
            return pltpu.make_async_copy(table_hbm.at[idx_v.at[ts, g % nchunk_tok]], rows_v.at[slot],
                                         sem_g.at[slot])

        def write(g, slot):
            dst = out_hbm.at[pl.ds((base * nchunk_tok + g) * SC_GATHER_ROWS, SC_GATHER_ROWS)]
            return pltpu.make_async_copy(rows_v.at[slot], dst, sem_w.at[slot])

        def idx_copy(i, ts):
            return pltpu.make_async_copy(idx_hbm.at[base + i], idx_v.at[ts], sem_t.at[ts])

        idx_copy(0, 0).start()
        idx_copy(0, 0).wait()
        idx_copy(1, 1).start()
        for g0 in range(ahead):
            gather(g0, g0).start()

        @pl.loop(0, nchunks)
        def _(g):
            slot = g % nb
            nxt = g + ahead
            nslot = nxt % nb

            @pl.when(nxt < nchunks)
            def _():
                @pl.when(nxt >= nb)
                def _():
                    write(nxt - nb, nslot).wait()

                @pl.when(nxt % nchunk_tok == 0)
                def _():
                    idx_copy(nxt // nchunk_tok, (nxt // nchunk_tok) % 2).wait()

                gather(nxt, nslot).start()

            gather(g, slot).wait()
            write(g, slot).start()

            @pl.when((g % nchunk_tok == nchunk_tok - 1) & (g // nchunk_tok + 2 < tpw))
            def _():
                idx_copy(g // nchunk_tok + 2, (g // nchunk_tok) % 2).start()

        for j in range(nb):
            gl = nchunks - nb + j
            write(gl, gl % nb).wait()

    return stage(table, idx)


def _peer_staged_kernel(st_ref, c_ref, h_ref, gt_ref, prev_ref, o_ref, *, tbs):
    del prev_ref
    i = pl.program_id(0)
    lane = lax.broadcasted_iota(I32, (PEER_SLOTS, 128), 1)
    for t in range(tbs):
        tg = i * tbs + t
        col0 = pl.multiple_of((tg // 128) * 128, 128)
        gate = jnp.sum(jnp.where(lane == tg % 128, gt_ref[:, pl.ds(col0, 128)], 0.0), axis=-1, keepdims=True)
        y = _peer_token(st_ref[t], c_ref[pl.ds(t, 1), :], gate)
        o_ref[pl.ds(t, 1), :] = h_ref[pl.ds(t, 1), :] + y


def _peer_apply_staged(staged, c, h, gate_t_staged, partial, t_direct):
    t_staged = staged.shape[0]
    tbs = STAGED_TOKENS_PER_STEP
    assert t_staged % tbs == 0 and t_direct % tbs == 0 and t_staged % 128 == 0
    off = t_direct // tbs
    row = lambda i: (i + off, 0)
    return pl.pallas_call(
        functools.partial(_peer_staged_kernel, tbs=tbs), name="peer_apply_staged", grid=(t_staged // tbs,),
        out_shape=jax.ShapeDtypeStruct(partial.shape, F32),
        in_specs=[pl.BlockSpec((tbs, PEER_SLOTS, D_MODEL), lambda i: (i, 0, 0)),
                  pl.BlockSpec((tbs, D_MODEL), row), pl.BlockSpec((tbs, D_MODEL), row),
                  pl.BlockSpec((PEER_SLOTS, t_staged), lambda i: (0, 0)),
                  pl.BlockSpec(memory_space=pl.ANY)],
        out_specs=pl.BlockSpec((tbs, D_MODEL), row),
        input_output_aliases={4: 0},
        compiler_params=_cparams("arbitrary"))(staged, c, h, gate_t_staged, partial)


def _peer_apply(idx_t, c, h, gate_t, uv, staged_pieces):
    t = c.shape[0]
    start = t - sum(staged_pieces)
    out = _peer_apply_direct(idx_t, c, h, gate_t, uv[:, None, :], start)
    for n in staged_pieces:
        idx_sc = idx_t[:, start:start + n].T.reshape(n, PEER_SLOTS // SC_GATHER_ROWS, SC_GATHER_ROWS)
        staged = _sc_stage_rows(uv, idx_sc).reshape(n, PEER_SLOTS, D_MODEL)
        out = _peer_apply_staged(staged, c, h, gate_t[:, start:start + n], out, start)
        start += n
    return out


def _final_kernel(h_ref, p_ref, gple_ref, wg_ref, wp_ref, gfin_ref, y_ref):
    h = h_ref[...]
    e = _rms(h, gple_ref[...])
    gate = jax.nn.sigmoid(jnp.dot(e.astype(BF16), wg_ref[...], preferred_element_type=F32))
    up = jnp.dot(p_ref[...].astype(BF16), wp_ref[...], preferred_element_type=F32)
    y_ref[...] = _rms(h + up * gate, gfin_ref[...])


def _final(h, p, g_ple, w_gate, w_ple, g_final):
    t = h.shape[0]
    tm = _row_tile(t, 512)
    row = lambda i: (i, 0)
    fixed = lambda i: (0, 0)
    return pl.pallas_call(
        _final_kernel, name="ple_final", grid=(t // tm,),
        out_shape=jax.ShapeDtypeStruct((t, D_MODEL), F32),
        in_specs=[pl.BlockSpec((tm, D_MODEL), row), pl.BlockSpec((tm, PLE_DIM), row),
                  pl.BlockSpec((1, D_MODEL), fixed), pl.BlockSpec((D_MODEL, D_MODEL), fixed),
                  pl.BlockSpec((PLE_DIM, D_MODEL), fixed), pl.BlockSpec((1, D_MODEL), fixed)],
        out_specs=pl.BlockSpec((tm, D_MODEL), row),
        compiler_params=_cparams("parallel"))(h, p, g_ple, w_gate, w_ple, g_final)


STAGED_FRACTION = 0.75
STAGED_PIECE = 2 * SC_WORKERS * 128


def _staged_tokens(t):
    n = int(t * STAGED_FRACTION) // STAGED_PIECE
    return (STAGED_PIECE,) * n if n else ()


def _layer(x, p, kv_buf, conv_prefix, s0, wts, batch, seq):
    t = batch * seq
    q, k, v, dqkv, z, ab = _inproj(x, wts["norm_mix"], wts["w_in"])
    if kv_buf is None:
        att = _swa_prompt(q, k, v, wts["sinks"], batch, seq)
        k_new = k.reshape(batch, seq, ATT_KV)[:, -WINDOW:]
        v_new = v.reshape(batch, seq, ATT_KV)[:, -WINDOW:]
    else:
        att, k_new, v_new = _swa_sample(q.reshape(batch, seq, ATT_Q), k.reshape(batch, seq, ATT_KV),
                                        v.reshape(batch, seq, ATT_KV), kv_buf[0], kv_buf[1], wts["sinks"])
        att = att.reshape(t, ATT_Q)
    dn, s_new = _gdn(dqkv, z, ab, conv_prefix, s0, wts["conv_w"], wts["a_log"], wts["dt_bias"],
                     wts["dn_norm"], batch, seq)
    conv_new = dqkv.reshape(batch, seq, DN_QKV)[:, -(CONV_W - 1):]
    h, c, pq = _outproj(x, att, dn, wts["w_out"], wts["norm_ffn"], wts["peer_wq"])
    idx_t, gate_t = _route(pq, wts["keys1"], wts["keys2"])
    h2 = _peer_apply(idx_t, c, h, gate_t, wts["uv"], _staged_tokens(t))
    y = _final(h2, p, wts["norm_ple"], wts["ple_gate"], wts["ple_in"], wts["norm_final"])
    return y, k_new, v_new, conv_new, s_new


def kernel(x_prompt, x_sample, p_prompt, p_sample, cache_swa_k, cache_swa_v, state_conv, state_delta, norm_mix, w_in, conv_w, attn_sinks, dn_a_log, dn_dt_bias, dn_norm, w_out, norm_ffn, peer_wq, peer_keys1, peer_keys2, peer_u, peer_v, norm_ple, ple_in, ple_gate, norm_final):
    depth = w_in.shape[0]
    assert depth == 1
    bp, lp = x_prompt.shape[0], x_prompt.shape[1]
    bs, ls = x_sample.shape[0], x_sample.shape[1]
    l = 0
    wts = dict(
        norm_mix=norm_mix[l].reshape(1, D_MODEL),
        w_in=jnp.pad(w_in[l], ((0, 0), (0, IN_WIDTH_PADDED - IN_WIDTH))).astype(BF16),
        conv_w=conv_w[l], sinks=attn_sinks[l], a_log=dn_a_log[l], dt_bias=dn_dt_bias[l], dn_norm=dn_norm[l],
        w_out=w_out[l].astype(BF16), norm_ffn=norm_ffn[l].reshape(1, D_MODEL), peer_wq=peer_wq[l].astype(BF16),
        keys1=peer_keys1[l].astype(BF16), keys2=peer_keys2[l].astype(BF16),
        uv=_pack_expert_rows(peer_u[l], peer_v[l]),
        norm_ple=norm_ple[l].reshape(1, D_MODEL), ple_in=ple_in[l].astype(BF16),
        ple_gate=ple_gate[l].astype(BF16), norm_final=norm_final.reshape(1, D_MODEL))
    conv0 = jnp.zeros((bp, CONV_W - 1, DN_QKV), F32)
    s0 = jnp.zeros((bp, DN_HEADS, HEAD_DIM, HEAD_DIM), F32)
    yp, kp, vp, cp, sp = _layer(x_prompt.reshape(bp * lp, D_MODEL), p_prompt[l].reshape(bp * lp, PLE_DIM),
                                None, conv0, s0, wts, bp, lp)
    kv_buf = (cache_swa_k[l].reshape(bs, WINDOW, ATT_KV), cache_swa_v[l].reshape(bs, WINDOW, ATT_KV))
    ys, kn, vn, cn, sn = _layer(x_sample.reshape(bs * ls, D_MODEL), p_sample[l].reshape(bs * ls, PLE_DIM),
                                kv_buf, state_conv[l], state_delta[l], wts, bs, ls)
    kvshape = lambda b: (1, b, WINDOW, ATT_KV_HEADS, HEAD_DIM)
    return (yp.reshape(bp, lp, D_MODEL), ys.reshape(bs, ls, D_MODEL),
            kp.reshape(kvshape(bp)), vp.reshape(kvshape(bp)), cp[None], sp[None],
            kn.reshape(kvshape(bs)), vn.reshape(kvshape(bs)), cn[None], sn[None])
```

```python
import dataclasses
import functools
import math

import jax
import jax.numpy as jnp
from jax import lax
from jax.experimental import pallas as pl
from jax.experimental.pallas import tpu as pltpu
from jax.experimental.pallas import tpu_sc as plsc

F32 = jnp.float32
BF16 = jnp.bfloat16
I32 = jnp.int32
HIGHEST = lax.Precision.HIGHEST

D_MODEL = 1024
HEAD_DIM = 64
ATT_HEADS = 8
ATT_KV_HEADS = 2
ATT_GROUP = 4
WINDOW = 128
ATT_Q = ATT_HEADS * HEAD_DIM
ATT_KV = ATT_KV_HEADS * HEAD_DIM
DN_HEADS = 8
DN_WIDTH = DN_HEADS * HEAD_DIM
DN_QKV = 3 * DN_WIDTH
CONV_W = 4
DN_CHUNK = 64
IN_WIDTH = ATT_Q + 2 * ATT_KV + DN_QKV + DN_WIDTH + 2 * DN_HEADS
IN_WIDTH_PADDED = 2944
N_KEYS = 128
PEER_HEADS = 8
PEER_QDIM = 256
PEER_TOPK = 16
PEER_SLOTS = PEER_HEADS * PEER_TOPK
PLE_DIM = 256
EPS = 1e-6
NEG_INF = float("-inf")
INDEX_SENTINEL = 1.0e9

VMEM_LIMIT = 48 * 1024 * 1024


def _cparams(*sem):
    return pltpu.CompilerParams(dimension_semantics=sem, vmem_limit_bytes=VMEM_LIMIT)


def _rms(x, g):
    return x * lax.rsqrt(jnp.mean(x * x, axis=-1, keepdims=True) + EPS) * g


def _bdot(a, b):
    return jnp.dot(a.astype(BF16), b.astype(BF16), preferred_element_type=F32)


def _fdot(a, b):
    return jnp.dot(a, b, precision=HIGHEST, preferred_element_type=F32)


_NN = (((1,), (0,)), ((), ()))
_NT = (((1,), (1,)), ((), ()))
_TN = (((0,), (0,)), ((), ()))


def _row_tile(t, target):
    tile = min(t, target)
    assert t % tile == 0, (t, tile)
    return tile


def _inproj_kernel(x_ref, g_ref, w_ref, q_ref, k_ref, v_ref, dqkv_ref, z_ref, ab_ref):
    a = _rms(x_ref[...], g_ref[...])
    p = jnp.dot(a.astype(BF16), w_ref[...], preferred_element_type=F32)
    c0, c1, c2, c3, c4 = ATT_Q, ATT_Q + ATT_KV, ATT_Q + 2 * ATT_KV, ATT_Q + 2 * ATT_KV + DN_QKV, \
        ATT_Q + 2 * ATT_KV + DN_QKV + DN_WIDTH
    q_ref[...] = p[:, :c0]
    k_ref[...] = p[:, c0:c1]
    v_ref[...] = p[:, c1:c2]
    dqkv_ref[...] = p[:, c2:c3]
    z_ref[...] = p[:, c3:c4]
    ab_ref[...] = p[:, c4:c4 + 2 * DN_HEADS]


def _inproj(x, g, w_pad):
    t = x.shape[0]
    tm = _row_tile(t, 512)
    widths = (ATT_Q, ATT_KV, ATT_KV, DN_QKV, DN_WIDTH, 2 * DN_HEADS)
    row = lambda i: (i, 0)
    fixed = lambda i: (0, 0)
    return pl.pallas_call(
        _inproj_kernel, name="inproj", grid=(t // tm,),
        out_shape=[jax.ShapeDtypeStruct((t, w), F32) for w in widths],
        in_specs=[pl.BlockSpec((tm, D_MODEL), row), pl.BlockSpec((1, D_MODEL), fixed),
                  pl.BlockSpec((D_MODEL, IN_WIDTH_PADDED), fixed)],
        out_specs=[pl.BlockSpec((tm, w), row) for w in widths],
        compiler_params=_cparams("parallel"))(x, g, w_pad)


def _softmax_sink_pv(s, sink, vband):
    m = jnp.maximum(jnp.max(s, axis=-1, keepdims=True), sink)
    e = jnp.exp(s - m)
    denom = jnp.sum(e, axis=-1, keepdims=True) + jnp.exp(sink - m)
    return _bdot(e / denom, vband)


def _alibi_slope(h):
    return 2.0 ** (-8.0 * (h + 1) / ATT_HEADS)


def _group_columns(kh, rows, sink_ref):
    rg = lax.broadcasted_iota(I32, (ATT_GROUP * rows, 1), 0) // rows
    slope = jnp.full((ATT_GROUP * rows, 1), _alibi_slope(kh * ATT_GROUP), F32)
    sink = jnp.full((ATT_GROUP * rows, 1), sink_ref[kh * ATT_GROUP], F32)
    for j in range(1, ATT_GROUP):
        slope = jnp.where(rg == j, _alibi_slope(kh * ATT_GROUP + j), slope)
        sink = jnp.where(rg == j, sink_ref[kh * ATT_GROUP + j], sink)
    return slope, sink


def _attend_kv_head(q_all, kband, vband, kh, dist, valid, sink_ref):
    rows = q_all.shape[0]
    heads = range(kh * ATT_GROUP, (kh + 1) * ATT_GROUP)
    q4 = jnp.concatenate([q_all[:, h * HEAD_DIM:(h + 1) * HEAD_DIM] for h in heads], axis=0)
    k = kband[:, kh * HEAD_DIM:(kh + 1) * HEAD_DIM]
    v = vband[:, kh * HEAD_DIM:(kh + 1) * HEAD_DIM]
    s = lax.dot_general(q4.astype(BF16), k.astype(BF16), _NT, preferred_element_type=F32) * (HEAD_DIM ** -0.5)
    slope, sink = _group_columns(kh, rows, sink_ref)
    s = jnp.where(valid, s - slope * dist, NEG_INF)
    o4 = _softmax_sink_pv(s, sink, v)
    return [o4[j * rows:(j + 1) * rows] for j in range(ATT_GROUP)]


def _swa_prompt_kernel(sink_ref, q_ref, kc_ref, kp_ref, vc_ref, vp_ref, o_ref):
    n = pl.program_id(1)
    rows = ATT_GROUP * WINDOW
    qi = lax.broadcasted_iota(I32, (rows, 2 * WINDOW), 0) % WINDOW
    kj = lax.broadcasted_iota(I32, (rows, 2 * WINDOW), 1)
    dist = WINDOW + qi - kj
    kpos = n * WINDOW - WINDOW + kj
    valid = (dist >= 0) & (dist <= WINDOW) & (kpos >= 0)
    distf = dist.astype(F32)
    kband = jnp.concatenate([kp_ref[...], kc_ref[...]], axis=0)
    vband = jnp.concatenate([vp_ref[...], vc_ref[...]], axis=0)
    q_all = q_ref[...]
    outs = []
    for kh in range(ATT_KV_HEADS):
        outs += _attend_kv_head(q_all, kband, vband, kh, distf, valid, sink_ref)
    o_ref[...] = jnp.concatenate(outs, axis=-1)


def _swa_prompt(q, k, v, sinks, batch, seq):
    nblk = seq // WINDOW
    cur = lambda b, n: (b * nblk + n, 0)
    prev = lambda b, n: (b * nblk + jnp.maximum(n - 1, 0), 0)
    return pl.pallas_call(
        _swa_prompt_kernel, name="swa_prompt", grid=(batch, nblk),
        out_shape=jax.ShapeDtypeStruct((batch * seq, ATT_Q), F32),
        in_specs=[pl.BlockSpec(memory_space=pltpu.SMEM),
                  pl.BlockSpec((WINDOW, ATT_Q), cur),
                  pl.BlockSpec((WINDOW, ATT_KV), cur), pl.BlockSpec((WINDOW, ATT_KV), prev),
                  pl.BlockSpec((WINDOW, ATT_KV), cur), pl.BlockSpec((WINDOW, ATT_KV), prev)],
        out_specs=pl.BlockSpec((WINDOW, ATT_Q), cur),
        compiler_params=_cparams("parallel", "parallel"))(sinks, q, k, k, v, v)


def _swa_sample_kernel(sink_ref, q_ref, k_ref, v_ref, kb_ref, vb_ref, o_ref, ko_ref, vo_ref, *, bt, seq):
    klen = WINDOW + seq
    rows = ATT_GROUP * seq
    qi = lax.broadcasted_iota(I32, (rows, klen), 0) % seq
    kj = lax.broadcasted_iota(I32, (rows, klen), 1)
    dist = qi - (kj - WINDOW)
    valid = (dist >= 0) & (dist <= WINDOW)
    distf = dist.astype(F32)
    for b in range(bt):
        kc = jnp.concatenate([kb_ref[b], k_ref[b]], axis=0)
        vc = jnp.concatenate([vb_ref[b], v_ref[b]], axis=0)
        ko_ref[b] = kc[seq:]
        vo_ref[b] = vc[seq:]
        q_all = q_ref[b]
        outs = []
        for kh in range(ATT_KV_HEADS):
            outs += _attend_kv_head(q_all, kc, vc, kh, distf, valid, sink_ref)
        o_ref[b] = jnp.concatenate(outs, axis=-1)


def _swa_sample(q, k, v, k_buf, v_buf, sinks):
    batch, seq = q.shape[0], q.shape[1]
    bt = _row_tile(batch, 8)
    blk = lambda w, r: pl.BlockSpec((bt, r, w), lambda i: (i, 0, 0))
    return pl.pallas_call(
        functools.partial(_swa_sample_kernel, bt=bt, seq=seq), name="swa_sample", grid=(batch // bt,),
        out_shape=[jax.ShapeDtypeStruct((batch, seq, ATT_Q), F32),
                   jax.ShapeDtypeStruct((batch, WINDOW, ATT_KV), F32),
                   jax.ShapeDtypeStruct((batch, WINDOW, ATT_KV), F32)],
        in_specs=[pl.BlockSpec(memory_space=pltpu.SMEM), blk(ATT_Q, seq), blk(ATT_KV, seq), blk(ATT_KV, seq),
                  blk(ATT_KV, WINDOW), blk(ATT_KV, WINDOW)],
        out_specs=[blk(ATT_Q, seq), blk(ATT_KV, WINDOW), blk(ATT_KV, WINDOW)],
        compiler_params=_cparams("parallel"))(sinks, q, k, v, k_buf, v_buf)


def _split_bf16(a, cache):
    if id(a) not in cache:
        hi = a.astype(BF16)
        cache[id(a)] = (a, (hi, (a - hi.astype(F32)).astype(BF16)))
    return cache[id(a)][1]


MXU_DEPTH = 256


def _dot3_impl(a, b, dims=_NN, *, cache):
    (ca,), (cb,) = dims[0]
    depth = a.shape[ca]
    ah, al = _split_bf16(a, cache)
    bh, bl = _split_bf16(b, cache)
    f = lambda x, y: lax.dot_general(x, y, dims, preferred_element_type=F32)
    if 3 * depth <= MXU_DEPTH:
        return f(jnp.concatenate([ah, al, ah], axis=ca), jnp.concatenate([bh, bh, bl], axis=cb))
    return f(jnp.concatenate([ah, al], axis=ca), jnp.concatenate([bh, bh], axis=cb)) + f(ah, bl)


def _gdn_kernel(alog_ref, dtb_ref, dqkv_ref, z_ref, ab_ref, cw_ref, nrm_ref, pre_ref, s0_ref,
                o_ref, sfin_ref, tail_ref, state_ref, *, c, g, nsub):
    n = pl.program_id(1)
    ngrp = DN_HEADS // g
    r = g * c
    rows = nsub * c
    hd = HEAD_DIM
    _dot3 = functools.partial(_dot3_impl, cache={})

    @pl.when(n == 0)
    def _():
        tail_ref[...] = jnp.concatenate([jnp.zeros((8 - (CONV_W - 1), DN_QKV), F32), pre_ref[0]], axis=0)
        for gi in range(ngrp):
            state_ref[gi] = jnp.concatenate([s0_ref[0, gi * g + j] for j in range(g)], axis=1)

    x = dqkv_ref[...]
    hist = jnp.concatenate([tail_ref[...], x], axis=0)
    tail_ref[...] = hist[rows:]
    cw = cw_ref[...]
    y = x * cw[CONV_W - 1:CONV_W]
    for d in range(1, CONV_W):
        y = y + hist[8 - d:8 - d + rows] * cw[CONV_W - 1 - d:CONV_W - d]
    qkv = y * jax.nn.sigmoid(y)
    z = z_ref[...]
    ab = ab_ref[...]
    beta_all = jax.nn.sigmoid(ab[:, DN_HEADS:2 * DN_HEADS])
    g_all = -jnp.exp(alog_ref[...]) * jax.nn.softplus(ab[:, 0:DN_HEADS] + dtb_ref[...])

    ri = lax.broadcasted_iota(I32, (r, r), 0)
    ci = lax.broadcasted_iota(I32, (r, r), 1)
    same = (ri // c) == (ci // c)
    lower = same & (ri >= ci)
    strict = same & (ri > ci)
    tri = lower.astype(F32)
    blk = same.astype(F32)
    eye = (ri == ci).astype(F32)
    rowhead = lax.broadcasted_iota(I32, (r, hd), 0) // c

    def stack(a, sub, gi, base):
        return jnp.concatenate([a[sub * c:(sub + 1) * c, base + (gi * g + j) * hd:base + (gi * g + j + 1) * hd]
                                for j in range(g)], axis=0)

    def column(a, sub, gi):
        return jnp.concatenate([a[sub * c:(sub + 1) * c, gi * g + j:gi * g + j + 1] for j in range(g)], axis=0)

    def own_block(m):
        acc = jnp.where(rowhead == 0, m[:, 0:hd], 0.0)
        for j in range(1, g):
            acc = acc + jnp.where(rowhead == j, m[:, j * hd:(j + 1) * hd], 0.0)
        return acc

    keys = [(sub, gi) for sub in range(nsub) for gi in range(ngrp)]
    grp = {}
    for sub, gi in keys:
        q = stack(qkv, sub, gi, 0)
        k = stack(qkv, sub, gi, DN_WIDTH)
        v = stack(qkv, sub, gi, 2 * DN_WIDTH)
        q = q * lax.rsqrt(jnp.sum(q * q, axis=-1, keepdims=True) + EPS) * (hd ** -0.5)
        k = k * lax.rsqrt(jnp.sum(k * k, axis=-1, keepdims=True) + EPS)
        beta = column(beta_all, sub, gi)
        gr = column(g_all, sub, gi)
        gcol = _fdot(tri, gr)
        grow = lax.dot_general(gr, tri, (((0,), (1,)), ((), ())), precision=HIGHEST,
                               preferred_element_type=F32)
        glast = _fdot(blk, gr)
        decay = jnp.exp(jnp.where(lower, gcol - grow, NEG_INF))
        kb = k * beta
        lm = jnp.where(strict, _dot3(kb, k, _NT) * decay, 0.0)
        grp[sub, gi] = dict(q=q, k=k, v=v, beta=beta, gcol=gcol, glast=glast, decay=decay, kb=kb, lm=lm)

    pair = ((ri // 2) == (ci // 2)) & (ri % 2 == 1) & (ci % 2 == 0)
    invs = {key: eye - jnp.where(pair, grp[key]["lm"], 0.0) for key in keys}
    s = 2
    while s < c:
        off = ((ri // (2 * s)) == (ci // (2 * s))) & ((ri // s) % 2 == 1) & ((ci // s) % 2 == 0)
        tmp = {key: _dot3(invs[key], jnp.where(off, grp[key]["lm"], 0.0)) for key in keys}
        invs = {key: invs[key] - _dot3(tmp[key], invs[key]) for key in keys}
        s *= 2

    for sub in range(nsub):
        outs = [None] * DN_HEADS
        for gi in range(ngrp):
            d = grp[sub, gi]
            eg = jnp.exp(d["gcol"])
            sol = _dot3(invs[sub, gi], jnp.concatenate([d["v"] * d["beta"], d["kb"] * eg], axis=1))
            u = sol[:, :hd]
            w = sol[:, hd:]
            qk = _dot3(d["q"], d["k"], _NT) * d["decay"]
            ktail = d["k"] * jnp.exp(d["glast"] - d["gcol"])
            qhead = d["q"] * eg
            st = state_ref[gi]
            vnew = u - own_block(_dot3(w, st))
            o = own_block(_dot3(qhead, st)) + _dot3(qk, vnew)
            vblk = jnp.concatenate([jnp.where(rowhead == j, vnew, 0.0) for j in range(g)], axis=1)
            elast = jnp.exp(d["glast"])
            keep = jnp.concatenate([jnp.broadcast_to(elast[j * c:j * c + 1, :], (1, hd)) for j in range(g)],
                                   axis=1)
            state_ref[gi] = st * keep + _dot3(ktail, vblk, _TN)
            zs = stack(z, sub, gi, 0)
            o = _rms(o, nrm_ref[...]) * (zs * jax.nn.sigmoid(zs))
            for j in range(g):
                outs[gi * g + j] = o[j * c:(j + 1) * c]
        o_ref[sub * c:(sub + 1) * c, :] = jnp.concatenate(outs, axis=-1)

    @pl.when(n == pl.num_programs(1) - 1)
    def _():
        for gi in range(ngrp):
            for j in range(g):
                sfin_ref[0, gi * g + j] = state_ref[gi][:, j * hd:(j + 1) * hd]


GDN_CHUNKS_PER_STEP = 2


def _gdn(dqkv, z, ab, conv_prefix, s0, conv_w, a_log, dt_bias, dn_norm, batch, seq):
    c = min(DN_CHUNK, seq)
    nchunk = seq // c
    assert seq % c == 0 and c % 8 == 0
    g = min(DN_HEADS, max(1, 128 // c))
    nsub = GDN_CHUNKS_PER_STEP if nchunk % GDN_CHUNKS_PER_STEP == 0 else 1
    nstep = nchunk // nsub
    rows = nsub * c
    tok = lambda b, n: (b * nstep + n, 0)
    fixed = lambda b, n: (0, 0)
    return pl.pallas_call(
        functools.partial(_gdn_kernel, c=c, g=g, nsub=nsub), name="gdn", grid=(batch, nstep),
        out_shape=[jax.ShapeDtypeStruct((batch * seq, DN_WIDTH), F32),
                   jax.ShapeDtypeStruct((batch, DN_HEADS, HEAD_DIM, HEAD_DIM), F32)],
        in_specs=[pl.BlockSpec((1, DN_HEADS), fixed), pl.BlockSpec((1, DN_HEADS), fixed),
                  pl.BlockSpec((rows, DN_QKV), tok), pl.BlockSpec((rows, DN_WIDTH), tok),
                  pl.BlockSpec((rows, 2 * DN_HEADS), tok),
                  pl.BlockSpec((CONV_W, DN_QKV), fixed), pl.BlockSpec((1, HEAD_DIM), fixed),
                  pl.BlockSpec((1, CONV_W - 1, DN_QKV), lambda b, n: (b, 0, 0)),
                  pl.BlockSpec((1, DN_HEADS, HEAD_DIM, HEAD_DIM), lambda b, n: (b, 0, 0, 0))],
        out_specs=[pl.BlockSpec((rows, DN_WIDTH), tok),
                   pl.BlockSpec((1, DN_HEADS, HEAD_DIM, HEAD_DIM), lambda b, n: (b, 0, 0, 0))],
        scratch_shapes=[pltpu.VMEM((8, DN_QKV), F32),
                        pltpu.VMEM((DN_HEADS // g, HEAD_DIM, g * HEAD_DIM), F32)],
        compiler_params=_cparams("parallel", "arbitrary"))(
            a_log.reshape(1, DN_HEADS), dt_bias.reshape(1, DN_HEADS), dqkv, z, ab, conv_w,
            dn_norm.reshape(1, HEAD_DIM), conv_prefix, s0)


def _outproj_kernel(x_ref, att_ref, dn_ref, wo_ref, g_ref, wq_ref, h_ref, c_ref, q_ref):
    mix = jnp.concatenate([att_ref[...], dn_ref[...]], axis=-1)
    h = x_ref[...] + jnp.dot(mix.astype(BF16), wo_ref[...], preferred_element_type=F32)
    h_ref[...] = h
    cn = _rms(h, g_ref[...])
    c_ref[...] = cn
    q_ref[...] = jnp.dot(cn.astype(BF16), wq_ref[...], preferred_element_type=F32)


def _outproj(x, att, dn, w_out, g_ffn, wq):
    t = x.shape[0]
    tm = _row_tile(t, 512)
    row = lambda i: (i, 0)
    fixed = lambda i: (0, 0)
    qw = PEER_HEADS * PEER_QDIM
    return pl.pallas_call(
        _outproj_kernel, name="outproj", grid=(t // tm,),
        out_shape=[jax.ShapeDtypeStruct((t, D_MODEL), F32), jax.ShapeDtypeStruct((t, D_MODEL), F32),
                   jax.ShapeDtypeStruct((t, qw), F32)],
        in_specs=[pl.BlockSpec((tm, D_MODEL), row), pl.BlockSpec((tm, ATT_Q), row),
                  pl.BlockSpec((tm, DN_WIDTH), row), pl.BlockSpec((D_MODEL, D_MODEL), fixed),
                  pl.BlockSpec((1, D_MODEL), fixed), pl.BlockSpec((D_MODEL, qw), fixed)],
        out_specs=[pl.BlockSpec((tm, D_MODEL), row), pl.BlockSpec((tm, D_MODEL), row),
                   pl.BlockSpec((tm, qw), row)],
        compiler_params=_cparams("parallel"))(x, att, dn, w_out, g_ffn, wq)


def _top16_rows(s, iota):
    big = jnp.float32(INDEX_SENTINEL)
    vals, idxs = [], []
    for _ in range(PEER_TOPK):
        m = jnp.max(s, axis=0, keepdims=True)
        i = jnp.min(jnp.where(s == m, iota, big), axis=0, keepdims=True)
        vals.append(m)
        idxs.append(i)
        s = jnp.where(iota == i, NEG_INF, s)
    return jnp.concatenate(vals, axis=0), jnp.concatenate(idxs, axis=0)


def _candidates(a, b, row8):
    t = a.shape[1]
    lo = row8 < 4
    bc = lambda r, n: jnp.broadcast_to(a[r:r + 1], (n, t))
    b8 = b[0:8]
    b44 = jnp.where(lo, b8, pltpu.roll(b8, 4, 0))
    parts = [bc(0, 8), bc(0, 8), bc(1, 8), bc(2, 8), bc(3, 8),
             jnp.where(lo, bc(4, 8), bc(5, 8)), jnp.where(lo, bc(6, 8), bc(7, 8)), a[8:16]]
    others = [b[0:8], b[8:16], b8, b8, b8, b44, b44, jnp.broadcast_to(b[0:1], (8, t))]
    return parts, others


def _flat_index_column():
    r = lax.broadcasted_iota(I32, (8, 1), 0)
    lo = r < 4
    cols = [r, 8 + r, 16 + r, 32 + r, 48 + r, jnp.where(lo, 64 + r, 80 + r - 4),
            jnp.where(lo, 96 + r, 112 + r - 4), (8 + r) * 16]
    return jnp.concatenate(cols, axis=0).astype(F32)


def _route_kernel(q_ref, k1_ref, k2_ref, idx_ref, gate_ref, *, tb):
    iota = lax.broadcasted_iota(I32, (N_KEYS, tb), 0).astype(F32)
    row8 = lax.broadcasted_iota(I32, (8, tb), 0)
    flat = jnp.broadcast_to(_flat_index_column(), (4 * PEER_TOPK, tb))
    big = jnp.float32(INDEX_SENTINEL)
    half = PEER_QDIM // 2
    for h in range(PEER_HEADS):
        q1 = q_ref[:, h * PEER_QDIM:h * PEER_QDIM + half]
        q2 = q_ref[:, h * PEER_QDIM + half:(h + 1) * PEER_QDIM]
        nt = (((1,), (1,)), ((), ()))
        s1 = lax.dot_general(k1_ref[h], q1.astype(BF16), nt, preferred_element_type=F32)
        s2 = lax.dot_general(k2_ref[h], q2.astype(BF16), nt, preferred_element_type=F32)
        v1, i1 = _top16_rows(s1, iota)
        v2, i2 = _top16_rows(s2, iota)
        pa, pb = _candidates(v1, v2, row8)
        cand = jnp.concatenate([x + y for x, y in zip(pa, pb)], axis=0)
        ea, eb = _candidates(i1, i2, row8)
        expert = jnp.concatenate([x * float(N_KEYS) + y for x, y in zip(ea, eb)], axis=0)
        scs, exs = [], []
        for _ in range(PEER_TOPK):
            m = jnp.max(cand, axis=0, keepdims=True)
            c = jnp.min(jnp.where(cand == m, flat, big), axis=0, keepdims=True)
            hit = flat == c
            exs.append(jnp.max(jnp.where(hit, expert, -1.0), axis=0, keepdims=True))
            scs.append(m)
            cand = jnp.where(hit, NEG_INF, cand)
        sc = jnp.concatenate(scs, axis=0)
        e = jnp.exp(sc - sc[0:1])
        gate_ref[h * PEER_TOPK:(h + 1) * PEER_TOPK, :] = e / jnp.sum(e, axis=0, keepdims=True)
        idx_ref[h * PEER_TOPK:(h + 1) * PEER_TOPK, :] = jnp.concatenate(exs, axis=0).astype(I32)


def _route(q, keys1, keys2):
    t = q.shape[0]
    tb = _row_tile(t, 256)
    kspec = pl.BlockSpec((PEER_HEADS, N_KEYS, PEER_QDIM // 2), lambda i: (0, 0, 0))
    return pl.pallas_call(
        functools.partial(_route_kernel, tb=tb), name="peer_route", grid=(t // tb,),
        out_shape=[jax.ShapeDtypeStruct((PEER_SLOTS, t), I32), jax.ShapeDtypeStruct((PEER_SLOTS, t), F32)],
        in_specs=[pl.BlockSpec((tb, PEER_HEADS * PEER_QDIM), lambda i: (i, 0)), kspec, kspec],
        out_specs=[pl.BlockSpec((PEER_SLOTS, tb), lambda i: (0, i)),
                   pl.BlockSpec((PEER_SLOTS, tb), lambda i: (0, i))],
        compiler_params=_cparams("parallel"))(q, keys1, keys2)


GATHER_SLOTS = 3
SC_WORKERS = 32
SC_GATHER_ROWS = 16
SC_BUFFERS = 6
STAGED_TOKENS_PER_STEP = 16


def _peer_token(packed, x, gate):
    u = lax.bitcast_convert_type(packed & jnp.uint32(0xFFFF0000), F32)
    v = lax.bitcast_convert_type(packed << 16, F32)
    p = u * x
    acc = p[:, 0:128]
    for j in range(1, D_MODEL // 128):
        acc = acc + p[:, j * 128:(j + 1) * 128]
    s = jnp.sum(acc, axis=-1, keepdims=True)
    act = 0.5 * s * (1.0 + lax.erf(s * (1.0 / math.sqrt(2.0))))
    return jnp.sum((gate * act) * v, axis=0, keepdims=True)


def _peer_apply_kernel(idx_ref, c_ref, h_ref, gt_ref, uv_ref, o_ref, buf, sem, *, tb):
    ahead = GATHER_SLOTS - 1

    def issue(t, slot):
        for k in range(PEER_SLOTS):
            e = idx_ref[k, t]
            pltpu.make_async_copy(uv_ref.at[e], buf.at[slot, pl.ds(k, 1), :],
                                  sem.at[slot]).start(priority=k % 2)

    def wait(slot):
        pltpu.make_async_copy(buf.at[slot], buf.at[slot], sem.at[slot]).wait()

    lane = lax.broadcasted_iota(I32, (PEER_SLOTS, tb), 1)

    def apply(t, slot):
        gate = jnp.sum(jnp.where(lane == t, gt_ref[...], 0.0), axis=-1, keepdims=True)
        y = _peer_token(buf[slot], c_ref[pl.ds(t, 1), :], gate)
        o_ref[pl.ds(t, 1), :] = h_ref[pl.ds(t, 1), :] + y

    for t0 in range(ahead):
        issue(t0, t0)

    def body(i, carry):
        for j in range(GATHER_SLOTS):
            t = i * GATHER_SLOTS + j
            issue(t + ahead, (j + ahead) % GATHER_SLOTS)
            wait(j)
            apply(t, j)
        return carry

    nmain = (tb - ahead) // GATHER_SLOTS
    lax.fori_loop(0, nmain, body, 0)
    for t in range(nmain * GATHER_SLOTS, tb):
        if t + ahead < tb:
            issue(t + ahead, (t + ahead) % GATHER_SLOTS)
        wait(t % GATHER_SLOTS)
        apply(t, t % GATHER_SLOTS)


def _pack_expert_rows(u, v):
    hi = lax.bitcast_convert_type(u.astype(BF16), jnp.uint16).astype(jnp.uint32) << 16
    lo = lax.bitcast_convert_type(v.astype(BF16), jnp.uint16).astype(jnp.uint32)
    return hi | lo


def _peer_apply_direct(idx_t, c, h, gate_t, uv_rows, t_direct):
    t = c.shape[0]
    tb = _row_tile(t_direct, 256)
    row = lambda i: (i, 0)
    return pl.pallas_call(
        functools.partial(_peer_apply_kernel, tb=tb), name="peer_apply", grid=(t_direct // tb,),
        out_shape=jax.ShapeDtypeStruct((t, D_MODEL), F32),
        in_specs=[pl.BlockSpec((PEER_SLOTS, tb), lambda i: (0, i), memory_space=pltpu.SMEM),
                  pl.BlockSpec((tb, D_MODEL), row), pl.BlockSpec((tb, D_MODEL), row),
                  pl.BlockSpec((PEER_SLOTS, tb), lambda i: (0, i)),
                  pl.BlockSpec(memory_space=pl.ANY)],
        out_specs=pl.BlockSpec((tb, D_MODEL), row),
        scratch_shapes=[pltpu.VMEM((GATHER_SLOTS, PEER_SLOTS, D_MODEL), jnp.uint32),
                        pltpu.SemaphoreType.DMA((GATHER_SLOTS,))],
        compiler_params=_cparams("arbitrary"))(idx_t, c, h, gate_t, uv_rows)


def _sc_stage_rows(table, idx):
    t_total, nchunk_tok, _ = idx.shape
    d = table.shape[1]
    tpw = t_total // SC_WORKERS
    assert t_total % SC_WORKERS == 0 and tpw >= 2
    nb = SC_BUFFERS
    ahead = nb // 2
    params = pltpu.CompilerParams()
    if "needs_layout_passes" in pltpu.CompilerParams.__dataclass_fields__:
        params = dataclasses.replace(params, needs_layout_passes=False)
    mesh = plsc.VectorSubcoreMesh(core_axis_name="c", subcore_axis_name="s")

    @functools.partial(
        pl.kernel, mesh=mesh, compiler_params=params,
        out_type=jax.ShapeDtypeStruct((t_total * nchunk_tok * SC_GATHER_ROWS, d), table.dtype),
        scratch_types=[pltpu.VMEM((2, nchunk_tok, SC_GATHER_ROWS), I32),
                       pltpu.VMEM((nb, SC_GATHER_ROWS, d), table.dtype),
                       pltpu.SemaphoreType.DMA((nb,)), pltpu.SemaphoreType.DMA((nb,)),
                       pltpu.SemaphoreType.DMA((2,))])
    def stage(table_hbm, idx_hbm, out_hbm, idx_v, rows_v, sem_g, sem_w, sem_t):
        wid = lax.axis_index("s") * 2 + lax.axis_index("c")
        base = wid * tpw
        nchunks = tpw * nchunk_tok

        def gather(g, slot):
            ts = (g // nchunk_tok) % 2
            return pltpu.make_async_copy(table_hbm.at[idx_v.at[ts, g % nchunk_tok]], rows_v.at[slot],
                                         sem_g.at[slot])

        def write(g, slot):
            dst = out_hbm.at[pl.ds((base * nchunk_tok + g) * SC_GATHER_ROWS, SC_GATHER_ROWS)]
            return pltpu.make_async_copy(rows_v.at[slot], dst, sem_w.at[slot])

        def idx_copy(i, ts):
            return pltpu.make_async_copy(idx_hbm.at[base + i], idx_v.at[ts], sem_t.at[ts])

        idx_copy(0, 0).start()
        idx_copy(0, 0).wait()
        idx_copy(1, 1).start()
        for g0 in range(ahead):
            gather(g0, g0).start()

        @pl.loop(0, nchunks)
        def _(g):
            slot = g % nb
            nxt = g + ahead
            nslot = nxt % nb

            @pl.when(nxt < nchunks)
            def _():
                @pl.when(nxt >= nb)
                def _():
                    write(nxt - nb, nslot).wait()

                @pl.when(nxt % nchunk_tok == 0)
                def _():
                    idx_copy(nxt // nchunk_tok, (nxt // nchunk_tok) % 2).wait()

                gather(nxt, nslot).start()

            gather(g, slot).wait()
            write(g, slot).start()

            @pl.when((g % nchunk_tok == nchunk_tok - 1) & (g // nchunk_tok + 2 < tpw))
            def _():
                idx_copy(g // nchunk_tok + 2, (g // nchunk_tok) % 2).start()

        for j in range(nb):
            gl = nchunks - nb + j
            write(gl, gl % nb).wait()

    return stage(table, idx)


def _peer_staged_kernel(st_ref, c_ref, h_ref, gt_ref, prev_ref, o_ref, *, tbs):
    del prev_ref
    i = pl.program_id(0)
    lane = lax.broadcasted_iota(I32, (PEER_SLOTS, 128), 1)
    for t in range(tbs):
        tg = i * tbs + t
        col0 = pl.multiple_of((tg // 128) * 128, 128)
        gate = jnp.sum(jnp.where(lane == tg % 128, gt_ref[:, pl.ds(col0, 128)], 0.0), axis=-1, keepdims=True)
        y = _peer_token(st_ref[t], c_ref[pl.ds(t, 1), :], gate)
        o_ref[pl.ds(t, 1), :] = h_ref[pl.ds(t, 1), :] + y


def _peer_apply_staged(staged, c, h, gate_t_staged, partial, t_direct):
    t_staged = staged.shape[0]
    tbs = STAGED_TOKENS_PER_STEP
    assert t_staged % tbs == 0 and t_direct % tbs == 0 and t_staged % 128 == 0
    off = t_direct // tbs
    row = lambda i: (i + off, 0)
    return pl.pallas_call(
        functools.partial(_peer_staged_kernel, tbs=tbs), name="peer_apply_staged", grid=(t_staged // tbs,),
        out_shape=jax.ShapeDtypeStruct(partial.shape, F32),
        in_specs=[pl.BlockSpec((tbs, PEER_SLOTS, D_MODEL), lambda i: (i, 0, 0)),
                  pl.BlockSpec((tbs, D_MODEL), row), pl.BlockSpec((tbs, D_MODEL), row),
                  pl.BlockSpec((PEER_SLOTS, t_staged), lambda i: (0, 0)),
                  pl.BlockSpec(memory_space=pl.ANY)],
        out_specs=pl.BlockSpec((tbs, D_MODEL), row),
        input_output_aliases={4: 0},
        compiler_params=_cparams("arbitrary"))(staged, c, h, gate_t_staged, partial)


def _peer_apply(idx_t, c, h, gate_t, uv, staged_pieces):
    t = c.shape[0]
    start = t - sum(staged_pieces)
    out = _peer_apply_direct(idx_t, c, h, gate_t, uv[:, None, :], start)
    for n in staged_pieces:
        idx_sc = idx_t[:, start:start + n].T.reshape(n, PEER_SLOTS // SC_GATHER_ROWS, SC_GATHER_ROWS)
        staged = _sc_stage_rows(uv, idx_sc).reshape(n, PEER_SLOTS, D_MODEL)
        out = _peer_apply_staged(staged, c, h, gate_t[:, start:start + n], out, start)
        start += n
    return out


def _final_kernel(h_ref, p_ref, gple_ref, wg_ref, wp_ref, gfin_ref, y_ref):
    h = h_ref[...]
    e = _rms(h, gple_ref[...])
    gate = jax.nn.sigmoid(jnp.dot(e.astype(BF16), wg_ref[...], preferred_element_type=F32))
    up = jnp.dot(p_ref[...].astype(BF16), wp_ref[...], preferred_element_type=F32)
    y_ref[...] = _rms(h + up * gate, gfin_ref[...])


def _final_into_kernel(h_ref, p_ref, gple_ref, wg_ref, wp_ref, gfin_ref, prev_ref, y_ref):
    del prev_ref
    _final_kernel(h_ref, p_ref, gple_ref, wg_ref, wp_ref, gfin_ref, y_ref)


def _final(h, p, g_ple, w_gate, w_ple, g_final, into=None, row0=0, total=None):
    t = h.shape[0]
    total = t if total is None else total
    tm = _row_tile(t, 512)
    assert row0 % tm == 0
    row = lambda i: (i, 0)
    out_row = lambda i: (i + row0 // tm, 0)
    fixed = lambda i: (0, 0)
    in_specs = [pl.BlockSpec((tm, D_MODEL), row), pl.BlockSpec((tm, PLE_DIM), row),
                pl.BlockSpec((1, D_MODEL), fixed), pl.BlockSpec((D_MODEL, D_MODEL), fixed),
                pl.BlockSpec((PLE_DIM, D_MODEL), fixed), pl.BlockSpec((1, D_MODEL), fixed)]
    args = [h, p, g_ple, w_gate, w_ple, g_final]
    body = _final_kernel
    aliases = {}
    if into is not None:
        in_specs.append(pl.BlockSpec(memory_space=pl.ANY))
        args.append(into)
        aliases = {len(args) - 1: 0}
        body = _final_into_kernel
    return pl.pallas_call(
        body, name="ple_final", grid=(t // tm,),
        out_shape=jax.ShapeDtypeStruct((total, D_MODEL), F32),
        in_specs=in_specs, out_specs=pl.BlockSpec((tm, D_MODEL), out_row),
        input_output_aliases=aliases,
        compiler_params=_cparams("parallel"))(*args)


STAGED_FRACTION = 0.75
STAGED_UNIT = SC_WORKERS * 128
STAGED_PIECE = 2 * STAGED_UNIT


def _staged_tokens(t):
    total = int(t * STAGED_FRACTION) // STAGED_UNIT * STAGED_UNIT
    if total < STAGED_PIECE:
        return ()
    pieces = [STAGED_PIECE] * (total // STAGED_PIECE)
    if total % STAGED_PIECE:
        pieces.append(total % STAGED_PIECE)
    return tuple(pieces)


def _layer_front(x, kv_buf, conv_prefix, s0, wts, batch, seq):
    t = batch * seq
    q, k, v, dqkv, z, ab = _inproj(x, wts["norm_mix"], wts["w_in"])
    if kv_buf is None:
        att = _swa_prompt(q, k, v, wts["sinks"], batch, seq)
        k_new = k.reshape(batch, seq, ATT_KV)[:, -WINDOW:]
        v_new = v.reshape(batch, seq, ATT_KV)[:, -WINDOW:]
    else:
        att, k_new, v_new = _swa_sample(q.reshape(batch, seq, ATT_Q), k.reshape(batch, seq, ATT_KV),
                                        v.reshape(batch, seq, ATT_KV), kv_buf[0], kv_buf[1], wts["sinks"])
        att = att.reshape(t, ATT_Q)
    dn, s_new = _gdn(dqkv, z, ab, conv_prefix, s0, wts["conv_w"], wts["a_log"], wts["dt_bias"],
                     wts["dn_norm"], batch, seq)
    conv_new = dqkv.reshape(batch, seq, DN_QKV)[:, -(CONV_W - 1):]
    h, c, pq = _outproj(x, att, dn, wts["w_out"], wts["norm_ffn"], wts["peer_wq"])
    idx_t, gate_t = _route(pq, wts["keys1"], wts["keys2"])
    return dict(h=h, c=c, idx_t=idx_t, gate_t=gate_t, state=(k_new, v_new, conv_new, s_new))


def _layer_back(front, p, wts, into=None, row0=0, total=None):
    t = front["c"].shape[0]
    h2 = _peer_apply(front["idx_t"], front["c"], front["h"], front["gate_t"], wts["uv"], _staged_tokens(t))
    return _final(h2, p, wts["norm_ple"], wts["ple_gate"], wts["ple_in"], wts["norm_final"], into, row0, total)


PROMPT_GROUPS = 2


def kernel(x_prompt, x_sample, p_prompt, p_sample, cache_swa_k, cache_swa_v, state_conv, state_delta, norm_mix, w_in, conv_w, attn_sinks, dn_a_log, dn_dt_bias, dn_norm, w_out, norm_ffn, peer_wq, peer_keys1, peer_keys2, peer_u, peer_v, norm_ple, ple_in, ple_gate, norm_final):
    depth = w_in.shape[0]
    assert depth == 1
    bp, lp = x_prompt.shape[0], x_prompt.shape[1]
    bs, ls = x_sample.shape[0], x_sample.shape[1]
    l = 0
    wts = dict(
        norm_mix=norm_mix[l].reshape(1, D_MODEL),
        w_in=jnp.pad(w_in[l], ((0, 0), (0, IN_WIDTH_PADDED - IN_WIDTH))).astype(BF16),
        conv_w=conv_w[l], sinks=attn_sinks[l], a_log=dn_a_log[l], dt_bias=dn_dt_bias[l], dn_norm=dn_norm[l],
        w_out=w_out[l].astype(BF16), norm_ffn=norm_ffn[l].reshape(1, D_MODEL), peer_wq=peer_wq[l].astype(BF16),
        keys1=peer_keys1[l].astype(BF16), keys2=peer_keys2[l].astype(BF16),
        uv=_pack_expert_rows(peer_u[l], peer_v[l]),
        norm_ple=norm_ple[l].reshape(1, D_MODEL), ple_in=ple_in[l].astype(BF16),
        ple_gate=ple_gate[l].astype(BF16), norm_final=norm_final.reshape(1, D_MODEL))
    ngrp = PROMPT_GROUPS if bp % PROMPT_GROUPS == 0 else 1
    bg = bp // ngrp
    fronts = []
    for gi in range(ngrp):
        xg = x_prompt[gi * bg:(gi + 1) * bg].reshape(bg * lp, D_MODEL)
        fronts.append(_layer_front(xg, None, jnp.zeros((bg, CONV_W - 1, DN_QKV), F32),
                                   jnp.zeros((bg, DN_HEADS, HEAD_DIM, HEAD_DIM), F32), wts, bg, lp))
    kv_buf = (cache_swa_k[l].reshape(bs, WINDOW, ATT_KV), cache_swa_v[l].reshape(bs, WINDOW, ATT_KV))
    front_s = _layer_front(x_sample.reshape(bs * ls, D_MODEL), kv_buf, state_conv[l], state_delta[l], wts, bs, ls)
    yp = None
    for gi in range(ngrp):
        pg = p_prompt[l, gi * bg:(gi + 1) * bg].reshape(bg * lp, PLE_DIM)
        yp = _layer_back(fronts[gi], pg, wts, yp, gi * bg * lp, bp * lp)
    ys = _layer_back(front_s, p_sample[l].reshape(bs * ls, PLE_DIM), wts)
    kp, vp, cp, sp = (jnp.concatenate([f["state"][j] for f in fronts], axis=0) for j in range(4))
    kn, vn, cn, sn = front_s["state"]
    kvshape = lambda b: (1, b, WINDOW, ATT_KV_HEADS, HEAD_DIM)
    return (yp.reshape(bp, lp, D_MODEL), ys.reshape(bs, ls, D_MODEL),
            kp.reshape(kvshape(bp)), vp.reshape(kvshape(bp)), cp[None], sp[None],
            kn.reshape(kvshape(bs)), vn.reshape(kvshape(bs)), cn[None], sn[None])
```

```python
import dataclasses
import functools
import math

import jax
import jax.numpy as jnp
from jax import lax
from jax.experimental import pallas as pl
from jax.experimental.pallas import tpu as pltpu
from jax.experimental.pallas import tpu_sc as plsc

F32 = jnp.float32
BF16 = jnp.bfloat16
I32 = jnp.int32
HIGHEST = lax.Precision.HIGHEST

D_MODEL = 1024
HEAD_DIM = 64
ATT_HEADS = 8
ATT_KV_HEADS = 2
ATT_GROUP = 4
WINDOW = 128
ATT_Q = ATT_HEADS * HEAD_DIM
ATT_KV = ATT_KV_HEADS * HEAD_DIM
DN_HEADS = 8
DN_WIDTH = DN_HEADS * HEAD_DIM
DN_QKV = 3 * DN_WIDTH
CONV_W = 4
DN_CHUNK = 64
IN_WIDTH = ATT_Q + 2 * ATT_KV + DN_QKV + DN_WIDTH + 2 * DN_HEADS
IN_WIDTH_PADDED = 2944
N_KEYS = 128
PEER_HEADS = 8
PEER_QDIM = 256
PEER_TOPK = 16
PEER_SLOTS = PEER_HEADS * PEER_TOPK
PLE_DIM = 256
EPS = 1e-6
NEG_INF = float("-inf")
INDEX_SENTINEL = 1.0e9

VMEM_LIMIT = 48 * 1024 * 1024


def _cparams(*sem):
    return pltpu.CompilerParams(dimension_semantics=sem, vmem_limit_bytes=VMEM_LIMIT)


def _rms(x, g):
    return x * lax.rsqrt(jnp.mean(x * x, axis=-1, keepdims=True) + EPS) * g


def _bdot(a, b):
    return jnp.dot(a.astype(BF16), b.astype(BF16), preferred_element_type=F32)


def _fdot(a, b):
    return jnp.dot(a, b, precision=HIGHEST, preferred_element_type=F32)


_NN = (((1,), (0,)), ((), ()))
_NT = (((1,), (1,)), ((), ()))
_TN = (((0,), (0,)), ((), ()))


def _row_tile(t, target):
    tile = min(t, target)
    assert t % tile == 0, (t, tile)
    return tile


def _inproj_kernel(x_ref, g_ref, w_ref, q_ref, k_ref, v_ref, dqkv_ref, z_ref, ab_ref):
    a = _rms(x_ref[...], g_ref[...])
    p = jnp.dot(a.astype(BF16), w_ref[...], preferred_element_type=F32)
    c0, c1, c2, c3, c4 = ATT_Q, ATT_Q + ATT_KV, ATT_Q + 2 * ATT_KV, ATT_Q + 2 * ATT_KV + DN_QKV, \
        ATT_Q + 2 * ATT_KV + DN_QKV + DN_WIDTH
    q_ref[...] = p[:, :c0]
    k_ref[...] = p[:, c0:c1]
    v_ref[...] = p[:, c1:c2]
    dqkv_ref[...] = p[:, c2:c3]
    z_ref[...] = p[:, c3:c4]
    ab_ref[...] = p[:, c4:c4 + 2 * DN_HEADS]


def _inproj(x, g, w_pad):
    t = x.shape[0]
    tm = _row_tile(t, 512)
    widths = (ATT_Q, ATT_KV, ATT_KV, DN_QKV, DN_WIDTH, 2 * DN_HEADS)
    row = lambda i: (i, 0)
    fixed = lambda i: (0, 0)
    return pl.pallas_call(
        _inproj_kernel, name="inproj", grid=(t // tm,),
        out_shape=[jax.ShapeDtypeStruct((t, w), F32) for w in widths],
        in_specs=[pl.BlockSpec((tm, D_MODEL), row), pl.BlockSpec((1, D_MODEL), fixed),
                  pl.BlockSpec((D_MODEL, IN_WIDTH_PADDED), fixed)],
        out_specs=[pl.BlockSpec((tm, w), row) for w in widths],
        compiler_params=_cparams("parallel"))(x, g, w_pad)


def _softmax_sink_pv(s, sink, vband):
    m = jnp.maximum(jnp.max(s, axis=-1, keepdims=True), sink)
    e = jnp.exp(s - m)
    denom = jnp.sum(e, axis=-1, keepdims=True) + jnp.exp(sink - m)
    return _bdot(e / denom, vband)


def _alibi_slope(h):
    return 2.0 ** (-8.0 * (h + 1) / ATT_HEADS)


def _group_columns(kh, rows, sink_ref):
    rg = lax.broadcasted_iota(I32, (ATT_GROUP * rows, 1), 0) // rows
    slope = jnp.full((ATT_GROUP * rows, 1), _alibi_slope(kh * ATT_GROUP), F32)
    sink = jnp.full((ATT_GROUP * rows, 1), sink_ref[kh * ATT_GROUP], F32)
    for j in range(1, ATT_GROUP):
        slope = jnp.where(rg == j, _alibi_slope(kh * ATT_GROUP + j), slope)
        sink = jnp.where(rg == j, sink_ref[kh * ATT_GROUP + j], sink)
    return slope, sink


def _attend_kv_head(q_all, kband, vband, kh, dist, valid, sink_ref):
    rows = q_all.shape[0]
    heads = range(kh * ATT_GROUP, (kh + 1) * ATT_GROUP)
    q4 = jnp.concatenate([q_all[:, h * HEAD_DIM:(h + 1) * HEAD_DIM] for h in heads], axis=0)
    k = kband[:, kh * HEAD_DIM:(kh + 1) * HEAD_DIM]
    v = vband[:, kh * HEAD_DIM:(kh + 1) * HEAD_DIM]
    s = lax.dot_general(q4.astype(BF16), k.astype(BF16), _NT, preferred_element_type=F32) * (HEAD_DIM ** -0.5)
    slope, sink = _group_columns(kh, rows, sink_ref)
    s = jnp.where(valid, s - slope * dist, NEG_INF)
    o4 = _softmax_sink_pv(s, sink, v)
    return [o4[j * rows:(j + 1) * rows] for j in range(ATT_GROUP)]


def _swa_prompt_kernel(sink_ref, q_ref, kc_ref, kp_ref, vc_ref, vp_ref, o_ref):
    n = pl.program_id(1)
    rows = ATT_GROUP * WINDOW
    qi = lax.broadcasted_iota(I32, (rows, 2 * WINDOW), 0) % WINDOW
    kj = lax.broadcasted_iota(I32, (rows, 2 * WINDOW), 1)
    dist = WINDOW + qi - kj
    kpos = n * WINDOW - WINDOW + kj
    valid = (dist >= 0) & (dist <= WINDOW) & (kpos >= 0)
    distf = dist.astype(F32)
    kband = jnp.concatenate([kp_ref[...], kc_ref[...]], axis=0)
    vband = jnp.concatenate([vp_ref[...], vc_ref[...]], axis=0)
    q_all = q_ref[...]
    outs = []
    for kh in range(ATT_KV_HEADS):
        outs += _attend_kv_head(q_all, kband, vband, kh, distf, valid, sink_ref)
    o_ref[...] = jnp.concatenate(outs, axis=-1)


def _swa_prompt(q, k, v, sinks, batch, seq):
    nblk = seq // WINDOW
    cur = lambda b, n: (b * nblk + n, 0)
    prev = lambda b, n: (b * nblk + jnp.maximum(n - 1, 0), 0)
    return pl.pallas_call(
        _swa_prompt_kernel, name="swa_prompt", grid=(batch, nblk),
        out_shape=jax.ShapeDtypeStruct((batch * seq, ATT_Q), F32),
        in_specs=[pl.BlockSpec(memory_space=pltpu.SMEM),
                  pl.BlockSpec((WINDOW, ATT_Q), cur),
                  pl.BlockSpec((WINDOW, ATT_KV), cur), pl.BlockSpec((WINDOW, ATT_KV), prev),
                  pl.BlockSpec((WINDOW, ATT_KV), cur), pl.BlockSpec((WINDOW, ATT_KV), prev)],
        out_specs=pl.BlockSpec((WINDOW, ATT_Q), cur),
        compiler_params=_cparams("parallel", "parallel"))(sinks, q, k, k, v, v)


def _swa_sample_kernel(sink_ref, q_ref, k_ref, v_ref, kb_ref, vb_ref, o_ref, ko_ref, vo_ref, *, bt, seq):
    klen = WINDOW + seq
    rows = ATT_GROUP * seq
    qi = lax.broadcasted_iota(I32, (rows, klen), 0) % seq
    kj = lax.broadcasted_iota(I32, (rows, klen), 1)
    dist = qi - (kj - WINDOW)
    valid = (dist >= 0) & (dist <= WINDOW)
    distf = dist.astype(F32)
    for b in range(bt):
        kc = jnp.concatenate([kb_ref[b], k_ref[b]], axis=0)
        vc = jnp.concatenate([vb_ref[b], v_ref[b]], axis=0)
        ko_ref[b] = kc[seq:]
        vo_ref[b] = vc[seq:]
        q_all = q_ref[b]
        outs = []
        for kh in range(ATT_KV_HEADS):
            outs += _attend_kv_head(q_all, kc, vc, kh, distf, valid, sink_ref)
        o_ref[b] = jnp.concatenate(outs, axis=-1)


def _swa_sample(q, k, v, k_buf, v_buf, sinks):
    batch, seq = q.shape[0], q.shape[1]
    bt = _row_tile(batch, 8)
    blk = lambda w, r: pl.BlockSpec((bt, r, w), lambda i: (i, 0, 0))
    return pl.pallas_call(
        functools.partial(_swa_sample_kernel, bt=bt, seq=seq), name="swa_sample", grid=(batch // bt,),
        out_shape=[jax.ShapeDtypeStruct((batch, seq, ATT_Q), F32),
                   jax.ShapeDtypeStruct((batch, WINDOW, ATT_KV), F32),
                   jax.ShapeDtypeStruct((batch, WINDOW, ATT_KV), F32)],
        in_specs=[pl.BlockSpec(memory_space=pltpu.SMEM), blk(ATT_Q, seq), blk(ATT_KV, seq), blk(ATT_KV, seq),
                  blk(ATT_KV, WINDOW), blk(ATT_KV, WINDOW)],
        out_specs=[blk(ATT_Q, seq), blk(ATT_KV, WINDOW), blk(ATT_KV, WINDOW)],
        compiler_params=_cparams("parallel"))(sinks, q, k, v, k_buf, v_buf)


def _split_bf16(a, cache):
    if id(a) not in cache:
        hi = a.astype(BF16)
        cache[id(a)] = (a, (hi, (a - hi.astype(F32)).astype(BF16)))
    return cache[id(a)][1]


MXU_DEPTH = 256


def _dot3_impl(a, b, dims=_NN, *, cache):
    (ca,), (cb,) = dims[0]
    depth = a.shape[ca]
    ah, al = _split_bf16(a, cache)
    bh, bl = _split_bf16(b, cache)
    f = lambda x, y: lax.dot_general(x, y, dims, preferred_element_type=F32)
    if 3 * depth <= MXU_DEPTH:
        return f(jnp.concatenate([ah, al, ah], axis=ca), jnp.concatenate([bh, bh, bl], axis=cb))
    return f(jnp.concatenate([ah, al], axis=ca), jnp.concatenate([bh, bh], axis=cb)) + f(ah, bl)


def _gdn_kernel(alog_ref, dtb_ref, dqkv_ref, z_ref, ab_ref, cw_ref, nrm_ref, pre_ref, s0_ref,
                o_ref, sfin_ref, tail_ref, state_ref, *, c, g, nsub):
    n = pl.program_id(1)
    ngrp = DN_HEADS // g
    r = g * c
    rows = nsub * c
    hd = HEAD_DIM
    _dot3 = functools.partial(_dot3_impl, cache={})

    @pl.when(n == 0)
    def _():
        tail_ref[...] = jnp.concatenate([jnp.zeros((8 - (CONV_W - 1), DN_QKV), F32), pre_ref[0]], axis=0)
        for gi in range(ngrp):
            state_ref[gi] = jnp.concatenate([s0_ref[0, gi * g + j] for j in range(g)], axis=1)

    x = dqkv_ref[...]
    hist = jnp.concatenate([tail_ref[...], x], axis=0)
    tail_ref[...] = hist[rows:]
    cw = cw_ref[...]
    y = x * cw[CONV_W - 1:CONV_W]
    for d in range(1, CONV_W):
        y = y + hist[8 - d:8 - d + rows] * cw[CONV_W - 1 - d:CONV_W - d]
    qkv = y * jax.nn.sigmoid(y)
    z = z_ref[...]
    ab = ab_ref[...]
    beta_all = jax.nn.sigmoid(ab[:, DN_HEADS:2 * DN_HEADS])
    g_all = -jnp.exp(alog_ref[...]) * jax.nn.softplus(ab[:, 0:DN_HEADS] + dtb_ref[...])

    ri = lax.broadcasted_iota(I32, (r, r), 0)
    ci = lax.broadcasted_iota(I32, (r, r), 1)
    same = (ri // c) == (ci // c)
    lower = same & (ri >= ci)
    strict = same & (ri > ci)
    tri = lower.astype(F32)
    blk = same.astype(F32)
    eye = (ri == ci).astype(F32)
    rowhead = lax.broadcasted_iota(I32, (r, hd), 0) // c

    def stack(a, sub, gi, base):
        return jnp.concatenate([a[sub * c:(sub + 1) * c, base + (gi * g + j) * hd:base + (gi * g + j + 1) * hd]
                                for j in range(g)], axis=0)

    def column(a, sub, gi):
        return jnp.concatenate([a[sub * c:(sub + 1) * c, gi * g + j:gi * g + j + 1] for j in range(g)], axis=0)

    def own_block(m):
        acc = jnp.where(rowhead == 0, m[:, 0:hd], 0.0)
        for j in range(1, g):
            acc = acc + jnp.where(rowhead == j, m[:, j * hd:(j + 1) * hd], 0.0)
        return acc

    keys = [(sub, gi) for sub in range(nsub) for gi in range(ngrp)]
    grp = {}
    for sub, gi in keys:
        q = stack(qkv, sub, gi, 0)
        k = stack(qkv, sub, gi, DN_WIDTH)
        v = stack(qkv, sub, gi, 2 * DN_WIDTH)
        q = q * lax.rsqrt(jnp.sum(q * q, axis=-1, keepdims=True) + EPS) * (hd ** -0.5)
        k = k * lax.rsqrt(jnp.sum(k * k, axis=-1, keepdims=True) + EPS)
        beta = column(beta_all, sub, gi)
        gr = column(g_all, sub, gi)
        gcol = _fdot(tri, gr)
        grow = lax.dot_general(gr, tri, (((0,), (1,)), ((), ())), precision=HIGHEST,
                               preferred_element_type=F32)
        glast = _fdot(blk, gr)
        decay = jnp.exp(jnp.where(lower, gcol - grow, NEG_INF))
        kb = k * beta
        lm = jnp.where(strict, _dot3(kb, k, _NT) * decay, 0.0)
        grp[sub, gi] = dict(q=q, k=k, v=v, beta=beta, gcol=gcol, glast=glast, decay=decay, kb=kb, lm=lm)

    pair = ((ri // 2) == (ci // 2)) & (ri % 2 == 1) & (ci % 2 == 0)
    invs = {key: eye - jnp.where(pair, grp[key]["lm"], 0.0) for key in keys}
    s = 2
    while s < c:
        off = ((ri // (2 * s)) == (ci // (2 * s))) & ((ri // s) % 2 == 1) & ((ci // s) % 2 == 0)
        tmp = {key: _dot3(invs[key], jnp.where(off, grp[key]["lm"], 0.0)) for key in keys}
        invs = {key: invs[key] - _dot3(tmp[key], invs[key]) for key in keys}
        s *= 2

    for sub in range(nsub):
        outs = [None] * DN_HEADS
        for gi in range(ngrp):
            d = grp[sub, gi]
            eg = jnp.exp(d["gcol"])
            sol = _dot3(invs[sub, gi], jnp.concatenate([d["v"] * d["beta"], d["kb"] * eg], axis=1))
            u = sol[:, :hd]
            w = sol[:, hd:]
            qk = _dot3(d["q"], d["k"], _NT) * d["decay"]
            ktail = d["k"] * jnp.exp(d["glast"] - d["gcol"])
            qhead = d["q"] * eg
            st = state_ref[gi]
            vnew = u - own_block(_dot3(w, st))
            o = own_block(_dot3(qhead, st)) + _dot3(qk, vnew)
            vblk = jnp.concatenate([jnp.where(rowhead == j, vnew, 0.0) for j in range(g)], axis=1)
            elast = jnp.exp(d["glast"])
            keep = jnp.concatenate([jnp.broadcast_to(elast[j * c:j * c + 1, :], (1, hd)) for j in range(g)],
                                   axis=1)
            state_ref[gi] = st * keep + _dot3(ktail, vblk, _TN)
            zs = stack(z, sub, gi, 0)
            o = _rms(o, nrm_ref[...]) * (zs * jax.nn.sigmoid(zs))
            for j in range(g):
                outs[gi * g + j] = o[j * c:(j + 1) * c]
        o_ref[sub * c:(sub + 1) * c, :] = jnp.concatenate(outs, axis=-1)

    @pl.when(n == pl.num_programs(1) - 1)
    def _():
        for gi in range(ngrp):
            for j in range(g):
                sfin_ref[0, gi * g + j] = state_ref[gi][:, j * hd:(j + 1) * hd]


GDN_CHUNKS_PER_STEP = 2


def _gdn(dqkv, z, ab, conv_prefix, s0, conv_w, a_log, dt_bias, dn_norm, batch, seq):
    c = min(DN_CHUNK, seq)
    nchunk = seq // c
    assert seq % c == 0 and c % 8 == 0
    g = min(DN_HEADS, max(1, 128 // c))
    nsub = GDN_CHUNKS_PER_STEP if nchunk % GDN_CHUNKS_PER_STEP == 0 else 1
    nstep = nchunk // nsub
    rows = nsub * c
    tok = lambda b, n: (b * nstep + n, 0)
    fixed = lambda b, n: (0, 0)
    return pl.pallas_call(
        functools.partial(_gdn_kernel, c=c, g=g, nsub=nsub), name="gdn", grid=(batch, nstep),
        out_shape=[jax.ShapeDtypeStruct((batch * seq, DN_WIDTH), F32),
                   jax.ShapeDtypeStruct((batch, DN_HEADS, HEAD_DIM, HEAD_DIM), F32)],
        in_specs=[pl.BlockSpec((1, DN_HEADS), fixed), pl.BlockSpec((1, DN_HEADS), fixed),
                  pl.BlockSpec((rows, DN_QKV), tok), pl.BlockSpec((rows, DN_WIDTH), tok),
                  pl.BlockSpec((rows, 2 * DN_HEADS), tok),
                  pl.BlockSpec((CONV_W, DN_QKV), fixed), pl.BlockSpec((1, HEAD_DIM), fixed),
                  pl.BlockSpec((1, CONV_W - 1, DN_QKV), lambda b, n: (b, 0, 0)),
                  pl.BlockSpec((1, DN_HEADS, HEAD_DIM, HEAD_DIM), lambda b, n: (b, 0, 0, 0))],
        out_specs=[pl.BlockSpec((rows, DN_WIDTH), tok),
                   pl.BlockSpec((1, DN_HEADS, HEAD_DIM, HEAD_DIM), lambda b, n: (b, 0, 0, 0))],
        scratch_shapes=[pltpu.VMEM((8, DN_QKV), F32),
                        pltpu.VMEM((DN_HEADS // g, HEAD_DIM, g * HEAD_DIM), F32)],
        compiler_params=_cparams("parallel", "arbitrary"))(
            a_log.reshape(1, DN_HEADS), dt_bias.reshape(1, DN_HEADS), dqkv, z, ab, conv_w,
            dn_norm.reshape(1, HEAD_DIM), conv_prefix, s0)


def _outproj_kernel(x_ref, att_ref, dn_ref, wo_ref, g_ref, wq_ref, h_ref, c_ref, q_ref):
    mix = jnp.concatenate([att_ref[...], dn_ref[...]], axis=-1)
    h = x_ref[...] + jnp.dot(mix.astype(BF16), wo_ref[...], preferred_element_type=F32)
    h_ref[...] = h
    cn = _rms(h, g_ref[...])
    c_ref[...] = cn
    q_ref[...] = jnp.dot(cn.astype(BF16), wq_ref[...], preferred_element_type=F32)


def _outproj(x, att, dn, w_out, g_ffn, wq):
    t = x.shape[0]
    tm = _row_tile(t, 512)
    row = lambda i: (i, 0)
    fixed = lambda i: (0, 0)
    qw = PEER_HEADS * PEER_QDIM
    return pl.pallas_call(
        _outproj_kernel, name="outproj", grid=(t // tm,),
        out_shape=[jax.ShapeDtypeStruct((t, D_MODEL), F32), jax.ShapeDtypeStruct((t, D_MODEL), F32),
                   jax.ShapeDtypeStruct((t, qw), F32)],
        in_specs=[pl.BlockSpec((tm, D_MODEL), row), pl.BlockSpec((tm, ATT_Q), row),
                  pl.BlockSpec((tm, DN_WIDTH), row), pl.BlockSpec((D_MODEL, D_MODEL), fixed),
                  pl.BlockSpec((1, D_MODEL), fixed), pl.BlockSpec((D_MODEL, qw), fixed)],
        out_specs=[pl.BlockSpec((tm, D_MODEL), row), pl.BlockSpec((tm, D_MODEL), row),
                   pl.BlockSpec((tm, qw), row)],
        compiler_params=_cparams("parallel"))(x, att, dn, w_out, g_ffn, wq)


def _top16_rows(s, iota):
    big = jnp.float32(INDEX_SENTINEL)
    vals, idxs = [], []
    for _ in range(PEER_TOPK):
        m = jnp.max(s, axis=0, keepdims=True)
        i = jnp.min(jnp.where(s == m, iota, big), axis=0, keepdims=True)
        vals.append(m)
        idxs.append(i)
        s = jnp.where(iota == i, NEG_INF, s)
    return jnp.concatenate(vals, axis=0), jnp.concatenate(idxs, axis=0)


def _candidates(a, b, row8):
    t = a.shape[1]
    lo = row8 < 4
    bc = lambda r, n: jnp.broadcast_to(a[r:r + 1], (n, t))
    b8 = b[0:8]
    b44 = jnp.where(lo, b8, pltpu.roll(b8, 4, 0))
    parts = [bc(0, 8), bc(0, 8), bc(1, 8), bc(2, 8), bc(3, 8),
             jnp.where(lo, bc(4, 8), bc(5, 8)), jnp.where(lo, bc(6, 8), bc(7, 8)), a[8:16]]
    others = [b[0:8], b[8:16], b8, b8, b8, b44, b44, jnp.broadcast_to(b[0:1], (8, t))]
    return parts, others


def _flat_index_column():
    r = lax.broadcasted_iota(I32, (8, 1), 0)
    lo = r < 4
    cols = [r, 8 + r, 16 + r, 32 + r, 48 + r, jnp.where(lo, 64 + r, 80 + r - 4),
            jnp.where(lo, 96 + r, 112 + r - 4), (8 + r) * 16]
    return jnp.concatenate(cols, axis=0).astype(F32)


def _route_kernel(q_ref, k1_ref, k2_ref, idx_ref, gate_ref, *, tb):
    iota = lax.broadcasted_iota(I32, (N_KEYS, tb), 0).astype(F32)
    row8 = lax.broadcasted_iota(I32, (8, tb), 0)
    flat = jnp.broadcast_to(_flat_index_column(), (4 * PEER_TOPK, tb))
    big = jnp.float32(INDEX_SENTINEL)
    half = PEER_QDIM // 2
    for h in range(PEER_HEADS):
        q1 = q_ref[:, h * PEER_QDIM:h * PEER_QDIM + half]
        q2 = q_ref[:, h * PEER_QDIM + half:(h + 1) * PEER_QDIM]
        nt = (((1,), (1,)), ((), ()))
        s1 = lax.dot_general(k1_ref[h], q1.astype(BF16), nt, preferred_element_type=F32)
        s2 = lax.dot_general(k2_ref[h], q2.astype(BF16), nt, preferred_element_type=F32)
        v1, i1 = _top16_rows(s1, iota)
        v2, i2 = _top16_rows(s2, iota)
        pa, pb = _candidates(v1, v2, row8)
        cand = jnp.concatenate([x + y for x, y in zip(pa, pb)], axis=0)
        ea, eb = _candidates(i1, i2, row8)
        expert = jnp.concatenate([x * float(N_KEYS) + y for x, y in zip(ea, eb)], axis=0)
        scs, exs = [], []
        for _ in range(PEER_TOPK):
            m = jnp.max(cand, axis=0, keepdims=True)
            c = jnp.min(jnp.where(cand == m, flat, big), axis=0, keepdims=True)
            hit = flat == c
            exs.append(jnp.max(jnp.where(hit, expert, -1.0), axis=0, keepdims=True))
            scs.append(m)
            cand = jnp.where(hit, NEG_INF, cand)
        sc = jnp.concatenate(scs, axis=0)
        e = jnp.exp(sc - sc[0:1])
        gate_ref[h * PEER_TOPK:(h + 1) * PEER_TOPK, :] = e / jnp.sum(e, axis=0, keepdims=True)
        idx_ref[h * PEER_TOPK:(h + 1) * PEER_TOPK, :] = jnp.concatenate(exs, axis=0).astype(I32)


def _route(q, keys1, keys2):
    t = q.shape[0]
    tb = _row_tile(t, 256)
    kspec = pl.BlockSpec((PEER_HEADS, N_KEYS, PEER_QDIM // 2), lambda i: (0, 0, 0))
    return pl.pallas_call(
        functools.partial(_route_kernel, tb=tb), name="peer_route", grid=(t // tb,),
        out_shape=[jax.ShapeDtypeStruct((PEER_SLOTS, t), I32), jax.ShapeDtypeStruct((PEER_SLOTS, t), F32)],
        in_specs=[pl.BlockSpec((tb, PEER_HEADS * PEER_QDIM), lambda i: (i, 0)), kspec, kspec],
        out_specs=[pl.BlockSpec((PEER_SLOTS, tb), lambda i: (0, i)),
                   pl.BlockSpec((PEER_SLOTS, tb), lambda i: (0, i))],
        compiler_params=_cparams("parallel"))(q, keys1, keys2)


GATHER_SLOTS = 3
SC_WORKERS = 32
SC_GATHER_ROWS = 16
SC_BUFFERS = 6
STAGED_TOKENS_PER_STEP = 16


def _peer_token(packed, x, gate):
    u = lax.bitcast_convert_type(packed & jnp.uint32(0xFFFF0000), F32)
    v = lax.bitcast_convert_type(packed << 16, F32)
    p = u * x
    acc = p[:, 0:128]
    for j in range(1, D_MODEL // 128):
        acc = acc + p[:, j * 128:(j + 1) * 128]
    s = jnp.sum(acc, axis=-1, keepdims=True)
    act = 0.5 * s * (1.0 + lax.erf(s * (1.0 / math.sqrt(2.0))))
    return jnp.sum((gate * act) * v, axis=0, keepdims=True)


def _peer_apply_kernel(idx_ref, c_ref, h_ref, gt_ref, uv_ref, o_ref, buf, sem, *, tb):
    ahead = GATHER_SLOTS - 1

    def issue(t, slot):
        for k in range(PEER_SLOTS):
            e = idx_ref[k, t]
            pltpu.make_async_copy(uv_ref.at[e], buf.at[slot, pl.ds(k, 1), :],
                                  sem.at[slot]).start(priority=k % 2)

    def wait(slot):
        pltpu.make_async_copy(buf.at[slot], buf.at[slot], sem.at[slot]).wait()

    lane = lax.broadcasted_iota(I32, (PEER_SLOTS, tb), 1)

    def apply(t, slot):
        gate = jnp.sum(jnp.where(lane == t, gt_ref[...], 0.0), axis=-1, keepdims=True)
        y = _peer_token(buf[slot], c_ref[pl.ds(t, 1), :], gate)
        o_ref[pl.ds(t, 1), :] = h_ref[pl.ds(t, 1), :] + y

    for t0 in range(ahead):
        issue(t0, t0)

    def body(i, carry):
        for j in range(GATHER_SLOTS):
            t = i * GATHER_SLOTS + j
            issue(t + ahead, (j + ahead) % GATHER_SLOTS)
            wait(j)
            apply(t, j)
        return carry

    nmain = (tb - ahead) // GATHER_SLOTS
    lax.fori_loop(0, nmain, body, 0)
    for t in range(nmain * GATHER_SLOTS, tb):
        if t + ahead < tb:
            issue(t + ahead, (t + ahead) % GATHER_SLOTS)
        wait(t % GATHER_SLOTS)
        apply(t, t % GATHER_SLOTS)


def _pack_expert_rows(u, v):
    hi = lax.bitcast_convert_type(u.astype(BF16), jnp.uint16).astype(jnp.uint32) << 16
    lo = lax.bitcast_convert_type(v.astype(BF16), jnp.uint16).astype(jnp.uint32)
    return hi | lo


def _peer_apply_direct(idx_t, c, h, gate_t, uv_rows, t_direct):
    t = c.shape[0]
    tb = _row_tile(t_direct, 256)
    row = lambda i: (i, 0)
    return pl.pallas_call(
        functools.partial(_peer_apply_kernel, tb=tb), name="peer_apply", grid=(t_direct // tb,),
        out_shape=jax.ShapeDtypeStruct((t, D_MODEL), F32),
        in_specs=[pl.BlockSpec((PEER_SLOTS, tb), lambda i: (0, i), memory_space=pltpu.SMEM),
                  pl.BlockSpec((tb, D_MODEL), row), pl.BlockSpec((tb, D_MODEL), row),
                  pl.BlockSpec((PEER_SLOTS, tb), lambda i: (0, i)),
                  pl.BlockSpec(memory_space=pl.ANY)],
        out_specs=pl.BlockSpec((tb, D_MODEL), row),
        scratch_shapes=[pltpu.VMEM((GATHER_SLOTS, PEER_SLOTS, D_MODEL), jnp.uint32),
                        pltpu.SemaphoreType.DMA((GATHER_SLOTS,))],
        compiler_params=_cparams("arbitrary"))(idx_t, c, h, gate_t, uv_rows)


def _sc_stage_rows(table, idx):
    t_total, nchunk_tok, _ = idx.shape
    d = table.shape[1]
    tpw = t_total // SC_WORKERS
    assert t_total % SC_WORKERS == 0 and tpw >= 2
    nb = SC_BUFFERS
    ahead = nb // 2
    params = pltpu.CompilerParams()
    if "needs_layout_passes" in pltpu.CompilerParams.__dataclass_fields__:
        params = dataclasses.replace(params, needs_layout_passes=False)
    mesh = plsc.VectorSubcoreMesh(core_axis_name="c", subcore_axis_name="s")

    @functools.partial(
        pl.kernel, mesh=mesh, compiler_params=params,
        out_type=jax.ShapeDtypeStruct((t_total * nchunk_tok * SC_GATHER_ROWS, d), table.dtype),
        scratch_types=[pltpu.VMEM((2, nchunk_tok, SC_GATHER_ROWS), I32),
                       pltpu.VMEM((nb, SC_GATHER_ROWS, d), table.dtype),
                       pltpu.SemaphoreType.DMA((nb,)), pltpu.SemaphoreType.DMA((nb,)),
                       pltpu.SemaphoreType.DMA((2,))])
    def stage(table_hbm, idx_hbm, out_hbm, idx_v, rows_v, sem_g, sem_w, sem_t):
        wid = lax.axis_index("s") * 2 + lax.axis_index("c")
        base = wid * tpw
        nchunks = tpw * nchunk_tok

        def gather(g, slot):
            ts = (g // nchunk_tok) % 2
            return pltpu.make_async_copy(table_hbm.at[idx_v.at[ts, g % nchunk_tok]], rows_v.at[slot],
                                         sem_g.at[slot])

        def write(g, slot):
            dst = out_hbm.at[pl.ds((base * nchunk_tok + g) * SC_GATHER_ROWS, SC_GATHER_ROWS)]
            return pltpu.make_async_copy(rows_v.at[slot], dst, sem_w.at[slot])

        def idx_copy(i, ts):
            return pltpu.make_async_copy(idx_hbm.at[base + i], idx_v.at[ts], sem_t.at[ts])

        idx_copy(0, 0).start()
        idx_copy(0, 0).wait()
        idx_copy(1, 1).start()
        for g0 in range(ahead):
            gather(g0, g0).start()

        @pl.loop(0, nchunks)
        def _(g):
            slot = g % nb
            nxt = g + ahead
            nslot = nxt % nb

            @pl.when(nxt < nchunks)
            def _():
                @pl.when(nxt >= nb)
                def _():
                    write(nxt - nb, nslot).wait()

                @pl.when(nxt % nchunk_tok == 0)
                def _():
                    idx_copy(nxt // nchunk_tok, (nxt // nchunk_tok) % 2).wait()

                gather(nxt, nslot).start()

            gather(g, slot).wait()
            write(g, slot).start()

            @pl.when((g % nchunk_tok == nchunk_tok - 1) & (g // nchunk_tok + 2 < tpw))
            def _():
                idx_copy(g // nchunk_tok + 2, (g // nchunk_tok) % 2).start()

        for j in range(nb):
            gl = nchunks - nb + j
            write(gl, gl % nb).wait()

    return stage(table, idx)


def _peer_staged_kernel(st_ref, c_ref, h_ref, gt_ref, *rest, tbs):
    o_ref = rest[-1]
    i = pl.program_id(0)
    lane = lax.broadcasted_iota(I32, (PEER_SLOTS, 128), 1)
    for t in range(tbs):
        tg = i * tbs + t
        col0 = pl.multiple_of((tg // 128) * 128, 128)
        gate = jnp.sum(jnp.where(lane == tg % 128, gt_ref[:, pl.ds(col0, 128)], 0.0), axis=-1, keepdims=True)
        y = _peer_token(st_ref[t], c_ref[pl.ds(t, 1), :], gate)
        o_ref[pl.ds(t, 1), :] = h_ref[pl.ds(t, 1), :] + y


def _peer_apply_staged(staged, c, h, gate_t_staged, partial, start):
    t_staged = staged.shape[0]
    tbs = STAGED_TOKENS_PER_STEP
    assert t_staged % tbs == 0 and start % tbs == 0 and t_staged % 128 == 0
    off = start // tbs
    row = lambda i: (i + off, 0)
    in_specs = [pl.BlockSpec((tbs, PEER_SLOTS, D_MODEL), lambda i: (i, 0, 0)),
                pl.BlockSpec((tbs, D_MODEL), row), pl.BlockSpec((tbs, D_MODEL), row),
                pl.BlockSpec((PEER_SLOTS, t_staged), lambda i: (0, 0))]
    args = [staged, c, h, gate_t_staged]
    aliases = {}
    if partial is not None:
        in_specs.append(pl.BlockSpec(memory_space=pl.ANY))
        args.append(partial)
        aliases = {4: 0}
    return pl.pallas_call(
        functools.partial(_peer_staged_kernel, tbs=tbs), name="peer_apply_staged", grid=(t_staged // tbs,),
        out_shape=jax.ShapeDtypeStruct(c.shape, F32),
        in_specs=in_specs, out_specs=pl.BlockSpec((tbs, D_MODEL), row),
        input_output_aliases=aliases,
        compiler_params=_cparams("arbitrary"))(*args)


def _peer_apply(idx_t, c, h, gate_t, uv, staged_pieces):
    t = c.shape[0]
    start = t - sum(staged_pieces)
    out = _peer_apply_direct(idx_t, c, h, gate_t, uv[:, None, :], start) if start else None
    for n in staged_pieces:
        idx_sc = idx_t[:, start:start + n].T.reshape(n, PEER_SLOTS // SC_GATHER_ROWS, SC_GATHER_ROWS)
        staged = _sc_stage_rows(uv, idx_sc).reshape(n, PEER_SLOTS, D_MODEL)
        out = _peer_apply_staged(staged, c, h, gate_t[:, start:start + n], out, start)
        start += n
    return out


def _final_kernel(h_ref, p_ref, gple_ref, wg_ref, wp_ref, gfin_ref, y_ref):
    h = h_ref[...]
    e = _rms(h, gple_ref[...])
    gate = jax.nn.sigmoid(jnp.dot(e.astype(BF16), wg_ref[...], preferred_element_type=F32))
    up = jnp.dot(p_ref[...].astype(BF16), wp_ref[...], preferred_element_type=F32)
    y_ref[...] = _rms(h + up * gate, gfin_ref[...])


def _final_into_kernel(h_ref, p_ref, gple_ref, wg_ref, wp_ref, gfin_ref, prev_ref, y_ref):
    del prev_ref
    _final_kernel(h_ref, p_ref, gple_ref, wg_ref, wp_ref, gfin_ref, y_ref)


def _final(h, p, g_ple, w_gate, w_ple, g_final, into=None, row0=0, total=None):
    t = h.shape[0]
    total = t if total is None else total
    tm = _row_tile(t, 512)
    assert row0 % tm == 0
    row = lambda i: (i, 0)
    out_row = lambda i: (i + row0 // tm, 0)
    fixed = lambda i: (0, 0)
    in_specs = [pl.BlockSpec((tm, D_MODEL), row), pl.BlockSpec((tm, PLE_DIM), row),
                pl.BlockSpec((1, D_MODEL), fixed), pl.BlockSpec((D_MODEL, D_MODEL), fixed),
                pl.BlockSpec((PLE_DIM, D_MODEL), fixed), pl.BlockSpec((1, D_MODEL), fixed)]
    args = [h, p, g_ple, w_gate, w_ple, g_final]
    body = _final_kernel
    aliases = {}
    if into is not None:
        in_specs.append(pl.BlockSpec(memory_space=pl.ANY))
        args.append(into)
        aliases = {len(args) - 1: 0}
        body = _final_into_kernel
    return pl.pallas_call(
        body, name="ple_final", grid=(t // tm,),
        out_shape=jax.ShapeDtypeStruct((total, D_MODEL), F32),
        in_specs=in_specs, out_specs=pl.BlockSpec((tm, D_MODEL), out_row),
        input_output_aliases=aliases,
        compiler_params=_cparams("parallel"))(*args)


STAGED_FRACTIONS = (0.75,)
STAGED_UNIT = SC_WORKERS * 32
STAGED_PIECE = 8 * STAGED_UNIT


def _staged_tokens(t, fraction):
    total = int(t * fraction) // STAGED_UNIT * STAGED_UNIT
    if total < STAGED_UNIT or t < STAGED_PIECE:
        return ()
    pieces = [STAGED_PIECE] * (total // STAGED_PIECE)
    if total % STAGED_PIECE:
        pieces.append(total % STAGED_PIECE)
    return tuple(pieces)


def _layer_front(x, kv_buf, conv_prefix, s0, wts, batch, seq):
    t = batch * seq
    q, k, v, dqkv, z, ab = _inproj(x, wts["norm_mix"], wts["w_in"])
    if kv_buf is None:
        att = _swa_prompt(q, k, v, wts["sinks"], batch, seq)
        k_new = k.reshape(batch, seq, ATT_KV)[:, -WINDOW:]
        v_new = v.reshape(batch, seq, ATT_KV)[:, -WINDOW:]
    else:
        att, k_new, v_new = _swa_sample(q.reshape(batch, seq, ATT_Q), k.reshape(batch, seq, ATT_KV),
                                        v.reshape(batch, seq, ATT_KV), kv_buf[0], kv_buf[1], wts["sinks"])
        att = att.reshape(t, ATT_Q)
    dn, s_new = _gdn(dqkv, z, ab, conv_prefix, s0, wts["conv_w"], wts["a_log"], wts["dt_bias"],
                     wts["dn_norm"], batch, seq)
    conv_new = dqkv.reshape(batch, seq, DN_QKV)[:, -(CONV_W - 1):]
    h, c, pq = _outproj(x, att, dn, wts["w_out"], wts["norm_ffn"], wts["peer_wq"])
    idx_t, gate_t = _route(pq, wts["keys1"], wts["keys2"])
    return dict(h=h, c=c, idx_t=idx_t, gate_t=gate_t, state=(k_new, v_new, conv_new, s_new))


def _layer_back(front, p, wts, staged_fraction=0.0, into=None, row0=0, total=None):
    t = front["c"].shape[0]
    h2 = _peer_apply(front["idx_t"], front["c"], front["h"], front["gate_t"], wts["uv"],
                     _staged_tokens(t, staged_fraction))
    return _final(h2, p, wts["norm_ple"], wts["ple_gate"], wts["ple_in"], wts["norm_final"], into, row0, total)


PROMPT_GROUPS = 4


def kernel(x_prompt, x_sample, p_prompt, p_sample, cache_swa_k, cache_swa_v, state_conv, state_delta, norm_mix, w_in, conv_w, attn_sinks, dn_a_log, dn_dt_bias, dn_norm, w_out, norm_ffn, peer_wq, peer_keys1, peer_keys2, peer_u, peer_v, norm_ple, ple_in, ple_gate, norm_final):
    depth = w_in.shape[0]
    assert depth == 1
    bp, lp = x_prompt.shape[0], x_prompt.shape[1]
    bs, ls = x_sample.shape[0], x_sample.shape[1]
    l = 0
    wts = dict(
        norm_mix=norm_mix[l].reshape(1, D_MODEL),
        w_in=jnp.pad(w_in[l], ((0, 0), (0, IN_WIDTH_PADDED - IN_WIDTH))).astype(BF16),
        conv_w=conv_w[l], sinks=attn_sinks[l], a_log=dn_a_log[l], dt_bias=dn_dt_bias[l], dn_norm=dn_norm[l],
        w_out=w_out[l].astype(BF16), norm_ffn=norm_ffn[l].reshape(1, D_MODEL), peer_wq=peer_wq[l].astype(BF16),
        keys1=peer_keys1[l].astype(BF16), keys2=peer_keys2[l].astype(BF16),
        uv=_pack_expert_rows(peer_u[l], peer_v[l]),
        norm_ple=norm_ple[l].reshape(1, D_MODEL), ple_in=ple_in[l].astype(BF16),
        ple_gate=ple_gate[l].astype(BF16), norm_final=norm_final.reshape(1, D_MODEL))
    ngrp = PROMPT_GROUPS if bp % PROMPT_GROUPS == 0 else 1
    bg = bp // ngrp
    fronts = []
    for gi in range(ngrp):
        xg = x_prompt[gi * bg:(gi + 1) * bg].reshape(bg * lp, D_MODEL)
        fronts.append(_layer_front(xg, None, jnp.zeros((bg, CONV_W - 1, DN_QKV), F32),
                                   jnp.zeros((bg, DN_HEADS, HEAD_DIM, HEAD_DIM), F32), wts, bg, lp))
    kv_buf = (cache_swa_k[l].reshape(bs, WINDOW, ATT_KV), cache_swa_v[l].reshape(bs, WINDOW, ATT_KV))
    front_s = _layer_front(x_sample.reshape(bs * ls, D_MODEL), kv_buf, state_conv[l], state_delta[l], wts, bs, ls)
    yp = None
    for gi in range(ngrp):
        pg = p_prompt[l, gi * bg:(gi + 1) * bg].reshape(bg * lp, PLE_DIM)
        yp = _layer_back(fronts[gi], pg, wts, STAGED_FRACTIONS[gi % len(STAGED_FRACTIONS)], yp, gi * bg * lp, bp * lp)
    ys = _layer_back(front_s, p_sample[l].reshape(bs * ls, PLE_DIM), wts)
    kp, vp, cp, sp = (jnp.concatenate([f["state"][j] for f in fronts], axis=0) for j in range(4))
    kn, vn, cn, sn = front_s["state"]
    kvshape = lambda b: (1, b, WINDOW, ATT_KV_HEADS, HEAD_DIM)
    return (yp.reshape(bp, lp, D_MODEL), ys.reshape(bs, ls, D_MODEL),
            kp.reshape(kvshape(bp)), vp.reshape(kvshape(bp)), cp[None], sp[None],
            kn.reshape(kvshape(bs)), vn.reshape(kvshape(bs)), cn[None], sn[None])
```

```python
import dataclasses
import functools
import math

import jax
import jax.numpy as jnp
from jax import lax
from jax.experimental import pallas as pl
from jax.experimental.pallas import tpu as pltpu
from jax.experimental.pallas import tpu_sc as plsc

F32 = jnp.float32
BF16 = jnp.bfloat16
I32 = jnp.int32
HIGHEST = lax.Precision.HIGHEST

D_MODEL = 1024
HEAD_DIM = 64
ATT_HEADS = 8
ATT_KV_HEADS = 2
ATT_GROUP = 4
WINDOW = 128
ATT_Q = ATT_HEADS * HEAD_DIM
ATT_KV = ATT_KV_HEADS * HEAD_DIM
DN_HEADS = 8
DN_WIDTH = DN_HEADS * HEAD_DIM
DN_QKV = 3 * DN_WIDTH
CONV_W = 4
DN_CHUNK = 64
IN_WIDTH = ATT_Q + 2 * ATT_KV + DN_QKV + DN_WIDTH + 2 * DN_HEADS
IN_WIDTH_PADDED = 2944
N_KEYS = 128
PEER_HEADS = 8
PEER_QDIM = 256
PEER_TOPK = 16
PEER_SLOTS = PEER_HEADS * PEER_TOPK
PLE_DIM = 256
EPS = 1e-6
NEG_INF = float("-inf")
INDEX_SENTINEL = 1.0e9

VMEM_LIMIT = 48 * 1024 * 1024


def _cparams(*sem):
    return pltpu.CompilerParams(dimension_semantics=sem, vmem_limit_bytes=VMEM_LIMIT)


def _rms(x, g):
    return x * lax.rsqrt(jnp.mean(x * x, axis=-1, keepdims=True) + EPS) * g


def _bdot(a, b):
    return jnp.dot(a.astype(BF16), b.astype(BF16), preferred_element_type=F32)


def _fdot(a, b):
    return jnp.dot(a, b, precision=HIGHEST, preferred_element_type=F32)


_NN = (((1,), (0,)), ((), ()))
_NT = (((1,), (1,)), ((), ()))
_TN = (((0,), (0,)), ((), ()))


def _row_tile(t, target):
    tile = min(t, target)
    assert t % tile == 0, (t, tile)
    return tile


def _inproj_kernel(x_ref, g_ref, w_ref, q_ref, k_ref, v_ref, dqkv_ref, z_ref, ab_ref):
    a = _rms(x_ref[...], g_ref[...])
    p = jnp.dot(a.astype(BF16), w_ref[...], preferred_element_type=F32)
    c0, c1, c2, c3, c4 = ATT_Q, ATT_Q + ATT_KV, ATT_Q + 2 * ATT_KV, ATT_Q + 2 * ATT_KV + DN_QKV, \
        ATT_Q + 2 * ATT_KV + DN_QKV + DN_WIDTH
    q_ref[...] = p[:, :c0]
    k_ref[...] = p[:, c0:c1]
    v_ref[...] = p[:, c1:c2]
    dqkv_ref[...] = p[:, c2:c3]
    z_ref[...] = p[:, c3:c4]
    ab_ref[...] = p[:, c4:c4 + 2 * DN_HEADS]


def _inproj(x, g, w_pad):
    t = x.shape[0]
    tm = _row_tile(t, 512)
    widths = (ATT_Q, ATT_KV, ATT_KV, DN_QKV, DN_WIDTH, 2 * DN_HEADS)
    row = lambda i: (i, 0)
    fixed = lambda i: (0, 0)
    return pl.pallas_call(
        _inproj_kernel, name="inproj", grid=(t // tm,),
        out_shape=[jax.ShapeDtypeStruct((t, w), F32) for w in widths],
        in_specs=[pl.BlockSpec((tm, D_MODEL), row), pl.BlockSpec((1, D_MODEL), fixed),
                  pl.BlockSpec((D_MODEL, IN_WIDTH_PADDED), fixed)],
        out_specs=[pl.BlockSpec((tm, w), row) for w in widths],
        compiler_params=_cparams("parallel"))(x, g, w_pad)


def _softmax_sink_pv(s, sink, vband):
    m = jnp.maximum(jnp.max(s, axis=-1, keepdims=True), sink)
    e = jnp.exp(s - m)
    denom = jnp.sum(e, axis=-1, keepdims=True) + jnp.exp(sink - m)
    return _bdot(e / denom, vband)


def _alibi_slope(h):
    return 2.0 ** (-8.0 * (h + 1) / ATT_HEADS)


def _group_columns(kh, rows, sink_ref):
    rg = lax.broadcasted_iota(I32, (ATT_GROUP * rows, 1), 0) // rows
    slope = jnp.full((ATT_GROUP * rows, 1), _alibi_slope(kh * ATT_GROUP), F32)
    sink = jnp.full((ATT_GROUP * rows, 1), sink_ref[kh * ATT_GROUP], F32)
    for j in range(1, ATT_GROUP):
        slope = jnp.where(rg == j, _alibi_slope(kh * ATT_GROUP + j), slope)
        sink = jnp.where(rg == j, sink_ref[kh * ATT_GROUP + j], sink)
    return slope, sink


def _attend_kv_head(q_all, kband, vband, kh, dist, valid, sink_ref):
    rows = q_all.shape[0]
    heads = range(kh * ATT_GROUP, (kh + 1) * ATT_GROUP)
    q4 = jnp.concatenate([q_all[:, h * HEAD_DIM:(h + 1) * HEAD_DIM] for h in heads], axis=0)
    k = kband[:, kh * HEAD_DIM:(kh + 1) * HEAD_DIM]
    v = vband[:, kh * HEAD_DIM:(kh + 1) * HEAD_DIM]
    s = lax.dot_general(q4.astype(BF16), k.astype(BF16), _NT, preferred_element_type=F32) * (HEAD_DIM ** -0.5)
    slope, sink = _group_columns(kh, rows, sink_ref)
    s = jnp.where(valid, s - slope * dist, NEG_INF)
    o4 = _softmax_sink_pv(s, sink, v)
    return [o4[j * rows:(j + 1) * rows] for j in range(ATT_GROUP)]


def _swa_prompt_kernel(sink_ref, q_ref, kc_ref, kp_ref, vc_ref, vp_ref, o_ref):
    n = pl.program_id(1)
    rows = ATT_GROUP * WINDOW
    qi = lax.broadcasted_iota(I32, (rows, 2 * WINDOW), 0) % WINDOW
    kj = lax.broadcasted_iota(I32, (rows, 2 * WINDOW), 1)
    dist = WINDOW + qi - kj
    kpos = n * WINDOW - WINDOW + kj
    valid = (dist >= 0) & (dist <= WINDOW) & (kpos >= 0)
    distf = dist.astype(F32)
    kband = jnp.concatenate([kp_ref[...], kc_ref[...]], axis=0)
    vband = jnp.concatenate([vp_ref[...], vc_ref[...]], axis=0)
    q_all = q_ref[...]
    outs = []
    for kh in range(ATT_KV_HEADS):
        outs += _attend_kv_head(q_all, kband, vband, kh, distf, valid, sink_ref)
    o_ref[...] = jnp.concatenate(outs, axis=-1)


def _swa_prompt(q, k, v, sinks, batch, seq):
    nblk = seq // WINDOW
    cur = lambda b, n: (b * nblk + n, 0)
    prev = lambda b, n: (b * nblk + jnp.maximum(n - 1, 0), 0)
    return pl.pallas_call(
        _swa_prompt_kernel, name="swa_prompt", grid=(batch, nblk),
        out_shape=jax.ShapeDtypeStruct((batch * seq, ATT_Q), F32),
        in_specs=[pl.BlockSpec(memory_space=pltpu.SMEM),
                  pl.BlockSpec((WINDOW, ATT_Q), cur),
                  pl.BlockSpec((WINDOW, ATT_KV), cur), pl.BlockSpec((WINDOW, ATT_KV), prev),
                  pl.BlockSpec((WINDOW, ATT_KV), cur), pl.BlockSpec((WINDOW, ATT_KV), prev)],
        out_specs=pl.BlockSpec((WINDOW, ATT_Q), cur),
        compiler_params=_cparams("parallel", "parallel"))(sinks, q, k, k, v, v)


def _swa_sample_kernel(sink_ref, q_ref, k_ref, v_ref, kb_ref, vb_ref, o_ref, ko_ref, vo_ref, *, bt, seq):
    klen = WINDOW + seq
    rows = ATT_GROUP * seq
    qi = lax.broadcasted_iota(I32, (rows, klen), 0) % seq
    kj = lax.broadcasted_iota(I32, (rows, klen), 1)
    dist = qi - (kj - WINDOW)
    valid = (dist >= 0) & (dist <= WINDOW)
    distf = dist.astype(F32)
    for b in range(bt):
        kc = jnp.concatenate([kb_ref[b], k_ref[b]], axis=0)
        vc = jnp.concatenate([vb_ref[b], v_ref[b]], axis=0)
        ko_ref[b] = kc[seq:]
        vo_ref[b] = vc[seq:]
        q_all = q_ref[b]
        outs = []
        for kh in range(ATT_KV_HEADS):
            outs += _attend_kv_head(q_all, kc, vc, kh, distf, valid, sink_ref)
        o_ref[b] = jnp.concatenate(outs, axis=-1)


def _swa_sample(q, k, v, k_buf, v_buf, sinks):
    batch, seq = q.shape[0], q.shape[1]
    bt = _row_tile(batch, 8)
    blk = lambda w, r: pl.BlockSpec((bt, r, w), lambda i: (i, 0, 0))
    return pl.pallas_call(
        functools.partial(_swa_sample_kernel, bt=bt, seq=seq), name="swa_sample", grid=(batch // bt,),
        out_shape=[jax.ShapeDtypeStruct((batch, seq, ATT_Q), F32),
                   jax.ShapeDtypeStruct((batch, WINDOW, ATT_KV), F32),
                   jax.ShapeDtypeStruct((batch, WINDOW, ATT_KV), F32)],
        in_specs=[pl.BlockSpec(memory_space=pltpu.SMEM), blk(ATT_Q, seq), blk(ATT_KV, seq), blk(ATT_KV, seq),
                  blk(ATT_KV, WINDOW), blk(ATT_KV, WINDOW)],
        out_specs=[blk(ATT_Q, seq), blk(ATT_KV, WINDOW), blk(ATT_KV, WINDOW)],
        compiler_params=_cparams("parallel"))(sinks, q, k, v, k_buf, v_buf)


def _split_bf16(a, cache):
    if id(a) not in cache:
        hi = a.astype(BF16)
        cache[id(a)] = (a, (hi, (a - hi.astype(F32)).astype(BF16)))
    return cache[id(a)][1]


MXU_DEPTH = 256


def _dot3_impl(a, b, dims=_NN, *, cache):
    (ca,), (cb,) = dims[0]
    depth = a.shape[ca]
    ah, al = _split_bf16(a, cache)
    bh, bl = _split_bf16(b, cache)
    f = lambda x, y: lax.dot_general(x, y, dims, preferred_element_type=F32)
    if 3 * depth <= MXU_DEPTH:
        return f(jnp.concatenate([ah, al, ah], axis=ca), jnp.concatenate([bh, bh, bl], axis=cb))
    return f(jnp.concatenate([ah, al], axis=ca), jnp.concatenate([bh, bh], axis=cb)) + f(ah, bl)


def _gdn_kernel(alog_ref, dtb_ref, dqkv_ref, z_ref, ab_ref, cw_ref, nrm_ref, pre_ref, s0_ref,
                o_ref, sfin_ref, tail_ref, state_ref, *, c, g, nsub):
    n = pl.program_id(1)
    ngrp = DN_HEADS // g
    r = g * c
    rows = nsub * c
    hd = HEAD_DIM
    _dot3 = functools.partial(_dot3_impl, cache={})

    @pl.when(n == 0)
    def _():
        tail_ref[...] = jnp.concatenate([jnp.zeros((8 - (CONV_W - 1), DN_QKV), F32), pre_ref[0]], axis=0)
        for gi in range(ngrp):
            state_ref[gi] = jnp.concatenate([s0_ref[0, gi * g + j] for j in range(g)], axis=1)

    x = dqkv_ref[...]
    hist = jnp.concatenate([tail_ref[...], x], axis=0)
    tail_ref[...] = hist[rows:]
    cw = cw_ref[...]
    y = x * cw[CONV_W - 1:CONV_W]
    for d in range(1, CONV_W):
        y = y + hist[8 - d:8 - d + rows] * cw[CONV_W - 1 - d:CONV_W - d]
    qkv = y * jax.nn.sigmoid(y)
    z = z_ref[...]
    ab = ab_ref[...]
    beta_all = jax.nn.sigmoid(ab[:, DN_HEADS:2 * DN_HEADS])
    g_all = -jnp.exp(alog_ref[...]) * jax.nn.softplus(ab[:, 0:DN_HEADS] + dtb_ref[...])

    ri = lax.broadcasted_iota(I32, (r, r), 0)
    ci = lax.broadcasted_iota(I32, (r, r), 1)
    same = (ri // c) == (ci // c)
    lower = same & (ri >= ci)
    strict = same & (ri > ci)
    tri = lower.astype(F32)
    blk = same.astype(F32)
    eye = (ri == ci).astype(F32)
    rowhead = lax.broadcasted_iota(I32, (r, hd), 0) // c

    def stack(a, sub, gi, base):
        return jnp.concatenate([a[sub * c:(sub + 1) * c, base + (gi * g + j) * hd:base + (gi * g + j + 1) * hd]
                                for j in range(g)], axis=0)

    def column(a, sub, gi):
        return jnp.concatenate([a[sub * c:(sub + 1) * c, gi * g + j:gi * g + j + 1] for j in range(g)], axis=0)

    def own_block(m):
        acc = jnp.where(rowhead == 0, m[:, 0:hd], 0.0)
        for j in range(1, g):
            acc = acc + jnp.where(rowhead == j, m[:, j * hd:(j + 1) * hd], 0.0)
        return acc

    keys = [(sub, gi) for sub in range(nsub) for gi in range(ngrp)]
    grp = {}
    for sub, gi in keys:
        q = stack(qkv, sub, gi, 0)
        k = stack(qkv, sub, gi, DN_WIDTH)
        v = stack(qkv, sub, gi, 2 * DN_WIDTH)
        q = q * lax.rsqrt(jnp.sum(q * q, axis=-1, keepdims=True) + EPS) * (hd ** -0.5)
        k = k * lax.rsqrt(jnp.sum(k * k, axis=-1, keepdims=True) + EPS)
        beta = column(beta_all, sub, gi)
        gr = column(g_all, sub, gi)
        gcol = _fdot(tri, gr)
        grow = lax.dot_general(gr, tri, (((0,), (1,)), ((), ())), precision=HIGHEST,
                               preferred_element_type=F32)
        glast = _fdot(blk, gr)
        decay = jnp.exp(jnp.where(lower, gcol - grow, NEG_INF))
        kb = k * beta
        lm = jnp.where(strict, _dot3(kb, k, _NT) * decay, 0.0)
        grp[sub, gi] = dict(q=q, k=k, v=v, beta=beta, gcol=gcol, glast=glast, decay=decay, kb=kb, lm=lm)

    pair = ((ri // 2) == (ci // 2)) & (ri % 2 == 1) & (ci % 2 == 0)
    invs = {key: eye - jnp.where(pair, grp[key]["lm"], 0.0) for key in keys}
    s = 2
    while s < c:
        off = ((ri // (2 * s)) == (ci // (2 * s))) & ((ri // s) % 2 == 1) & ((ci // s) % 2 == 0)
        tmp = {key: _dot3(invs[key], jnp.where(off, grp[key]["lm"], 0.0)) for key in keys}
        invs = {key: invs[key] - _dot3(tmp[key], invs[key]) for key in keys}
        s *= 2

    for sub in range(nsub):
        outs = [None] * DN_HEADS
        for gi in range(ngrp):
            d = grp[sub, gi]
            eg = jnp.exp(d["gcol"])
            sol = _dot3(invs[sub, gi], jnp.concatenate([d["v"] * d["beta"], d["kb"] * eg], axis=1))
            u = sol[:, :hd]
            w = sol[:, hd:]
            qk = _dot3(d["q"], d["k"], _NT) * d["decay"]
            ktail = d["k"] * jnp.exp(d["glast"] - d["gcol"])
            qhead = d["q"] * eg
            st = state_ref[gi]
            vnew = u - own_block(_dot3(w, st))
            o = own_block(_dot3(qhead, st)) + _dot3(qk, vnew)
            vblk = jnp.concatenate([jnp.where(rowhead == j, vnew, 0.0) for j in range(g)], axis=1)
            elast = jnp.exp(d["glast"])
            keep = jnp.concatenate([jnp.broadcast_to(elast[j * c:j * c + 1, :], (1, hd)) for j in range(g)],
                                   axis=1)
            state_ref[gi] = st * keep + _dot3(ktail, vblk, _TN)
            zs = stack(z, sub, gi, 0)
            o = _rms(o, nrm_ref[...]) * (zs * jax.nn.sigmoid(zs))
            for j in range(g):
                outs[gi * g + j] = o[j * c:(j + 1) * c]
        o_ref[sub * c:(sub + 1) * c, :] = jnp.concatenate(outs, axis=-1)

    @pl.when(n == pl.num_programs(1) - 1)
    def _():
        for gi in range(ngrp):
            for j in range(g):
                sfin_ref[0, gi * g + j] = state_ref[gi][:, j * hd:(j + 1) * hd]


GDN_CHUNKS_PER_STEP = 2


def _gdn(dqkv, z, ab, conv_prefix, s0, conv_w, a_log, dt_bias, dn_norm, batch, seq):
    c = min(DN_CHUNK, seq)
    nchunk = seq // c
    assert seq % c == 0 and c % 8 == 0
    g = min(DN_HEADS, max(1, 128 // c))
    nsub = GDN_CHUNKS_PER_STEP if nchunk % GDN_CHUNKS_PER_STEP == 0 else 1
    nstep = nchunk // nsub
    rows = nsub * c
    tok = lambda b, n: (b * nstep + n, 0)
    fixed = lambda b, n: (0, 0)
    return pl.pallas_call(
        functools.partial(_gdn_kernel, c=c, g=g, nsub=nsub), name="gdn", grid=(batch, nstep),
        out_shape=[jax.ShapeDtypeStruct((batch * seq, DN_WIDTH), F32),
                   jax.ShapeDtypeStruct((batch, DN_HEADS, HEAD_DIM, HEAD_DIM), F32)],
        in_specs=[pl.BlockSpec((1, DN_HEADS), fixed), pl.BlockSpec((1, DN_HEADS), fixed),
                  pl.BlockSpec((rows, DN_QKV), tok), pl.BlockSpec((rows, DN_WIDTH), tok),
                  pl.BlockSpec((rows, 2 * DN_HEADS), tok),
                  pl.BlockSpec((CONV_W, DN_QKV), fixed), pl.BlockSpec((1, HEAD_DIM), fixed),
                  pl.BlockSpec((1, CONV_W - 1, DN_QKV), lambda b, n: (b, 0, 0)),
                  pl.BlockSpec((1, DN_HEADS, HEAD_DIM, HEAD_DIM), lambda b, n: (b, 0, 0, 0))],
        out_specs=[pl.BlockSpec((rows, DN_WIDTH), tok),
                   pl.BlockSpec((1, DN_HEADS, HEAD_DIM, HEAD_DIM), lambda b, n: (b, 0, 0, 0))],
        scratch_shapes=[pltpu.VMEM((8, DN_QKV), F32),
                        pltpu.VMEM((DN_HEADS // g, HEAD_DIM, g * HEAD_DIM), F32)],
        compiler_params=_cparams("parallel", "arbitrary"))(
            a_log.reshape(1, DN_HEADS), dt_bias.reshape(1, DN_HEADS), dqkv, z, ab, conv_w,
            dn_norm.reshape(1, HEAD_DIM), conv_prefix, s0)


def _outproj_kernel(x_ref, att_ref, dn_ref, wo_ref, g_ref, wq_ref, h_ref, c_ref, q_ref):
    mix = jnp.concatenate([att_ref[...], dn_ref[...]], axis=-1)
    h = x_ref[...] + jnp.dot(mix.astype(BF16), wo_ref[...], preferred_element_type=F32)
    h_ref[...] = h
    cn = _rms(h, g_ref[...])
    c_ref[...] = cn
    q_ref[...] = jnp.dot(cn.astype(BF16), wq_ref[...], preferred_element_type=F32)


def _outproj(x, att, dn, w_out, g_ffn, wq):
    t = x.shape[0]
    tm = _row_tile(t, 512)
    row = lambda i: (i, 0)
    fixed = lambda i: (0, 0)
    qw = PEER_HEADS * PEER_QDIM
    return pl.pallas_call(
        _outproj_kernel, name="outproj", grid=(t // tm,),
        out_shape=[jax.ShapeDtypeStruct((t, D_MODEL), F32), jax.ShapeDtypeStruct((t, D_MODEL), F32),
                   jax.ShapeDtypeStruct((t, qw), F32)],
        in_specs=[pl.BlockSpec((tm, D_MODEL), row), pl.BlockSpec((tm, ATT_Q), row),
                  pl.BlockSpec((tm, DN_WIDTH), row), pl.BlockSpec((D_MODEL, D_MODEL), fixed),
                  pl.BlockSpec((1, D_MODEL), fixed), pl.BlockSpec((D_MODEL, qw), fixed)],
        out_specs=[pl.BlockSpec((tm, D_MODEL), row), pl.BlockSpec((tm, D_MODEL), row),
                   pl.BlockSpec((tm, qw), row)],
        compiler_params=_cparams("parallel"))(x, att, dn, w_out, g_ffn, wq)


def _top16_rows(s, iota):
    big = jnp.float32(INDEX_SENTINEL)
    vals, idxs = [], []
    for _ in range(PEER_TOPK):
        m = jnp.max(s, axis=0, keepdims=True)
        i = jnp.min(jnp.where(s == m, iota, big), axis=0, keepdims=True)
        vals.append(m)
        idxs.append(i)
        s = jnp.where(iota == i, NEG_INF, s)
    return jnp.concatenate(vals, axis=0), jnp.concatenate(idxs, axis=0)


def _candidates(a, b, row8):
    t = a.shape[1]
    lo = row8 < 4
    bc = lambda r, n: jnp.broadcast_to(a[r:r + 1], (n, t))
    b8 = b[0:8]
    b44 = jnp.where(lo, b8, pltpu.roll(b8, 4, 0))
    parts = [bc(0, 8), bc(0, 8), bc(1, 8), bc(2, 8), bc(3, 8),
             jnp.where(lo, bc(4, 8), bc(5, 8)), jnp.where(lo, bc(6, 8), bc(7, 8)), a[8:16]]
    others = [b[0:8], b[8:16], b8, b8, b8, b44, b44, jnp.broadcast_to(b[0:1], (8, t))]
    return parts, others


def _flat_index_column():
    r = lax.broadcasted_iota(I32, (8, 1), 0)
    lo = r < 4
    cols = [r, 8 + r, 16 + r, 32 + r, 48 + r, jnp.where(lo, 64 + r, 80 + r - 4),
            jnp.where(lo, 96 + r, 112 + r - 4), (8 + r) * 16]
    return jnp.concatenate(cols, axis=0).astype(F32)


def _route_kernel(q_ref, k1_ref, k2_ref, idx_ref, gate_ref, *, tb):
    iota = lax.broadcasted_iota(I32, (N_KEYS, tb), 0).astype(F32)
    row8 = lax.broadcasted_iota(I32, (8, tb), 0)
    flat = jnp.broadcast_to(_flat_index_column(), (4 * PEER_TOPK, tb))
    big = jnp.float32(INDEX_SENTINEL)
    half = PEER_QDIM // 2
    for h in range(PEER_HEADS):
        q1 = q_ref[:, h * PEER_QDIM:h * PEER_QDIM + half]
        q2 = q_ref[:, h * PEER_QDIM + half:(h + 1) * PEER_QDIM]
        nt = (((1,), (1,)), ((), ()))
        s1 = lax.dot_general(k1_ref[h], q1.astype(BF16), nt, preferred_element_type=F32)
        s2 = lax.dot_general(k2_ref[h], q2.astype(BF16), nt, preferred_element_type=F32)
        v1, i1 = _top16_rows(s1, iota)
        v2, i2 = _top16_rows(s2, iota)
        pa, pb = _candidates(v1, v2, row8)
        cand = jnp.concatenate([x + y for x, y in zip(pa, pb)], axis=0)
        ea, eb = _candidates(i1, i2, row8)
        expert = jnp.concatenate([x * float(N_KEYS) + y for x, y in zip(ea, eb)], axis=0)
        scs, exs = [], []
        for _ in range(PEER_TOPK):
            m = jnp.max(cand, axis=0, keepdims=True)
            c = jnp.min(jnp.where(cand == m, flat, big), axis=0, keepdims=True)
            hit = flat == c
            exs.append(jnp.max(jnp.where(hit, expert, -1.0), axis=0, keepdims=True))
            scs.append(m)
            cand = jnp.where(hit, NEG_INF, cand)
        sc = jnp.concatenate(scs, axis=0)
        e = jnp.exp(sc - sc[0:1])
        gate_ref[h * PEER_TOPK:(h + 1) * PEER_TOPK, :] = e / jnp.sum(e, axis=0, keepdims=True)
        idx_ref[h * PEER_TOPK:(h + 1) * PEER_TOPK, :] = jnp.concatenate(exs, axis=0).astype(I32)


def _route(q, keys1, keys2):
    t = q.shape[0]
    tb = _row_tile(t, 256)
    kspec = pl.BlockSpec((PEER_HEADS, N_KEYS, PEER_QDIM // 2), lambda i: (0, 0, 0))
    return pl.pallas_call(
        functools.partial(_route_kernel, tb=tb), name="peer_route", grid=(t // tb,),
        out_shape=[jax.ShapeDtypeStruct((PEER_SLOTS, t), I32), jax.ShapeDtypeStruct((PEER_SLOTS, t), F32)],
        in_specs=[pl.BlockSpec((tb, PEER_HEADS * PEER_QDIM), lambda i: (i, 0)), kspec, kspec],
        out_specs=[pl.BlockSpec((PEER_SLOTS, tb), lambda i: (0, i)),
                   pl.BlockSpec((PEER_SLOTS, tb), lambda i: (0, i))],
        compiler_params=_cparams("parallel"))(q, keys1, keys2)


GATHER_SLOTS = 3
SC_WORKERS = 32
SC_GATHER_ROWS = 16
SC_BUFFERS = 6
STAGED_TOKENS_PER_STEP = 16


def _peer_token(packed, x, gate):
    u = lax.bitcast_convert_type(packed & jnp.uint32(0xFFFF0000), F32)
    v = lax.bitcast_convert_type(packed << 16, F32)
    p = u * x
    acc = p[:, 0:128]
    for j in range(1, D_MODEL // 128):
        acc = acc + p[:, j * 128:(j + 1) * 128]
    s = jnp.sum(acc, axis=-1, keepdims=True)
    act = 0.5 * s * (1.0 + lax.erf(s * (1.0 / math.sqrt(2.0))))
    return jnp.sum((gate * act) * v, axis=0, keepdims=True)


def _peer_apply_kernel(idx_ref, c_ref, h_ref, gt_ref, uv_ref, o_ref, buf, sem, *, tb):
    ahead = GATHER_SLOTS - 1

    def issue(t, slot):
        for k in range(PEER_SLOTS):
            e = idx_ref[k, t]
            pltpu.make_async_copy(uv_ref.at[e], buf.at[slot, pl.ds(k, 1), :],
                                  sem.at[slot]).start(priority=k % 2)

    def wait(slot):
        pltpu.make_async_copy(buf.at[slot], buf.at[slot], sem.at[slot]).wait()

    lane = lax.broadcasted_iota(I32, (PEER_SLOTS, tb), 1)

    def apply(t, slot):
        gate = jnp.sum(jnp.where(lane == t, gt_ref[...], 0.0), axis=-1, keepdims=True)
        y = _peer_token(buf[slot], c_ref[pl.ds(t, 1), :], gate)
        o_ref[pl.ds(t, 1), :] = h_ref[pl.ds(t, 1), :] + y

    for t0 in range(ahead):
        issue(t0, t0)

    def body(i, carry):
        for j in range(GATHER_SLOTS):
            t = i * GATHER_SLOTS + j
            issue(t + ahead, (j + ahead) % GATHER_SLOTS)
            wait(j)
            apply(t, j)
        return carry

    nmain = (tb - ahead) // GATHER_SLOTS
    lax.fori_loop(0, nmain, body, 0)
    for t in range(nmain * GATHER_SLOTS, tb):
        if t + ahead < tb:
            issue(t + ahead, (t + ahead) % GATHER_SLOTS)
        wait(t % GATHER_SLOTS)
        apply(t, t % GATHER_SLOTS)


def _pack_expert_rows(u, v):
    hi = lax.bitcast_convert_type(u.astype(BF16), jnp.uint16).astype(jnp.uint32) << 16
    lo = lax.bitcast_convert_type(v.astype(BF16), jnp.uint16).astype(jnp.uint32)
    return hi | lo


def _peer_apply_direct(idx_t, c, h, gate_t, uv_rows, t_direct):
    t = c.shape[0]
    tb = _row_tile(t_direct, 256)
    row = lambda i: (i, 0)
    return pl.pallas_call(
        functools.partial(_peer_apply_kernel, tb=tb), name="peer_apply", grid=(t_direct // tb,),
        out_shape=jax.ShapeDtypeStruct((t, D_MODEL), F32),
        in_specs=[pl.BlockSpec((PEER_SLOTS, tb), lambda i: (0, i), memory_space=pltpu.SMEM),
                  pl.BlockSpec((tb, D_MODEL), row), pl.BlockSpec((tb, D_MODEL), row),
                  pl.BlockSpec((PEER_SLOTS, tb), lambda i: (0, i)),
                  pl.BlockSpec(memory_space=pl.ANY)],
        out_specs=pl.BlockSpec((tb, D_MODEL), row),
        scratch_shapes=[pltpu.VMEM((GATHER_SLOTS, PEER_SLOTS, D_MODEL), jnp.uint32),
                        pltpu.SemaphoreType.DMA((GATHER_SLOTS,))],
        compiler_params=_cparams("arbitrary"))(idx_t, c, h, gate_t, uv_rows)


def _sc_stage_rows(table, idx):
    t_total, nchunk_tok, _ = idx.shape
    d = table.shape[1]
    tpw = t_total // SC_WORKERS
    assert t_total % SC_WORKERS == 0 and tpw >= 2
    nb = SC_BUFFERS
    ahead = nb // 2
    params = pltpu.CompilerParams()
    if "needs_layout_passes" in pltpu.CompilerParams.__dataclass_fields__:
        params = dataclasses.replace(params, needs_layout_passes=False)
    mesh = plsc.VectorSubcoreMesh(core_axis_name="c", subcore_axis_name="s")

    @functools.partial(
        pl.kernel, mesh=mesh, compiler_params=params,
        out_type=jax.ShapeDtypeStruct((t_total * nchunk_tok * SC_GATHER_ROWS, d), table.dtype),
        scratch_types=[pltpu.VMEM((2, nchunk_tok, SC_GATHER_ROWS), I32),
                       pltpu.VMEM((nb, SC_GATHER_ROWS, d), table.dtype),
                       pltpu.SemaphoreType.DMA((nb,)), pltpu.SemaphoreType.DMA((nb,)),
                       pltpu.SemaphoreType.DMA((2,))])
    def stage(table_hbm, idx_hbm, out_hbm, idx_v, rows_v, sem_g, sem_w, sem_t):
        wid = lax.axis_index("s") * 2 + lax.axis_index("c")
        base = wid * tpw
        nchunks = tpw * nchunk_tok

        def gather(g, slot):
            ts = (g // nchunk_tok) % 2
            return pltpu.make_async_copy(table_hbm.at[idx_v.at[ts, g % nchunk_tok]], rows_v.at[slot],
                                         sem_g.at[slot])

        def write(g, slot):
            dst = out_hbm.at[pl.ds((base * nchunk_tok + g) * SC_GATHER_ROWS, SC_GATHER_ROWS)]
            return pltpu.make_async_copy(rows_v.at[slot], dst, sem_w.at[slot])

        def idx_copy(i, ts):
            return pltpu.make_async_copy(idx_hbm.at[base + i], idx_v.at[ts], sem_t.at[ts])

        idx_copy(0, 0).start()
        idx_copy(0, 0).wait()
        idx_copy(1, 1).start()
        for g0 in range(ahead):
            gather(g0, g0).start()

        @pl.loop(0, nchunks)
        def _(g):
            slot = g % nb
            nxt = g + ahead
            nslot = nxt % nb

            @pl.when(nxt < nchunks)
            def _():
                @pl.when(nxt >= nb)
                def _():
                    write(nxt - nb, nslot).wait()

                @pl.when(nxt % nchunk_tok == 0)
                def _():
                    idx_copy(nxt // nchunk_tok, (nxt // nchunk_tok) % 2).wait()

                gather(nxt, nslot).start()

            gather(g, slot).wait()
            write(g, slot).start()

            @pl.when((g % nchunk_tok == nchunk_tok - 1) & (g // nchunk_tok + 2 < tpw))
            def _():
                idx_copy(g // nchunk_tok + 2, (g // nchunk_tok) % 2).start()

        for j in range(nb):
            gl = nchunks - nb + j
            write(gl, gl % nb).wait()

    return stage(table, idx)


def _peer_staged_kernel(st_ref, c_ref, h_ref, gt_ref, *rest, tbs):
    o_ref = rest[-1]
    i = pl.program_id(0)
    lane = lax.broadcasted_iota(I32, (PEER_SLOTS, 128), 1)
    for t in range(tbs):
        tg = i * tbs + t
        col0 = pl.multiple_of((tg // 128) * 128, 128)
        gate = jnp.sum(jnp.where(lane == tg % 128, gt_ref[:, pl.ds(col0, 128)], 0.0), axis=-1, keepdims=True)
        y = _peer_token(st_ref[t], c_ref[pl.ds(t, 1), :], gate)
        o_ref[pl.ds(t, 1), :] = h_ref[pl.ds(t, 1), :] + y


def _peer_apply_staged(staged, c, h, gate_t_staged, partial, start):
    t_staged = staged.shape[0]
    tbs = STAGED_TOKENS_PER_STEP
    assert t_staged % tbs == 0 and start % tbs == 0 and t_staged % 128 == 0
    off = start // tbs
    row = lambda i: (i + off, 0)
    in_specs = [pl.BlockSpec((tbs, PEER_SLOTS, D_MODEL), lambda i: (i, 0, 0)),
                pl.BlockSpec((tbs, D_MODEL), row), pl.BlockSpec((tbs, D_MODEL), row),
                pl.BlockSpec((PEER_SLOTS, t_staged), lambda i: (0, 0))]
    args = [staged, c, h, gate_t_staged]
    aliases = {}
    if partial is not None:
        in_specs.append(pl.BlockSpec(memory_space=pl.ANY))
        args.append(partial)
        aliases = {4: 0}
    return pl.pallas_call(
        functools.partial(_peer_staged_kernel, tbs=tbs), name="peer_apply_staged", grid=(t_staged // tbs,),
        out_shape=jax.ShapeDtypeStruct(c.shape, F32),
        in_specs=in_specs, out_specs=pl.BlockSpec((tbs, D_MODEL), row),
        input_output_aliases=aliases,
        compiler_params=_cparams("arbitrary"))(*args)


def _peer_apply(idx_t, c, h, gate_t, uv, staged_pieces):
    t = c.shape[0]
    start = t - sum(staged_pieces)
    out = _peer_apply_direct(idx_t, c, h, gate_t, uv[:, None, :], start) if start else None
    for n in staged_pieces:
        idx_sc = idx_t[:, start:start + n].T.reshape(n, PEER_SLOTS // SC_GATHER_ROWS, SC_GATHER_ROWS)
        staged = _sc_stage_rows(uv, idx_sc).reshape(n, PEER_SLOTS, D_MODEL)
        out = _peer_apply_staged(staged, c, h, gate_t[:, start:start + n], out, start)
        start += n
    return out


def _final_kernel(h_ref, p_ref, gple_ref, wg_ref, wp_ref, gfin_ref, y_ref):
    h = h_ref[...]
    e = _rms(h, gple_ref[...])
    gate = jax.nn.sigmoid(jnp.dot(e.astype(BF16), wg_ref[...], preferred_element_type=F32))
    up = jnp.dot(p_ref[...].astype(BF16), wp_ref[...], preferred_element_type=F32)
    y_ref[...] = _rms(h + up * gate, gfin_ref[...])


def _final_into_kernel(h_ref, p_ref, gple_ref, wg_ref, wp_ref, gfin_ref, prev_ref, y_ref):
    del prev_ref
    _final_kernel(h_ref, p_ref, gple_ref, wg_ref, wp_ref, gfin_ref, y_ref)


def _final(h, p, g_ple, w_gate, w_ple, g_final, into=None, row0=0, total=None):
    t = h.shape[0]
    total = t if total is None else total
    tm = _row_tile(t, 512)
    assert row0 % tm == 0
    row = lambda i: (i, 0)
    out_row = lambda i: (i + row0 // tm, 0)
    fixed = lambda i: (0, 0)
    in_specs = [pl.BlockSpec((tm, D_MODEL), row), pl.BlockSpec((tm, PLE_DIM), row),
                pl.BlockSpec((1, D_MODEL), fixed), pl.BlockSpec((D_MODEL, D_MODEL), fixed),
                pl.BlockSpec((PLE_DIM, D_MODEL), fixed), pl.BlockSpec((1, D_MODEL), fixed)]
    args = [h, p, g_ple, w_gate, w_ple, g_final]
    body = _final_kernel
    aliases = {}
    if into is not None:
        in_specs.append(pl.BlockSpec(memory_space=pl.ANY))
        args.append(into)
        aliases = {len(args) - 1: 0}
        body = _final_into_kernel
    return pl.pallas_call(
        body, name="ple_final", grid=(t // tm,),
        out_shape=jax.ShapeDtypeStruct((total, D_MODEL), F32),
        in_specs=in_specs, out_specs=pl.BlockSpec((tm, D_MODEL), out_row),
        input_output_aliases=aliases,
        compiler_params=_cparams("parallel"))(*args)


STAGED_FRACTIONS = (1.0, 0.75, 0.75, 0.75)
STAGED_UNIT = SC_WORKERS * 32
STAGED_PIECE = 8 * STAGED_UNIT


def _staged_tokens(t, fraction):
    total = int(t * fraction) // STAGED_UNIT * STAGED_UNIT
    if total < STAGED_UNIT or t < STAGED_PIECE:
        return ()
    pieces = [STAGED_PIECE] * (total // STAGED_PIECE)
    if total % STAGED_PIECE:
        pieces.append(total % STAGED_PIECE)
    return tuple(pieces)


def _layer_front(x, kv_buf, conv_prefix, s0, wts, batch, seq):
    t = batch * seq
    q, k, v, dqkv, z, ab = _inproj(x, wts["norm_mix"], wts["w_in"])
    if kv_buf is None:
        att = _swa_prompt(q, k, v, wts["sinks"], batch, seq)
        k_new = k.reshape(batch, seq, ATT_KV)[:, -WINDOW:]
        v_new = v.reshape(batch, seq, ATT_KV)[:, -WINDOW:]
    else:
        att, k_new, v_new = _swa_sample(q.reshape(batch, seq, ATT_Q), k.reshape(batch, seq, ATT_KV),
                                        v.reshape(batch, seq, ATT_KV), kv_buf[0], kv_buf[1], wts["sinks"])
        att = att.reshape(t, ATT_Q)
    dn, s_new = _gdn(dqkv, z, ab, conv_prefix, s0, wts["conv_w"], wts["a_log"], wts["dt_bias"],
                     wts["dn_norm"], batch, seq)
    conv_new = dqkv.reshape(batch, seq, DN_QKV)[:, -(CONV_W - 1):]
    h, c, pq = _outproj(x, att, dn, wts["w_out"], wts["norm_ffn"], wts["peer_wq"])
    idx_t, gate_t = _route(pq, wts["keys1"], wts["keys2"])
    return dict(h=h, c=c, idx_t=idx_t, gate_t=gate_t, state=(k_new, v_new, conv_new, s_new))


def _layer_back(front, p, wts, staged_fraction=0.0, into=None, row0=0, total=None):
    t = front["c"].shape[0]
    h2 = _peer_apply(front["idx_t"], front["c"], front["h"], front["gate_t"], wts["uv"],
                     _staged_tokens(t, staged_fraction))
    return _final(h2, p, wts["norm_ple"], wts["ple_gate"], wts["ple_in"], wts["norm_final"], into, row0, total)


PROMPT_GROUPS = 4


def kernel(x_prompt, x_sample, p_prompt, p_sample, cache_swa_k, cache_swa_v, state_conv, state_delta, norm_mix, w_in, conv_w, attn_sinks, dn_a_log, dn_dt_bias, dn_norm, w_out, norm_ffn, peer_wq, peer_keys1, peer_keys2, peer_u, peer_v, norm_ple, ple_in, ple_gate, norm_final):
    depth = w_in.shape[0]
    assert depth == 1
    bp, lp = x_prompt.shape[0], x_prompt.shape[1]
    bs, ls = x_sample.shape[0], x_sample.shape[1]
    l = 0
    wts = dict(
        norm_mix=norm_mix[l].reshape(1, D_MODEL),
        w_in=jnp.pad(w_in[l], ((0, 0), (0, IN_WIDTH_PADDED - IN_WIDTH))).astype(BF16),
        conv_w=conv_w[l], sinks=attn_sinks[l], a_log=dn_a_log[l], dt_bias=dn_dt_bias[l], dn_norm=dn_norm[l],
        w_out=w_out[l].astype(BF16), norm_ffn=norm_ffn[l].reshape(1, D_MODEL), peer_wq=peer_wq[l].astype(BF16),
        keys1=peer_keys1[l].astype(BF16), keys2=peer_keys2[l].astype(BF16),
        uv=_pack_expert_rows(peer_u[l], peer_v[l]),
        norm_ple=norm_ple[l].reshape(1, D_MODEL), ple_in=ple_in[l].astype(BF16),
        ple_gate=ple_gate[l].astype(BF16), norm_final=norm_final.reshape(1, D_MODEL))
    ngrp = PROMPT_GROUPS if bp % PROMPT_GROUPS == 0 else 1
    bg = bp // ngrp
    fronts = []
    for gi in range(ngrp):
        xg = x_prompt[gi * bg:(gi + 1) * bg].reshape(bg * lp, D_MODEL)
        fronts.append(_layer_front(xg, None, jnp.zeros((bg, CONV_W - 1, DN_QKV), F32),
                                   jnp.zeros((bg, DN_HEADS, HEAD_DIM, HEAD_DIM), F32), wts, bg, lp))
    kv_buf = (cache_swa_k[l].reshape(bs, WINDOW, ATT_KV), cache_swa_v[l].reshape(bs, WINDOW, ATT_KV))
    front_s = _layer_front(x_sample.reshape(bs * ls, D_MODEL), kv_buf, state_conv[l], state_delta[l], wts, bs, ls)
    yp = None
    for gi in range(ngrp):
        pg = p_prompt[l, gi * bg:(gi + 1) * bg].reshape(bg * lp, PLE_DIM)
        yp = _layer_back(fronts[gi], pg, wts, STAGED_FRACTIONS[gi % len(STAGED_FRACTIONS)], yp, gi * bg * lp, bp * lp)
    ys = _layer_back(front_s, p_sample[l].reshape(bs * ls, PLE_DIM), wts)
    kp, vp, cp, sp = (jnp.concatenate([f["state"][j] for f in fronts], axis=0) for j in range(4))
    kn, vn, cn, sn = front_s["state"]
    kvshape = lambda b: (1, b, WINDOW, ATT_KV_HEADS, HEAD_DIM)
    return (yp.reshape(bp, lp, D_MODEL), ys.reshape(bs, ls, D_MODEL),
            kp.reshape(kvshape(bp)), vp.reshape(kvshape(bp)), cp[None], sp[None],
            kn.reshape(kvshape(bs)), vn.reshape(kvshape(bs)), cn[None], sn[None])
```

```python
import dataclasses
import functools
import math

import jax
import jax.numpy as jnp
from jax import lax
from jax.experimental import pallas as pl
from jax.experimental.pallas import tpu as pltpu
from jax.experimental.pallas import tpu_sc as plsc

F32 = jnp.float32
BF16 = jnp.bfloat16
I32 = jnp.int32
HIGHEST = lax.Precision.HIGHEST

D_MODEL = 1024
HEAD_DIM = 64
ATT_HEADS = 8
ATT_KV_HEADS = 2
ATT_GROUP = 4
WINDOW = 128
ATT_Q = ATT_HEADS * HEAD_DIM
ATT_KV = ATT_KV_HEADS * HEAD_DIM
DN_HEADS = 8
DN_WIDTH = DN_HEADS * HEAD_DIM
DN_QKV = 3 * DN_WIDTH
CONV_W = 4
DN_CHUNK = 64
IN_WIDTH = ATT_Q + 2 * ATT_KV + DN_QKV + DN_WIDTH + 2 * DN_HEADS
IN_WIDTH_PADDED = 2944
N_KEYS = 128
PEER_HEADS = 8
PEER_QDIM = 256
PEER_TOPK = 16
PEER_SLOTS = PEER_HEADS * PEER_TOPK
PLE_DIM = 256
EPS = 1e-6
NEG_INF = float("-inf")
INDEX_SENTINEL = 1.0e9

VMEM_LIMIT = 48 * 1024 * 1024


def _cparams(*sem):
    return pltpu.CompilerParams(dimension_semantics=sem, vmem_limit_bytes=VMEM_LIMIT)


def _rms(x, g):
    return x * lax.rsqrt(jnp.mean(x * x, axis=-1, keepdims=True) + EPS) * g


def _bdot(a, b):
    return jnp.dot(a.astype(BF16), b.astype(BF16), preferred_element_type=F32)


def _fdot(a, b):
    return jnp.dot(a, b, precision=HIGHEST, preferred_element_type=F32)


_NN = (((1,), (0,)), ((), ()))
_NT = (((1,), (1,)), ((), ()))
_TN = (((0,), (0,)), ((), ()))


def _row_tile(t, target):
    tile = min(t, target)
    assert t % tile == 0, (t, tile)
    return tile


def _inproj_kernel(x_ref, g_ref, w_ref, q_ref, k_ref, v_ref, dqkv_ref, z_ref, ab_ref):
    a = _rms(x_ref[...], g_ref[...])
    p = jnp.dot(a.astype(BF16), w_ref[...], preferred_element_type=F32)
    c0, c1, c2, c3, c4 = ATT_Q, ATT_Q + ATT_KV, ATT_Q + 2 * ATT_KV, ATT_Q + 2 * ATT_KV + DN_QKV, \
        ATT_Q + 2 * ATT_KV + DN_QKV + DN_WIDTH
    q_ref[...] = p[:, :c0]
    k_ref[...] = p[:, c0:c1]
    v_ref[...] = p[:, c1:c2]
    dqkv_ref[...] = p[:, c2:c3]
    z_ref[...] = p[:, c3:c4]
    ab_ref[...] = p[:, c4:c4 + 2 * DN_HEADS]


def _inproj(x, g, w_pad):
    t = x.shape[0]
    tm = _row_tile(t, 512)
    widths = (ATT_Q, ATT_KV, ATT_KV, DN_QKV, DN_WIDTH, 2 * DN_HEADS)
    row = lambda i: (i, 0)
    fixed = lambda i: (0, 0)
    return pl.pallas_call(
        _inproj_kernel, name="inproj", grid=(t // tm,),
        out_shape=[jax.ShapeDtypeStruct((t, w), F32) for w in widths],
        in_specs=[pl.BlockSpec((tm, D_MODEL), row), pl.BlockSpec((1, D_MODEL), fixed),
                  pl.BlockSpec((D_MODEL, IN_WIDTH_PADDED), fixed)],
        out_specs=[pl.BlockSpec((tm, w), row) for w in widths],
        compiler_params=_cparams("parallel"))(x, g, w_pad)


def _softmax_sink_pv(s, sink, vband):
    m = jnp.maximum(jnp.max(s, axis=-1, keepdims=True), sink)
    e = jnp.exp(s - m)
    denom = jnp.sum(e, axis=-1, keepdims=True) + jnp.exp(sink - m)
    return _bdot(e / denom, vband)


def _alibi_slope(h):
    return 2.0 ** (-8.0 * (h + 1) / ATT_HEADS)


def _group_columns(kh, rows, sink_ref):
    rg = lax.broadcasted_iota(I32, (ATT_GROUP * rows, 1), 0) // rows
    slope = jnp.full((ATT_GROUP * rows, 1), _alibi_slope(kh * ATT_GROUP), F32)
    sink = jnp.full((ATT_GROUP * rows, 1), sink_ref[kh * ATT_GROUP], F32)
    for j in range(1, ATT_GROUP):
        slope = jnp.where(rg == j, _alibi_slope(kh * ATT_GROUP + j), slope)
        sink = jnp.where(rg == j, sink_ref[kh * ATT_GROUP + j], sink)
    return slope, sink


def _attend_kv_head(q_all, kband, vband, kh, dist, valid, sink_ref):
    rows = q_all.shape[0]
    heads = range(kh * ATT_GROUP, (kh + 1) * ATT_GROUP)
    q4 = jnp.concatenate([q_all[:, h * HEAD_DIM:(h + 1) * HEAD_DIM] for h in heads], axis=0)
    k = kband[:, kh * HEAD_DIM:(kh + 1) * HEAD_DIM]
    v = vband[:, kh * HEAD_DIM:(kh + 1) * HEAD_DIM]
    s = lax.dot_general(q4.astype(BF16), k.astype(BF16), _NT, preferred_element_type=F32) * (HEAD_DIM ** -0.5)
    slope, sink = _group_columns(kh, rows, sink_ref)
    s = jnp.where(valid, s - slope * dist, NEG_INF)
    o4 = _softmax_sink_pv(s, sink, v)
    return [o4[j * rows:(j + 1) * rows] for j in range(ATT_GROUP)]


def _swa_prompt_kernel(sink_ref, q_ref, kc_ref, kp_ref, vc_ref, vp_ref, o_ref):
    n = pl.program_id(1)
    rows = ATT_GROUP * WINDOW
    qi = lax.broadcasted_iota(I32, (rows, 2 * WINDOW), 0) % WINDOW
    kj = lax.broadcasted_iota(I32, (rows, 2 * WINDOW), 1)
    dist = WINDOW + qi - kj
    kpos = n * WINDOW - WINDOW + kj
    valid = (dist >= 0) & (dist <= WINDOW) & (kpos >= 0)
    distf = dist.astype(F32)
    kband = jnp.concatenate([kp_ref[...], kc_ref[...]], axis=0)
    vband = jnp.concatenate([vp_ref[...], vc_ref[...]], axis=0)
    q_all = q_ref[...]
    outs = []
    for kh in range(ATT_KV_HEADS):
        outs += _attend_kv_head(q_all, kband, vband, kh, distf, valid, sink_ref)
    o_ref[...] = jnp.concatenate(outs, axis=-1)


def _swa_prompt(q, k, v, sinks, batch, seq):
    nblk = seq // WINDOW
    cur = lambda b, n: (b * nblk + n, 0)
    prev = lambda b, n: (b * nblk + jnp.maximum(n - 1, 0), 0)
    return pl.pallas_call(
        _swa_prompt_kernel, name="swa_prompt", grid=(batch, nblk),
        out_shape=jax.ShapeDtypeStruct((batch * seq, ATT_Q), F32),
        in_specs=[pl.BlockSpec(memory_space=pltpu.SMEM),
                  pl.BlockSpec((WINDOW, ATT_Q), cur),
                  pl.BlockSpec((WINDOW, ATT_KV), cur), pl.BlockSpec((WINDOW, ATT_KV), prev),
                  pl.BlockSpec((WINDOW, ATT_KV), cur), pl.BlockSpec((WINDOW, ATT_KV), prev)],
        out_specs=pl.BlockSpec((WINDOW, ATT_Q), cur),
        compiler_params=_cparams("parallel", "parallel"))(sinks, q, k, k, v, v)


def _swa_sample_kernel(sink_ref, q_ref, k_ref, v_ref, kb_ref, vb_ref, o_ref, ko_ref, vo_ref, *, bt, seq):
    klen = WINDOW + seq
    rows = ATT_GROUP * seq
    qi = lax.broadcasted_iota(I32, (rows, klen), 0) % seq
    kj = lax.broadcasted_iota(I32, (rows, klen), 1)
    dist = qi - (kj - WINDOW)
    valid = (dist >= 0) & (dist <= WINDOW)
    distf = dist.astype(F32)
    for b in range(bt):
        kc = jnp.concatenate([kb_ref[b], k_ref[b]], axis=0)
        vc = jnp.concatenate([vb_ref[b], v_ref[b]], axis=0)
        ko_ref[b] = kc[seq:]
        vo_ref[b] = vc[seq:]
        q_all = q_ref[b]
        outs = []
        for kh in range(ATT_KV_HEADS):
            outs += _attend_kv_head(q_all, kc, vc, kh, distf, valid, sink_ref)
        o_ref[b] = jnp.concatenate(outs, axis=-1)


def _swa_sample(q, k, v, k_buf, v_buf, sinks):
    batch, seq = q.shape[0], q.shape[1]
    bt = _row_tile(batch, 8)
    blk = lambda w, r: pl.BlockSpec((bt, r, w), lambda i: (i, 0, 0))
    return pl.pallas_call(
        functools.partial(_swa_sample_kernel, bt=bt, seq=seq), name="swa_sample", grid=(batch // bt,),
        out_shape=[jax.ShapeDtypeStruct((batch, seq, ATT_Q), F32),
                   jax.ShapeDtypeStruct((batch, WINDOW, ATT_KV), F32),
                   jax.ShapeDtypeStruct((batch, WINDOW, ATT_KV), F32)],
        in_specs=[pl.BlockSpec(memory_space=pltpu.SMEM), blk(ATT_Q, seq), blk(ATT_KV, seq), blk(ATT_KV, seq),
                  blk(ATT_KV, WINDOW), blk(ATT_KV, WINDOW)],
        out_specs=[blk(ATT_Q, seq), blk(ATT_KV, WINDOW), blk(ATT_KV, WINDOW)],
        compiler_params=_cparams("parallel"))(sinks, q, k, v, k_buf, v_buf)


def _split_bf16(a, cache):
    if id(a) not in cache:
        hi = a.astype(BF16)
        cache[id(a)] = (a, (hi, (a - hi.astype(F32)).astype(BF16)))
    return cache[id(a)][1]


MXU_DEPTH = 256


def _dot3_impl(a, b, dims=_NN, *, cache):
    (ca,), (cb,) = dims[0]
    depth = a.shape[ca]
    ah, al = _split_bf16(a, cache)
    bh, bl = _split_bf16(b, cache)
    f = lambda x, y: lax.dot_general(x, y, dims, preferred_element_type=F32)
    if 3 * depth <= MXU_DEPTH:
        return f(jnp.concatenate([ah, al, ah], axis=ca), jnp.concatenate([bh, bh, bl], axis=cb))
    return f(jnp.concatenate([ah, al], axis=ca), jnp.concatenate([bh, bh], axis=cb)) + f(ah, bl)


def _gdn_kernel(alog_ref, dtb_ref, dqkv_ref, z_ref, ab_ref, cw_ref, nrm_ref, pre_ref, s0_ref,
                o_ref, sfin_ref, tail_ref, state_ref, *, c, g, nsub):
    n = pl.program_id(1)
    ngrp = DN_HEADS // g
    r = g * c
    rows = nsub * c
    hd = HEAD_DIM
    _dot3 = functools.partial(_dot3_impl, cache={})

    @pl.when(n == 0)
    def _():
        tail_ref[...] = jnp.concatenate([jnp.zeros((8 - (CONV_W - 1), DN_QKV), F32), pre_ref[0]], axis=0)
        for gi in range(ngrp):
            state_ref[gi] = jnp.concatenate([s0_ref[0, gi * g + j] for j in range(g)], axis=1)

    x = dqkv_ref[...]
    hist = jnp.concatenate([tail_ref[...], x], axis=0)
    tail_ref[...] = hist[rows:]
    cw = cw_ref[...]
    y = x * cw[CONV_W - 1:CONV_W]
    for d in range(1, CONV_W):
        y = y + hist[8 - d:8 - d + rows] * cw[CONV_W - 1 - d:CONV_W - d]
    qkv = y * jax.nn.sigmoid(y)
    z = z_ref[...]
    ab = ab_ref[...]
    beta_all = jax.nn.sigmoid(ab[:, DN_HEADS:2 * DN_HEADS])
    g_all = -jnp.exp(alog_ref[...]) * jax.nn.softplus(ab[:, 0:DN_HEADS] + dtb_ref[...])

    ri = lax.broadcasted_iota(I32, (r, r), 0)
    ci = lax.broadcasted_iota(I32, (r, r), 1)
    same = (ri // c) == (ci // c)
    lower = same & (ri >= ci)
    strict = same & (ri > ci)
    tri = lower.astype(F32)
    blk = same.astype(F32)
    eye = (ri == ci).astype(F32)
    rowhead = lax.broadcasted_iota(I32, (r, hd), 0) // c

    def stack(a, sub, gi, base):
        return jnp.concatenate([a[sub * c:(sub + 1) * c, base + (gi * g + j) * hd:base + (gi * g + j + 1) * hd]
                                for j in range(g)], axis=0)

    def column(a, sub, gi):
        return jnp.concatenate([a[sub * c:(sub + 1) * c, gi * g + j:gi * g + j + 1] for j in range(g)], axis=0)

    def own_block(m):
        acc = jnp.where(rowhead == 0, m[:, 0:hd], 0.0)
        for j in range(1, g):
            acc = acc + jnp.where(rowhead == j, m[:, j * hd:(j + 1) * hd], 0.0)
        return acc

    keys = [(sub, gi) for sub in range(nsub) for gi in range(ngrp)]
    grp = {}
    for sub, gi in keys:
        q = stack(qkv, sub, gi, 0)
        k = stack(qkv, sub, gi, DN_WIDTH)
        v = stack(qkv, sub, gi, 2 * DN_WIDTH)
        q = q * lax.rsqrt(jnp.sum(q * q, axis=-1, keepdims=True) + EPS) * (hd ** -0.5)
        k = k * lax.rsqrt(jnp.sum(k * k, axis=-1, keepdims=True) + EPS)
        beta = column(beta_all, sub, gi)
        gr = column(g_all, sub, gi)
        gcol = _fdot(tri, gr)
        grow = lax.dot_general(gr, tri, (((0,), (1,)), ((), ())), precision=HIGHEST,
                               preferred_element_type=F32)
        glast = _fdot(blk, gr)
        decay = jnp.exp(jnp.where(lower, gcol - grow, NEG_INF))
        kb = k * beta
        lm = jnp.where(strict, _dot3(kb, k, _NT) * decay, 0.0)
        grp[sub, gi] = dict(q=q, k=k, v=v, beta=beta, gcol=gcol, glast=glast, decay=decay, kb=kb, lm=lm)

    pair = ((ri // 2) == (ci // 2)) & (ri % 2 == 1) & (ci % 2 == 0)
    invs = {key: eye - jnp.where(pair, grp[key]["lm"], 0.0) for key in keys}
    s = 2
    while s < c:
        off = ((ri // (2 * s)) == (ci // (2 * s))) & ((ri // s) % 2 == 1) & ((ci // s) % 2 == 0)
        tmp = {key: _dot3(invs[key], jnp.where(off, grp[key]["lm"], 0.0)) for key in keys}
        invs = {key: invs[key] - _dot3(tmp[key], invs[key]) for key in keys}
        s *= 2

    for sub in range(nsub):
        outs = [None] * DN_HEADS
        for gi in range(ngrp):
            d = grp[sub, gi]
            eg = jnp.exp(d["gcol"])
            sol = _dot3(invs[sub, gi], jnp.concatenate([d["v"] * d["beta"], d["kb"] * eg], axis=1))
            u = sol[:, :hd]
            w = sol[:, hd:]
            qk = _dot3(d["q"], d["k"], _NT) * d["decay"]
            ktail = d["k"] * jnp.exp(d["glast"] - d["gcol"])
            qhead = d["q"] * eg
            st = state_ref[gi]
            vnew = u - own_block(_dot3(w, st))
            o = own_block(_dot3(qhead, st)) + _dot3(qk, vnew)
            vblk = jnp.concatenate([jnp.where(rowhead == j, vnew, 0.0) for j in range(g)], axis=1)
            elast = jnp.exp(d["glast"])
            keep = jnp.concatenate([jnp.broadcast_to(elast[j * c:j * c + 1, :], (1, hd)) for j in range(g)],
                                   axis=1)
            state_ref[gi] = st * keep + _dot3(ktail, vblk, _TN)
            zs = stack(z, sub, gi, 0)
            o = _rms(o, nrm_ref[...]) * (zs * jax.nn.sigmoid(zs))
            for j in range(g):
                outs[gi * g + j] = o[j * c:(j + 1) * c]
        o_ref[sub * c:(sub + 1) * c, :] = jnp.concatenate(outs, axis=-1)

    @pl.when(n == pl.num_programs(1) - 1)
    def _():
        for gi in range(ngrp):
            for j in range(g):
                sfin_ref[0, gi * g + j] = state_ref[gi][:, j * hd:(j + 1) * hd]


GDN_CHUNKS_PER_STEP = 2


def _gdn(dqkv, z, ab, conv_prefix, s0, conv_w, a_log, dt_bias, dn_norm, batch, seq):
    c = min(DN_CHUNK, seq)
    nchunk = seq // c
    assert seq % c == 0 and c % 8 == 0
    g = min(DN_HEADS, max(1, 128 // c))
    nsub = GDN_CHUNKS_PER_STEP if nchunk % GDN_CHUNKS_PER_STEP == 0 else 1
    nstep = nchunk // nsub
    rows = nsub * c
    tok = lambda b, n: (b * nstep + n, 0)
    fixed = lambda b, n: (0, 0)
    return pl.pallas_call(
        functools.partial(_gdn_kernel, c=c, g=g, nsub=nsub), name="gdn", grid=(batch, nstep),
        out_shape=[jax.ShapeDtypeStruct((batch * seq, DN_WIDTH), F32),
                   jax.ShapeDtypeStruct((batch, DN_HEADS, HEAD_DIM, HEAD_DIM), F32)],
        in_specs=[pl.BlockSpec((1, DN_HEADS), fixed), pl.BlockSpec((1, DN_HEADS), fixed),
                  pl.BlockSpec((rows, DN_QKV), tok), pl.BlockSpec((rows, DN_WIDTH), tok),
                  pl.BlockSpec((rows, 2 * DN_HEADS), tok),
                  pl.BlockSpec((CONV_W, DN_QKV), fixed), pl.BlockSpec((1, HEAD_DIM), fixed),
                  pl.BlockSpec((1, CONV_W - 1, DN_QKV), lambda b, n: (b, 0, 0)),
                  pl.BlockSpec((1, DN_HEADS, HEAD_DIM, HEAD_DIM), lambda b, n: (b, 0, 0, 0))],
        out_specs=[pl.BlockSpec((rows, DN_WIDTH), tok),
                   pl.BlockSpec((1, DN_HEADS, HEAD_DIM, HEAD_DIM), lambda b, n: (b, 0, 0, 0))],
        scratch_shapes=[pltpu.VMEM((8, DN_QKV), F32),
                        pltpu.VMEM((DN_HEADS // g, HEAD_DIM, g * HEAD_DIM), F32)],
        compiler_params=_cparams("parallel", "arbitrary"))(
            a_log.reshape(1, DN_HEADS), dt_bias.reshape(1, DN_HEADS), dqkv, z, ab, conv_w,
            dn_norm.reshape(1, HEAD_DIM), conv_prefix, s0)


def _outproj_kernel(x_ref, att_ref, dn_ref, wo_ref, g_ref, wq_ref, h_ref, c_ref, q_ref):
    mix = jnp.concatenate([att_ref[...], dn_ref[...]], axis=-1)
    h = x_ref[...] + jnp.dot(mix.astype(BF16), wo_ref[...], preferred_element_type=F32)
    h_ref[...] = h
    cn = _rms(h, g_ref[...])
    c_ref[...] = cn
    q_ref[...] = jnp.dot(cn.astype(BF16), wq_ref[...], preferred_element_type=F32)


def _outproj(x, att, dn, w_out, g_ffn, wq):
    t = x.shape[0]
    tm = _row_tile(t, 512)
    row = lambda i: (i, 0)
    fixed = lambda i: (0, 0)
    qw = PEER_HEADS * PEER_QDIM
    return pl.pallas_call(
        _outproj_kernel, name="outproj", grid=(t // tm,),
        out_shape=[jax.ShapeDtypeStruct((t, D_MODEL), F32), jax.ShapeDtypeStruct((t, D_MODEL), F32),
                   jax.ShapeDtypeStruct((t, qw), F32)],
        in_specs=[pl.BlockSpec((tm, D_MODEL), row), pl.BlockSpec((tm, ATT_Q), row),
                  pl.BlockSpec((tm, DN_WIDTH), row), pl.BlockSpec((D_MODEL, D_MODEL), fixed),
                  pl.BlockSpec((1, D_MODEL), fixed), pl.BlockSpec((D_MODEL, qw), fixed)],
        out_specs=[pl.BlockSpec((tm, D_MODEL), row), pl.BlockSpec((tm, D_MODEL), row),
                   pl.BlockSpec((tm, qw), row)],
        compiler_params=_cparams("parallel"))(x, att, dn, w_out, g_ffn, wq)


def _top16_rows(s, iota):
    big = jnp.float32(INDEX_SENTINEL)
    vals, idxs = [], []
    for _ in range(PEER_TOPK):
        m = jnp.max(s, axis=0, keepdims=True)
        i = jnp.min(jnp.where(s == m, iota, big), axis=0, keepdims=True)
        vals.append(m)
        idxs.append(i)
        s = jnp.where(iota == i, NEG_INF, s)
    return jnp.concatenate(vals, axis=0), jnp.concatenate(idxs, axis=0)


def _candidates(a, b, row8):
    t = a.shape[1]
    lo = row8 < 4
    bc = lambda r, n: jnp.broadcast_to(a[r:r + 1], (n, t))
    b8 = b[0:8]
    b44 = jnp.where(lo, b8, pltpu.roll(b8, 4, 0))
    parts = [bc(0, 8), bc(0, 8), bc(1, 8), bc(2, 8), bc(3, 8),
             jnp.where(lo, bc(4, 8), bc(5, 8)), jnp.where(lo, bc(6, 8), bc(7, 8)), a[8:16]]
    others = [b[0:8], b[8:16], b8, b8, b8, b44, b44, jnp.broadcast_to(b[0:1], (8, t))]
    return parts, others


def _flat_index_column():
    r = lax.broadcasted_iota(I32, (8, 1), 0)
    lo = r < 4
    cols = [r, 8 + r, 16 + r, 32 + r, 48 + r, jnp.where(lo, 64 + r, 80 + r - 4),
            jnp.where(lo, 96 + r, 112 + r - 4), (8 + r) * 16]
    return jnp.concatenate(cols, axis=0).astype(F32)


def _route_kernel(q_ref, k1_ref, k2_ref, idx_ref, gate_ref, *, tb):
    iota = lax.broadcasted_iota(I32, (N_KEYS, tb), 0).astype(F32)
    row8 = lax.broadcasted_iota(I32, (8, tb), 0)
    flat = jnp.broadcast_to(_flat_index_column(), (4 * PEER_TOPK, tb))
    big = jnp.float32(INDEX_SENTINEL)
    half = PEER_QDIM // 2
    for h in range(PEER_HEADS):
        q1 = q_ref[:, h * PEER_QDIM:h * PEER_QDIM + half]
        q2 = q_ref[:, h * PEER_QDIM + half:(h + 1) * PEER_QDIM]
        nt = (((1,), (1,)), ((), ()))
        s1 = lax.dot_general(k1_ref[h], q1.astype(BF16), nt, preferred_element_type=F32)
        s2 = lax.dot_general(k2_ref[h], q2.astype(BF16), nt, preferred_element_type=F32)
        v1, i1 = _top16_rows(s1, iota)
        v2, i2 = _top16_rows(s2, iota)
        pa, pb = _candidates(v1, v2, row8)
        cand = jnp.concatenate([x + y for x, y in zip(pa, pb)], axis=0)
        ea, eb = _candidates(i1, i2, row8)
        expert = jnp.concatenate([x * float(N_KEYS) + y for x, y in zip(ea, eb)], axis=0)
        scs, exs = [], []
        for _ in range(PEER_TOPK):
            m = jnp.max(cand, axis=0, keepdims=True)
            c = jnp.min(jnp.where(cand == m, flat, big), axis=0, keepdims=True)
            hit = flat == c
            exs.append(jnp.max(jnp.where(hit, expert, -1.0), axis=0, keepdims=True))
            scs.append(m)
            cand = jnp.where(hit, NEG_INF, cand)
        sc = jnp.concatenate(scs, axis=0)
        e = jnp.exp(sc - sc[0:1])
        gate_ref[h * PEER_TOPK:(h + 1) * PEER_TOPK, :] = e / jnp.sum(e, axis=0, keepdims=True)
        idx_ref[h * PEER_TOPK:(h + 1) * PEER_TOPK, :] = jnp.concatenate(exs, axis=0).astype(I32)


def _route(q, keys1, keys2):
    t = q.shape[0]
    tb = _row_tile(t, 256)
    kspec = pl.BlockSpec((PEER_HEADS, N_KEYS, PEER_QDIM // 2), lambda i: (0, 0, 0))
    return pl.pallas_call(
        functools.partial(_route_kernel, tb=tb), name="peer_route", grid=(t // tb,),
        out_shape=[jax.ShapeDtypeStruct((PEER_SLOTS, t), I32), jax.ShapeDtypeStruct((PEER_SLOTS, t), F32)],
        in_specs=[pl.BlockSpec((tb, PEER_HEADS * PEER_QDIM), lambda i: (i, 0)), kspec, kspec],
        out_specs=[pl.BlockSpec((PEER_SLOTS, tb), lambda i: (0, i)),
                   pl.BlockSpec((PEER_SLOTS, tb), lambda i: (0, i))],
        compiler_params=_cparams("parallel"))(q, keys1, keys2)


GATHER_SLOTS = 3
SC_WORKERS = 32
SC_GATHER_ROWS = 16
SC_BUFFERS = 6
STAGED_TOKENS_PER_STEP = 32


def _peer_token(packed, x, gate):
    u = lax.bitcast_convert_type(packed & jnp.uint32(0xFFFF0000), F32)
    v = lax.bitcast_convert_type(packed << 16, F32)
    p = u * x
    acc = p[:, 0:128]
    for j in range(1, D_MODEL // 128):
        acc = acc + p[:, j * 128:(j + 1) * 128]
    s = jnp.sum(acc, axis=-1, keepdims=True)
    act = 0.5 * s * (1.0 + lax.erf(s * (1.0 / math.sqrt(2.0))))
    return jnp.sum((gate * act) * v, axis=0, keepdims=True)


def _peer_apply_kernel(idx_ref, c_ref, h_ref, gt_ref, uv_ref, o_ref, buf, sem, *, tb):
    ahead = GATHER_SLOTS - 1

    def issue(t, slot):
        for k in range(PEER_SLOTS):
            e = idx_ref[k, t]
            pltpu.make_async_copy(uv_ref.at[e], buf.at[slot, pl.ds(k, 1), :],
                                  sem.at[slot]).start(priority=k % 2)

    def wait(slot):
        pltpu.make_async_copy(buf.at[slot], buf.at[slot], sem.at[slot]).wait()

    lane = lax.broadcasted_iota(I32, (PEER_SLOTS, tb), 1)

    def apply(t, slot):
        gate = jnp.sum(jnp.where(lane == t, gt_ref[...], 0.0), axis=-1, keepdims=True)
        y = _peer_token(buf[slot], c_ref[pl.ds(t, 1), :], gate)
        o_ref[pl.ds(t, 1), :] = h_ref[pl.ds(t, 1), :] + y

    for t0 in range(ahead):
        issue(t0, t0)

    def body(i, carry):
        for j in range(GATHER_SLOTS):
            t = i * GATHER_SLOTS + j
            issue(t + ahead, (j + ahead) % GATHER_SLOTS)
            wait(j)
            apply(t, j)
        return carry

    nmain = (tb - ahead) // GATHER_SLOTS
    lax.fori_loop(0, nmain, body, 0)
    for t in range(nmain * GATHER_SLOTS, tb):
        if t + ahead < tb:
            issue(t + ahead, (t + ahead) % GATHER_SLOTS)
        wait(t % GATHER_SLOTS)
        apply(t, t % GATHER_SLOTS)


def _pack_expert_rows(u, v):
    hi = lax.bitcast_convert_type(u.astype(BF16), jnp.uint16).astype(jnp.uint32) << 16
    lo = lax.bitcast_convert_type(v.astype(BF16), jnp.uint16).astype(jnp.uint32)
    return hi | lo


def _peer_apply_direct(idx_t, c, h, gate_t, uv_rows, t_direct):
    t = c.shape[0]
    tb = _row_tile(t_direct, 256)
    row = lambda i: (i, 0)
    return pl.pallas_call(
        functools.partial(_peer_apply_kernel, tb=tb), name="peer_apply", grid=(t_direct // tb,),
        out_shape=jax.ShapeDtypeStruct((t, D_MODEL), F32),
        in_specs=[pl.BlockSpec((PEER_SLOTS, tb), lambda i: (0, i), memory_space=pltpu.SMEM),
                  pl.BlockSpec((tb, D_MODEL), row), pl.BlockSpec((tb, D_MODEL), row),
                  pl.BlockSpec((PEER_SLOTS, tb), lambda i: (0, i)),
                  pl.BlockSpec(memory_space=pl.ANY)],
        out_specs=pl.BlockSpec((tb, D_MODEL), row),
        scratch_shapes=[pltpu.VMEM((GATHER_SLOTS, PEER_SLOTS, D_MODEL), jnp.uint32),
                        pltpu.SemaphoreType.DMA((GATHER_SLOTS,))],
        compiler_params=_cparams("arbitrary"))(idx_t, c, h, gate_t, uv_rows)


def _sc_stage_rows(table, idx):
    t_total, nchunk_tok, _ = idx.shape
    d = table.shape[1]
    tpw = t_total // SC_WORKERS
    assert t_total % SC_WORKERS == 0 and tpw >= 2
    nb = SC_BUFFERS
    ahead = nb // 2
    params = pltpu.CompilerParams()
    if "needs_layout_passes" in pltpu.CompilerParams.__dataclass_fields__:
        params = dataclasses.replace(params, needs_layout_passes=False)
    mesh = plsc.VectorSubcoreMesh(core_axis_name="c", subcore_axis_name="s")

    @functools.partial(
        pl.kernel, mesh=mesh, compiler_params=params,
        out_type=jax.ShapeDtypeStruct((t_total * nchunk_tok * SC_GATHER_ROWS, d), table.dtype),
        scratch_types=[pltpu.VMEM((2, nchunk_tok, SC_GATHER_ROWS), I32),
                       pltpu.VMEM((nb, SC_GATHER_ROWS, d), table.dtype),
                       pltpu.SemaphoreType.DMA((nb,)), pltpu.SemaphoreType.DMA((nb,)),
                       pltpu.SemaphoreType.DMA((2,))])
    def stage(table_hbm, idx_hbm, out_hbm, idx_v, rows_v, sem_g, sem_w, sem_t):
        wid = lax.axis_index("s") * 2 + lax.axis_index("c")
        base = wid * tpw
        nchunks = tpw * nchunk_tok

        def gather(g, slot):
            ts = (g // nchunk_tok) % 2
            return pltpu.make_async_copy(table_hbm.at[idx_v.at[ts, g % nchunk_tok]], rows_v.at[slot],
                                         sem_g.at[slot])

        def write(g, slot):
            dst = out_hbm.at[pl.ds((base * nchunk_tok + g) * SC_GATHER_ROWS, SC_GATHER_ROWS)]
            return pltpu.make_async_copy(rows_v.at[slot], dst, sem_w.at[slot])

        def idx_copy(i, ts):
            return pltpu.make_async_copy(idx_hbm.at[base + i], idx_v.at[ts], sem_t.at[ts])

        idx_copy(0, 0).start()
        idx_copy(0, 0).wait()
        idx_copy(1, 1).start()
        for g0 in range(ahead):
            gather(g0, g0).start()

        @pl.loop(0, nchunks)
        def _(g):
            slot = g % nb
            nxt = g + ahead
            nslot = nxt % nb

            @pl.when(nxt < nchunks)
            def _():
                @pl.when(nxt >= nb)
                def _():
                    write(nxt - nb, nslot).wait()

                @pl.when(nxt % nchunk_tok == 0)
                def _():
                    idx_copy(nxt // nchunk_tok, (nxt // nchunk_tok) % 2).wait()

                gather(nxt, nslot).start()

            gather(g, slot).wait()
            write(g, slot).start()

            @pl.when((g % nchunk_tok == nchunk_tok - 1) & (g // nchunk_tok + 2 < tpw))
            def _():
                idx_copy(g // nchunk_tok + 2, (g // nchunk_tok) % 2).start()

        for j in range(nb):
            gl = nchunks - nb + j
            write(gl, gl % nb).wait()

    return stage(table, idx)


def _peer_staged_kernel(st_ref, c_ref, h_ref, gt_ref, *rest, tbs):
    o_ref = rest[-1]
    i = pl.program_id(0)
    lane = lax.broadcasted_iota(I32, (PEER_SLOTS, 128), 1)
    for t in range(tbs):
        tg = i * tbs + t
        col0 = pl.multiple_of((tg // 128) * 128, 128)
        gate = jnp.sum(jnp.where(lane == tg % 128, gt_ref[:, pl.ds(col0, 128)], 0.0), axis=-1, keepdims=True)
        y = _peer_token(st_ref[t], c_ref[pl.ds(t, 1), :], gate)
        o_ref[pl.ds(t, 1), :] = h_ref[pl.ds(t, 1), :] + y


def _peer_apply_staged(staged, c, h, gate_t_staged, partial, start):
    t_staged = staged.shape[0]
    tbs = STAGED_TOKENS_PER_STEP
    assert t_staged % tbs == 0 and start % tbs == 0 and t_staged % 128 == 0
    off = start // tbs
    row = lambda i: (i + off, 0)
    in_specs = [pl.BlockSpec((tbs, PEER_SLOTS, D_MODEL), lambda i: (i, 0, 0)),
                pl.BlockSpec((tbs, D_MODEL), row), pl.BlockSpec((tbs, D_MODEL), row),
                pl.BlockSpec((PEER_SLOTS, t_staged), lambda i: (0, 0))]
    args = [staged, c, h, gate_t_staged]
    aliases = {}
    if partial is not None:
        in_specs.append(pl.BlockSpec(memory_space=pl.ANY))
        args.append(partial)
        aliases = {4: 0}
    return pl.pallas_call(
        functools.partial(_peer_staged_kernel, tbs=tbs), name="peer_apply_staged", grid=(t_staged // tbs,),
        out_shape=jax.ShapeDtypeStruct(c.shape, F32),
        in_specs=in_specs, out_specs=pl.BlockSpec((tbs, D_MODEL), row),
        input_output_aliases=aliases,
        compiler_params=_cparams("arbitrary"))(*args)


def _peer_apply(idx_t, c, h, gate_t, uv, staged_pieces):
    t = c.shape[0]
    start = t - sum(staged_pieces)
    out = _peer_apply_direct(idx_t, c, h, gate_t, uv[:, None, :], start) if start else None
    for n in staged_pieces:
        idx_sc = idx_t[:, start:start + n].T.reshape(n, PEER_SLOTS // SC_GATHER_ROWS, SC_GATHER_ROWS)
        staged = _sc_stage_rows(uv, idx_sc).reshape(n, PEER_SLOTS, D_MODEL)
        out = _peer_apply_staged(staged, c, h, gate_t[:, start:start + n], out, start)
        start += n
    return out


def _final_kernel(h_ref, p_ref, gple_ref, wg_ref, wp_ref, gfin_ref, y_ref):
    h = h_ref[...]
    e = _rms(h, gple_ref[...])
    gate = jax.nn.sigmoid(jnp.dot(e.astype(BF16), wg_ref[...], preferred_element_type=F32))
    up = jnp.dot(p_ref[...].astype(BF16), wp_ref[...], preferred_element_type=F32)
    y_ref[...] = _rms(h + up * gate, gfin_ref[...])


def _final_into_kernel(h_ref, p_ref, gple_ref, wg_ref, wp_ref, gfin_ref, prev_ref, y_ref):
    del prev_ref
    _final_kernel(h_ref, p_ref, gple_ref, wg_ref, wp_ref, gfin_ref, y_ref)


def _final(h, p, g_ple, w_gate, w_ple, g_final, into=None, row0=0, total=None):
    t = h.shape[0]
    total = t if total is None else total
    tm = _row_tile(t, 512)
    assert row0 % tm == 0
    row = lambda i: (i, 0)
    out_row = lambda i: (i + row0 // tm, 0)
    fixed = lambda i: (0, 0)
    in_specs = [pl.BlockSpec((tm, D_MODEL), row), pl.BlockSpec((tm, PLE_DIM), row),
                pl.BlockSpec((1, D_MODEL), fixed), pl.BlockSpec((D_MODEL, D_MODEL), fixed),
                pl.BlockSpec((PLE_DIM, D_MODEL), fixed), pl.BlockSpec((1, D_MODEL), fixed)]
    args = [h, p, g_ple, w_gate, w_ple, g_final]
    body = _final_kernel
    aliases = {}
    if into is not None:
        in_specs.append(pl.BlockSpec(memory_space=pl.ANY))
        args.append(into)
        aliases = {len(args) - 1: 0}
        body = _final_into_kernel
    return pl.pallas_call(
        body, name="ple_final", grid=(t // tm,),
        out_shape=jax.ShapeDtypeStruct((total, D_MODEL), F32),
        in_specs=in_specs, out_specs=pl.BlockSpec((tm, D_MODEL), out_row),
        input_output_aliases=aliases,
        compiler_params=_cparams("parallel"))(*args)


STAGED_FRACTIONS = (1.0, 0.75, 0.75, 0.75)
STAGED_UNIT = SC_WORKERS * 32
STAGED_PIECE = 8 * STAGED_UNIT


def _staged_tokens(t, fraction):
    total = int(t * fraction) // STAGED_UNIT * STAGED_UNIT
    if total < STAGED_UNIT or t < STAGED_PIECE:
        return ()
    pieces = [STAGED_PIECE] * (total // STAGED_PIECE)
    if total % STAGED_PIECE:
        pieces.append(total % STAGED_PIECE)
    return tuple(pieces)


def _layer_front(x, kv_buf, conv_prefix, s0, wts, batch, seq):
    t = batch * seq
    q, k, v, dqkv, z, ab = _inproj(x, wts["norm_mix"], wts["w_in"])
    if kv_buf is None:
        att = _swa_prompt(q, k, v, wts["sinks"], batch, seq)
        k_new = k.reshape(batch, seq, ATT_KV)[:, -WINDOW:]
        v_new = v.reshape(batch, seq, ATT_KV)[:, -WINDOW:]
    else:
        att, k_new, v_new = _swa_sample(q.reshape(batch, seq, ATT_Q), k.reshape(batch, seq, ATT_KV),
                                        v.reshape(batch, seq, ATT_KV), kv_buf[0], kv_buf[1], wts["sinks"])
        att = att.reshape(t, ATT_Q)
    dn, s_new = _gdn(dqkv, z, ab, conv_prefix, s0, wts["conv_w"], wts["a_log"], wts["dt_bias"],
                     wts["dn_norm"], batch, seq)
    conv_new = dqkv.reshape(batch, seq, DN_QKV)[:, -(CONV_W - 1):]
    h, c, pq = _outproj(x, att, dn, wts["w_out"], wts["norm_ffn"], wts["peer_wq"])
    idx_t, gate_t = _route(pq, wts["keys1"], wts["keys2"])
    return dict(h=h, c=c, idx_t=idx_t, gate_t=gate_t, state=(k_new, v_new, conv_new, s_new))


def _layer_back(front, p, wts, staged_fraction=0.0, into=None, row0=0, total=None):
    t = front["c"].shape[0]
    h2 = _peer_apply(front["idx_t"], front["c"], front["h"], front["gate_t"], wts["uv"],
                     _staged_tokens(t, staged_fraction))
    return _final(h2, p, wts["norm_ple"], wts["ple_gate"], wts["ple_in"], wts["norm_final"], into, row0, total)


PROMPT_GROUPS = 4


def kernel(x_prompt, x_sample, p_prompt, p_sample, cache_swa_k, cache_swa_v, state_conv, state_delta, norm_mix, w_in, conv_w, attn_sinks, dn_a_log, dn_dt_bias, dn_norm, w_out, norm_ffn, peer_wq, peer_keys1, peer_keys2, peer_u, peer_v, norm_ple, ple_in, ple_gate, norm_final):
    depth = w_in.shape[0]
    assert depth == 1
    bp, lp = x_prompt.shape[0], x_prompt.shape[1]
    bs, ls = x_sample.shape[0], x_sample.shape[1]
    l = 0
    wts = dict(
        norm_mix=norm_mix[l].reshape(1, D_MODEL),
        w_in=jnp.pad(w_in[l], ((0, 0), (0, IN_WIDTH_PADDED - IN_WIDTH))).astype(BF16),
        conv_w=conv_w[l], sinks=attn_sinks[l], a_log=dn_a_log[l], dt_bias=dn_dt_bias[l], dn_norm=dn_norm[l],
        w_out=w_out[l].astype(BF16), norm_ffn=norm_ffn[l].reshape(1, D_MODEL), peer_wq=peer_wq[l].astype(BF16),
        keys1=peer_keys1[l].astype(BF16), keys2=peer_keys2[l].astype(BF16),
        uv=_pack_expert_rows(peer_u[l], peer_v[l]),
        norm_ple=norm_ple[l].reshape(1, D_MODEL), ple_in=ple_in[l].astype(BF16),
        ple_gate=ple_gate[l].astype(BF16), norm_final=norm_final.reshape(1, D_MODEL))
    ngrp = PROMPT_GROUPS if bp % PROMPT_GROUPS == 0 else 1
    bg = bp // ngrp
    fronts = []
    for gi in range(ngrp):
        xg = x_prompt[gi * bg:(gi + 1) * bg].reshape(bg * lp, D_MODEL)
        fronts.append(_layer_front(xg, None, jnp.zeros((bg, CONV_W - 1, DN_QKV), F32),
                                   jnp.zeros((bg, DN_HEADS, HEAD_DIM, HEAD_DIM), F32), wts, bg, lp))
    kv_buf = (cache_swa_k[l].reshape(bs, WINDOW, ATT_KV), cache_swa_v[l].reshape(bs, WINDOW, ATT_KV))
    front_s = _layer_front(x_sample.reshape(bs * ls, D_MODEL), kv_buf, state_conv[l], state_delta[l], wts, bs, ls)
    yp = None
    for gi in range(ngrp):
        pg = p_prompt[l, gi * bg:(gi + 1) * bg].reshape(bg * lp, PLE_DIM)
        yp = _layer_back(fronts[gi], pg, wts, STAGED_FRACTIONS[gi % len(STAGED_FRACTIONS)], yp, gi * bg * lp, bp * lp)
    ys = _layer_back(front_s, p_sample[l].reshape(bs * ls, PLE_DIM), wts)
    kp, vp, cp, sp = (jnp.concatenate([f["state"][j] for f in fronts], axis=0) for j in range(4))
    kn, vn, cn, sn = front_s["state"]
    kvshape = lambda b: (1, b, WINDOW, ATT_KV_HEADS, HEAD_DIM)
    return (yp.reshape(bp, lp, D_MODEL), ys.reshape(bs, ls, D_MODEL),
            kp.reshape(kvshape(bp)), vp.reshape(kvshape(bp)), cp[None], sp[None],
            kn.reshape(kvshape(bs)), vn.reshape(kvshape(bs)), cn[None], sn[None])
```

```python
import dataclasses
import functools
import math

import jax
import jax.numpy as jnp
from jax import lax
from jax.experimental import pallas as pl
from jax.experimental.pallas import tpu as pltpu
from jax.experimental.pallas import tpu_sc as plsc

F32 = jnp.float32
BF16 = jnp.bfloat16
I32 = jnp.int32
HIGHEST = lax.Precision.HIGHEST

D_MODEL = 1024
HEAD_DIM = 64
ATT_HEADS = 8
ATT_KV_HEADS = 2
ATT_GROUP = 4
WINDOW = 128
ATT_Q = ATT_HEADS * HEAD_DIM
ATT_KV = ATT_KV_HEADS * HEAD_DIM
DN_HEADS = 8
DN_WIDTH = DN_HEADS * HEAD_DIM
DN_QKV = 3 * DN_WIDTH
CONV_W = 4
DN_CHUNK = 64
IN_WIDTH = ATT_Q + 2 * ATT_KV + DN_QKV + DN_WIDTH + 2 * DN_HEADS
IN_WIDTH_PADDED = 2944
N_KEYS = 128
PEER_HEADS = 8
PEER_QDIM = 256
PEER_TOPK = 16
PEER_SLOTS = PEER_HEADS * PEER_TOPK
PLE_DIM = 256
EPS = 1e-6
NEG_INF = float("-inf")
INDEX_SENTINEL = 1.0e9

VMEM_LIMIT = 48 * 1024 * 1024


def _cparams(*sem):
    return pltpu.CompilerParams(dimension_semantics=sem, vmem_limit_bytes=VMEM_LIMIT)


def _rms(x, g):
    return x * lax.rsqrt(jnp.mean(x * x, axis=-1, keepdims=True) + EPS) * g


def _bdot(a, b):
    return jnp.dot(a.astype(BF16), b.astype(BF16), preferred_element_type=F32)


def _fdot(a, b):
    return jnp.dot(a, b, precision=HIGHEST, preferred_element_type=F32)


_NN = (((1,), (0,)), ((), ()))
_NT = (((1,), (1,)), ((), ()))
_TN = (((0,), (0,)), ((), ()))


def _row_tile(t, target):
    tile = min(t, target)
    assert t % tile == 0, (t, tile)
    return tile


def _inproj_kernel(x_ref, g_ref, w_ref, q_ref, k_ref, v_ref, dqkv_ref, z_ref, ab_ref):
    a = _rms(x_ref[...], g_ref[...])
    p = jnp.dot(a.astype(BF16), w_ref[...], preferred_element_type=F32)
    c0, c1, c2, c3, c4 = ATT_Q, ATT_Q + ATT_KV, ATT_Q + 2 * ATT_KV, ATT_Q + 2 * ATT_KV + DN_QKV, \
        ATT_Q + 2 * ATT_KV + DN_QKV + DN_WIDTH
    q_ref[...] = p[:, :c0]
    k_ref[...] = p[:, c0:c1]
    v_ref[...] = p[:, c1:c2]
    dqkv_ref[...] = p[:, c2:c3]
    z_ref[...] = p[:, c3:c4]
    ab_ref[...] = p[:, c4:c4 + 2 * DN_HEADS]


def _inproj(x, g, w_pad):
    t = x.shape[0]
    tm = _row_tile(t, 512)
    widths = (ATT_Q, ATT_KV, ATT_KV, DN_QKV, DN_WIDTH, 2 * DN_HEADS)
    row = lambda i: (i, 0)
    fixed = lambda i: (0, 0)
    return pl.pallas_call(
        _inproj_kernel, name="inproj", grid=(t // tm,),
        out_shape=[jax.ShapeDtypeStruct((t, w), F32) for w in widths],
        in_specs=[pl.BlockSpec((tm, D_MODEL), row), pl.BlockSpec((1, D_MODEL), fixed),
                  pl.BlockSpec((D_MODEL, IN_WIDTH_PADDED), fixed)],
        out_specs=[pl.BlockSpec((tm, w), row) for w in widths],
        compiler_params=_cparams("parallel"))(x, g, w_pad)


def _softmax_sink_pv(s, sink, vband):
    m = jnp.maximum(jnp.max(s, axis=-1, keepdims=True), sink)
    e = jnp.exp(s - m)
    denom = jnp.sum(e, axis=-1, keepdims=True) + jnp.exp(sink - m)
    return _bdot(e / denom, vband)


def _alibi_slope(h):
    return 2.0 ** (-8.0 * (h + 1) / ATT_HEADS)


def _group_columns(kh, rows, sink_ref):
    rg = lax.broadcasted_iota(I32, (ATT_GROUP * rows, 1), 0) // rows
    slope = jnp.full((ATT_GROUP * rows, 1), _alibi_slope(kh * ATT_GROUP), F32)
    sink = jnp.full((ATT_GROUP * rows, 1), sink_ref[kh * ATT_GROUP], F32)
    for j in range(1, ATT_GROUP):
        slope = jnp.where(rg == j, _alibi_slope(kh * ATT_GROUP + j), slope)
        sink = jnp.where(rg == j, sink_ref[kh * ATT_GROUP + j], sink)
    return slope, sink


def _attend_kv_head(q_all, kband, vband, kh, dist, valid, sink_ref):
    rows = q_all.shape[0]
    heads = range(kh * ATT_GROUP, (kh + 1) * ATT_GROUP)
    q4 = jnp.concatenate([q_all[:, h * HEAD_DIM:(h + 1) * HEAD_DIM] for h in heads], axis=0)
    k = kband[:, kh * HEAD_DIM:(kh + 1) * HEAD_DIM]
    v = vband[:, kh * HEAD_DIM:(kh + 1) * HEAD_DIM]
    s = lax.dot_general(q4.astype(BF16), k.astype(BF16), _NT, preferred_element_type=F32) * (HEAD_DIM ** -0.5)
    slope, sink = _group_columns(kh, rows, sink_ref)
    s = jnp.where(valid, s - slope * dist, NEG_INF)
    o4 = _softmax_sink_pv(s, sink, v)
    return [o4[j * rows:(j + 1) * rows] for j in range(ATT_GROUP)]


def _swa_prompt_kernel(sink_ref, q_ref, kc_ref, kp_ref, vc_ref, vp_ref, o_ref):
    n = pl.program_id(1)
    rows = ATT_GROUP * WINDOW
    qi = lax.broadcasted_iota(I32, (rows, 2 * WINDOW), 0) % WINDOW
    kj = lax.broadcasted_iota(I32, (rows, 2 * WINDOW), 1)
    dist = WINDOW + qi - kj
    kpos = n * WINDOW - WINDOW + kj
    valid = (dist >= 0) & (dist <= WINDOW) & (kpos >= 0)
    distf = dist.astype(F32)
    kband = jnp.concatenate([kp_ref[...], kc_ref[...]], axis=0)
    vband = jnp.concatenate([vp_ref[...], vc_ref[...]], axis=0)
    q_all = q_ref[...]
    outs = []
    for kh in range(ATT_KV_HEADS):
        outs += _attend_kv_head(q_all, kband, vband, kh, distf, valid, sink_ref)
    o_ref[...] = jnp.concatenate(outs, axis=-1)


def _swa_prompt(q, k, v, sinks, batch, seq):
    nblk = seq // WINDOW
    cur = lambda b, n: (b * nblk + n, 0)
    prev = lambda b, n: (b * nblk + jnp.maximum(n - 1, 0), 0)
    return pl.pallas_call(
        _swa_prompt_kernel, name="swa_prompt", grid=(batch, nblk),
        out_shape=jax.ShapeDtypeStruct((batch * seq, ATT_Q), F32),
        in_specs=[pl.BlockSpec(memory_space=pltpu.SMEM),
                  pl.BlockSpec((WINDOW, ATT_Q), cur),
                  pl.BlockSpec((WINDOW, ATT_KV), cur), pl.BlockSpec((WINDOW, ATT_KV), prev),
                  pl.BlockSpec((WINDOW, ATT_KV), cur), pl.BlockSpec((WINDOW, ATT_KV), prev)],
        out_specs=pl.BlockSpec((WINDOW, ATT_Q), cur),
        compiler_params=_cparams("parallel", "parallel"))(sinks, q, k, k, v, v)


def _swa_sample_kernel(sink_ref, q_ref, k_ref, v_ref, kb_ref, vb_ref, o_ref, ko_ref, vo_ref, *, bt, seq):
    klen = WINDOW + seq
    rows = ATT_GROUP * seq
    qi = lax.broadcasted_iota(I32, (rows, klen), 0) % seq
    kj = lax.broadcasted_iota(I32, (rows, klen), 1)
    dist = qi - (kj - WINDOW)
    valid = (dist >= 0) & (dist <= WINDOW)
    distf = dist.astype(F32)
    for b in range(bt):
        kc = jnp.concatenate([kb_ref[b], k_ref[b]], axis=0)
        vc = jnp.concatenate([vb_ref[b], v_ref[b]], axis=0)
        ko_ref[b] = kc[seq:]
        vo_ref[b] = vc[seq:]
        q_all = q_ref[b]
        outs = []
        for kh in range(ATT_KV_HEADS):
            outs += _attend_kv_head(q_all, kc, vc, kh, distf, valid, sink_ref)
        o_ref[b] = jnp.concatenate(outs, axis=-1)


def _swa_sample(q, k, v, k_buf, v_buf, sinks):
    batch, seq = q.shape[0], q.shape[1]
    bt = _row_tile(batch, 8)
    blk = lambda w, r: pl.BlockSpec((bt, r, w), lambda i: (i, 0, 0))
    return pl.pallas_call(
        functools.partial(_swa_sample_kernel, bt=bt, seq=seq), name="swa_sample", grid=(batch // bt,),
        out_shape=[jax.ShapeDtypeStruct((batch, seq, ATT_Q), F32),
                   jax.ShapeDtypeStruct((batch, WINDOW, ATT_KV), F32),
                   jax.ShapeDtypeStruct((batch, WINDOW, ATT_KV), F32)],
        in_specs=[pl.BlockSpec(memory_space=pltpu.SMEM), blk(ATT_Q, seq), blk(ATT_KV, seq), blk(ATT_KV, seq),
                  blk(ATT_KV, WINDOW), blk(ATT_KV, WINDOW)],
        out_specs=[blk(ATT_Q, seq), blk(ATT_KV, WINDOW), blk(ATT_KV, WINDOW)],
        compiler_params=_cparams("parallel"))(sinks, q, k, v, k_buf, v_buf)


def _split_bf16(a, cache):
    if id(a) not in cache:
        hi = a.astype(BF16)
        cache[id(a)] = (a, (hi, (a - hi.astype(F32)).astype(BF16)))
    return cache[id(a)][1]


MXU_DEPTH = 256


def _dot3_impl(a, b, dims=_NN, *, cache):
    (ca,), (cb,) = dims[0]
    depth = a.shape[ca]
    ah, al = _split_bf16(a, cache)
    bh, bl = _split_bf16(b, cache)
    f = lambda x, y: lax.dot_general(x, y, dims, preferred_element_type=F32)
    if 3 * depth <= MXU_DEPTH:
        return f(jnp.concatenate([ah, al, ah], axis=ca), jnp.concatenate([bh, bh, bl], axis=cb))
    return f(jnp.concatenate([ah, al], axis=ca), jnp.concatenate([bh, bh], axis=cb)) + f(ah, bl)


def _gdn_kernel(alog_ref, dtb_ref, dqkv_ref, z_ref, ab_ref, cw_ref, nrm_ref, pre_ref, s0_ref,
                o_ref, sfin_ref, tail_ref, state_ref, *, c, g, nsub):
    n = pl.program_id(1)
    ngrp = DN_HEADS // g
    r = g * c
    rows = nsub * c
    hd = HEAD_DIM
    _dot3 = functools.partial(_dot3_impl, cache={})

    @pl.when(n == 0)
    def _():
        tail_ref[...] = jnp.concatenate([jnp.zeros((8 - (CONV_W - 1), DN_QKV), F32), pre_ref[0]], axis=0)
        for gi in range(ngrp):
            state_ref[gi] = jnp.concatenate([s0_ref[0, gi * g + j] for j in range(g)], axis=1)

    x = dqkv_ref[...]
    hist = jnp.concatenate([tail_ref[...], x], axis=0)
    tail_ref[...] = hist[rows:]
    cw = cw_ref[...]
    y = x * cw[CONV_W - 1:CONV_W]
    for d in range(1, CONV_W):
        y = y + hist[8 - d:8 - d + rows] * cw[CONV_W - 1 - d:CONV_W - d]
    qkv = y * jax.nn.sigmoid(y)
    z = z_ref[...]
    ab = ab_ref[...]
    beta_all = jax.nn.sigmoid(ab[:, DN_HEADS:2 * DN_HEADS])
    g_all = -jnp.exp(alog_ref[...]) * jax.nn.softplus(ab[:, 0:DN_HEADS] + dtb_ref[...])

    ri = lax.broadcasted_iota(I32, (r, r), 0)
    ci = lax.broadcasted_iota(I32, (r, r), 1)
    same = (ri // c) == (ci // c)
    lower = same & (ri >= ci)
    strict = same & (ri > ci)
    tri = lower.astype(F32)
    blk = same.astype(F32)
    eye = (ri == ci).astype(F32)
    rowhead = lax.broadcasted_iota(I32, (r, hd), 0) // c

    def stack(a, sub, gi, base):
        return jnp.concatenate([a[sub * c:(sub + 1) * c, base + (gi * g + j) * hd:base + (gi * g + j + 1) * hd]
                                for j in range(g)], axis=0)

    def column(a, sub, gi):
        return jnp.concatenate([a[sub * c:(sub + 1) * c, gi * g + j:gi * g + j + 1] for j in range(g)], axis=0)

    def own_block(m):
        acc = jnp.where(rowhead == 0, m[:, 0:hd], 0.0)
        for j in range(1, g):
            acc = acc + jnp.where(rowhead == j, m[:, j * hd:(j + 1) * hd], 0.0)
        return acc

    keys = [(sub, gi) for sub in range(nsub) for gi in range(ngrp)]
    grp = {}
    for sub, gi in keys:
        q = stack(qkv, sub, gi, 0)
        k = stack(qkv, sub, gi, DN_WIDTH)
        v = stack(qkv, sub, gi, 2 * DN_WIDTH)
        q = q * lax.rsqrt(jnp.sum(q * q, axis=-1, keepdims=True) + EPS) * (hd ** -0.5)
        k = k * lax.rsqrt(jnp.sum(k * k, axis=-1, keepdims=True) + EPS)
        beta = column(beta_all, sub, gi)
        gr = column(g_all, sub, gi)
        gcol = _fdot(tri, gr)
        grow = lax.dot_general(gr, tri, (((0,), (1,)), ((), ())), precision=HIGHEST,
                               preferred_element_type=F32)
        glast = _fdot(blk, gr)
        decay = jnp.exp(jnp.where(lower, gcol - grow, NEG_INF))
        kb = k * beta
        lm = jnp.where(strict, _dot3(kb, k, _NT) * decay, 0.0)
        grp[sub, gi] = dict(q=q, k=k, v=v, beta=beta, gcol=gcol, glast=glast, decay=decay, kb=kb, lm=lm)

    pair = ((ri // 2) == (ci // 2)) & (ri % 2 == 1) & (ci % 2 == 0)
    invs = {key: eye - jnp.where(pair, grp[key]["lm"], 0.0) for key in keys}
    s = 2
    while s < c:
        off = ((ri // (2 * s)) == (ci // (2 * s))) & ((ri // s) % 2 == 1) & ((ci // s) % 2 == 0)
        tmp = {key: _dot3(invs[key], jnp.where(off, grp[key]["lm"], 0.0)) for key in keys}
        invs = {key: invs[key] - _dot3(tmp[key], invs[key]) for key in keys}
        s *= 2

    for sub in range(nsub):
        outs = [None] * DN_HEADS
        for gi in range(ngrp):
            d = grp[sub, gi]
            eg = jnp.exp(d["gcol"])
            sol = _dot3(invs[sub, gi], jnp.concatenate([d["v"] * d["beta"], d["kb"] * eg], axis=1))
            u = sol[:, :hd]
            w = sol[:, hd:]
            qk = _dot3(d["q"], d["k"], _NT) * d["decay"]
            ktail = d["k"] * jnp.exp(d["glast"] - d["gcol"])
            qhead = d["q"] * eg
            st = state_ref[gi]
            vnew = u - own_block(_dot3(w, st))
            o = own_block(_dot3(qhead, st)) + _dot3(qk, vnew)
            vblk = jnp.concatenate([jnp.where(rowhead == j, vnew, 0.0) for j in range(g)], axis=1)
            elast = jnp.exp(d["glast"])
            keep = jnp.concatenate([jnp.broadcast_to(elast[j * c:j * c + 1, :], (1, hd)) for j in range(g)],
                                   axis=1)
            state_ref[gi] = st * keep + _dot3(ktail, vblk, _TN)
            zs = stack(z, sub, gi, 0)
            o = _rms(o, nrm_ref[...]) * (zs * jax.nn.sigmoid(zs))
            for j in range(g):
                outs[gi * g + j] = o[j * c:(j + 1) * c]
        o_ref[sub * c:(sub + 1) * c, :] = jnp.concatenate(outs, axis=-1)

    @pl.when(n == pl.num_programs(1) - 1)
    def _():
        for gi in range(ngrp):
            for j in range(g):
                sfin_ref[0, gi * g + j] = state_ref[gi][:, j * hd:(j + 1) * hd]


GDN_CHUNKS_PER_STEP = 2


def _gdn(dqkv, z, ab, conv_prefix, s0, conv_w, a_log, dt_bias, dn_norm, batch, seq):
    c = min(DN_CHUNK, seq)
    nchunk = seq // c
    assert seq % c == 0 and c % 8 == 0
    g = min(DN_HEADS, max(1, 128 // c))
    nsub = GDN_CHUNKS_PER_STEP if nchunk % GDN_CHUNKS_PER_STEP == 0 else 1
    nstep = nchunk // nsub
    rows = nsub * c
    tok = lambda b, n: (b * nstep + n, 0)
    fixed = lambda b, n: (0, 0)
    return pl.pallas_call(
        functools.partial(_gdn_kernel, c=c, g=g, nsub=nsub), name="gdn", grid=(batch, nstep),
        out_shape=[jax.ShapeDtypeStruct((batch * seq, DN_WIDTH), F32),
                   jax.ShapeDtypeStruct((batch, DN_HEADS, HEAD_DIM, HEAD_DIM), F32)],
        in_specs=[pl.BlockSpec((1, DN_HEADS), fixed), pl.BlockSpec((1, DN_HEADS), fixed),
                  pl.BlockSpec((rows, DN_QKV), tok), pl.BlockSpec((rows, DN_WIDTH), tok),
                  pl.BlockSpec((rows, 2 * DN_HEADS), tok),
                  pl.BlockSpec((CONV_W, DN_QKV), fixed), pl.BlockSpec((1, HEAD_DIM), fixed),
                  pl.BlockSpec((1, CONV_W - 1, DN_QKV), lambda b, n: (b, 0, 0)),
                  pl.BlockSpec((1, DN_HEADS, HEAD_DIM, HEAD_DIM), lambda b, n: (b, 0, 0, 0))],
        out_specs=[pl.BlockSpec((rows, DN_WIDTH), tok),
                   pl.BlockSpec((1, DN_HEADS, HEAD_DIM, HEAD_DIM), lambda b, n: (b, 0, 0, 0))],
        scratch_shapes=[pltpu.VMEM((8, DN_QKV), F32),
                        pltpu.VMEM((DN_HEADS // g, HEAD_DIM, g * HEAD_DIM), F32)],
        compiler_params=_cparams("parallel", "arbitrary"))(
            a_log.reshape(1, DN_HEADS), dt_bias.reshape(1, DN_HEADS), dqkv, z, ab, conv_w,
            dn_norm.reshape(1, HEAD_DIM), conv_prefix, s0)


def _outproj_kernel(x_ref, att_ref, dn_ref, wo_ref, g_ref, wq_ref, h_ref, c_ref, q_ref):
    mix = jnp.concatenate([att_ref[...], dn_ref[...]], axis=-1)
    h = x_ref[...] + jnp.dot(mix.astype(BF16), wo_ref[...], preferred_element_type=F32)
    h_ref[...] = h
    cn = _rms(h, g_ref[...])
    c_ref[...] = cn
    q_ref[...] = jnp.dot(cn.astype(BF16), wq_ref[...], preferred_element_type=F32)


def _outproj(x, att, dn, w_out, g_ffn, wq):
    t = x.shape[0]
    tm = _row_tile(t, 512)
    row = lambda i: (i, 0)
    fixed = lambda i: (0, 0)
    qw = PEER_HEADS * PEER_QDIM
    return pl.pallas_call(
        _outproj_kernel, name="outproj", grid=(t // tm,),
        out_shape=[jax.ShapeDtypeStruct((t, D_MODEL), F32), jax.ShapeDtypeStruct((t, D_MODEL), F32),
                   jax.ShapeDtypeStruct((t, qw), F32)],
        in_specs=[pl.BlockSpec((tm, D_MODEL), row), pl.BlockSpec((tm, ATT_Q), row),
                  pl.BlockSpec((tm, DN_WIDTH), row), pl.BlockSpec((D_MODEL, D_MODEL), fixed),
                  pl.BlockSpec((1, D_MODEL), fixed), pl.BlockSpec((D_MODEL, qw), fixed)],
        out_specs=[pl.BlockSpec((tm, D_MODEL), row), pl.BlockSpec((tm, D_MODEL), row),
                   pl.BlockSpec((tm, qw), row)],
        compiler_params=_cparams("parallel"))(x, att, dn, w_out, g_ffn, wq)


def _top16_rows(s, iota):
    big = jnp.float32(INDEX_SENTINEL)
    vals, idxs = [], []
    for _ in range(PEER_TOPK):
        m = jnp.max(s, axis=0, keepdims=True)
        i = jnp.min(jnp.where(s == m, iota, big), axis=0, keepdims=True)
        vals.append(m)
        idxs.append(i)
        s = jnp.where(iota == i, NEG_INF, s)
    return jnp.concatenate(vals, axis=0), jnp.concatenate(idxs, axis=0)


def _candidates(a, b, row8):
    t = a.shape[1]
    lo = row8 < 4
    bc = lambda r, n: jnp.broadcast_to(a[r:r + 1], (n, t))
    b8 = b[0:8]
    b44 = jnp.where(lo, b8, pltpu.roll(b8, 4, 0))
    parts = [bc(0, 8), bc(0, 8), bc(1, 8), bc(2, 8), bc(3, 8),
             jnp.where(lo, bc(4, 8), bc(5, 8)), jnp.where(lo, bc(6, 8), bc(7, 8)), a[8:16]]
    others = [b[0:8], b[8:16], b8, b8, b8, b44, b44, jnp.broadcast_to(b[0:1], (8, t))]
    return parts, others


def _flat_index_column():
    r = lax.broadcasted_iota(I32, (8, 1), 0)
    lo = r < 4
    cols = [r, 8 + r, 16 + r, 32 + r, 48 + r, jnp.where(lo, 64 + r, 80 + r - 4),
            jnp.where(lo, 96 + r, 112 + r - 4), (8 + r) * 16]
    return jnp.concatenate(cols, axis=0).astype(F32)


def _route_kernel(q_ref, k1_ref, k2_ref, idx_ref, gate_ref, *, tb):
    iota = lax.broadcasted_iota(I32, (N_KEYS, tb), 0).astype(F32)
    row8 = lax.broadcasted_iota(I32, (8, tb), 0)
    flat = jnp.broadcast_to(_flat_index_column(), (4 * PEER_TOPK, tb))
    big = jnp.float32(INDEX_SENTINEL)
    half = PEER_QDIM // 2
    for h in range(PEER_HEADS):
        q1 = q_ref[:, h * PEER_QDIM:h * PEER_QDIM + half]
        q2 = q_ref[:, h * PEER_QDIM + half:(h + 1) * PEER_QDIM]
        nt = (((1,), (1,)), ((), ()))
        s1 = lax.dot_general(k1_ref[h], q1.astype(BF16), nt, preferred_element_type=F32)
        s2 = lax.dot_general(k2_ref[h], q2.astype(BF16), nt, preferred_element_type=F32)
        v1, i1 = _top16_rows(s1, iota)
        v2, i2 = _top16_rows(s2, iota)
        pa, pb = _candidates(v1, v2, row8)
        cand = jnp.concatenate([x + y for x, y in zip(pa, pb)], axis=0)
        ea, eb = _candidates(i1, i2, row8)
        expert = jnp.concatenate([x * float(N_KEYS) + y for x, y in zip(ea, eb)], axis=0)
        scs, exs = [], []
        for _ in range(PEER_TOPK):
            m = jnp.max(cand, axis=0, keepdims=True)
            c = jnp.min(jnp.where(cand == m, flat, big), axis=0, keepdims=True)
            hit = flat == c
            exs.append(jnp.max(jnp.where(hit, expert, -1.0), axis=0, keepdims=True))
            scs.append(m)
            cand = jnp.where(hit, NEG_INF, cand)
        sc = jnp.concatenate(scs, axis=0)
        e = jnp.exp(sc - sc[0:1])
        gate_ref[h * PEER_TOPK:(h + 1) * PEER_TOPK, :] = e / jnp.sum(e, axis=0, keepdims=True)
        idx_ref[h * PEER_TOPK:(h + 1) * PEER_TOPK, :] = jnp.concatenate(exs, axis=0).astype(I32)


def _route(q, keys1, keys2):
    t = q.shape[0]
    tb = _row_tile(t, 256)
    kspec = pl.BlockSpec((PEER_HEADS, N_KEYS, PEER_QDIM // 2), lambda i: (0, 0, 0))
    return pl.pallas_call(
        functools.partial(_route_kernel, tb=tb), name="peer_route", grid=(t // tb,),
        out_shape=[jax.ShapeDtypeStruct((PEER_SLOTS, t), I32), jax.ShapeDtypeStruct((PEER_SLOTS, t), F32)],
        in_specs=[pl.BlockSpec((tb, PEER_HEADS * PEER_QDIM), lambda i: (i, 0)), kspec, kspec],
        out_specs=[pl.BlockSpec((PEER_SLOTS, tb), lambda i: (0, i)),
                   pl.BlockSpec((PEER_SLOTS, tb), lambda i: (0, i))],
        compiler_params=_cparams("parallel"))(q, keys1, keys2)


GATHER_SLOTS = 4
SC_WORKERS = 32
SC_GATHER_ROWS = 16
SC_BUFFERS = 6
STAGED_TOKENS_PER_STEP = 32


def _peer_token(packed, x, gate):
    u = lax.bitcast_convert_type(packed & jnp.uint32(0xFFFF0000), F32)
    v = lax.bitcast_convert_type(packed << 16, F32)
    p = u * x
    acc = p[:, 0:128]
    for j in range(1, D_MODEL // 128):
        acc = acc + p[:, j * 128:(j + 1) * 128]
    s = jnp.sum(acc, axis=-1, keepdims=True)
    act = 0.5 * s * (1.0 + lax.erf(s * (1.0 / math.sqrt(2.0))))
    return jnp.sum((gate * act) * v, axis=0, keepdims=True)


def _peer_apply_kernel(idx_ref, c_ref, h_ref, gt_ref, uv_ref, o_ref, buf, sem, *, tb):
    ahead = GATHER_SLOTS - 1

    def issue(t, slot):
        for k in range(PEER_SLOTS):
            e = idx_ref[k, t]
            pltpu.make_async_copy(uv_ref.at[e], buf.at[slot, pl.ds(k, 1), :],
                                  sem.at[slot]).start(priority=k % 2)

    def wait(slot):
        pltpu.make_async_copy(buf.at[slot], buf.at[slot], sem.at[slot]).wait()

    lane = lax.broadcasted_iota(I32, (PEER_SLOTS, tb), 1)

    def apply(t, slot):
        gate = jnp.sum(jnp.where(lane == t, gt_ref[...], 0.0), axis=-1, keepdims=True)
        y = _peer_token(buf[slot], c_ref[pl.ds(t, 1), :], gate)
        o_ref[pl.ds(t, 1), :] = h_ref[pl.ds(t, 1), :] + y

    for t0 in range(ahead):
        issue(t0, t0)

    def body(i, carry):
        for j in range(GATHER_SLOTS):
            t = i * GATHER_SLOTS + j
            issue(t + ahead, (j + ahead) % GATHER_SLOTS)
            wait(j)
            apply(t, j)
        return carry

    nmain = (tb - ahead) // GATHER_SLOTS
    lax.fori_loop(0, nmain, body, 0)
    for t in range(nmain * GATHER_SLOTS, tb):
        if t + ahead < tb:
            issue(t + ahead, (t + ahead) % GATHER_SLOTS)
        wait(t % GATHER_SLOTS)
        apply(t, t % GATHER_SLOTS)


def _pack_expert_rows(u, v):
    hi = lax.bitcast_convert_type(u.astype(BF16), jnp.uint16).astype(jnp.uint32) << 16
    lo = lax.bitcast_convert_type(v.astype(BF16), jnp.uint16).astype(jnp.uint32)
    return hi | lo


def _peer_apply_direct(idx_t, c, h, gate_t, uv_rows, t_direct):
    t = c.shape[0]
    tb = _row_tile(t_direct, 256)
    row = lambda i: (i, 0)
    return pl.pallas_call(
        functools.partial(_peer_apply_kernel, tb=tb), name="peer_apply", grid=(t_direct // tb,),
        out_shape=jax.ShapeDtypeStruct((t, D_MODEL), F32),
        in_specs=[pl.BlockSpec((PEER_SLOTS, tb), lambda i: (0, i), memory_space=pltpu.SMEM),
                  pl.BlockSpec((tb, D_MODEL), row), pl.BlockSpec((tb, D_MODEL), row),
                  pl.BlockSpec((PEER_SLOTS, tb), lambda i: (0, i)),
                  pl.BlockSpec(memory_space=pl.ANY)],
        out_specs=pl.BlockSpec((tb, D_MODEL), row),
        scratch_shapes=[pltpu.VMEM((GATHER_SLOTS, PEER_SLOTS, D_MODEL), jnp.uint32),
                        pltpu.SemaphoreType.DMA((GATHER_SLOTS,))],
        compiler_params=_cparams("arbitrary"))(idx_t, c, h, gate_t, uv_rows)


def _sc_stage_rows(table, idx):
    t_total, nchunk_tok, _ = idx.shape
    d = table.shape[1]
    tpw = t_total // SC_WORKERS
    assert t_total % SC_WORKERS == 0 and tpw >= 2
    nb = SC_BUFFERS
    ahead = nb // 2
    params = pltpu.CompilerParams()
    if "needs_layout_passes" in pltpu.CompilerParams.__dataclass_fields__:
        params = dataclasses.replace(params, needs_layout_passes=False)
    mesh = plsc.VectorSubcoreMesh(core_axis_name="c", subcore_axis_name="s")

    @functools.partial(
        pl.kernel, mesh=mesh, compiler_params=params,
        out_type=jax.ShapeDtypeStruct((t_total * nchunk_tok * SC_GATHER_ROWS, d), table.dtype),
        scratch_types=[pltpu.VMEM((2, nchunk_tok, SC_GATHER_ROWS), I32),
                       pltpu.VMEM((nb, SC_GATHER_ROWS, d), table.dtype),
                       pltpu.SemaphoreType.DMA((nb,)), pltpu.SemaphoreType.DMA((nb,)),
                       pltpu.SemaphoreType.DMA((2,))])
    def stage(table_hbm, idx_hbm, out_hbm, idx_v, rows_v, sem_g, sem_w, sem_t):
        wid = lax.axis_index("s") * 2 + lax.axis_index("c")
        base = wid * tpw
        nchunks = tpw * nchunk_tok

        def gather(g, slot):
            ts = (g // nchunk_tok) % 2
            return pltpu.make_async_copy(table_hbm.at[idx_v.at[ts, g % nchunk_tok]], rows_v.at[slot],
                                         sem_g.at[slot])

        def write(g, slot):
            dst = out_hbm.at[pl.ds((base * nchunk_tok + g) * SC_GATHER_ROWS, SC_GATHER_ROWS)]
            return pltpu.make_async_copy(rows_v.at[slot], dst, sem_w.at[slot])

        def idx_copy(i, ts):
            return pltpu.make_async_copy(idx_hbm.at[base + i], idx_v.at[ts], sem_t.at[ts])

        idx_copy(0, 0).start()
        idx_copy(0, 0).wait()
        idx_copy(1, 1).start()
        for g0 in range(ahead):
            gather(g0, g0).start()

        @pl.loop(0, nchunks)
        def _(g):
            slot = g % nb
            nxt = g + ahead
            nslot = nxt % nb

            @pl.when(nxt < nchunks)
            def _():
                @pl.when(nxt >= nb)
                def _():
                    write(nxt - nb, nslot).wait()

                @pl.when(nxt % nchunk_tok == 0)
                def _():
                    idx_copy(nxt // nchunk_tok, (nxt // nchunk_tok) % 2).wait()

                gather(nxt, nslot).start()

            gather(g, slot).wait()
            write(g, slot).start()

            @pl.when((g % nchunk_tok == nchunk_tok - 1) & (g // nchunk_tok + 2 < tpw))
            def _():
                idx_copy(g // nchunk_tok + 2, (g // nchunk_tok) % 2).start()

        for j in range(nb):
            gl = nchunks - nb + j
            write(gl, gl % nb).wait()

    return stage(table, idx)


def _peer_staged_kernel(st_ref, c_ref, h_ref, gt_ref, *rest, tbs):
    o_ref = rest[-1]
    i = pl.program_id(0)
    lane = lax.broadcasted_iota(I32, (PEER_SLOTS, 128), 1)
    for t in range(tbs):
        tg = i * tbs + t
        col0 = pl.multiple_of((tg // 128) * 128, 128)
        gate = jnp.sum(jnp.where(lane == tg % 128, gt_ref[:, pl.ds(col0, 128)], 0.0), axis=-1, keepdims=True)
        y = _peer_token(st_ref[t], c_ref[pl.ds(t, 1), :], gate)
        o_ref[pl.ds(t, 1), :] = h_ref[pl.ds(t, 1), :] + y


def _peer_apply_staged(staged, c, h, gate_t_staged, partial, start):
    t_staged = staged.shape[0]
    tbs = STAGED_TOKENS_PER_STEP
    assert t_staged % tbs == 0 and start % tbs == 0 and t_staged % 128 == 0
    off = start // tbs
    row = lambda i: (i + off, 0)
    in_specs = [pl.BlockSpec((tbs, PEER_SLOTS, D_MODEL), lambda i: (i, 0, 0)),
                pl.BlockSpec((tbs, D_MODEL), row), pl.BlockSpec((tbs, D_MODEL), row),
                pl.BlockSpec((PEER_SLOTS, t_staged), lambda i: (0, 0))]
    args = [staged, c, h, gate_t_staged]
    aliases = {}
    if partial is not None:
        in_specs.append(pl.BlockSpec(memory_space=pl.ANY))
        args.append(partial)
        aliases = {4: 0}
    return pl.pallas_call(
        functools.partial(_peer_staged_kernel, tbs=tbs), name="peer_apply_staged", grid=(t_staged // tbs,),
        out_shape=jax.ShapeDtypeStruct(c.shape, F32),
        in_specs=in_specs, out_specs=pl.BlockSpec((tbs, D_MODEL), row),
        input_output_aliases=aliases,
        compiler_params=_cparams("arbitrary"))(*args)


def _peer_apply(idx_t, c, h, gate_t, uv, staged_pieces):
    t = c.shape[0]
    start = t - sum(staged_pieces)
    out = _peer_apply_direct(idx_t, c, h, gate_t, uv[:, None, :], start) if start else None
    for n in staged_pieces:
        idx_sc = idx_t[:, start:start + n].T.reshape(n, PEER_SLOTS // SC_GATHER_ROWS, SC_GATHER_ROWS)
        staged = _sc_stage_rows(uv, idx_sc).reshape(n, PEER_SLOTS, D_MODEL)
        out = _peer_apply_staged(staged, c, h, gate_t[:, start:start + n], out, start)
        start += n
    return out


def _final_kernel(h_ref, p_ref, gple_ref, wg_ref, wp_ref, gfin_ref, y_ref):
    h = h_ref[...]
    e = _rms(h, gple_ref[...])
    gate = jax.nn.sigmoid(jnp.dot(e.astype(BF16), wg_ref[...], preferred_element_type=F32))
    up = jnp.dot(p_ref[...].astype(BF16), wp_ref[...], preferred_element_type=F32)
    y_ref[...] = _rms(h + up * gate, gfin_ref[...])


def _final_into_kernel(h_ref, p_ref, gple_ref, wg_ref, wp_ref, gfin_ref, prev_ref, y_ref):
    del prev_ref
    _final_kernel(h_ref, p_ref, gple_ref, wg_ref, wp_ref, gfin_ref, y_ref)


def _final(h, p, g_ple, w_gate, w_ple, g_final, into=None, row0=0, total=None):
    t = h.shape[0]
    total = t if total is None else total
    tm = _row_tile(t, 512)
    assert row0 % tm == 0
    row = lambda i: (i, 0)
    out_row = lambda i: (i + row0 // tm, 0)
    fixed = lambda i: (0, 0)
    in_specs = [pl.BlockSpec((tm, D_MODEL), row), pl.BlockSpec((tm, PLE_DIM), row),
                pl.BlockSpec((1, D_MODEL), fixed), pl.BlockSpec((D_MODEL, D_MODEL), fixed),
                pl.BlockSpec((PLE_DIM, D_MODEL), fixed), pl.BlockSpec((1, D_MODEL), fixed)]
    args = [h, p, g_ple, w_gate, w_ple, g_final]
    body = _final_kernel
    aliases = {}
    if into is not None:
        in_specs.append(pl.BlockSpec(memory_space=pl.ANY))
        args.append(into)
        aliases = {len(args) - 1: 0}
        body = _final_into_kernel
    return pl.pallas_call(
        body, name="ple_final", grid=(t // tm,),
        out_shape=jax.ShapeDtypeStruct((total, D_MODEL), F32),
        in_specs=in_specs, out_specs=pl.BlockSpec((tm, D_MODEL), out_row),
        input_output_aliases=aliases,
        compiler_params=_cparams("parallel"))(*args)


STAGED_FRACTIONS = (1.0, 0.75, 0.75, 0.625)
STAGED_UNIT = SC_WORKERS * 32
STAGED_PIECE = 8 * STAGED_UNIT


def _staged_tokens(t, fraction):
    total = int(t * fraction) // STAGED_UNIT * STAGED_UNIT
    if total < STAGED_UNIT or t < STAGED_PIECE:
        return ()
    pieces = [STAGED_PIECE] * (total // STAGED_PIECE)
    if total % STAGED_PIECE:
        pieces.append(total % STAGED_PIECE)
    return tuple(pieces)


def _layer_front(x, kv_buf, conv_prefix, s0, wts, batch, seq):
    t = batch * seq
    q, k, v, dqkv, z, ab = _inproj(x, wts["norm_mix"], wts["w_in"])
    if kv_buf is None:
        att = _swa_prompt(q, k, v, wts["sinks"], batch, seq)
        k_new = k.reshape(batch, seq, ATT_KV)[:, -WINDOW:]
        v_new = v.reshape(batch, seq, ATT_KV)[:, -WINDOW:]
    else:
        att, k_new, v_new = _swa_sample(q.reshape(batch, seq, ATT_Q), k.reshape(batch, seq, ATT_KV),
                                        v.reshape(batch, seq, ATT_KV), kv_buf[0], kv_buf[1], wts["sinks"])
        att = att.reshape(t, ATT_Q)
    dn, s_new = _gdn(dqkv, z, ab, conv_prefix, s0, wts["conv_w"], wts["a_log"], wts["dt_bias"],
                     wts["dn_norm"], batch, seq)
    conv_new = dqkv.reshape(batch, seq, DN_QKV)[:, -(CONV_W - 1):]
    h, c, pq = _outproj(x, att, dn, wts["w_out"], wts["norm_ffn"], wts["peer_wq"])
    idx_t, gate_t = _route(pq, wts["keys1"], wts["keys2"])
    return dict(h=h, c=c, idx_t=idx_t, gate_t=gate_t, state=(k_new, v_new, conv_new, s_new))


def _layer_back(front, p, wts, staged_fraction=0.0, into=None, row0=0, total=None):
    t = front["c"].shape[0]
    h2 = _peer_apply(front["idx_t"], front["c"], front["h"], front["gate_t"], wts["uv"],
                     _staged_tokens(t, staged_fraction))
    return _final(h2, p, wts["norm_ple"], wts["ple_gate"], wts["ple_in"], wts["norm_final"], into, row0, total)


PROMPT_GROUPS = 4


def kernel(x_prompt, x_sample, p_prompt, p_sample, cache_swa_k, cache_swa_v, state_conv, state_delta, norm_mix, w_in, conv_w, attn_sinks, dn_a_log, dn_dt_bias, dn_norm, w_out, norm_ffn, peer_wq, peer_keys1, peer_keys2, peer_u, peer_v, norm_ple, ple_in, ple_gate, norm_final):
    depth = w_in.shape[0]
    assert depth == 1
    bp, lp = x_prompt.shape[0], x_prompt.shape[1]
    bs, ls = x_sample.shape[0], x_sample.shape[1]
    l = 0
    wts = dict(
        norm_mix=norm_mix[l].reshape(1, D_MODEL),
        w_in=jnp.pad(w_in[l], ((0, 0), (0, IN_WIDTH_PADDED - IN_WIDTH))).astype(BF16),
        conv_w=conv_w[l], sinks=attn_sinks[l], a_log=dn_a_log[l], dt_bias=dn_dt_bias[l], dn_norm=dn_norm[l],
        w_out=w_out[l].astype(BF16), norm_ffn=norm_ffn[l].reshape(1, D_MODEL), peer_wq=peer_wq[l].astype(BF16),
        keys1=peer_keys1[l].astype(BF16), keys2=peer_keys2[l].astype(BF16),
        uv=_pack_expert_rows(peer_u[l], peer_v[l]),
        norm_ple=norm_ple[l].reshape(1, D_MODEL), ple_in=ple_in[l].astype(BF16),
        ple_gate=ple_gate[l].astype(BF16), norm_final=norm_final.reshape(1, D_MODEL))
    ngrp = PROMPT_GROUPS if bp % PROMPT_GROUPS == 0 else 1
    bg = bp // ngrp
    fronts = []
    for gi in range(ngrp):
        xg = x_prompt[gi * bg:(gi + 1) * bg].reshape(bg * lp, D_MODEL)
        fronts.append(_layer_front(xg, None, jnp.zeros((bg, CONV_W - 1, DN_QKV), F32),
                                   jnp.zeros((bg, DN_HEADS, HEAD_DIM, HEAD_DIM), F32), wts, bg, lp))
    kv_buf = (cache_swa_k[l].reshape(bs, WINDOW, ATT_KV), cache_swa_v[l].reshape(bs, WINDOW, ATT_KV))
    front_s = _layer_front(x_sample.reshape(bs * ls, D_MODEL), kv_buf, state_conv[l], state_delta[l], wts, bs, ls)
    yp = None
    for gi in range(ngrp):
        pg = p_prompt[l, gi * bg:(gi + 1) * bg].reshape(bg * lp, PLE_DIM)
        yp = _layer_back(fronts[gi], pg, wts, STAGED_FRACTIONS[gi % len(STAGED_FRACTIONS)], yp, gi * bg * lp, bp * lp)
    ys = _layer_back(front_s, p_sample[l].reshape(bs * ls, PLE_DIM), wts)
    kp, vp, cp, sp = (jnp.concatenate([f["state"][j] for f in fronts], axis=0) for j in range(4))
    kn, vn, cn, sn = front_s["state"]
    kvshape = lambda b: (1, b, WINDOW, ATT_KV_HEADS, HEAD_DIM)
    return (yp.reshape(bp, lp, D_MODEL), ys.reshape(bs, ls, D_MODEL),
            kp.reshape(kvshape(bp)), vp.reshape(kvshape(bp)), cp[None], sp[None],
            kn.reshape(kvshape(bs)), vn.reshape(kvshape(bs)), cn[None], sn[None])
```

```python
import dataclasses
import functools
import math

import jax
import jax.numpy as jnp
from jax import lax
from jax.experimental import pallas as pl
from jax.experimental.pallas import tpu as pltpu
from jax.experimental.pallas import tpu_sc as plsc

F32 = jnp.float32
BF16 = jnp.bfloat16
I32 = jnp.int32
HIGHEST = lax.Precision.HIGHEST

D_MODEL = 1024
HEAD_DIM = 64
ATT_HEADS = 8
ATT_KV_HEADS = 2
ATT_GROUP = 4
WINDOW = 128
ATT_Q = ATT_HEADS * HEAD_DIM
ATT_KV = ATT_KV_HEADS * HEAD_DIM
DN_HEADS = 8
DN_WIDTH = DN_HEADS * HEAD_DIM
DN_QKV = 3 * DN_WIDTH
CONV_W = 4
DN_CHUNK = 64
IN_WIDTH = ATT_Q + 2 * ATT_KV + DN_QKV + DN_WIDTH + 2 * DN_HEADS
IN_WIDTH_PADDED = 2944
N_KEYS = 128
PEER_HEADS = 8
PEER_QDIM = 256
PEER_TOPK = 16
PEER_SLOTS = PEER_HEADS * PEER_TOPK
PLE_DIM = 256
EPS = 1e-6
NEG_INF = float("-inf")
INDEX_SENTINEL = 1.0e9

VMEM_LIMIT = 48 * 1024 * 1024


def _cparams(*sem):
    return pltpu.CompilerParams(dimension_semantics=sem, vmem_limit_bytes=VMEM_LIMIT)


def _rms(x, g):
    return x * lax.rsqrt(jnp.mean(x * x, axis=-1, keepdims=True) + EPS) * g


def _bdot(a, b):
    return jnp.dot(a.astype(BF16), b.astype(BF16), preferred_element_type=F32)


def _fdot(a, b):
    return jnp.dot(a, b, precision=HIGHEST, preferred_element_type=F32)


_NN = (((1,), (0,)), ((), ()))
_NT = (((1,), (1,)), ((), ()))
_TN = (((0,), (0,)), ((), ()))


def _row_tile(t, target):
    tile = min(t, target)
    assert t % tile == 0, (t, tile)
    return tile


def _inproj_kernel(x_ref, g_ref, w_ref, q_ref, k_ref, v_ref, dqkv_ref, z_ref, ab_ref):
    a = _rms(x_ref[...], g_ref[...])
    p = jnp.dot(a.astype(BF16), w_ref[...], preferred_element_type=F32)
    c0, c1, c2, c3, c4 = ATT_Q, ATT_Q + ATT_KV, ATT_Q + 2 * ATT_KV, ATT_Q + 2 * ATT_KV + DN_QKV, \
        ATT_Q + 2 * ATT_KV + DN_QKV + DN_WIDTH
    q_ref[...] = p[:, :c0]
    k_ref[...] = p[:, c0:c1]
    v_ref[...] = p[:, c1:c2]
    dqkv_ref[...] = p[:, c2:c3]
    z_ref[...] = p[:, c3:c4]
    ab_ref[...] = p[:, c4:c4 + 2 * DN_HEADS]


def _inproj(x, g, w_pad):
    t = x.shape[0]
    tm = _row_tile(t, 512)
    widths = (ATT_Q, ATT_KV, ATT_KV, DN_QKV, DN_WIDTH, 2 * DN_HEADS)
    row = lambda i: (i, 0)
    fixed = lambda i: (0, 0)
    return pl.pallas_call(
        _inproj_kernel, name="inproj", grid=(t // tm,),
        out_shape=[jax.ShapeDtypeStruct((t, w), F32) for w in widths],
        in_specs=[pl.BlockSpec((tm, D_MODEL), row), pl.BlockSpec((1, D_MODEL), fixed),
                  pl.BlockSpec((D_MODEL, IN_WIDTH_PADDED), fixed)],
        out_specs=[pl.BlockSpec((tm, w), row) for w in widths],
        compiler_params=_cparams("parallel"))(x, g, w_pad)


def _softmax_sink_pv(s, sink, vband):
    m = jnp.maximum(jnp.max(s, axis=-1, keepdims=True), sink)
    e = jnp.exp(s - m)
    denom = jnp.sum(e, axis=-1, keepdims=True) + jnp.exp(sink - m)
    return _bdot(e / denom, vband)


def _alibi_slope(h):
    return 2.0 ** (-8.0 * (h + 1) / ATT_HEADS)


def _group_columns(kh, rows, sink_ref):
    rg = lax.broadcasted_iota(I32, (ATT_GROUP * rows, 1), 0) // rows
    slope = jnp.full((ATT_GROUP * rows, 1), _alibi_slope(kh * ATT_GROUP), F32)
    sink = jnp.full((ATT_GROUP * rows, 1), sink_ref[kh * ATT_GROUP], F32)
    for j in range(1, ATT_GROUP):
        slope = jnp.where(rg == j, _alibi_slope(kh * ATT_GROUP + j), slope)
        sink = jnp.where(rg == j, sink_ref[kh * ATT_GROUP + j], sink)
    return slope, sink


def _attend_kv_head(q_all, kband, vband, kh, dist, valid, sink_ref):
    rows = q_all.shape[0]
    heads = range(kh * ATT_GROUP, (kh + 1) * ATT_GROUP)
    q4 = jnp.concatenate([q_all[:, h * HEAD_DIM:(h + 1) * HEAD_DIM] for h in heads], axis=0)
    k = kband[:, kh * HEAD_DIM:(kh + 1) * HEAD_DIM]
    v = vband[:, kh * HEAD_DIM:(kh + 1) * HEAD_DIM]
    s = lax.dot_general(q4.astype(BF16), k.astype(BF16), _NT, preferred_element_type=F32) * (HEAD_DIM ** -0.5)
    slope, sink = _group_columns(kh, rows, sink_ref)
    s = jnp.where(valid, s - slope * dist, NEG_INF)
    o4 = _softmax_sink_pv(s, sink, v)
    return [o4[j * rows:(j + 1) * rows] for j in range(ATT_GROUP)]


def _swa_prompt_kernel(sink_ref, q_ref, kc_ref, kp_ref, vc_ref, vp_ref, o_ref):
    n = pl.program_id(1)
    rows = ATT_GROUP * WINDOW
    qi = lax.broadcasted_iota(I32, (rows, 2 * WINDOW), 0) % WINDOW
    kj = lax.broadcasted_iota(I32, (rows, 2 * WINDOW), 1)
    dist = WINDOW + qi - kj
    kpos = n * WINDOW - WINDOW + kj
    valid = (dist >= 0) & (dist <= WINDOW) & (kpos >= 0)
    distf = dist.astype(F32)
    kband = jnp.concatenate([kp_ref[...], kc_ref[...]], axis=0)
    vband = jnp.concatenate([vp_ref[...], vc_ref[...]], axis=0)
    q_all = q_ref[...]
    outs = []
    for kh in range(ATT_KV_HEADS):
        outs += _attend_kv_head(q_all, kband, vband, kh, distf, valid, sink_ref)
    o_ref[...] = jnp.concatenate(outs, axis=-1)


def _swa_prompt(q, k, v, sinks, batch, seq):
    nblk = seq // WINDOW
    cur = lambda b, n: (b * nblk + n, 0)
    prev = lambda b, n: (b * nblk + jnp.maximum(n - 1, 0), 0)
    return pl.pallas_call(
        _swa_prompt_kernel, name="swa_prompt", grid=(batch, nblk),
        out_shape=jax.ShapeDtypeStruct((batch * seq, ATT_Q), F32),
        in_specs=[pl.BlockSpec(memory_space=pltpu.SMEM),
                  pl.BlockSpec((WINDOW, ATT_Q), cur),
                  pl.BlockSpec((WINDOW, ATT_KV), cur), pl.BlockSpec((WINDOW, ATT_KV), prev),
                  pl.BlockSpec((WINDOW, ATT_KV), cur), pl.BlockSpec((WINDOW, ATT_KV), prev)],
        out_specs=pl.BlockSpec((WINDOW, ATT_Q), cur),
        compiler_params=_cparams("parallel", "parallel"))(sinks, q, k, k, v, v)


def _swa_sample_kernel(sink_ref, q_ref, k_ref, v_ref, kb_ref, vb_ref, o_ref, ko_ref, vo_ref, *, bt, seq):
    klen = WINDOW + seq
    rows = ATT_GROUP * seq
    qi = lax.broadcasted_iota(I32, (rows, klen), 0) % seq
    kj = lax.broadcasted_iota(I32, (rows, klen), 1)
    dist = qi - (kj - WINDOW)
    valid = (dist >= 0) & (dist <= WINDOW)
    distf = dist.astype(F32)
    for b in range(bt):
        kc = jnp.concatenate([kb_ref[b], k_ref[b]], axis=0)
        vc = jnp.concatenate([vb_ref[b], v_ref[b]], axis=0)
        ko_ref[b] = kc[seq:]
        vo_ref[b] = vc[seq:]
        q_all = q_ref[b]
        outs = []
        for kh in range(ATT_KV_HEADS):
            outs += _attend_kv_head(q_all, kc, vc, kh, distf, valid, sink_ref)
        o_ref[b] = jnp.concatenate(outs, axis=-1)


def _swa_sample(q, k, v, k_buf, v_buf, sinks):
    batch, seq = q.shape[0], q.shape[1]
    bt = _row_tile(batch, 8)
    blk = lambda w, r: pl.BlockSpec((bt, r, w), lambda i: (i, 0, 0))
    return pl.pallas_call(
        functools.partial(_swa_sample_kernel, bt=bt, seq=seq), name="swa_sample", grid=(batch // bt,),
        out_shape=[jax.ShapeDtypeStruct((batch, seq, ATT_Q), F32),
                   jax.ShapeDtypeStruct((batch, WINDOW, ATT_KV), F32),
                   jax.ShapeDtypeStruct((batch, WINDOW, ATT_KV), F32)],
        in_specs=[pl.BlockSpec(memory_space=pltpu.SMEM), blk(ATT_Q, seq), blk(ATT_KV, seq), blk(ATT_KV, seq),
                  blk(ATT_KV, WINDOW), blk(ATT_KV, WINDOW)],
        out_specs=[blk(ATT_Q, seq), blk(ATT_KV, WINDOW), blk(ATT_KV, WINDOW)],
        compiler_params=_cparams("parallel"))(sinks, q, k, v, k_buf, v_buf)


def _split_bf16(a, cache):
    if id(a) not in cache:
        hi = a.astype(BF16)
        cache[id(a)] = (a, (hi, (a - hi.astype(F32)).astype(BF16)))
    return cache[id(a)][1]


MXU_DEPTH = 256


def _dot3_impl(a, b, dims=_NN, *, cache):
    (ca,), (cb,) = dims[0]
    depth = a.shape[ca]
    ah, al = _split_bf16(a, cache)
    bh, bl = _split_bf16(b, cache)
    f = lambda x, y: lax.dot_general(x, y, dims, preferred_element_type=F32)
    if 3 * depth <= MXU_DEPTH:
        return f(jnp.concatenate([ah, al, ah], axis=ca), jnp.concatenate([bh, bh, bl], axis=cb))
    return f(jnp.concatenate([ah, al], axis=ca), jnp.concatenate([bh, bh], axis=cb)) + f(ah, bl)


def _gdn_kernel(alog_ref, dtb_ref, dqkv_ref, z_ref, ab_ref, cw_ref, nrm_ref, pre_ref, s0_ref,
                o_ref, sfin_ref, tail_ref, state_ref, *, c, g, nsub):
    n = pl.program_id(1)
    ngrp = DN_HEADS // g
    r = g * c
    rows = nsub * c
    hd = HEAD_DIM
    _dot3 = functools.partial(_dot3_impl, cache={})

    @pl.when(n == 0)
    def _():
        tail_ref[...] = jnp.concatenate([jnp.zeros((8 - (CONV_W - 1), DN_QKV), F32), pre_ref[0]], axis=0)
        for gi in range(ngrp):
            state_ref[gi] = jnp.concatenate([s0_ref[0, gi * g + j] for j in range(g)], axis=1)

    x = dqkv_ref[...]
    hist = jnp.concatenate([tail_ref[...], x], axis=0)
    tail_ref[...] = hist[rows:]
    cw = cw_ref[...]
    y = x * cw[CONV_W - 1:CONV_W]
    for d in range(1, CONV_W):
        y = y + hist[8 - d:8 - d + rows] * cw[CONV_W - 1 - d:CONV_W - d]
    qkv = y * jax.nn.sigmoid(y)
    z = z_ref[...]
    ab = ab_ref[...]
    beta_all = jax.nn.sigmoid(ab[:, DN_HEADS:2 * DN_HEADS])
    g_all = -jnp.exp(alog_ref[...]) * jax.nn.softplus(ab[:, 0:DN_HEADS] + dtb_ref[...])

    ri = lax.broadcasted_iota(I32, (r, r), 0)
    ci = lax.broadcasted_iota(I32, (r, r), 1)
    same = (ri // c) == (ci // c)
    lower = same & (ri >= ci)
    strict = same & (ri > ci)
    tri = lower.astype(F32)
    blk = same.astype(F32)
    eye = (ri == ci).astype(F32)
    rowhead = lax.broadcasted_iota(I32, (r, hd), 0) // c

    def stack(a, sub, gi, base):
        return jnp.concatenate([a[sub * c:(sub + 1) * c, base + (gi * g + j) * hd:base + (gi * g + j + 1) * hd]
                                for j in range(g)], axis=0)

    def column(a, sub, gi):
        return jnp.concatenate([a[sub * c:(sub + 1) * c, gi * g + j:gi * g + j + 1] for j in range(g)], axis=0)

    def own_block(m):
        acc = jnp.where(rowhead == 0, m[:, 0:hd], 0.0)
        for j in range(1, g):
            acc = acc + jnp.where(rowhead == j, m[:, j * hd:(j + 1) * hd], 0.0)
        return acc

    keys = [(sub, gi) for sub in range(nsub) for gi in range(ngrp)]
    grp = {}
    for sub, gi in keys:
        q = stack(qkv, sub, gi, 0)
        k = stack(qkv, sub, gi, DN_WIDTH)
        v = stack(qkv, sub, gi, 2 * DN_WIDTH)
        q = q * lax.rsqrt(jnp.sum(q * q, axis=-1, keepdims=True) + EPS) * (hd ** -0.5)
        k = k * lax.rsqrt(jnp.sum(k * k, axis=-1, keepdims=True) + EPS)
        beta = column(beta_all, sub, gi)
        gr = column(g_all, sub, gi)
        gcol = _fdot(tri, gr)
        grow = lax.dot_general(gr, tri, (((0,), (1,)), ((), ())), precision=HIGHEST,
                               preferred_element_type=F32)
        glast = _fdot(blk, gr)
        decay = jnp.exp(jnp.where(lower, gcol - grow, NEG_INF))
        kb = k * beta
        lm = jnp.where(strict, _dot3(kb, k, _NT) * decay, 0.0)
        grp[sub, gi] = dict(q=q, k=k, v=v, beta=beta, gcol=gcol, glast=glast, decay=decay, kb=kb, lm=lm)

    pair = ((ri // 2) == (ci // 2)) & (ri % 2 == 1) & (ci % 2 == 0)
    invs = {key: eye - jnp.where(pair, grp[key]["lm"], 0.0) for key in keys}
    s = 2
    while s < c:
        off = ((ri // (2 * s)) == (ci // (2 * s))) & ((ri // s) % 2 == 1) & ((ci // s) % 2 == 0)
        tmp = {key: _dot3(invs[key], jnp.where(off, grp[key]["lm"], 0.0)) for key in keys}
        invs = {key: invs[key] - _dot3(tmp[key], invs[key]) for key in keys}
        s *= 2

    for sub in range(nsub):
        outs = [None] * DN_HEADS
        for gi in range(ngrp):
            d = grp[sub, gi]
            eg = jnp.exp(d["gcol"])
            sol = _dot3(invs[sub, gi], jnp.concatenate([d["v"] * d["beta"], d["kb"] * eg], axis=1))
            u = sol[:, :hd]
            w = sol[:, hd:]
            qk = _dot3(d["q"], d["k"], _NT) * d["decay"]
            ktail = d["k"] * jnp.exp(d["glast"] - d["gcol"])
            qhead = d["q"] * eg
            st = state_ref[gi]
            vnew = u - own_block(_dot3(w, st))
            o = own_block(_dot3(qhead, st)) + _dot3(qk, vnew)
            vblk = jnp.concatenate([jnp.where(rowhead == j, vnew, 0.0) for j in range(g)], axis=1)
            elast = jnp.exp(d["glast"])
            keep = jnp.concatenate([jnp.broadcast_to(elast[j * c:j * c + 1, :], (1, hd)) for j in range(g)],
                                   axis=1)
            state_ref[gi] = st * keep + _dot3(ktail, vblk, _TN)
            zs = stack(z, sub, gi, 0)
            o = _rms(o, nrm_ref[...]) * (zs * jax.nn.sigmoid(zs))
            for j in range(g):
                outs[gi * g + j] = o[j * c:(j + 1) * c]
        o_ref[sub * c:(sub + 1) * c, :] = jnp.concatenate(outs, axis=-1)

    @pl.when(n == pl.num_programs(1) - 1)
    def _():
        for gi in range(ngrp):
            for j in range(g):
                sfin_ref[0, gi * g + j] = state_ref[gi][:, j * hd:(j + 1) * hd]


GDN_CHUNKS_PER_STEP = 2


def _gdn(dqkv, z, ab, conv_prefix, s0, conv_w, a_log, dt_bias, dn_norm, batch, seq):
    c = min(DN_CHUNK, seq)
    nchunk = seq // c
    assert seq % c == 0 and c % 8 == 0
    g = min(DN_HEADS, max(1, 128 // c))
    nsub = GDN_CHUNKS_PER_STEP if nchunk % GDN_CHUNKS_PER_STEP == 0 else 1
    nstep = nchunk // nsub
    rows = nsub * c
    tok = lambda b, n: (b * nstep + n, 0)
    fixed = lambda b, n: (0, 0)
    return pl.pallas_call(
        functools.partial(_gdn_kernel, c=c, g=g, nsub=nsub), name="gdn", grid=(batch, nstep),
        out_shape=[jax.ShapeDtypeStruct((batch * seq, DN_WIDTH), F32),
                   jax.ShapeDtypeStruct((batch, DN_HEADS, HEAD_DIM, HEAD_DIM), F32)],
        in_specs=[pl.BlockSpec((1, DN_HEADS), fixed), pl.BlockSpec((1, DN_HEADS), fixed),
                  pl.BlockSpec((rows, DN_QKV), tok), pl.BlockSpec((rows, DN_WIDTH), tok),
                  pl.BlockSpec((rows, 2 * DN_HEADS), tok),
                  pl.BlockSpec((CONV_W, DN_QKV), fixed), pl.BlockSpec((1, HEAD_DIM), fixed),
                  pl.BlockSpec((1, CONV_W - 1, DN_QKV), lambda b, n: (b, 0, 0)),
                  pl.BlockSpec((1, DN_HEADS, HEAD_DIM, HEAD_DIM), lambda b, n: (b, 0, 0, 0))],
        out_specs=[pl.BlockSpec((rows, DN_WIDTH), tok),
                   pl.BlockSpec((1, DN_HEADS, HEAD_DIM, HEAD_DIM), lambda b, n: (b, 0, 0, 0))],
        scratch_shapes=[pltpu.VMEM((8, DN_QKV), F32),
                        pltpu.VMEM((DN_HEADS // g, HEAD_DIM, g * HEAD_DIM), F32)],
        compiler_params=_cparams("parallel", "arbitrary"))(
            a_log.reshape(1, DN_HEADS), dt_bias.reshape(1, DN_HEADS), dqkv, z, ab, conv_w,
            dn_norm.reshape(1, HEAD_DIM), conv_prefix, s0)


def _outproj_kernel(x_ref, att_ref, dn_ref, wo_ref, g_ref, wq_ref, h_ref, c_ref, q_ref):
    mix = jnp.concatenate([att_ref[...], dn_ref[...]], axis=-1)
    h = x_ref[...] + jnp.dot(mix.astype(BF16), wo_ref[...], preferred_element_type=F32)
    h_ref[...] = h
    cn = _rms(h, g_ref[...])
    c_ref[...] = cn
    q_ref[...] = jnp.dot(cn.astype(BF16), wq_ref[...], preferred_element_type=F32)


def _outproj(x, att, dn, w_out, g_ffn, wq):
    t = x.shape[0]
    tm = _row_tile(t, 512)
    row = lambda i: (i, 0)
    fixed = lambda i: (0, 0)
    qw = PEER_HEADS * PEER_QDIM
    return pl.pallas_call(
        _outproj_kernel, name="outproj", grid=(t // tm,),
        out_shape=[jax.ShapeDtypeStruct((t, D_MODEL), F32), jax.ShapeDtypeStruct((t, D_MODEL), F32),
                   jax.ShapeDtypeStruct((t, qw), F32)],
        in_specs=[pl.BlockSpec((tm, D_MODEL), row), pl.BlockSpec((tm, ATT_Q), row),
                  pl.BlockSpec((tm, DN_WIDTH), row), pl.BlockSpec((D_MODEL, D_MODEL), fixed),
                  pl.BlockSpec((1, D_MODEL), fixed), pl.BlockSpec((D_MODEL, qw), fixed)],
        out_specs=[pl.BlockSpec((tm, D_MODEL), row), pl.BlockSpec((tm, D_MODEL), row),
                   pl.BlockSpec((tm, qw), row)],
        compiler_params=_cparams("parallel"))(x, att, dn, w_out, g_ffn, wq)


def _top16_rows(s, iota):
    big = jnp.float32(INDEX_SENTINEL)
    vals, idxs = [], []
    for _ in range(PEER_TOPK):
        m = jnp.max(s, axis=0, keepdims=True)
        i = jnp.min(jnp.where(s == m, iota, big), axis=0, keepdims=True)
        vals.append(m)
        idxs.append(i)
        s = jnp.where(iota == i, NEG_INF, s)
    return jnp.concatenate(vals, axis=0), jnp.concatenate(idxs, axis=0)


def _candidates(a, b, row8):
    t = a.shape[1]
    lo = row8 < 4
    bc = lambda r, n: jnp.broadcast_to(a[r:r + 1], (n, t))
    b8 = b[0:8]
    b44 = jnp.where(lo, b8, pltpu.roll(b8, 4, 0))
    parts = [bc(0, 8), bc(0, 8), bc(1, 8), bc(2, 8), bc(3, 8),
             jnp.where(lo, bc(4, 8), bc(5, 8)), jnp.where(lo, bc(6, 8), bc(7, 8)), a[8:16]]
    others = [b[0:8], b[8:16], b8, b8, b8, b44, b44, jnp.broadcast_to(b[0:1], (8, t))]
    return parts, others


def _flat_index_column():
    r = lax.broadcasted_iota(I32, (8, 1), 0)
    lo = r < 4
    cols = [r, 8 + r, 16 + r, 32 + r, 48 + r, jnp.where(lo, 64 + r, 80 + r - 4),
            jnp.where(lo, 96 + r, 112 + r - 4), (8 + r) * 16]
    return jnp.concatenate(cols, axis=0).astype(F32)


def _route_kernel(q_ref, k1_ref, k2_ref, idx_ref, gate_ref, *, tb):
    iota = lax.broadcasted_iota(I32, (N_KEYS, tb), 0).astype(F32)
    row8 = lax.broadcasted_iota(I32, (8, tb), 0)
    flat = jnp.broadcast_to(_flat_index_column(), (4 * PEER_TOPK, tb))
    big = jnp.float32(INDEX_SENTINEL)
    half = PEER_QDIM // 2
    for h in range(PEER_HEADS):
        q1 = q_ref[:, h * PEER_QDIM:h * PEER_QDIM + half]
        q2 = q_ref[:, h * PEER_QDIM + half:(h + 1) * PEER_QDIM]
        nt = (((1,), (1,)), ((), ()))
        s1 = lax.dot_general(k1_ref[h], q1.astype(BF16), nt, preferred_element_type=F32)
        s2 = lax.dot_general(k2_ref[h], q2.astype(BF16), nt, preferred_element_type=F32)
        v1, i1 = _top16_rows(s1, iota)
        v2, i2 = _top16_rows(s2, iota)
        pa, pb = _candidates(v1, v2, row8)
        cand = jnp.concatenate([x + y for x, y in zip(pa, pb)], axis=0)
        ea, eb = _candidates(i1, i2, row8)
        expert = jnp.concatenate([x * float(N_KEYS) + y for x, y in zip(ea, eb)], axis=0)
        scs, exs = [], []
        for _ in range(PEER_TOPK):
            m = jnp.max(cand, axis=0, keepdims=True)
            c = jnp.min(jnp.where(cand == m, flat, big), axis=0, keepdims=True)
            hit = flat == c
            exs.append(jnp.max(jnp.where(hit, expert, -1.0), axis=0, keepdims=True))
            scs.append(m)
            cand = jnp.where(hit, NEG_INF, cand)
        sc = jnp.concatenate(scs, axis=0)
        e = jnp.exp(sc - sc[0:1])
        gate_ref[h * PEER_TOPK:(h + 1) * PEER_TOPK, :] = e / jnp.sum(e, axis=0, keepdims=True)
        idx_ref[h * PEER_TOPK:(h + 1) * PEER_TOPK, :] = jnp.concatenate(exs, axis=0).astype(I32)


def _route(q, keys1, keys2):
    t = q.shape[0]
    tb = _row_tile(t, 256)
    kspec = pl.BlockSpec((PEER_HEADS, N_KEYS, PEER_QDIM // 2), lambda i: (0, 0, 0))
    return pl.pallas_call(
        functools.partial(_route_kernel, tb=tb), name="peer_route", grid=(t // tb,),
        out_shape=[jax.ShapeDtypeStruct((PEER_SLOTS, t), I32), jax.ShapeDtypeStruct((PEER_SLOTS, t), F32)],
        in_specs=[pl.BlockSpec((tb, PEER_HEADS * PEER_QDIM), lambda i: (i, 0)), kspec, kspec],
        out_specs=[pl.BlockSpec((PEER_SLOTS, tb), lambda i: (0, i)),
                   pl.BlockSpec((PEER_SLOTS, tb), lambda i: (0, i))],
        compiler_params=_cparams("parallel"))(q, keys1, keys2)


GATHER_SLOTS = 4
SC_WORKERS = 32
SC_GATHER_ROWS = 16
SC_BUFFERS = 6
STAGED_TOKENS_PER_STEP = 32


def _peer_token(packed, x, gate):
    u = lax.bitcast_convert_type(packed & jnp.uint32(0xFFFF0000), F32)
    v = lax.bitcast_convert_type(packed << 16, F32)
    p = u * x
    acc = p[:, 0:128]
    for j in range(1, D_MODEL // 128):
        acc = acc + p[:, j * 128:(j + 1) * 128]
    s = jnp.sum(acc, axis=-1, keepdims=True)
    act = 0.5 * s * (1.0 + lax.erf(s * (1.0 / math.sqrt(2.0))))
    return jnp.sum((gate * act) * v, axis=0, keepdims=True)


def _peer_apply_kernel(idx_ref, c_ref, h_ref, gt_ref, uv_ref, o_ref, buf, sem, *, tb):
    ahead = GATHER_SLOTS - 1

    def issue(t, slot):
        for k in range(PEER_SLOTS):
            e = idx_ref[k, t]
            pltpu.make_async_copy(uv_ref.at[e], buf.at[slot, pl.ds(k, 1), :],
                                  sem.at[slot]).start(priority=k % 2)

    def wait(slot):
        pltpu.make_async_copy(buf.at[slot], buf.at[slot], sem.at[slot]).wait()

    lane = lax.broadcasted_iota(I32, (PEER_SLOTS, tb), 1)

    def apply(t, slot):
        gate = jnp.sum(jnp.where(lane == t, gt_ref[...], 0.0), axis=-1, keepdims=True)
        y = _peer_token(buf[slot], c_ref[pl.ds(t, 1), :], gate)
        o_ref[pl.ds(t, 1), :] = h_ref[pl.ds(t, 1), :] + y

    for t0 in range(ahead):
        issue(t0, t0)

    def body(i, carry):
        for j in range(GATHER_SLOTS):
            t = i * GATHER_SLOTS + j
            issue(t + ahead, (j + ahead) % GATHER_SLOTS)
            wait(j)
            apply(t, j)
        return carry

    nmain = (tb - ahead) // GATHER_SLOTS
    lax.fori_loop(0, nmain, body, 0)
    for t in range(nmain * GATHER_SLOTS, tb):
        if t + ahead < tb:
            issue(t + ahead, (t + ahead) % GATHER_SLOTS)
        wait(t % GATHER_SLOTS)
        apply(t, t % GATHER_SLOTS)


def _pack_kernel(u_ref, v_ref, o_ref):
    hi = lax.bitcast_convert_type(u_ref[...].astype(BF16).astype(F32), jnp.uint32)
    lo = lax.bitcast_convert_type(v_ref[...].astype(BF16).astype(F32), jnp.uint32) >> 16
    o_ref[...] = hi | lo


def _pack_expert_rows(u, v):
    n, d = u.shape
    tm = _row_tile(n, 512)
    spec = pl.BlockSpec((tm, d), lambda i: (i, 0))
    return pl.pallas_call(
        _pack_kernel, name="pack_expert_rows", grid=(n // tm,),
        out_shape=jax.ShapeDtypeStruct((n, d), jnp.uint32),
        in_specs=[spec, spec], out_specs=spec,
        compiler_params=_cparams("parallel"))(u, v)


def _peer_apply_direct(idx_t, c, h, gate_t, uv_rows, t_direct):
    t = c.shape[0]
    tb = _row_tile(t_direct, 256)
    row = lambda i: (i, 0)
    return pl.pallas_call(
        functools.partial(_peer_apply_kernel, tb=tb), name="peer_apply", grid=(t_direct // tb,),
        out_shape=jax.ShapeDtypeStruct((t, D_MODEL), F32),
        in_specs=[pl.BlockSpec((PEER_SLOTS, tb), lambda i: (0, i), memory_space=pltpu.SMEM),
                  pl.BlockSpec((tb, D_MODEL), row), pl.BlockSpec((tb, D_MODEL), row),
                  pl.BlockSpec((PEER_SLOTS, tb), lambda i: (0, i)),
                  pl.BlockSpec(memory_space=pl.ANY)],
        out_specs=pl.BlockSpec((tb, D_MODEL), row),
        scratch_shapes=[pltpu.VMEM((GATHER_SLOTS, PEER_SLOTS, D_MODEL), jnp.uint32),
                        pltpu.SemaphoreType.DMA((GATHER_SLOTS,))],
        compiler_params=_cparams("arbitrary"))(idx_t, c, h, gate_t, uv_rows)


def _sc_stage_rows(table, idx):
    t_total, nchunk_tok, _ = idx.shape
    d = table.shape[1]
    tpw = t_total // SC_WORKERS
    assert t_total % SC_WORKERS == 0 and tpw >= 2
    nb = SC_BUFFERS
    ahead = nb // 2
    params = pltpu.CompilerParams()
    if "needs_layout_passes" in pltpu.CompilerParams.__dataclass_fields__:
        params = dataclasses.replace(params, needs_layout_passes=False)
    mesh = plsc.VectorSubcoreMesh(core_axis_name="c", subcore_axis_name="s")

    @functools.partial(
        pl.kernel, mesh=mesh, compiler_params=params,
        out_type=jax.ShapeDtypeStruct((t_total * nchunk_tok * SC_GATHER_ROWS, d), table.dtype),
        scratch_types=[pltpu.VMEM((2, nchunk_tok, SC_GATHER_ROWS), I32),
                       pltpu.VMEM((nb, SC_GATHER_ROWS, d), table.dtype),
                       pltpu.SemaphoreType.DMA((nb,)), pltpu.SemaphoreType.DMA((nb,)),
                       pltpu.SemaphoreType.DMA((2,))])
    def stage(table_hbm, idx_hbm, out_hbm, idx_v, rows_v, sem_g, sem_w, sem_t):
        wid = lax.axis_index("s") * 2 + lax.axis_index("c")
        base = wid * tpw
        nchunks = tpw * nchunk_tok

        def gather(g, slot):
            ts = (g // nchunk_tok) % 2
            return pltpu.make_async_copy(table_hbm.at[idx_v.at[ts, g % nchunk_tok]], rows_v.at[slot],
                                         sem_g.at[slot])

        def write(g, slot):
            dst = out_hbm.at[pl.ds((base * nchunk_tok + g) * SC_GATHER_ROWS, SC_GATHER_ROWS)]
            return pltpu.make_async_copy(rows_v.at[slot], dst, sem_w.at[slot])

        def idx_copy(i, ts):
            return pltpu.make_async_copy(idx_hbm.at[base + i], idx_v.at[ts], sem_t.at[ts])

        idx_copy(0, 0).start()
        idx_copy(0, 0).wait()
        idx_copy(1, 1).start()
        for g0 in range(ahead):
            gather(g0, g0).start()

        @pl.loop(0, nchunks)
        def _(g):
            slot = g % nb
            nxt = g + ahead
            nslot = nxt % nb

            @pl.when(nxt < nchunks)
            def _():
                @pl.when(nxt >= nb)
                def _():
                    write(nxt - nb, nslot).wait()

                @pl.when(nxt % nchunk_tok == 0)
                def _():
                    idx_copy(nxt // nchunk_tok, (nxt // nchunk_tok) % 2).wait()

                gather(nxt, nslot).start()

            gather(g, slot).wait()
            write(g, slot).start()

            @pl.when((g % nchunk_tok == nchunk_tok - 1) & (g // nchunk_tok + 2 < tpw))
            def _():
                idx_copy(g // nchunk_tok + 2, (g // nchunk_tok) % 2).start()

        for j in range(nb):
            gl = nchunks - nb + j
            write(gl, gl % nb).wait()

    return stage(table, idx)


def _peer_staged_kernel(st_ref, c_ref, h_ref, gt_ref, *rest, tbs):
    o_ref = rest[-1]
    i = pl.program_id(0)
    lane = lax.broadcasted_iota(I32, (PEER_SLOTS, 128), 1)
    for t in range(tbs):
        tg = i * tbs + t
        col0 = pl.multiple_of((tg // 128) * 128, 128)
        gate = jnp.sum(jnp.where(lane == tg % 128, gt_ref[:, pl.ds(col0, 128)], 0.0), axis=-1, keepdims=True)
        y = _peer_token(st_ref[t], c_ref[pl.ds(t, 1), :], gate)
        o_ref[pl.ds(t, 1), :] = h_ref[pl.ds(t, 1), :] + y


def _peer_apply_staged(staged, c, h, gate_t_staged, partial, start):
    t_staged = staged.shape[0]
    tbs = STAGED_TOKENS_PER_STEP
    assert t_staged % tbs == 0 and start % tbs == 0 and t_staged % 128 == 0
    off = start // tbs
    row = lambda i: (i + off, 0)
    in_specs = [pl.BlockSpec((tbs, PEER_SLOTS, D_MODEL), lambda i: (i, 0, 0)),
                pl.BlockSpec((tbs, D_MODEL), row), pl.BlockSpec((tbs, D_MODEL), row),
                pl.BlockSpec((PEER_SLOTS, t_staged), lambda i: (0, 0))]
    args = [staged, c, h, gate_t_staged]
    aliases = {}
    if partial is not None:
        in_specs.append(pl.BlockSpec(memory_space=pl.ANY))
        args.append(partial)
        aliases = {4: 0}
    return pl.pallas_call(
        functools.partial(_peer_staged_kernel, tbs=tbs), name="peer_apply_staged", grid=(t_staged // tbs,),
        out_shape=jax.ShapeDtypeStruct(c.shape, F32),
        in_specs=in_specs, out_specs=pl.BlockSpec((tbs, D_MODEL), row),
        input_output_aliases=aliases,
        compiler_params=_cparams("arbitrary"))(*args)


def _peer_apply(idx_t, c, h, gate_t, uv, staged_pieces):
    t = c.shape[0]
    start = t - sum(staged_pieces)
    out = _peer_apply_direct(idx_t, c, h, gate_t, uv[:, None, :], start) if start else None
    for n in staged_pieces:
        idx_sc = idx_t[:, start:start + n].T.reshape(n, PEER_SLOTS // SC_GATHER_ROWS, SC_GATHER_ROWS)
        staged = _sc_stage_rows(uv, idx_sc).reshape(n, PEER_SLOTS, D_MODEL)
        out = _peer_apply_staged(staged, c, h, gate_t[:, start:start + n], out, start)
        start += n
    return out


def _final_kernel(h_ref, p_ref, gple_ref, wg_ref, wp_ref, gfin_ref, y_ref):
    h = h_ref[...]
    e = _rms(h, gple_ref[...])
    gate = jax.nn.sigmoid(jnp.dot(e.astype(BF16), wg_ref[...], preferred_element_type=F32))
    up = jnp.dot(p_ref[...].astype(BF16), wp_ref[...], preferred_element_type=F32)
    y_ref[...] = _rms(h + up * gate, gfin_ref[...])


def _final_into_kernel(h_ref, p_ref, gple_ref, wg_ref, wp_ref, gfin_ref, prev_ref, y_ref):
    del prev_ref
    _final_kernel(h_ref, p_ref, gple_ref, wg_ref, wp_ref, gfin_ref, y_ref)


def _final(h, p, g_ple, w_gate, w_ple, g_final, into=None, row0=0, total=None):
    t = h.shape[0]
    total = t if total is None else total
    tm = _row_tile(t, 512)
    assert row0 % tm == 0
    row = lambda i: (i, 0)
    out_row = lambda i: (i + row0 // tm, 0)
    fixed = lambda i: (0, 0)
    in_specs = [pl.BlockSpec((tm, D_MODEL), row), pl.BlockSpec((tm, PLE_DIM), row),
                pl.BlockSpec((1, D_MODEL), fixed), pl.BlockSpec((D_MODEL, D_MODEL), fixed),
                pl.BlockSpec((PLE_DIM, D_MODEL), fixed), pl.BlockSpec((1, D_MODEL), fixed)]
    args = [h, p, g_ple, w_gate, w_ple, g_final]
    body = _final_kernel
    aliases = {}
    if into is not None:
        in_specs.append(pl.BlockSpec(memory_space=pl.ANY))
        args.append(into)
        aliases = {len(args) - 1: 0}
        body = _final_into_kernel
    return pl.pallas_call(
        body, name="ple_final", grid=(t // tm,),
        out_shape=jax.ShapeDtypeStruct((total, D_MODEL), F32),
        in_specs=in_specs, out_specs=pl.BlockSpec((tm, D_MODEL), out_row),
        input_output_aliases=aliases,
        compiler_params=_cparams("parallel"))(*args)


STAGED_FRACTIONS = (1.0, 0.75, 0.75, 0.625)
STAGED_UNIT = SC_WORKERS * 32
STAGED_PIECE = 8 * STAGED_UNIT


def _staged_tokens(t, fraction):
    total = int(t * fraction) // STAGED_UNIT * STAGED_UNIT
    if total < STAGED_UNIT or t < STAGED_PIECE:
        return ()
    pieces = [STAGED_PIECE] * (total // STAGED_PIECE)
    if total % STAGED_PIECE:
        pieces.append(total % STAGED_PIECE)
    return tuple(pieces)


def _layer_front(x, kv_buf, conv_prefix, s0, wts, batch, seq):
    t = batch * seq
    q, k, v, dqkv, z, ab = _inproj(x, wts["norm_mix"], wts["w_in"])
    if kv_buf is None:
        att = _swa_prompt(q, k, v, wts["sinks"], batch, seq)
        k_new = k.reshape(batch, seq, ATT_KV)[:, -WINDOW:]
        v_new = v.reshape(batch, seq, ATT_KV)[:, -WINDOW:]
    else:
        att, k_new, v_new = _swa_sample(q.reshape(batch, seq, ATT_Q), k.reshape(batch, seq, ATT_KV),
                                        v.reshape(batch, seq, ATT_KV), kv_buf[0], kv_buf[1], wts["sinks"])
        att = att.reshape(t, ATT_Q)
    dn, s_new = _gdn(dqkv, z, ab, conv_prefix, s0, wts["conv_w"], wts["a_log"], wts["dt_bias"],
                     wts["dn_norm"], batch, seq)
    conv_new = dqkv.reshape(batch, seq, DN_QKV)[:, -(CONV_W - 1):]
    h, c, pq = _outproj(x, att, dn, wts["w_out"], wts["norm_ffn"], wts["peer_wq"])
    idx_t, gate_t = _route(pq, wts["keys1"], wts["keys2"])
    return dict(h=h, c=c, idx_t=idx_t, gate_t=gate_t, state=(k_new, v_new, conv_new, s_new))


def _layer_back(front, p, wts, staged_fraction=0.0, into=None, row0=0, total=None):
    t = front["c"].shape[0]
    h2 = _peer_apply(front["idx_t"], front["c"], front["h"], front["gate_t"], wts["uv"],
                     _staged_tokens(t, staged_fraction))
    return _final(h2, p, wts["norm_ple"], wts["ple_gate"], wts["ple_in"], wts["norm_final"], into, row0, total)


PROMPT_GROUPS = 4


def kernel(x_prompt, x_sample, p_prompt, p_sample, cache_swa_k, cache_swa_v, state_conv, state_delta, norm_mix, w_in, conv_w, attn_sinks, dn_a_log, dn_dt_bias, dn_norm, w_out, norm_ffn, peer_wq, peer_keys1, peer_keys2, peer_u, peer_v, norm_ple, ple_in, ple_gate, norm_final):
    depth = w_in.shape[0]
    assert depth == 1
    bp, lp = x_prompt.shape[0], x_prompt.shape[1]
    bs, ls = x_sample.shape[0], x_sample.shape[1]
    l = 0
    wts = dict(
        norm_mix=norm_mix[l].reshape(1, D_MODEL),
        w_in=jnp.pad(w_in[l], ((0, 0), (0, IN_WIDTH_PADDED - IN_WIDTH))).astype(BF16),
        conv_w=conv_w[l], sinks=attn_sinks[l], a_log=dn_a_log[l], dt_bias=dn_dt_bias[l], dn_norm=dn_norm[l],
        w_out=w_out[l].astype(BF16), norm_ffn=norm_ffn[l].reshape(1, D_MODEL), peer_wq=peer_wq[l].astype(BF16),
        keys1=peer_keys1[l].astype(BF16), keys2=peer_keys2[l].astype(BF16),
        uv=_pack_expert_rows(peer_u[l], peer_v[l]),
        norm_ple=norm_ple[l].reshape(1, D_MODEL), ple_in=ple_in[l].astype(BF16),
        ple_gate=ple_gate[l].astype(BF16), norm_final=norm_final.reshape(1, D_MODEL))
    ngrp = PROMPT_GROUPS if bp % PROMPT_GROUPS == 0 else 1
    bg = bp // ngrp
    fronts = []
    for gi in range(ngrp):
        xg = x_prompt[gi * bg:(gi + 1) * bg].reshape(bg * lp, D_MODEL)
        fronts.append(_layer_front(xg, None, jnp.zeros((bg, CONV_W - 1, DN_QKV), F32),
                                   jnp.zeros((bg, DN_HEADS, HEAD_DIM, HEAD_DIM), F32), wts, bg, lp))
    kv_buf = (cache_swa_k[l].reshape(bs, WINDOW, ATT_KV), cache_swa_v[l].reshape(bs, WINDOW, ATT_KV))
    front_s = _layer_front(x_sample.reshape(bs * ls, D_MODEL), kv_buf, state_conv[l], state_delta[l], wts, bs, ls)
    yp = None
    for gi in range(ngrp):
        pg = p_prompt[l, gi * bg:(gi + 1) * bg].reshape(bg * lp, PLE_DIM)
        yp = _layer_back(fronts[gi], pg, wts, STAGED_FRACTIONS[gi % len(STAGED_FRACTIONS)], yp, gi * bg * lp, bp * lp)
    ys = _layer_back(front_s, p_sample[l].reshape(bs * ls, PLE_DIM), wts)
    kp, vp, cp, sp = (jnp.concatenate([f["state"][j] for f in fronts], axis=0) for j in range(4))
    kn, vn, cn, sn = front_s["state"]
    kvshape = lambda b: (1, b, WINDOW, ATT_KV_HEADS, HEAD_DIM)
    return (yp.reshape(bp, lp, D_MODEL), ys.reshape(bs, ls, D_MODEL),
            kp.reshape(kvshape(bp)), vp.reshape(kvshape(bp)), cp[None], sp[None],
            kn.reshape(kvshape(bs)), vn.reshape(kvshape(bs)), cn[None], sn[None])
```

```python
import dataclasses
import functools
import math

import jax
import jax.numpy as jnp
from jax import lax
from jax.experimental import pallas as pl
from jax.experimental.pallas import tpu as pltpu
from jax.experimental.pallas import tpu_sc as plsc

F32 = jnp.float32
BF16 = jnp.bfloat16
I32 = jnp.int32
HIGHEST = lax.Precision.HIGHEST

D_MODEL = 1024
HEAD_DIM = 64
ATT_HEADS = 8
ATT_KV_HEADS = 2
ATT_GROUP = 4
WINDOW = 128
ATT_Q = ATT_HEADS * HEAD_DIM
ATT_KV = ATT_KV_HEADS * HEAD_DIM
DN_HEADS = 8
DN_WIDTH = DN_HEADS * HEAD_DIM
DN_QKV = 3 * DN_WIDTH
CONV_W = 4
DN_CHUNK = 64
IN_WIDTH = ATT_Q + 2 * ATT_KV + DN_QKV + DN_WIDTH + 2 * DN_HEADS
IN_WIDTH_PADDED = 2944
N_KEYS = 128
PEER_HEADS = 8
PEER_QDIM = 256
PEER_TOPK = 16
PEER_SLOTS = PEER_HEADS * PEER_TOPK
PLE_DIM = 256
EPS = 1e-6
NEG_INF = float("-inf")
INDEX_SENTINEL = 1.0e9

VMEM_LIMIT = 48 * 1024 * 1024


def _cparams(*sem):
    return pltpu.CompilerParams(dimension_semantics=sem, vmem_limit_bytes=VMEM_LIMIT)


def _rms(x, g):
    return x * lax.rsqrt(jnp.mean(x * x, axis=-1, keepdims=True) + EPS) * g


def _bdot(a, b):
    return jnp.dot(a.astype(BF16), b.astype(BF16), preferred_element_type=F32)


def _fdot(a, b):
    return jnp.dot(a, b, precision=HIGHEST, preferred_element_type=F32)


_NN = (((1,), (0,)), ((), ()))
_NT = (((1,), (1,)), ((), ()))
_TN = (((0,), (0,)), ((), ()))


def _row_tile(t, target):
    tile = min(t, target)
    assert t % tile == 0, (t, tile)
    return tile


def _inproj_kernel(x_ref, g_ref, w_ref, q_ref, k_ref, v_ref, dqkv_ref, z_ref, ab_ref):
    a = _rms(x_ref[...], g_ref[...])
    p = jnp.dot(a.astype(BF16), w_ref[...], preferred_element_type=F32)
    c0, c1, c2, c3, c4 = ATT_Q, ATT_Q + ATT_KV, ATT_Q + 2 * ATT_KV, ATT_Q + 2 * ATT_KV + DN_QKV, \
        ATT_Q + 2 * ATT_KV + DN_QKV + DN_WIDTH
    q_ref[...] = p[:, :c0]
    k_ref[...] = p[:, c0:c1]
    v_ref[...] = p[:, c1:c2]
    dqkv_ref[...] = p[:, c2:c3]
    z_ref[...] = p[:, c3:c4]
    ab_ref[...] = p[:, c4:c4 + 2 * DN_HEADS]


def _inproj(x, g, w_pad, t, row0):
    tm = _row_tile(t, 512)
    assert row0 % tm == 0
    widths = (ATT_Q, ATT_KV, ATT_KV, DN_QKV, DN_WIDTH, 2 * DN_HEADS)
    row = lambda i: (i, 0)
    xrow = lambda i: (i + row0 // tm, 0)
    fixed = lambda i: (0, 0)
    return pl.pallas_call(
        _inproj_kernel, name="inproj", grid=(t // tm,),
        out_shape=[jax.ShapeDtypeStruct((t, w), F32) for w in widths],
        in_specs=[pl.BlockSpec((tm, D_MODEL), xrow), pl.BlockSpec((1, D_MODEL), fixed),
                  pl.BlockSpec((D_MODEL, IN_WIDTH_PADDED), fixed)],
        out_specs=[pl.BlockSpec((tm, w), row) for w in widths],
        compiler_params=_cparams("parallel"))(x, g, w_pad)


def _softmax_sink_pv(s, sink, vband):
    m = jnp.maximum(jnp.max(s, axis=-1, keepdims=True), sink)
    e = jnp.exp(s - m)
    denom = jnp.sum(e, axis=-1, keepdims=True) + jnp.exp(sink - m)
    return _bdot(e / denom, vband)


def _alibi_slope(h):
    return 2.0 ** (-8.0 * (h + 1) / ATT_HEADS)


def _group_columns(kh, rows, sink_ref):
    rg = lax.broadcasted_iota(I32, (ATT_GROUP * rows, 1), 0) // rows
    slope = jnp.full((ATT_GROUP * rows, 1), _alibi_slope(kh * ATT_GROUP), F32)
    sink = jnp.full((ATT_GROUP * rows, 1), sink_ref[kh * ATT_GROUP], F32)
    for j in range(1, ATT_GROUP):
        slope = jnp.where(rg == j, _alibi_slope(kh * ATT_GROUP + j), slope)
        sink = jnp.where(rg == j, sink_ref[kh * ATT_GROUP + j], sink)
    return slope, sink


def _attend_kv_head(q_all, kband, vband, kh, dist, valid, sink_ref):
    rows = q_all.shape[0]
    heads = range(kh * ATT_GROUP, (kh + 1) * ATT_GROUP)
    q4 = jnp.concatenate([q_all[:, h * HEAD_DIM:(h + 1) * HEAD_DIM] for h in heads], axis=0)
    k = kband[:, kh * HEAD_DIM:(kh + 1) * HEAD_DIM]
    v = vband[:, kh * HEAD_DIM:(kh + 1) * HEAD_DIM]
    s = lax.dot_general(q4.astype(BF16), k.astype(BF16), _NT, preferred_element_type=F32) * (HEAD_DIM ** -0.5)
    slope, sink = _group_columns(kh, rows, sink_ref)
    s = jnp.where(valid, s - slope * dist, NEG_INF)
    o4 = _softmax_sink_pv(s, sink, v)
    return [o4[j * rows:(j + 1) * rows] for j in range(ATT_GROUP)]


def _swa_prompt_kernel(sink_ref, q_ref, kc_ref, kp_ref, vc_ref, vp_ref, o_ref):
    n = pl.program_id(1)
    rows = ATT_GROUP * WINDOW
    qi = lax.broadcasted_iota(I32, (rows, 2 * WINDOW), 0) % WINDOW
    kj = lax.broadcasted_iota(I32, (rows, 2 * WINDOW), 1)
    dist = WINDOW + qi - kj
    kpos = n * WINDOW - WINDOW + kj
    valid = (dist >= 0) & (dist <= WINDOW) & (kpos >= 0)
    distf = dist.astype(F32)
    kband = jnp.concatenate([kp_ref[...], kc_ref[...]], axis=0)
    vband = jnp.concatenate([vp_ref[...], vc_ref[...]], axis=0)
    q_all = q_ref[...]
    outs = []
    for kh in range(ATT_KV_HEADS):
        outs += _attend_kv_head(q_all, kband, vband, kh, distf, valid, sink_ref)
    o_ref[...] = jnp.concatenate(outs, axis=-1)


def _swa_prompt(q, k, v, sinks, batch, seq):
    nblk = seq // WINDOW
    cur = lambda b, n: (b * nblk + n, 0)
    prev = lambda b, n: (b * nblk + jnp.maximum(n - 1, 0), 0)
    return pl.pallas_call(
        _swa_prompt_kernel, name="swa_prompt", grid=(batch, nblk),
        out_shape=jax.ShapeDtypeStruct((batch * seq, ATT_Q), F32),
        in_specs=[pl.BlockSpec(memory_space=pltpu.SMEM),
                  pl.BlockSpec((WINDOW, ATT_Q), cur),
                  pl.BlockSpec((WINDOW, ATT_KV), cur), pl.BlockSpec((WINDOW, ATT_KV), prev),
                  pl.BlockSpec((WINDOW, ATT_KV), cur), pl.BlockSpec((WINDOW, ATT_KV), prev)],
        out_specs=pl.BlockSpec((WINDOW, ATT_Q), cur),
        compiler_params=_cparams("parallel", "parallel"))(sinks, q, k, k, v, v)


def _swa_sample_kernel(sink_ref, q_ref, k_ref, v_ref, kb_ref, vb_ref, o_ref, ko_ref, vo_ref, *, bt, seq):
    klen = WINDOW + seq
    rows = ATT_GROUP * seq
    qi = lax.broadcasted_iota(I32, (rows, klen), 0) % seq
    kj = lax.broadcasted_iota(I32, (rows, klen), 1)
    dist = qi - (kj - WINDOW)
    valid = (dist >= 0) & (dist <= WINDOW)
    distf = dist.astype(F32)
    for b in range(bt):
        kc = jnp.concatenate([kb_ref[b], k_ref[b]], axis=0)
        vc = jnp.concatenate([vb_ref[b], v_ref[b]], axis=0)
        ko_ref[b] = kc[seq:]
        vo_ref[b] = vc[seq:]
        q_all = q_ref[b]
        outs = []
        for kh in range(ATT_KV_HEADS):
            outs += _attend_kv_head(q_all, kc, vc, kh, distf, valid, sink_ref)
        o_ref[b] = jnp.concatenate(outs, axis=-1)


def _swa_sample(q, k, v, k_buf, v_buf, sinks):
    batch, seq = q.shape[0], q.shape[1]
    bt = _row_tile(batch, 8)
    blk = lambda w, r: pl.BlockSpec((bt, r, w), lambda i: (i, 0, 0))
    return pl.pallas_call(
        functools.partial(_swa_sample_kernel, bt=bt, seq=seq), name="swa_sample", grid=(batch // bt,),
        out_shape=[jax.ShapeDtypeStruct((batch, seq, ATT_Q), F32),
                   jax.ShapeDtypeStruct((batch, WINDOW, ATT_KV), F32),
                   jax.ShapeDtypeStruct((batch, WINDOW, ATT_KV), F32)],
        in_specs=[pl.BlockSpec(memory_space=pltpu.SMEM), blk(ATT_Q, seq), blk(ATT_KV, seq), blk(ATT_KV, seq),
                  blk(ATT_KV, WINDOW), blk(ATT_KV, WINDOW)],
        out_specs=[blk(ATT_Q, seq), blk(ATT_KV, WINDOW), blk(ATT_KV, WINDOW)],
        compiler_params=_cparams("parallel"))(sinks, q, k, v, k_buf, v_buf)


def _split_bf16(a, cache):
    if id(a) not in cache:
        hi = a.astype(BF16)
        cache[id(a)] = (a, (hi, (a - hi.astype(F32)).astype(BF16)))
    return cache[id(a)][1]


MXU_DEPTH = 256


def _dot3_impl(a, b, dims=_NN, *, cache):
    (ca,), (cb,) = dims[0]
    depth = a.shape[ca]
    ah, al = _split_bf16(a, cache)
    bh, bl = _split_bf16(b, cache)
    f = lambda x, y: lax.dot_general(x, y, dims, preferred_element_type=F32)
    if 3 * depth <= MXU_DEPTH:
        return f(jnp.concatenate([ah, al, ah], axis=ca), jnp.concatenate([bh, bh, bl], axis=cb))
    return f(jnp.concatenate([ah, al], axis=ca), jnp.concatenate([bh, bh], axis=cb)) + f(ah, bl)


def _gdn_kernel(alog_ref, dtb_ref, dqkv_ref, z_ref, ab_ref, cw_ref, nrm_ref, pre_ref, s0_ref,
                o_ref, sfin_ref, tail_ref, state_ref, *, c, g, nsub):
    n = pl.program_id(1)
    ngrp = DN_HEADS // g
    r = g * c
    rows = nsub * c
    hd = HEAD_DIM
    _dot3 = functools.partial(_dot3_impl, cache={})

    @pl.when(n == 0)
    def _():
        tail_ref[...] = jnp.concatenate([jnp.zeros((8 - (CONV_W - 1), DN_QKV), F32), pre_ref[0]], axis=0)
        for gi in range(ngrp):
            state_ref[gi] = jnp.concatenate([s0_ref[0, gi * g + j] for j in range(g)], axis=1)

    x = dqkv_ref[...]
    hist = jnp.concatenate([tail_ref[...], x], axis=0)
    tail_ref[...] = hist[rows:]
    cw = cw_ref[...]
    y = x * cw[CONV_W - 1:CONV_W]
    for d in range(1, CONV_W):
        y = y + hist[8 - d:8 - d + rows] * cw[CONV_W - 1 - d:CONV_W - d]
    qkv = y * jax.nn.sigmoid(y)
    z = z_ref[...]
    ab = ab_ref[...]
    beta_all = jax.nn.sigmoid(ab[:, DN_HEADS:2 * DN_HEADS])
    g_all = -jnp.exp(alog_ref[...]) * jax.nn.softplus(ab[:, 0:DN_HEADS] + dtb_ref[...])

    ri = lax.broadcasted_iota(I32, (r, r), 0)
    ci = lax.broadcasted_iota(I32, (r, r), 1)
    same = (ri // c) == (ci // c)
    lower = same & (ri >= ci)
    strict = same & (ri > ci)
    tri = lower.astype(F32)
    blk = same.astype(F32)
    eye = (ri == ci).astype(F32)
    rowhead = lax.broadcasted_iota(I32, (r, hd), 0) // c

    def stack(a, sub, gi, base):
        return jnp.concatenate([a[sub * c:(sub + 1) * c, base + (gi * g + j) * hd:base + (gi * g + j + 1) * hd]
                                for j in range(g)], axis=0)

    def column(a, sub, gi):
        return jnp.concatenate([a[sub * c:(sub + 1) * c, gi * g + j:gi * g + j + 1] for j in range(g)], axis=0)

    def own_block(m):
        acc = jnp.where(rowhead == 0, m[:, 0:hd], 0.0)
        for j in range(1, g):
            acc = acc + jnp.where(rowhead == j, m[:, j * hd:(j + 1) * hd], 0.0)
        return acc

    keys = [(sub, gi) for sub in range(nsub) for gi in range(ngrp)]
    grp = {}
    for sub, gi in keys:
        q = stack(qkv, sub, gi, 0)
        k = stack(qkv, sub, gi, DN_WIDTH)
        v = stack(qkv, sub, gi, 2 * DN_WIDTH)
        q = q * lax.rsqrt(jnp.sum(q * q, axis=-1, keepdims=True) + EPS) * (hd ** -0.5)
        k = k * lax.rsqrt(jnp.sum(k * k, axis=-1, keepdims=True) + EPS)
        beta = column(beta_all, sub, gi)
        gr = column(g_all, sub, gi)
        gcol = _fdot(tri, gr)
        grow = lax.dot_general(gr, tri, (((0,), (1,)), ((), ())), precision=HIGHEST,
                               preferred_element_type=F32)
        glast = _fdot(blk, gr)
        decay = jnp.exp(jnp.where(lower, gcol - grow, NEG_INF))
        kb = k * beta
        lm = jnp.where(strict, _dot3(kb, k, _NT) * decay, 0.0)
        grp[sub, gi] = dict(q=q, k=k, v=v, beta=beta, gcol=gcol, glast=glast, decay=decay, kb=kb, lm=lm)

    pair = ((ri // 2) == (ci // 2)) & (ri % 2 == 1) & (ci % 2 == 0)
    invs = {key: eye - jnp.where(pair, grp[key]["lm"], 0.0) for key in keys}
    s = 2
    while s < c:
        off = ((ri // (2 * s)) == (ci // (2 * s))) & ((ri // s) % 2 == 1) & ((ci // s) % 2 == 0)
        tmp = {key: _dot3(invs[key], jnp.where(off, grp[key]["lm"], 0.0)) for key in keys}
        invs = {key: invs[key] - _dot3(tmp[key], invs[key]) for key in keys}
        s *= 2

    for sub in range(nsub):
        outs = [None] * DN_HEADS
        for gi in range(ngrp):
            d = grp[sub, gi]
            eg = jnp.exp(d["gcol"])
            sol = _dot3(invs[sub, gi], jnp.concatenate([d["v"] * d["beta"], d["kb"] * eg], axis=1))
            u = sol[:, :hd]
            w = sol[:, hd:]
            qk = _dot3(d["q"], d["k"], _NT) * d["decay"]
            ktail = d["k"] * jnp.exp(d["glast"] - d["gcol"])
            qhead = d["q"] * eg
            st = state_ref[gi]
            vnew = u - own_block(_dot3(w, st))
            o = own_block(_dot3(qhead, st)) + _dot3(qk, vnew)
            vblk = jnp.concatenate([jnp.where(rowhead == j, vnew, 0.0) for j in range(g)], axis=1)
            elast = jnp.exp(d["glast"])
            keep = jnp.concatenate([jnp.broadcast_to(elast[j * c:j * c + 1, :], (1, hd)) for j in range(g)],
                                   axis=1)
            state_ref[gi] = st * keep + _dot3(ktail, vblk, _TN)
            zs = stack(z, sub, gi, 0)
            o = _rms(o, nrm_ref[...]) * (zs * jax.nn.sigmoid(zs))
            for j in range(g):
                outs[gi * g + j] = o[j * c:(j + 1) * c]
        o_ref[sub * c:(sub + 1) * c, :] = jnp.concatenate(outs, axis=-1)

    @pl.when(n == pl.num_programs(1) - 1)
    def _():
        for gi in range(ngrp):
            for j in range(g):
                sfin_ref[0, gi * g + j] = state_ref[gi][:, j * hd:(j + 1) * hd]


GDN_CHUNKS_PER_STEP = 2


def _gdn(dqkv, z, ab, conv_prefix, s0, conv_w, a_log, dt_bias, dn_norm, batch, seq):
    c = min(DN_CHUNK, seq)
    nchunk = seq // c
    assert seq % c == 0 and c % 8 == 0
    g = min(DN_HEADS, max(1, 128 // c))
    nsub = GDN_CHUNKS_PER_STEP if nchunk % GDN_CHUNKS_PER_STEP == 0 else 1
    nstep = nchunk // nsub
    rows = nsub * c
    tok = lambda b, n: (b * nstep + n, 0)
    fixed = lambda b, n: (0, 0)
    return pl.pallas_call(
        functools.partial(_gdn_kernel, c=c, g=g, nsub=nsub), name="gdn", grid=(batch, nstep),
        out_shape=[jax.ShapeDtypeStruct((batch * seq, DN_WIDTH), F32),
                   jax.ShapeDtypeStruct((batch, DN_HEADS, HEAD_DIM, HEAD_DIM), F32)],
        in_specs=[pl.BlockSpec((1, DN_HEADS), fixed), pl.BlockSpec((1, DN_HEADS), fixed),
                  pl.BlockSpec((rows, DN_QKV), tok), pl.BlockSpec((rows, DN_WIDTH), tok),
                  pl.BlockSpec((rows, 2 * DN_HEADS), tok),
                  pl.BlockSpec((CONV_W, DN_QKV), fixed), pl.BlockSpec((1, HEAD_DIM), fixed),
                  pl.BlockSpec((1, CONV_W - 1, DN_QKV), lambda b, n: (b, 0, 0)),
                  pl.BlockSpec((1, DN_HEADS, HEAD_DIM, HEAD_DIM), lambda b, n: (b, 0, 0, 0))],
        out_specs=[pl.BlockSpec((rows, DN_WIDTH), tok),
                   pl.BlockSpec((1, DN_HEADS, HEAD_DIM, HEAD_DIM), lambda b, n: (b, 0, 0, 0))],
        scratch_shapes=[pltpu.VMEM((8, DN_QKV), F32),
                        pltpu.VMEM((DN_HEADS // g, HEAD_DIM, g * HEAD_DIM), F32)],
        compiler_params=_cparams("parallel", "arbitrary"))(
            a_log.reshape(1, DN_HEADS), dt_bias.reshape(1, DN_HEADS), dqkv, z, ab, conv_w,
            dn_norm.reshape(1, HEAD_DIM), conv_prefix, s0)


def _outproj_kernel(x_ref, att_ref, dn_ref, wo_ref, g_ref, wq_ref, h_ref, c_ref, q_ref):
    mix = jnp.concatenate([att_ref[...], dn_ref[...]], axis=-1)
    h = x_ref[...] + jnp.dot(mix.astype(BF16), wo_ref[...], preferred_element_type=F32)
    h_ref[...] = h
    cn = _rms(h, g_ref[...])
    c_ref[...] = cn
    q_ref[...] = jnp.dot(cn.astype(BF16), wq_ref[...], preferred_element_type=F32)


def _outproj(x, att, dn, w_out, g_ffn, wq, row0):
    t = att.shape[0]
    tm = _row_tile(t, 512)
    assert row0 % tm == 0
    row = lambda i: (i, 0)
    xrow = lambda i: (i + row0 // tm, 0)
    fixed = lambda i: (0, 0)
    qw = PEER_HEADS * PEER_QDIM
    return pl.pallas_call(
        _outproj_kernel, name="outproj", grid=(t // tm,),
        out_shape=[jax.ShapeDtypeStruct((t, D_MODEL), F32), jax.ShapeDtypeStruct((t, D_MODEL), F32),
                   jax.ShapeDtypeStruct((t, qw), F32)],
        in_specs=[pl.BlockSpec((tm, D_MODEL), xrow), pl.BlockSpec((tm, ATT_Q), row),
                  pl.BlockSpec((tm, DN_WIDTH), row), pl.BlockSpec((D_MODEL, D_MODEL), fixed),
                  pl.BlockSpec((1, D_MODEL), fixed), pl.BlockSpec((D_MODEL, qw), fixed)],
        out_specs=[pl.BlockSpec((tm, D_MODEL), row), pl.BlockSpec((tm, D_MODEL), row),
                   pl.BlockSpec((tm, qw), row)],
        compiler_params=_cparams("parallel"))(x, att, dn, w_out, g_ffn, wq)


def _top16_rows(s, iota):
    big = jnp.float32(INDEX_SENTINEL)
    vals, idxs = [], []
    for _ in range(PEER_TOPK):
        m = jnp.max(s, axis=0, keepdims=True)
        i = jnp.min(jnp.where(s == m, iota, big), axis=0, keepdims=True)
        vals.append(m)
        idxs.append(i)
        s = jnp.where(iota == i, NEG_INF, s)
    return jnp.concatenate(vals, axis=0), jnp.concatenate(idxs, axis=0)


def _candidates(a, b, row8):
    t = a.shape[1]
    lo = row8 < 4
    bc = lambda r, n: jnp.broadcast_to(a[r:r + 1], (n, t))
    b8 = b[0:8]
    b44 = jnp.where(lo, b8, pltpu.roll(b8, 4, 0))
    parts = [bc(0, 8), bc(0, 8), bc(1, 8), bc(2, 8), bc(3, 8),
             jnp.where(lo, bc(4, 8), bc(5, 8)), jnp.where(lo, bc(6, 8), bc(7, 8)), a[8:16]]
    others = [b[0:8], b[8:16], b8, b8, b8, b44, b44, jnp.broadcast_to(b[0:1], (8, t))]
    return parts, others


def _flat_index_column():
    r = lax.broadcasted_iota(I32, (8, 1), 0)
    lo = r < 4
    cols = [r, 8 + r, 16 + r, 32 + r, 48 + r, jnp.where(lo, 64 + r, 80 + r - 4),
            jnp.where(lo, 96 + r, 112 + r - 4), (8 + r) * 16]
    return jnp.concatenate(cols, axis=0).astype(F32)


def _route_kernel(q_ref, k1_ref, k2_ref, idx_ref, gate_ref, *, tb):
    iota = lax.broadcasted_iota(I32, (N_KEYS, tb), 0).astype(F32)
    row8 = lax.broadcasted_iota(I32, (8, tb), 0)
    flat = jnp.broadcast_to(_flat_index_column(), (4 * PEER_TOPK, tb))
    big = jnp.float32(INDEX_SENTINEL)
    half = PEER_QDIM // 2
    for h in range(PEER_HEADS):
        q1 = q_ref[:, h * PEER_QDIM:h * PEER_QDIM + half]
        q2 = q_ref[:, h * PEER_QDIM + half:(h + 1) * PEER_QDIM]
        nt = (((1,), (1,)), ((), ()))
        s1 = lax.dot_general(k1_ref[h], q1.astype(BF16), nt, preferred_element_type=F32)
        s2 = lax.dot_general(k2_ref[h], q2.astype(BF16), nt, preferred_element_type=F32)
        v1, i1 = _top16_rows(s1, iota)
        v2, i2 = _top16_rows(s2, iota)
        pa, pb = _candidates(v1, v2, row8)
        cand = jnp.concatenate([x + y for x, y in zip(pa, pb)], axis=0)
        ea, eb = _candidates(i1, i2, row8)
        expert = jnp.concatenate([x * float(N_KEYS) + y for x, y in zip(ea, eb)], axis=0)
        scs, exs = [], []
        for _ in range(PEER_TOPK):
            m = jnp.max(cand, axis=0, keepdims=True)
            c = jnp.min(jnp.where(cand == m, flat, big), axis=0, keepdims=True)
            hit = flat == c
            exs.append(jnp.max(jnp.where(hit, expert, -1.0), axis=0, keepdims=True))
            scs.append(m)
            cand = jnp.where(hit, NEG_INF, cand)
        sc = jnp.concatenate(scs, axis=0)
        e = jnp.exp(sc - sc[0:1])
        gate_ref[h * PEER_TOPK:(h + 1) * PEER_TOPK, :] = e / jnp.sum(e, axis=0, keepdims=True)
        idx_ref[h * PEER_TOPK:(h + 1) * PEER_TOPK, :] = jnp.concatenate(exs, axis=0).astype(I32)


def _route(q, keys1, keys2):
    t = q.shape[0]
    tb = _row_tile(t, 256)
    kspec = pl.BlockSpec((PEER_HEADS, N_KEYS, PEER_QDIM // 2), lambda i: (0, 0, 0))
    return pl.pallas_call(
        functools.partial(_route_kernel, tb=tb), name="peer_route", grid=(t // tb,),
        out_shape=[jax.ShapeDtypeStruct((PEER_SLOTS, t), I32), jax.ShapeDtypeStruct((PEER_SLOTS, t), F32)],
        in_specs=[pl.BlockSpec((tb, PEER_HEADS * PEER_QDIM), lambda i: (i, 0)), kspec, kspec],
        out_specs=[pl.BlockSpec((PEER_SLOTS, tb), lambda i: (0, i)),
                   pl.BlockSpec((PEER_SLOTS, tb), lambda i: (0, i))],
        compiler_params=_cparams("parallel"))(q, keys1, keys2)


GATHER_SLOTS = 4
SC_WORKERS = 32
SC_GATHER_ROWS = 16
SC_BUFFERS = 6
STAGED_TOKENS_PER_STEP = 32


def _peer_token(packed, x, gate):
    u = lax.bitcast_convert_type(packed & jnp.uint32(0xFFFF0000), F32)
    v = lax.bitcast_convert_type(packed << 16, F32)
    p = u * x
    acc = p[:, 0:128]
    for j in range(1, D_MODEL // 128):
        acc = acc + p[:, j * 128:(j + 1) * 128]
    s = jnp.sum(acc, axis=-1, keepdims=True)
    act = 0.5 * s * (1.0 + lax.erf(s * (1.0 / math.sqrt(2.0))))
    return jnp.sum((gate * act) * v, axis=0, keepdims=True)


def _peer_apply_kernel(idx_ref, c_ref, h_ref, gt_ref, uv_ref, o_ref, buf, sem, *, tb):
    ahead = GATHER_SLOTS - 1

    def issue(t, slot):
        for k in range(PEER_SLOTS):
            e = idx_ref[k, t]
            pltpu.make_async_copy(uv_ref.at[e], buf.at[slot, pl.ds(k, 1), :],
                                  sem.at[slot]).start(priority=k % 2)

    def wait(slot):
        pltpu.make_async_copy(buf.at[slot], buf.at[slot], sem.at[slot]).wait()

    lane = lax.broadcasted_iota(I32, (PEER_SLOTS, tb), 1)

    def apply(t, slot):
        gate = jnp.sum(jnp.where(lane == t, gt_ref[...], 0.0), axis=-1, keepdims=True)
        y = _peer_token(buf[slot], c_ref[pl.ds(t, 1), :], gate)
        o_ref[pl.ds(t, 1), :] = h_ref[pl.ds(t, 1), :] + y

    for t0 in range(ahead):
        issue(t0, t0)

    def body(i, carry):
        for j in range(GATHER_SLOTS):
            t = i * GATHER_SLOTS + j
            issue(t + ahead, (j + ahead) % GATHER_SLOTS)
            wait(j)
            apply(t, j)
        return carry

    nmain = (tb - ahead) // GATHER_SLOTS
    lax.fori_loop(0, nmain, body, 0)
    for t in range(nmain * GATHER_SLOTS, tb):
        if t + ahead < tb:
            issue(t + ahead, (t + ahead) % GATHER_SLOTS)
        wait(t % GATHER_SLOTS)
        apply(t, t % GATHER_SLOTS)


def _pack_kernel(u_ref, v_ref, o_ref):
    hi = lax.bitcast_convert_type(u_ref[...].astype(BF16).astype(F32), jnp.uint32)
    lo = lax.bitcast_convert_type(v_ref[...].astype(BF16).astype(F32), jnp.uint32) >> 16
    o_ref[...] = hi | lo


def _pack_expert_rows(u, v):
    n, d = u.shape
    tm = _row_tile(n, 512)
    spec = pl.BlockSpec((tm, d), lambda i: (i, 0))
    return pl.pallas_call(
        _pack_kernel, name="pack_expert_rows", grid=(n // tm,),
        out_shape=jax.ShapeDtypeStruct((n, d), jnp.uint32),
        in_specs=[spec, spec], out_specs=spec,
        compiler_params=_cparams("parallel"))(u, v)


def _peer_apply_direct(idx_t, c, h, gate_t, uv_rows, t_direct):
    t = c.shape[0]
    tb = _row_tile(t_direct, 256)
    row = lambda i: (i, 0)
    return pl.pallas_call(
        functools.partial(_peer_apply_kernel, tb=tb), name="peer_apply", grid=(t_direct // tb,),
        out_shape=jax.ShapeDtypeStruct((t, D_MODEL), F32),
        in_specs=[pl.BlockSpec((PEER_SLOTS, tb), lambda i: (0, i), memory_space=pltpu.SMEM),
                  pl.BlockSpec((tb, D_MODEL), row), pl.BlockSpec((tb, D_MODEL), row),
                  pl.BlockSpec((PEER_SLOTS, tb), lambda i: (0, i)),
                  pl.BlockSpec(memory_space=pl.ANY)],
        out_specs=pl.BlockSpec((tb, D_MODEL), row),
        scratch_shapes=[pltpu.VMEM((GATHER_SLOTS, PEER_SLOTS, D_MODEL), jnp.uint32),
                        pltpu.SemaphoreType.DMA((GATHER_SLOTS,))],
        compiler_params=_cparams("arbitrary"))(idx_t, c, h, gate_t, uv_rows)


def _sc_stage_rows(table, idx):
    t_total, nchunk_tok, _ = idx.shape
    d = table.shape[1]
    tpw = t_total // SC_WORKERS
    assert t_total % SC_WORKERS == 0 and tpw >= 2
    nb = SC_BUFFERS
    ahead = nb // 2
    params = pltpu.CompilerParams()
    if "needs_layout_passes" in pltpu.CompilerParams.__dataclass_fields__:
        params = dataclasses.replace(params, needs_layout_passes=False)
    mesh = plsc.VectorSubcoreMesh(core_axis_name="c", subcore_axis_name="s")

    @functools.partial(
        pl.kernel, mesh=mesh, compiler_params=params,
        out_type=jax.ShapeDtypeStruct((t_total * nchunk_tok * SC_GATHER_ROWS, d), table.dtype),
        scratch_types=[pltpu.VMEM((2, nchunk_tok, SC_GATHER_ROWS), I32),
                       pltpu.VMEM((nb, SC_GATHER_ROWS, d), table.dtype),
                       pltpu.SemaphoreType.DMA((nb,)), pltpu.SemaphoreType.DMA((nb,)),
                       pltpu.SemaphoreType.DMA((2,))])
    def stage(table_hbm, idx_hbm, out_hbm, idx_v, rows_v, sem_g, sem_w, sem_t):
        wid = lax.axis_index("s") * 2 + lax.axis_index("c")
        base = wid * tpw
        nchunks = tpw * nchunk_tok

        def gather(g, slot):
            ts = (g // nchunk_tok) % 2
            return pltpu.make_async_copy(table_hbm.at[idx_v.at[ts, g % nchunk_tok]], rows_v.at[slot],
                                         sem_g.at[slot])

        def write(g, slot):
            dst = out_hbm.at[pl.ds((base * nchunk_tok + g) * SC_GATHER_ROWS, SC_GATHER_ROWS)]
            return pltpu.make_async_copy(rows_v.at[slot], dst, sem_w.at[slot])

        def idx_copy(i, ts):
            return pltpu.make_async_copy(idx_hbm.at[base + i], idx_v.at[ts], sem_t.at[ts])

        idx_copy(0, 0).start()
        idx_copy(0, 0).wait()
        idx_copy(1, 1).start()
        for g0 in range(ahead):
            gather(g0, g0).start()

        @pl.loop(0, nchunks)
        def _(g):
            slot = g % nb
            nxt = g + ahead
            nslot = nxt % nb

            @pl.when(nxt < nchunks)
            def _():
                @pl.when(nxt >= nb)
                def _():
                    write(nxt - nb, nslot).wait()

                @pl.when(nxt % nchunk_tok == 0)
                def _():
                    idx_copy(nxt // nchunk_tok, (nxt // nchunk_tok) % 2).wait()

                gather(nxt, nslot).start()

            gather(g, slot).wait()
            write(g, slot).start()

            @pl.when((g % nchunk_tok == nchunk_tok - 1) & (g // nchunk_tok + 2 < tpw))
            def _():
                idx_copy(g // nchunk_tok + 2, (g // nchunk_tok) % 2).start()

        for j in range(nb):
            gl = nchunks - nb + j
            write(gl, gl % nb).wait()

    return stage(table, idx)


def _peer_staged_kernel(st_ref, c_ref, h_ref, gt_ref, *rest, tbs):
    o_ref = rest[-1]
    i = pl.program_id(0)
    lane = lax.broadcasted_iota(I32, (PEER_SLOTS, 128), 1)
    for t in range(tbs):
        tg = i * tbs + t
        col0 = pl.multiple_of((tg // 128) * 128, 128)
        gate = jnp.sum(jnp.where(lane == tg % 128, gt_ref[:, pl.ds(col0, 128)], 0.0), axis=-1, keepdims=True)
        y = _peer_token(st_ref[t], c_ref[pl.ds(t, 1), :], gate)
        o_ref[pl.ds(t, 1), :] = h_ref[pl.ds(t, 1), :] + y


def _peer_apply_staged(staged, c, h, gate_t_staged, partial, start):
    t_staged = staged.shape[0]
    tbs = STAGED_TOKENS_PER_STEP
    assert t_staged % tbs == 0 and start % tbs == 0 and t_staged % 128 == 0
    off = start // tbs
    row = lambda i: (i + off, 0)
    in_specs = [pl.BlockSpec((tbs, PEER_SLOTS, D_MODEL), lambda i: (i, 0, 0)),
                pl.BlockSpec((tbs, D_MODEL), row), pl.BlockSpec((tbs, D_MODEL), row),
                pl.BlockSpec((PEER_SLOTS, t_staged), lambda i: (0, 0))]
    args = [staged, c, h, gate_t_staged]
    aliases = {}
    if partial is not None:
        in_specs.append(pl.BlockSpec(memory_space=pl.ANY))
        args.append(partial)
        aliases = {4: 0}
    return pl.pallas_call(
        functools.partial(_peer_staged_kernel, tbs=tbs), name="peer_apply_staged", grid=(t_staged // tbs,),
        out_shape=jax.ShapeDtypeStruct(c.shape, F32),
        in_specs=in_specs, out_specs=pl.BlockSpec((tbs, D_MODEL), row),
        input_output_aliases=aliases,
        compiler_params=_cparams("arbitrary"))(*args)


def _peer_apply(idx_t, c, h, gate_t, uv, staged_pieces):
    t = c.shape[0]
    start = t - sum(staged_pieces)
    out = _peer_apply_direct(idx_t, c, h, gate_t, uv[:, None, :], start) if start else None
    for n in staged_pieces:
        idx_sc = idx_t[:, start:start + n].T.reshape(n, PEER_SLOTS // SC_GATHER_ROWS, SC_GATHER_ROWS)
        staged = _sc_stage_rows(uv, idx_sc).reshape(n, PEER_SLOTS, D_MODEL)
        out = _peer_apply_staged(staged, c, h, gate_t[:, start:start + n], out, start)
        start += n
    return out


def _final_kernel(h_ref, p_ref, gple_ref, wg_ref, wp_ref, gfin_ref, y_ref):
    h = h_ref[...]
    e = _rms(h, gple_ref[...])
    gate = jax.nn.sigmoid(jnp.dot(e.astype(BF16), wg_ref[...], preferred_element_type=F32))
    up = jnp.dot(p_ref[...].astype(BF16), wp_ref[...], preferred_element_type=F32)
    y_ref[...] = _rms(h + up * gate, gfin_ref[...])


def _final_into_kernel(h_ref, p_ref, gple_ref, wg_ref, wp_ref, gfin_ref, prev_ref, y_ref):
    del prev_ref
    _final_kernel(h_ref, p_ref, gple_ref, wg_ref, wp_ref, gfin_ref, y_ref)


def _final(h, p, g_ple, w_gate, w_ple, g_final, into=None, row0=0, total=None):
    t = h.shape[0]
    total = t if total is None else total
    tm = _row_tile(t, 512)
    assert row0 % tm == 0
    row = lambda i: (i, 0)
    out_row = lambda i: (i + row0 // tm, 0)
    fixed = lambda i: (0, 0)
    in_specs = [pl.BlockSpec((tm, D_MODEL), row), pl.BlockSpec((tm, PLE_DIM), row),
                pl.BlockSpec((1, D_MODEL), fixed), pl.BlockSpec((D_MODEL, D_MODEL), fixed),
                pl.BlockSpec((PLE_DIM, D_MODEL), fixed), pl.BlockSpec((1, D_MODEL), fixed)]
    args = [h, p, g_ple, w_gate, w_ple, g_final]
    body = _final_kernel
    aliases = {}
    if into is not None:
        in_specs.append(pl.BlockSpec(memory_space=pl.ANY))
        args.append(into)
        aliases = {len(args) - 1: 0}
        body = _final_into_kernel
    return pl.pallas_call(
        body, name="ple_final", grid=(t // tm,),
        out_shape=jax.ShapeDtypeStruct((total, D_MODEL), F32),
        in_specs=in_specs, out_specs=pl.BlockSpec((tm, D_MODEL), out_row),
        input_output_aliases=aliases,
        compiler_params=_cparams("parallel"))(*args)


STAGED_FRACTIONS = (1.0, 0.75, 0.75, 0.625)
STAGED_UNIT = SC_WORKERS * 32
STAGED_PIECE = 8 * STAGED_UNIT


def _staged_tokens(t, fraction):
    total = int(t * fraction) // STAGED_UNIT * STAGED_UNIT
    if total < STAGED_UNIT or t < STAGED_PIECE:
        return ()
    pieces = [STAGED_PIECE] * (total // STAGED_PIECE)
    if total % STAGED_PIECE:
        pieces.append(total % STAGED_PIECE)
    return tuple(pieces)


def _layer_front(x, kv_buf, conv_prefix, s0, wts, batch, seq, row0=0):
    t = batch * seq
    q, k, v, dqkv, z, ab = _inproj(x, wts["norm_mix"], wts["w_in"], t, row0)
    if kv_buf is None:
        att = _swa_prompt(q, k, v, wts["sinks"], batch, seq)
        k_new = k.reshape(batch, seq, ATT_KV)[:, -WINDOW:]
        v_new = v.reshape(batch, seq, ATT_KV)[:, -WINDOW:]
    else:
        att, k_new, v_new = _swa_sample(q.reshape(batch, seq, ATT_Q), k.reshape(batch, seq, ATT_KV),
                                        v.reshape(batch, seq, ATT_KV), kv_buf[0], kv_buf[1], wts["sinks"])
        att = att.reshape(t, ATT_Q)
    dn, s_new = _gdn(dqkv, z, ab, conv_prefix, s0, wts["conv_w"], wts["a_log"], wts["dt_bias"],
                     wts["dn_norm"], batch, seq)
    conv_new = dqkv.reshape(batch, seq, DN_QKV)[:, -(CONV_W - 1):]
    h, c, pq = _outproj(x, att, dn, wts["w_out"], wts["norm_ffn"], wts["peer_wq"], row0)
    idx_t, gate_t = _route(pq, wts["keys1"], wts["keys2"])
    return dict(h=h, c=c, idx_t=idx_t, gate_t=gate_t, state=(k_new, v_new, conv_new, s_new))


def _layer_back(front, p, wts, staged_fraction=0.0, into=None, row0=0, total=None):
    t = front["c"].shape[0]
    h2 = _peer_apply(front["idx_t"], front["c"], front["h"], front["gate_t"], wts["uv"],
                     _staged_tokens(t, staged_fraction))
    return _final(h2, p, wts["norm_ple"], wts["ple_gate"], wts["ple_in"], wts["norm_final"], into, row0, total)


PROMPT_GROUPS = 4


def kernel(x_prompt, x_sample, p_prompt, p_sample, cache_swa_k, cache_swa_v, state_conv, state_delta, norm_mix, w_in, conv_w, attn_sinks, dn_a_log, dn_dt_bias, dn_norm, w_out, norm_ffn, peer_wq, peer_keys1, peer_keys2, peer_u, peer_v, norm_ple, ple_in, ple_gate, norm_final):
    depth = w_in.shape[0]
    assert depth == 1
    bp, lp = x_prompt.shape[0], x_prompt.shape[1]
    bs, ls = x_sample.shape[0], x_sample.shape[1]
    l = 0
    wts = dict(
        norm_mix=norm_mix[l].reshape(1, D_MODEL),
        w_in=jnp.pad(w_in[l], ((0, 0), (0, IN_WIDTH_PADDED - IN_WIDTH))).astype(BF16),
        conv_w=conv_w[l], sinks=attn_sinks[l], a_log=dn_a_log[l], dt_bias=dn_dt_bias[l], dn_norm=dn_norm[l],
        w_out=w_out[l].astype(BF16), norm_ffn=norm_ffn[l].reshape(1, D_MODEL), peer_wq=peer_wq[l].astype(BF16),
        keys1=peer_keys1[l].astype(BF16), keys2=peer_keys2[l].astype(BF16),
        uv=_pack_expert_rows(peer_u[l], peer_v[l]),
        norm_ple=norm_ple[l].reshape(1, D_MODEL), ple_in=ple_in[l].astype(BF16),
        ple_gate=ple_gate[l].astype(BF16), norm_final=norm_final.reshape(1, D_MODEL))
    ngrp = PROMPT_GROUPS if bp % PROMPT_GROUPS == 0 else 1
    bg = bp // ngrp
    fronts = []
    x_all = x_prompt.reshape(bp * lp, D_MODEL)
    for gi in range(ngrp):
        fronts.append(_layer_front(x_all, None, jnp.zeros((bg, CONV_W - 1, DN_QKV), F32),
                                   jnp.zeros((bg, DN_HEADS, HEAD_DIM, HEAD_DIM), F32), wts, bg, lp, gi * bg * lp))
    kv_buf = (cache_swa_k[l].reshape(bs, WINDOW, ATT_KV), cache_swa_v[l].reshape(bs, WINDOW, ATT_KV))
    front_s = _layer_front(x_sample.reshape(bs * ls, D_MODEL), kv_buf, state_conv[l], state_delta[l], wts, bs, ls)
    yp = None
    for gi in range(ngrp):
        pg = p_prompt[l, gi * bg:(gi + 1) * bg].reshape(bg * lp, PLE_DIM)
        yp = _layer_back(fronts[gi], pg, wts, STAGED_FRACTIONS[gi % len(STAGED_FRACTIONS)], yp, gi * bg * lp, bp * lp)
    ys = _layer_back(front_s, p_sample[l].reshape(bs * ls, PLE_DIM), wts)
    kp, vp, cp, sp = (jnp.concatenate([f["state"][j] for f in fronts], axis=0) for j in range(4))
    kn, vn, cn, sn = front_s["state"]
    kvshape = lambda b: (1, b, WINDOW, ATT_KV_HEADS, HEAD_DIM)
    return (yp.reshape(bp, lp, D_MODEL), ys.reshape(bs, ls, D_MODEL),
            kp.reshape(kvshape(bp)), vp.reshape(kvshape(bp)), cp[None], sp[None],
            kn.reshape(kvshape(bs)), vn.reshape(kvshape(bs)), cn[None], sn[None])
```
